```python
import jax, jax.numpy as jnp
from jax import lax
import numpy as np

D_MODEL = 1024
BATCH = 4
SEQ = 4096
DEPTH = 1

D_CONV_A = D_MODEL
D_CONV_B = D_MODEL
K_SHORT = 3
K_CONF = 31
D_IN = 3 * D_CONV_A + 2 * D_CONV_B + 2 * D_MODEL
N_GROUPS = 4
EXPERTS_PER_GROUP = 8
N_EXPERTS = N_GROUPS * EXPERTS_PER_GROUP
TOP_K = 2
D_EXPERT = D_MODEL // 2
BLOCK_ROWS = 128
RMS_EPS = 1e-6
LN_EPS = 1e-5

kernel_name = "hybrid_shortconv_conformer_hmoe_block"


def rmsnorm(x, g):
    xf = x.astype(jnp.float32)
    y = xf * lax.rsqrt(jnp.mean(xf * xf, axis=-1, keepdims=True) + RMS_EPS)
    return (y * g.astype(jnp.float32)).astype(x.dtype)


def layernorm(x, g, b):
    xf = x.astype(jnp.float32)
    mu = jnp.mean(xf, axis=-1, keepdims=True)
    var = jnp.mean(jnp.square(xf - mu), axis=-1, keepdims=True)
    y = (xf - mu) * lax.rsqrt(var + LN_EPS)
    return (y * g.astype(jnp.float32) + b.astype(jnp.float32)).astype(x.dtype)


def depthwise_conv(x, w):
    k = w.shape[0]
    pad = k // 2
    return lax.conv_general_dilated(
        x, w[:, None, :].astype(x.dtype), window_strides=(1,), padding=[(pad, pad)],
        dimension_numbers=("NWC", "WIO", "NWC"), feature_group_count=x.shape[-1])


def mixer_sublayer(h, w_in, conv_a, w_a_out, conv_b, b_conv_b, ln_g, ln_b, w_b_out, w_o):
    z = h @ w_in
    cuts = np.cumsum([D_CONV_A, D_CONV_A, D_CONV_A, D_CONV_B, D_CONV_B, D_MODEL]).tolist()
    b_a, c_a, x_a, u_b, v_b, g_a, g_b = jnp.split(z, cuts, axis=-1)
    y_a = (b_a * depthwise_conv(c_a * x_a, conv_a)) @ w_a_out
    u = u_b * jax.nn.sigmoid(v_b)
    u = depthwise_conv(u, conv_b) + b_conv_b
    u = jax.nn.silu(layernorm(u, ln_g, ln_b))
    y_b = u @ w_b_out
    m = jax.nn.sigmoid(g_a) * y_a + jax.nn.sigmoid(g_b) * y_b
    return m @ w_o


def hier_moe(h, w_rg, b_rg, w_re, b_re, w_gate, w_up, w_down):
    bsz, s, d = h.shape
    xt = h.reshape(-1, d)
    n = xt.shape[0]
    lg = (xt @ w_rg).astype(jnp.float32) + b_rg.astype(jnp.float32)
    pg = jax.nn.softmax(lg, axis=-1)
    g_idx = jnp.argmax(lg, axis=-1)
    g_prob = jnp.take_along_axis(pg, g_idx[:, None], axis=-1)
    le = ((xt @ w_re).astype(jnp.float32) + b_re.astype(jnp.float32)).reshape(
        n, N_GROUPS, EXPERTS_PER_GROUP)
    le = jnp.take_along_axis(le, g_idx[:, None, None], axis=1)[:, 0]
    top_l, top_i = lax.top_k(le, TOP_K)
    top_p = jax.nn.softmax(top_l, axis=-1)
    weight = (g_prob * top_p).reshape(-1)
    expert = (g_idx[:, None] * EXPERTS_PER_GROUP + top_i).reshape(-1).astype(jnp.int32)
    token = jnp.repeat(jnp.arange(n, dtype=jnp.int32), TOP_K)
    order = jnp.argsort(expert)
    e_sorted = expert[order]
    t_sorted = token[order]
    w_sorted = weight[order]
    counts = jnp.bincount(expert, length=N_EXPERTS)
    padded = ((counts + BLOCK_ROWS - 1) // BLOCK_ROWS) * BLOCK_ROWS
    start = jnp.cumsum(counts) - counts
    pend = jnp.cumsum(padded)
    pstart = pend - padded
    rank = jnp.arange(n * TOP_K, dtype=jnp.int32) - start[e_sorted]
    dest = pstart[e_sorted] + rank
    n_rows = ((n * TOP_K + BLOCK_ROWS - 1) // BLOCK_ROWS) * BLOCK_ROWS + N_EXPERTS * BLOCK_ROWS
    n_blocks = n_rows // BLOCK_ROWS
    tok_buf = jnp.zeros((n_rows,), jnp.int32).at[dest].set(t_sorted)
    w_buf = jnp.zeros((n_rows,), jnp.float32).at[dest].set(w_sorted)
    blk_e = jnp.minimum(
        jnp.searchsorted(pend, jnp.arange(n_blocks) * BLOCK_ROWS, side="right"),
        N_EXPERTS - 1).astype(jnp.int32)
    xb = xt[tok_buf].reshape(n_blocks, BLOCK_ROWS, d)

    def expert_block(args):
        xblk, e = args
        a = xblk @ w_gate[e]
        u = xblk @ w_up[e]
        return (jax.nn.silu(a) * u) @ w_down[e]

    yb = lax.map(expert_block, (xb, blk_e)).reshape(n_rows, d)
    yb = yb * w_buf[:, None].astype(yb.dtype)
    out = jnp.zeros_like(xt).at[tok_buf].add(yb)
    return out.reshape(bsz, s, d)


def setup_inputs(seed: int = 0) -> dict:
    key = jax.random.key(seed)
    ks = jax.random.split(key, 24)
    D, L = D_MODEL, DEPTH

    def nrm(k, shape, scale):
        return jax.random.normal(k, shape, jnp.float32) * scale

    return {
        "x": nrm(ks[0], (BATCH, SEQ, D), 1.0),
        "c": nrm(ks[1], (BATCH, D), 1.0),
        "w_ada": nrm(ks[2], (L, D, 6 * D), 0.5 * D ** -0.5),
        "b_ada": nrm(ks[3], (L, 6 * D), 0.02),
        "g_norm1": 1.0 + nrm(ks[4], (L, D), 0.05),
        "w_in": nrm(ks[5], (L, D, D_IN), D ** -0.5),
        "conv_a": nrm(ks[6], (L, K_SHORT, D_CONV_A), K_SHORT ** -0.5),
        "w_a_out": nrm(ks[7], (L, D_CONV_A, D), D_CONV_A ** -0.5),
        "conv_b": nrm(ks[8], (L, K_CONF, D_CONV_B), K_CONF ** -0.5),
        "b_conv_b": nrm(ks[9], (L, D_CONV_B), 0.02),
        "ln_conv_g": 1.0 + nrm(ks[10], (L, D_CONV_B), 0.05),
        "ln_conv_b": nrm(ks[11], (L, D_CONV_B), 0.02),
        "w_b_out": nrm(ks[12], (L, D_CONV_B, D), D_CONV_B ** -0.5),
        "w_o": nrm(ks[13], (L, D, D), D ** -0.5),
        "g_norm2": 1.0 + nrm(ks[14], (L, D), 0.05),
        "w_router_g": nrm(ks[15], (L, D, N_GROUPS), D ** -0.5),
        "b_router_g": nrm(ks[16], (L, N_GROUPS), 0.01),
        "w_router_e": nrm(ks[17], (L, D, N_EXPERTS), D ** -0.5),
        "b_router_e": nrm(ks[18], (L, N_EXPERTS), 0.01),
        "w_gate": nrm(ks[19], (L, N_EXPERTS, D, D_EXPERT), D ** -0.5),
        "w_up": nrm(ks[20], (L, N_EXPERTS, D, D_EXPERT), D ** -0.5),
        "w_down": nrm(ks[21], (L, N_EXPERTS, D_EXPERT, D), D_EXPERT ** -0.5),
        "g_final": 1.0 + nrm(ks[22], (D,), 0.05),
    }


def reference(x, c, w_ada, b_ada, g_norm1, w_in, conv_a, w_a_out, conv_b, b_conv_b,
              ln_conv_g, ln_conv_b, w_b_out, w_o, g_norm2, w_router_g, b_router_g,
              w_router_e, b_router_e, w_gate, w_up, w_down, g_final):
    for l in range(DEPTH):
        mod = jax.nn.silu(c) @ w_ada[l] + b_ada[l]
        sh1, sc1, gt1, sh2, sc2, gt2 = jnp.split(mod[:, None, :], 6, axis=-1)
        h = rmsnorm(x, g_norm1[l]) * (1.0 + sc1) + sh1
        x = x + gt1 * mixer_sublayer(h, w_in[l], conv_a[l], w_a_out[l], conv_b[l], b_conv_b[l],
                                     ln_conv_g[l], ln_conv_b[l], w_b_out[l], w_o[l])
        h = rmsnorm(x, g_norm2[l]) * (1.0 + sc2) + sh2
        x = x + gt2 * hier_moe(h, w_router_g[l], b_router_g[l], w_router_e[l], b_router_e[l],
                               w_gate[l], w_up[l], w_down[l])
    return rmsnorm(x, g_final)
```

```python
import functools

import jax
import jax.numpy as jnp
from jax import lax
from jax.experimental import pallas as pl
from jax.experimental.pallas import tpu as pltpu

F32 = jnp.float32
BF16 = jnp.bfloat16

N_GROUPS = 4
EXPERTS_PER_GROUP = 8
N_EXPERTS = N_GROUPS * EXPERTS_PER_GROUP
TOP_K = 2
K_SHORT = 3
K_CONF = 31
RMS_EPS = 1e-6
LN_EPS = 1e-5

LANES = 128
SUBLANES = 8
T_MIX = 256
HALO_B = 16
HALO_A = 8
CONV_ROWS = 64
ROW_BLK = 128
T_OUT = 256
ROUTE_LANES = LANES
VMEM_LIMIT_BYTES = 56 * 1024 * 1024


def _sigmoid(v):
    return 1.0 / (1.0 + jnp.exp(-v))


def _split_bf16(v):
    hi = v.astype(BF16)
    lo = (v - hi.astype(F32)).astype(BF16)
    return hi, lo


def _dot(a, b):
    return jnp.dot(a, b, preferred_element_type=F32)


def _const_spec(shape):
    nd = len(shape)
    return pl.BlockSpec(shape, lambda *_: (0,) * nd, pipeline_mode=pl.Buffered(1))


def _mod_kernel(c_ref, w_ref, b_ref, o_ref):
    c = c_ref[...]
    a_hi, a_lo = _split_bf16(c * _sigmoid(c))
    w_hi, w_lo = _split_bf16(w_ref[...])
    o_ref[...] = _dot(a_hi, w_hi) + _dot(a_lo, w_hi) + _dot(a_hi, w_lo) + b_ref[...]


def _mod_call(c_pad, w_ada, b_ada):
    rows, d = c_pad.shape
    n_out = w_ada.shape[1]
    blk = 1024
    return pl.pallas_call(
        _mod_kernel,
        grid=(n_out // blk,),
        in_specs=[
            pl.BlockSpec((rows, d), lambda j: (0, 0)),
            pl.BlockSpec((d, blk), lambda j: (0, j)),
            pl.BlockSpec((1, blk), lambda j: (0, j)),
        ],
        out_specs=pl.BlockSpec((rows, blk), lambda j: (0, j)),
        out_shape=jax.ShapeDtypeStruct((rows, n_out), F32),
        compiler_params=pltpu.CompilerParams(dimension_semantics=("arbitrary",)),
        name="adaln_mod",
    )(c_pad, w_ada, b_ada)


def _depthwise_taps(buf_ref, w_ref, out_ref, n_taps, row0, n_rows, d):
    for cb in range(d // LANES):
        cols = slice(cb * LANES, (cb + 1) * LANES)
        for rb in range(n_rows // CONV_ROWS):
            acc = None
            for k in range(n_taps):
                term = buf_ref[pl.ds(row0 + rb * CONV_ROWS + k, CONV_ROWS), cols] * w_ref[k:k + 1, cols]
                acc = term if acc is None else acc + term
            out_ref[pl.ds(rb * CONV_ROWS, CONV_ROWS), cols] = acc


def _route(logits):
    shape = logits.shape
    lane = lax.broadcasted_iota(jnp.int32, shape, 1).astype(F32)
    big = float(4 * ROUTE_LANES)
    neg_inf = -jnp.inf

    def first_argmax(v):
        m = jnp.max(v, axis=-1, keepdims=True)
        return m, jnp.min(jnp.where(v == m, lane, big), axis=-1, keepdims=True)

    is_group = lane < N_GROUPS
    g_max, g_idx = first_argmax(jnp.where(is_group, logits, neg_inf))
    g_prob = 1.0 / jnp.sum(jnp.where(is_group, jnp.exp(logits - g_max), 0.0), axis=-1, keepdims=True)
    lo = N_GROUPS + EXPERTS_PER_GROUP * g_idx
    in_group = (lane >= lo) & (lane < lo + EXPERTS_PER_GROUP)
    e_logits = jnp.where(in_group, logits, neg_inf)
    m1, i1 = first_argmax(e_logits)
    m2, i2 = first_argmax(jnp.where(lane == i1, neg_inf, e_logits))
    t = jnp.exp(m2 - m1)
    p1 = 1.0 / (1.0 + t)
    p2 = t / (1.0 + t)
    rec = jnp.where(lane == 0.0, i1 - N_GROUPS, 0.0)
    rec = jnp.where(lane == 1.0, i2 - N_GROUPS, rec)
    rec = jnp.where(lane == 2.0, g_prob * p1, rec)
    rec = jnp.where(lane == 3.0, g_prob * p2, rec)
    return rec


def _mixer_kernel(x_ref, modn_ref, modp_ref, g1_ref, ca_ref, cb_ref, bcb_ref, lng_ref, lnb_ref, g2_ref,
                  win_ref, wa_ref, wb_ref, wo_ref, wrh_ref, wrl_ref, br_ref,
                  x1_ref, h2_ref, route_ref,
                  wu_ref, wcx_ref, ba_ref, sga_ref, sgb_ref, xs_ref, cva_ref, cvb_ref, *, n_tiles):
    i = pl.program_id(0)
    t_rows, d = x_ref.shape

    @pl.when(i == 0)
    def _init():
        for ref in (wu_ref, wcx_ref, ba_ref, sga_ref, sgb_ref, xs_ref):
            ref[...] = jnp.zeros(ref.shape, ref.dtype)

    x = x_ref[...]
    sh1 = modn_ref[0, 0:1, :]
    sc1 = modn_ref[0, 1:2, :]
    h = (x * lax.rsqrt(jnp.mean(x * x, axis=-1, keepdims=True) + RMS_EPS)) * g1_ref[...]
    hb = (h * (1.0 + sc1) + sh1).astype(BF16)

    def proj(g):
        return _dot(hb, win_ref[:, g * d:(g + 1) * d])

    b_a = proj(0)
    cx = proj(1) * proj(2)
    u = proj(3) * _sigmoid(proj(4))
    sg_a = _sigmoid(proj(5))
    sg_b = _sigmoid(proj(6))

    same_seq = ((i % n_tiles) != 0).astype(F32)
    wu_ref[pl.ds(HALO_B + t_rows, HALO_B), :] = u[0:HALO_B] * same_seq
    wcx_ref[pl.ds(HALO_A + t_rows, HALO_A), :] = cx[0:HALO_A] * same_seq

    _depthwise_taps(wcx_ref, ca_ref, cva_ref, K_SHORT, HALO_A - K_SHORT // 2, t_rows, d)
    _depthwise_taps(wu_ref, cb_ref, cvb_ref, K_CONF, HALO_B - K_CONF // 2, t_rows, d)

    y_a = _dot((ba_ref[...] * cva_ref[...]).astype(BF16), wa_ref[...])
    v = cvb_ref[...] + bcb_ref[...]
    mu = jnp.mean(v, axis=-1, keepdims=True)
    vc = v - mu
    var = jnp.mean(vc * vc, axis=-1, keepdims=True)
    v = (vc * lax.rsqrt(var + LN_EPS)) * lng_ref[...] + lnb_ref[...]
    y_b = _dot((v * _sigmoid(v)).astype(BF16), wb_ref[...])
    merged = sga_ref[...] * y_a + sgb_ref[...] * y_b
    mix = _dot(merged.astype(BF16), wo_ref[...])

    gt1 = modp_ref[0, 2:3, :]
    sh2 = modp_ref[0, 3:4, :]
    sc2 = modp_ref[0, 4:5, :]
    x1 = xs_ref[...] + gt1 * mix
    x1_ref[...] = x1
    h2 = (x1 * lax.rsqrt(jnp.mean(x1 * x1, axis=-1, keepdims=True) + RMS_EPS)) * g2_ref[...]
    h2 = h2 * (1.0 + sc2) + sh2
    h2_ref[...] = h2
    h2_hi, h2_lo = _split_bf16(h2)
    logits = _dot(h2_hi, wrh_ref[...]) + _dot(h2_lo, wrh_ref[...]) + _dot(h2_hi, wrl_ref[...]) + br_ref[...]
    route_ref[...] = _route(logits)

    tail_u = wu_ref[pl.ds(t_rows, HALO_B), :]
    wu_ref[pl.ds(0, HALO_B), :] = tail_u * same_seq
    wu_ref[pl.ds(HALO_B, t_rows), :] = u
    tail_cx = wcx_ref[pl.ds(t_rows, HALO_A), :]
    wcx_ref[pl.ds(0, HALO_A), :] = tail_cx * same_seq
    wcx_ref[pl.ds(HALO_A, t_rows), :] = cx
    ba_ref[...] = b_a
    sga_ref[...] = sg_a
    sgb_ref[...] = sg_b
    xs_ref[...] = x


def _mixer_call(x2d, mod3, g1, conv_a, conv_b, b_conv_b, ln_g, ln_b, g2,
                w_in, w_a, w_b, w_o, wr_hi, wr_lo, b_r, *, seq_len):
    n_tok, d = x2d.shape
    t = T_MIX
    n_tiles = seq_len // t
    n_chunks = n_tok // t
    last = n_chunks - 1

    def cur(i):
        return jnp.minimum(i, last)

    def prev(i):
        return jnp.maximum(i - 1, 0)

    row_spec_prev = lambda width: pl.BlockSpec((t, width), lambda i: (prev(i), 0))
    vec = lambda a: _const_spec(a.shape)
    in_specs = [
        pl.BlockSpec((t, d), lambda i: (cur(i), 0)),
        pl.BlockSpec((1,) + mod3.shape[1:], lambda i: (cur(i) // n_tiles, 0, 0)),
        pl.BlockSpec((1,) + mod3.shape[1:], lambda i: (prev(i) // n_tiles, 0, 0)),
        vec(g1), vec(conv_a), vec(conv_b), vec(b_conv_b), vec(ln_g), vec(ln_b), vec(g2),
        vec(w_in), vec(w_a), vec(w_b), vec(w_o), vec(wr_hi), vec(wr_lo), vec(b_r),
    ]
    out_specs = [row_spec_prev(d), row_spec_prev(d), row_spec_prev(ROUTE_LANES)]
    out_shape = [
        jax.ShapeDtypeStruct((n_tok, d), F32),
        jax.ShapeDtypeStruct((n_tok, d), F32),
        jax.ShapeDtypeStruct((n_tok, ROUTE_LANES), F32),
    ]
    scratch = [
        pltpu.VMEM((t + 2 * HALO_B, d), F32),
        pltpu.VMEM((t + 2 * HALO_A, d), F32),
        pltpu.VMEM((t, d), F32), pltpu.VMEM((t, d), F32), pltpu.VMEM((t, d), F32), pltpu.VMEM((t, d), F32),
        pltpu.VMEM((t, d), F32), pltpu.VMEM((t, d), F32),
    ]
    return pl.pallas_call(
        functools.partial(_mixer_kernel, n_tiles=n_tiles),
        grid=(n_chunks + 1,),
        in_specs=in_specs,
        out_specs=out_specs,
        out_shape=out_shape,
        scratch_shapes=scratch,
        compiler_params=pltpu.CompilerParams(
            dimension_semantics=("arbitrary",), vmem_limit_bytes=VMEM_LIMIT_BYTES),
        name="mixer_router",
    )(x2d, mod3, mod3, g1, conv_a, conv_b, b_conv_b, ln_g, ln_b, g2,
      w_in, w_a, w_b, w_o, wr_hi, wr_lo, b_r)


def _dispatch_plan(route, n_tok):
    n_assign = n_tok * TOP_K
    expert = route[:, 0:TOP_K].astype(jnp.int32).reshape(-1)
    weight = route[:, TOP_K:2 * TOP_K].reshape(-1)
    order = jnp.argsort(expert, stable=True)
    e_sorted = expert[order]
    counts = jnp.bincount(expert, length=N_EXPERTS)
    padded = ((counts + ROW_BLK - 1) // ROW_BLK) * ROW_BLK
    start = jnp.cumsum(counts) - counts
    pend = jnp.cumsum(padded)
    pstart = pend - padded
    dest = (pstart[e_sorted] + jnp.arange(n_assign, dtype=jnp.int32) - start[e_sorted]).astype(jnp.int32)
    n_rows = n_assign + N_EXPERTS * ROW_BLK
    n_blocks = n_rows // ROW_BLK
    tok_buf = jnp.zeros((n_rows,), jnp.int32).at[dest].set((order // TOP_K).astype(jnp.int32))
    w_buf = jnp.zeros((n_rows,), F32).at[dest].set(weight[order])
    pos = jnp.zeros((n_assign,), jnp.int32).at[order].set(dest)
    blk_e = jnp.minimum(
        jnp.searchsorted(pend, jnp.arange(n_blocks) * ROW_BLK, side="right"), N_EXPERTS - 1).astype(jnp.int32)
    n_used = (pend[-1] // ROW_BLK).astype(jnp.int32).reshape(1)
    return tok_buf, w_buf, pos, blk_e, n_used


def _row_gather_start(src_hbm, idx_of_row, dst, sem, n_rows):
    def body(r, carry):
        pltpu.make_async_copy(src_hbm.at[pl.ds(idx_of_row(r), 1), :], dst.at[pl.ds(r, 1), :], sem).start()
        return carry
    lax.fori_loop(0, n_rows, body, 0, unroll=8)


def _row_gather_wait(src_hbm, dst, sem, n_rows):
    pltpu.make_async_copy(src_hbm.at[pl.ds(0, n_rows), :], dst, sem).wait()


def _expert_kernel(blk_e_ref, n_used_ref, tokc_ref, tokn_ref, wrow_ref, h2_hbm, wg_ref, wu_ref, wd_ref,
                   y_ref, xbuf, sem):
    del blk_e_ref
    i = pl.program_id(0)
    n_used = n_used_ref[0]
    slot = i % 2

    @pl.when(i == 0)
    def _prime():
        _row_gather_start(h2_hbm, lambda r: tokc_ref[0, 0, r], xbuf.at[0], sem.at[0], ROW_BLK)

    @pl.when(i + 1 < n_used)
    def _prefetch():
        _row_gather_start(h2_hbm, lambda r: tokn_ref[0, 0, r], xbuf.at[1 - slot], sem.at[1 - slot], ROW_BLK)

    @pl.when(i < n_used)
    def _compute():
        _row_gather_wait(h2_hbm, xbuf.at[slot], sem.at[slot], ROW_BLK)
        xb = xbuf[slot].astype(BF16)
        a = _dot(xb, wg_ref[0])
        u = _dot(xb, wu_ref[0])
        hid = ((a * _sigmoid(a)) * u).astype(BF16)
        y_ref[...] = _dot(hid, wd_ref[0]) * wrow_ref[...]

    @pl.when(i >= n_used)
    def _unused_block():
        y_ref[...] = jnp.zeros(y_ref.shape, y_ref.dtype)


def _expert_call(h2, tok_buf, w_buf, blk_e, n_used, w_gate, w_up, w_down):
    n_tok, d = h2.shape
    n_rows = tok_buf.shape[0]
    n_blocks = n_rows // ROW_BLK
    d_e = w_gate.shape[-1]
    tok3 = tok_buf.reshape(n_blocks, 1, ROW_BLK)

    def used(i, n_used_ref):
        return jnp.minimum(i, n_used_ref[0] - 1)

    grid_spec = pltpu.PrefetchScalarGridSpec(
        num_scalar_prefetch=2,
        grid=(n_blocks,),
        in_specs=[
            pl.BlockSpec((1, 1, ROW_BLK), lambda i, be, nu: (used(i, nu), 0, 0), memory_space=pltpu.SMEM),
            pl.BlockSpec((1, 1, ROW_BLK), lambda i, be, nu: (used(i + 1, nu), 0, 0), memory_space=pltpu.SMEM),
            pl.BlockSpec((ROW_BLK, 1), lambda i, be, nu: (used(i, nu), 0)),
            pl.BlockSpec(memory_space=pl.ANY),
            pl.BlockSpec((1, d, d_e), lambda i, be, nu: (be[used(i, nu)], 0, 0)),
            pl.BlockSpec((1, d, d_e), lambda i, be, nu: (be[used(i, nu)], 0, 0)),
            pl.BlockSpec((1, d_e, d), lambda i, be, nu: (be[used(i, nu)], 0, 0)),
        ],
        out_specs=pl.BlockSpec((ROW_BLK, d), lambda i, be, nu: (i, 0)),
        scratch_shapes=[pltpu.VMEM((2, ROW_BLK, d), F32), pltpu.SemaphoreType.DMA((2,))],
    )
    return pl.pallas_call(
        _expert_kernel,
        grid_spec=grid_spec,
        out_shape=jax.ShapeDtypeStruct((n_rows, d), F32),
        compiler_params=pltpu.CompilerParams(
            dimension_semantics=("arbitrary",), vmem_limit_bytes=VMEM_LIMIT_BYTES),
        name="moe_experts",
    )(blk_e, n_used, tok3, tok3, w_buf.reshape(n_rows, 1), h2, w_gate, w_up, w_down)


def _combine_kernel(posc_ref, posn_ref, x1_ref, mod_ref, gf_ref, y_hbm, o_ref, ybuf, sem, *, n_steps, final_norm):
    i = pl.program_id(0)
    slot = i % 2
    t_rows = x1_ref.shape[0]

    def start(pos_ref, s):
        for k in range(TOP_K):
            _row_gather_start(y_hbm, lambda r, k=k: pos_ref[0, k, r], ybuf.at[s, k], sem.at[s], t_rows)

    @pl.when(i == 0)
    def _prime():
        start(posc_ref, 0)

    @pl.when(i + 1 < n_steps)
    def _prefetch():
        start(posn_ref, 1 - slot)

    for k in range(TOP_K):
        _row_gather_wait(y_hbm, ybuf.at[slot, k], sem.at[slot], t_rows)
    moe = ybuf[slot, 0] + ybuf[slot, 1]
    x2 = x1_ref[...] + mod_ref[0, 5:6, :] * moe
    if final_norm:
        x2 = (x2 * lax.rsqrt(jnp.mean(x2 * x2, axis=-1, keepdims=True) + RMS_EPS)) * gf_ref[...]
    o_ref[...] = x2


def _combine_call(x1, mod3, g_final, y_rows, pos, *, seq_len, final_norm):
    n_tok, d = x1.shape
    t = T_OUT
    n_steps = n_tok // t
    tiles_per_seq = seq_len // t
    pos3 = pos.reshape(n_steps, t, TOP_K).transpose(0, 2, 1)
    last = n_steps - 1
    return pl.pallas_call(
        functools.partial(_combine_kernel, n_steps=n_steps, final_norm=final_norm),
        grid=(n_steps,),
        in_specs=[
            pl.BlockSpec((1, TOP_K, t), lambda i: (i, 0, 0), memory_space=pltpu.SMEM),
            pl.BlockSpec((1, TOP_K, t), lambda i: (jnp.minimum(i + 1, last), 0, 0), memory_space=pltpu.SMEM),
            pl.BlockSpec((t, d), lambda i: (i, 0)),
            pl.BlockSpec((1,) + mod3.shape[1:], lambda i: (i // tiles_per_seq, 0, 0)),
            pl.BlockSpec((1, d), lambda i: (0, 0)),
            pl.BlockSpec(memory_space=pl.ANY),
        ],
        out_specs=pl.BlockSpec((t, d), lambda i: (i, 0)),
        out_shape=jax.ShapeDtypeStruct((n_tok, d), F32),
        scratch_shapes=[pltpu.VMEM((2, TOP_K, t, d), F32), pltpu.SemaphoreType.DMA((2,))],
        compiler_params=pltpu.CompilerParams(
            dimension_semantics=("arbitrary",), vmem_limit_bytes=VMEM_LIMIT_BYTES),
        name="moe_combine",
    )(pos3, pos3, x1, mod3, g_final, y_rows)


def kernel(x, c, w_ada, b_ada, g_norm1, w_in, conv_a, w_a_out, conv_b, b_conv_b, ln_conv_g, ln_conv_b,
           w_b_out, w_o, g_norm2, w_router_g, b_router_g, w_router_e, b_router_e, w_gate, w_up, w_down,
           g_final):
    bsz, seq_len, d = x.shape
    depth = w_ada.shape[0]
    n_tok = bsz * seq_len
    assert seq_len % T_MIX == 0 and seq_len % T_OUT == 0 and d % LANES == 0
    assert (n_tok * TOP_K) % ROW_BLK == 0
    assert N_GROUPS + N_EXPERTS <= ROUTE_LANES

    c_pad = jnp.zeros((SUBLANES, d), F32).at[:bsz].set(c)
    xt = x.reshape(n_tok, d)
    row = lambda a: a.reshape(1, -1)
    for l in range(depth):
        mod = _mod_call(c_pad, w_ada[l], row(b_ada[l]))
        mod3 = mod[:bsz].reshape(bsz, 6, d)
        w_r = jnp.zeros((d, ROUTE_LANES), F32)
        w_r = w_r.at[:, :N_GROUPS].set(w_router_g[l]).at[:, N_GROUPS:N_GROUPS + N_EXPERTS].set(w_router_e[l])
        b_r = jnp.zeros((1, ROUTE_LANES), F32)
        b_r = b_r.at[0, :N_GROUPS].set(b_router_g[l]).at[0, N_GROUPS:N_GROUPS + N_EXPERTS].set(b_router_e[l])
        wr_hi, wr_lo = _split_bf16(w_r)
        x1, h2, route = _mixer_call(
            xt, mod3, row(g_norm1[l]), conv_a[l], conv_b[l], row(b_conv_b[l]), row(ln_conv_g[l]),
            row(ln_conv_b[l]), row(g_norm2[l]),
            w_in[l].astype(BF16), w_a_out[l].astype(BF16), w_b_out[l].astype(BF16), w_o[l].astype(BF16),
            wr_hi, wr_lo, b_r, seq_len=seq_len)
        tok_buf, w_buf, pos, blk_e, n_used = _dispatch_plan(route, n_tok)
        y_rows = _expert_call(h2, tok_buf, w_buf, blk_e, n_used,
                              w_gate[l].astype(BF16), w_up[l].astype(BF16), w_down[l].astype(BF16))
        xt = _combine_call(x1, mod3, row(g_final), y_rows, pos, seq_len=seq_len,
                           final_norm=(l == depth - 1))
    return xt.reshape(bsz, seq_len, d)
```

```python
import functools

import jax
import jax.numpy as jnp
from jax import lax
from jax.experimental import pallas as pl
from jax.experimental.pallas import tpu as pltpu

F32 = jnp.float32
BF16 = jnp.bfloat16
U32 = jnp.uint32

N_GROUPS = 4
EXPERTS_PER_GROUP = 8
N_EXPERTS = N_GROUPS * EXPERTS_PER_GROUP
TOP_K = 2
K_SHORT = 3
K_CONF = 31
RMS_EPS = 1e-6
LN_EPS = 1e-5

LANES = 128
SUBLANES = 8
T_MIX = 256
HALO_B = 16
HALO_A = 8
CONV_ROWS = 64
ROW_BLK = 256
T_MOVE = 256
ROUTE_LANES = LANES
REC_E, REC_W, REC_RANK = 0, 2, 4
VMEM_LIMIT_BYTES = 56 * 1024 * 1024


def _sigmoid(v):
    return 1.0 / (1.0 + jnp.exp(-v))


def _split_bf16(v):
    hi = v.astype(BF16)
    lo = (v - hi.astype(F32)).astype(BF16)
    return hi, lo


def _dot(a, b):
    return jnp.dot(a, b, preferred_element_type=F32)


def _const_spec(shape):
    nd = len(shape)
    return pl.BlockSpec(shape, lambda *_: (0,) * nd, pipeline_mode=pl.Buffered(1))


def _lane_ids(shape):
    return lax.broadcasted_iota(jnp.int32, shape, 1).astype(F32)


def _mod_kernel(c_ref, w_ref, b_ref, o_ref):
    c = c_ref[...]
    a_hi, a_lo = _split_bf16(c * _sigmoid(c))
    w_hi, w_lo = _split_bf16(w_ref[...])
    o_ref[...] = _dot(a_hi, w_hi) + _dot(a_lo, w_hi) + _dot(a_hi, w_lo) + b_ref[...]


def _mod_call(c_pad, w_ada, b_ada):
    rows, d = c_pad.shape
    n_out = w_ada.shape[1]
    blk = 1024
    return pl.pallas_call(
        _mod_kernel,
        grid=(n_out // blk,),
        in_specs=[
            pl.BlockSpec((rows, d), lambda j: (0, 0)),
            pl.BlockSpec((d, blk), lambda j: (0, j)),
            pl.BlockSpec((1, blk), lambda j: (0, j)),
        ],
        out_specs=pl.BlockSpec((rows, blk), lambda j: (0, j)),
        out_shape=jax.ShapeDtypeStruct((rows, n_out), F32),
        compiler_params=pltpu.CompilerParams(dimension_semantics=("arbitrary",)),
        name="adaln_mod",
    )(c_pad, w_ada, b_ada)


def _depthwise_taps(buf_ref, w_ref, out_ref, n_taps, row0, n_rows, d):
    for cb in range(d // LANES):
        cols = slice(cb * LANES, (cb + 1) * LANES)
        for rb in range(n_rows // CONV_ROWS):
            acc = None
            for k in range(n_taps):
                term = buf_ref[pl.ds(row0 + rb * CONV_ROWS + k, CONV_ROWS), cols] * w_ref[k:k + 1, cols]
                acc = term if acc is None else acc + term
            out_ref[pl.ds(rb * CONV_ROWS, CONV_ROWS), cols] = acc


def _route(logits, run_ref, live):
    shape = logits.shape
    lane = _lane_ids(shape)
    big = float(4 * ROUTE_LANES)
    neg_inf = -jnp.inf

    def first_argmax(v):
        m = jnp.max(v, axis=-1, keepdims=True)
        return m, jnp.min(jnp.where(v == m, lane, big), axis=-1, keepdims=True)

    is_group = lane < N_GROUPS
    g_max, g_idx = first_argmax(jnp.where(is_group, logits, neg_inf))
    g_prob = 1.0 / jnp.sum(jnp.where(is_group, jnp.exp(logits - g_max), 0.0), axis=-1, keepdims=True)
    lo = N_GROUPS + EXPERTS_PER_GROUP * g_idx
    in_group = (lane >= lo) & (lane < lo + EXPERTS_PER_GROUP)
    e_logits = jnp.where(in_group, logits, neg_inf)
    m1, i1 = first_argmax(e_logits)
    m2, i2 = first_argmax(jnp.where(lane == i1, neg_inf, e_logits))
    t = jnp.exp(m2 - m1)
    p1 = 1.0 / (1.0 + t)
    p2 = t / (1.0 + t)
    e1 = i1 - N_GROUPS
    e2 = i2 - N_GROUPS

    hot1 = lane == e1
    hot2 = lane == e2
    both = jnp.where(hot1 | hot2, 1.0, 0.0)
    n_rows = shape[0]
    earlier = (lax.broadcasted_iota(jnp.int32, (n_rows, n_rows), 0)
               > lax.broadcasted_iota(jnp.int32, (n_rows, n_rows), 1))
    before = _dot(jnp.where(earlier, 1.0, 0.0).astype(BF16), both.astype(BF16)) + run_ref[0:1, :]
    rank1 = jnp.sum(jnp.where(hot1, before, 0.0), axis=-1, keepdims=True)
    rank2 = jnp.sum(jnp.where(hot2, before, 0.0), axis=-1, keepdims=True)
    run_ref[0:1, :] = run_ref[0:1, :] + live * jnp.sum(both, axis=0, keepdims=True)

    rec = jnp.where(lane == REC_E, e1, 0.0)
    rec = jnp.where(lane == REC_E + 1, e2, rec)
    rec = jnp.where(lane == REC_W, g_prob * p1, rec)
    rec = jnp.where(lane == REC_W + 1, g_prob * p2, rec)
    rec = jnp.where(lane == REC_RANK, rank1, rec)
    rec = jnp.where(lane == REC_RANK + 1, rank2, rec)
    return rec


def _pack_bf16_pairs(v_bf16):
    half = v_bf16.shape[1] // 2
    lo = lax.bitcast_convert_type(v_bf16[:, :half].astype(F32), U32)
    hi = lax.bitcast_convert_type(v_bf16[:, half:].astype(F32), U32)
    return (lo >> 16) | hi


def _unpack_bf16_pairs(p_u32):
    lo = lax.bitcast_convert_type(p_u32 << 16, F32).astype(BF16)
    hi = lax.bitcast_convert_type(p_u32 & jnp.uint32(0xFFFF0000), F32).astype(BF16)
    return jnp.concatenate([lo, hi], axis=1)


def _mixer_kernel(x_ref, modn_ref, modp_ref, g1_ref, ca_ref, cb_ref, bcb_ref, lng_ref, lnb_ref, g2_ref,
                  win_ref, wa_ref, wb_ref, wo_ref, wrh_ref, wrl_ref, br_ref,
                  x1_ref, h2p_ref, route_ref, cnt_ref,
                  wu_ref, wcx_ref, ba_ref, sga_ref, sgb_ref, xs_ref, cva_ref, cvb_ref, run_ref, *, n_tiles):
    i = pl.program_id(0)
    t_rows, d = x_ref.shape

    @pl.when(i == 0)
    def _init():
        for ref in (wu_ref, wcx_ref, ba_ref, sga_ref, sgb_ref, xs_ref, run_ref):
            ref[...] = jnp.zeros(ref.shape, ref.dtype)

    x = x_ref[...]
    sh1 = modn_ref[0, 0:1, :]
    sc1 = modn_ref[0, 1:2, :]
    h = (x * lax.rsqrt(jnp.mean(x * x, axis=-1, keepdims=True) + RMS_EPS)) * g1_ref[...]
    hb = (h * (1.0 + sc1) + sh1).astype(BF16)

    def proj(g):
        return _dot(hb, win_ref[:, g * d:(g + 1) * d])

    b_a = proj(0)
    cx = proj(1) * proj(2)
    u = proj(3) * _sigmoid(proj(4))
    sg_a = _sigmoid(proj(5))
    sg_b = _sigmoid(proj(6))

    same_seq = ((i % n_tiles) != 0).astype(F32)
    wu_ref[pl.ds(HALO_B + t_rows, HALO_B), :] = u[0:HALO_B] * same_seq
    wcx_ref[pl.ds(HALO_A + t_rows, HALO_A), :] = cx[0:HALO_A] * same_seq

    _depthwise_taps(wcx_ref, ca_ref, cva_ref, K_SHORT, HALO_A - K_SHORT // 2, t_rows, d)
    _depthwise_taps(wu_ref, cb_ref, cvb_ref, K_CONF, HALO_B - K_CONF // 2, t_rows, d)

    y_a = _dot((ba_ref[...] * cva_ref[...]).astype(BF16), wa_ref[...])
    v = cvb_ref[...] + bcb_ref[...]
    mu = jnp.mean(v, axis=-1, keepdims=True)
    vc = v - mu
    var = jnp.mean(vc * vc, axis=-1, keepdims=True)
    v = (vc * lax.rsqrt(var + LN_EPS)) * lng_ref[...] + lnb_ref[...]
    y_b = _dot((v * _sigmoid(v)).astype(BF16), wb_ref[...])
    merged = sga_ref[...] * y_a + sgb_ref[...] * y_b
    mix = _dot(merged.astype(BF16), wo_ref[...])

    gt1 = modp_ref[0, 2:3, :]
    sh2 = modp_ref[0, 3:4, :]
    sc2 = modp_ref[0, 4:5, :]
    x1 = xs_ref[...] + gt1 * mix
    x1_ref[...] = x1
    h2 = (x1 * lax.rsqrt(jnp.mean(x1 * x1, axis=-1, keepdims=True) + RMS_EPS)) * g2_ref[...]
    h2 = h2 * (1.0 + sc2) + sh2
    h2_hi, h2_lo = _split_bf16(h2)
    h2p_ref[...] = _pack_bf16_pairs(h2_hi)
    logits = _dot(h2_hi, wrh_ref[...]) + _dot(h2_lo, wrh_ref[...]) + _dot(h2_hi, wrl_ref[...]) + br_ref[...]
    route_ref[...] = _route(logits, run_ref, (i > 0).astype(F32))
    cnt_ref[...] = jnp.broadcast_to(run_ref[0:1, :], cnt_ref.shape)

    tail_u = wu_ref[pl.ds(t_rows, HALO_B), :]
    wu_ref[pl.ds(0, HALO_B), :] = tail_u * same_seq
    wu_ref[pl.ds(HALO_B, t_rows), :] = u
    tail_cx = wcx_ref[pl.ds(t_rows, HALO_A), :]
    wcx_ref[pl.ds(0, HALO_A), :] = tail_cx * same_seq
    wcx_ref[pl.ds(HALO_A, t_rows), :] = cx
    ba_ref[...] = b_a
    sga_ref[...] = sg_a
    sgb_ref[...] = sg_b
    xs_ref[...] = x


def _mixer_call(x2d, mod3, g1, conv_a, conv_b, b_conv_b, ln_g, ln_b, g2,
                w_in, w_a, w_b, w_o, wr_hi, wr_lo, b_r, *, seq_len):
    n_tok, d = x2d.shape
    t = T_MIX
    n_tiles = seq_len // t
    n_chunks = n_tok // t
    last = n_chunks - 1

    def cur(i):
        return jnp.minimum(i, last)

    def prev(i):
        return jnp.maximum(i - 1, 0)

    row_spec_prev = lambda width: pl.BlockSpec((t, width), lambda i: (prev(i), 0))
    vec = lambda a: _const_spec(a.shape)
    in_specs = [
        pl.BlockSpec((t, d), lambda i: (cur(i), 0)),
        pl.BlockSpec((1,) + mod3.shape[1:], lambda i: (cur(i) // n_tiles, 0, 0)),
        pl.BlockSpec((1,) + mod3.shape[1:], lambda i: (prev(i) // n_tiles, 0, 0)),
        vec(g1), vec(conv_a), vec(conv_b), vec(b_conv_b), vec(ln_g), vec(ln_b), vec(g2),
        vec(w_in), vec(w_a), vec(w_b), vec(w_o), vec(wr_hi), vec(wr_lo), vec(b_r),
    ]
    out_specs = [row_spec_prev(d), row_spec_prev(d // 2), row_spec_prev(ROUTE_LANES),
                 pl.BlockSpec((SUBLANES, ROUTE_LANES), lambda i: (0, 0))]
    out_shape = [
        jax.ShapeDtypeStruct((n_tok, d), F32),
        jax.ShapeDtypeStruct((n_tok, d // 2), U32),
        jax.ShapeDtypeStruct((n_tok, ROUTE_LANES), F32),
        jax.ShapeDtypeStruct((SUBLANES, ROUTE_LANES), F32),
    ]
    scratch = [
        pltpu.VMEM((t + 2 * HALO_B, d), F32),
        pltpu.VMEM((t + 2 * HALO_A, d), F32),
        pltpu.VMEM((t, d), F32), pltpu.VMEM((t, d), F32), pltpu.VMEM((t, d), F32), pltpu.VMEM((t, d), F32),
        pltpu.VMEM((t, d), F32), pltpu.VMEM((t, d), F32),
        pltpu.VMEM((SUBLANES, ROUTE_LANES), F32),
    ]
    return pl.pallas_call(
        functools.partial(_mixer_kernel, n_tiles=n_tiles),
        grid=(n_chunks + 1,),
        in_specs=in_specs,
        out_specs=out_specs,
        out_shape=out_shape,
        scratch_shapes=scratch,
        compiler_params=pltpu.CompilerParams(
            dimension_semantics=("arbitrary",), vmem_limit_bytes=VMEM_LIMIT_BYTES),
        name="mixer_router",
    )(x2d, mod3, mod3, g1, conv_a, conv_b, b_conv_b, ln_g, ln_b, g2,
      w_in, w_a, w_b, w_o, wr_hi, wr_lo, b_r)


def _segment_layout(counts_row, n_blocks):
    counts = counts_row[:N_EXPERTS].astype(jnp.int32)
    padded = ((counts + ROW_BLK - 1) // ROW_BLK) * ROW_BLK
    pend = jnp.cumsum(padded)
    pstart = pend - padded
    blk_row0 = jnp.arange(n_blocks, dtype=jnp.int32) * ROW_BLK
    blk_e = jnp.minimum(jnp.sum(pend[None, :] <= blk_row0[:, None], axis=1), N_EXPERTS - 1).astype(jnp.int32)
    n_used = (pend[-1] // ROW_BLK).astype(jnp.int32).reshape(1)
    pstart_row = jnp.zeros((1, ROUTE_LANES), F32).at[0, :N_EXPERTS].set(pstart.astype(F32))
    return pstart_row, blk_e, n_used


def _plan_kernel(route_ref, pstart_ref, pos_ref):
    rec = route_ref[...]
    lane = _lane_ids(rec.shape)
    pos = jnp.zeros(rec.shape, F32)
    for k in range(TOP_K):
        e = rec[:, REC_E + k:REC_E + k + 1]
        seg = jnp.sum(jnp.where(lane == e, pstart_ref[...], 0.0), axis=-1, keepdims=True)
        pos = jnp.where(lane == k, seg + rec[:, REC_RANK + k:REC_RANK + k + 1], pos)
    pos_ref[...] = pos.astype(jnp.int32)


def _plan_call(route, pstart_row):
    n_tok = route.shape[0]
    t = T_MOVE
    return pl.pallas_call(
        _plan_kernel,
        grid=(n_tok // t,),
        in_specs=[pl.BlockSpec((t, ROUTE_LANES), lambda i: (i, 0)),
                  pl.BlockSpec((1, ROUTE_LANES), lambda i: (0, 0))],
        out_specs=pl.BlockSpec((t, ROUTE_LANES), lambda i: (i, 0)),
        out_shape=jax.ShapeDtypeStruct((n_tok, ROUTE_LANES), jnp.int32),
        compiler_params=pltpu.CompilerParams(dimension_semantics=("arbitrary",)),
        name="moe_plan",
    )(route, pstart_row)


def _scatter_kernel(pos_ref, rows_ref, buf_in_hbm, buf_hbm, sem):
    del buf_in_hbm
    t_rows = rows_ref.shape[0]

    def body(r, carry):
        for k in range(TOP_K):
            pltpu.make_async_copy(rows_ref.at[pl.ds(r, 1), :],
                                  buf_hbm.at[pl.ds(pos_ref[0, k, r], 1), :], sem).start()
        return carry

    lax.fori_loop(0, t_rows, body, 0, unroll=8)
    for _ in range(TOP_K):
        pltpu.make_async_copy(rows_ref, buf_hbm.at[pl.ds(0, t_rows), :], sem).wait()


def _scatter_call(pos3, rows, n_rows):
    n_tok, width = rows.shape
    t = T_MOVE
    buf0 = jnp.zeros((n_rows, width), rows.dtype)
    return pl.pallas_call(
        _scatter_kernel,
        grid=(n_tok // t,),
        in_specs=[
            pl.BlockSpec((1, TOP_K, t), lambda i: (i, 0, 0), memory_space=pltpu.SMEM),
            pl.BlockSpec((t, width), lambda i: (i, 0)),
            pl.BlockSpec(memory_space=pl.ANY),
        ],
        out_specs=pl.BlockSpec(memory_space=pl.ANY),
        out_shape=jax.ShapeDtypeStruct((n_rows, width), rows.dtype),
        scratch_shapes=[pltpu.SemaphoreType.DMA(())],
        input_output_aliases={2: 0},
        compiler_params=pltpu.CompilerParams(dimension_semantics=("arbitrary",)),
        name="moe_scatter",
    )(pos3, rows, buf0)


def _expert_kernel(blk_e_ref, n_used_ref, xs_ref, wg_ref, wu_ref, wd_ref, y_ref):
    del blk_e_ref
    i = pl.program_id(0)
    n_used = n_used_ref[0]

    @pl.when(i < n_used)
    def _compute():
        xb = _unpack_bf16_pairs(xs_ref[...])
        a = _dot(xb, wg_ref[0])
        u = _dot(xb, wu_ref[0])
        hid = ((a * _sigmoid(a)) * u).astype(BF16)
        y_ref[...] = _dot(hid, wd_ref[0])

    @pl.when(i >= n_used)
    def _unused_block():
        y_ref[...] = jnp.zeros(y_ref.shape, y_ref.dtype)


def _expert_call(xs, blk_e, n_used, w_gate, w_up, w_down):
    n_rows, half = xs.shape
    d = 2 * half
    n_blocks = n_rows // ROW_BLK
    d_e = w_gate.shape[-1]

    def used(i, n_used_ref):
        return jnp.minimum(i, n_used_ref[0] - 1)

    grid_spec = pltpu.PrefetchScalarGridSpec(
        num_scalar_prefetch=2,
        grid=(n_blocks,),
        in_specs=[
            pl.BlockSpec((ROW_BLK, half), lambda i, be, nu: (used(i, nu), 0)),
            pl.BlockSpec((1, d, d_e), lambda i, be, nu: (be[used(i, nu)], 0, 0)),
            pl.BlockSpec((1, d, d_e), lambda i, be, nu: (be[used(i, nu)], 0, 0)),
            pl.BlockSpec((1, d_e, d), lambda i, be, nu: (be[used(i, nu)], 0, 0)),
        ],
        out_specs=pl.BlockSpec((ROW_BLK, d), lambda i, be, nu: (i, 0)),
    )
    return pl.pallas_call(
        _expert_kernel,
        grid_spec=grid_spec,
        out_shape=jax.ShapeDtypeStruct((n_rows, d), F32),
        compiler_params=pltpu.CompilerParams(
            dimension_semantics=("arbitrary",), vmem_limit_bytes=VMEM_LIMIT_BYTES),
        name="moe_experts",
    )(blk_e, n_used, xs, w_gate, w_up, w_down)


def _row_gather_start(src_hbm, idx_of_row, dst, sem, n_rows):
    def body(r, carry):
        pltpu.make_async_copy(src_hbm.at[pl.ds(idx_of_row(r), 1), :], dst.at[pl.ds(r, 1), :], sem).start()
        return carry
    lax.fori_loop(0, n_rows, body, 0, unroll=8)


def _row_gather_wait(src_hbm, dst, sem, n_rows):
    pltpu.make_async_copy(src_hbm.at[pl.ds(0, n_rows), :], dst, sem).wait()


def _combine_kernel(posc_ref, posn_ref, x1_ref, route_ref, mod_ref, gf_ref, y_hbm, o_ref, ybuf, sem,
                    *, n_steps, final_norm):
    i = pl.program_id(0)
    slot = i % 2
    t_rows = x1_ref.shape[0]

    def start(pos_ref, s):
        for k in range(TOP_K):
            _row_gather_start(y_hbm, lambda r, k=k: pos_ref[0, k, r], ybuf.at[s, k], sem.at[s], t_rows)

    @pl.when(i == 0)
    def _prime():
        start(posc_ref, 0)

    @pl.when(i + 1 < n_steps)
    def _prefetch():
        start(posn_ref, 1 - slot)

    for k in range(TOP_K):
        _row_gather_wait(y_hbm, ybuf.at[slot, k], sem.at[slot], t_rows)
    rec = route_ref[...]
    moe = ybuf[slot, 0] * rec[:, REC_W:REC_W + 1] + ybuf[slot, 1] * rec[:, REC_W + 1:REC_W + 2]
    x2 = x1_ref[...] + mod_ref[0, 5:6, :] * moe
    if final_norm:
        x2 = (x2 * lax.rsqrt(jnp.mean(x2 * x2, axis=-1, keepdims=True) + RMS_EPS)) * gf_ref[...]
    o_ref[...] = x2


def _combine_call(x1, route, mod3, g_final, y_rows, pos3, *, seq_len, final_norm):
    n_tok, d = x1.shape
    t = T_MOVE
    n_steps = n_tok // t
    tiles_per_seq = seq_len // t
    last = n_steps - 1
    return pl.pallas_call(
        functools.partial(_combine_kernel, n_steps=n_steps, final_norm=final_norm),
        grid=(n_steps,),
        in_specs=[
            pl.BlockSpec((1, TOP_K, t), lambda i: (i, 0, 0), memory_space=pltpu.SMEM),
            pl.BlockSpec((1, TOP_K, t), lambda i: (jnp.minimum(i + 1, last), 0, 0), memory_space=pltpu.SMEM),
            pl.BlockSpec((t, d), lambda i: (i, 0)),
            pl.BlockSpec((t, ROUTE_LANES), lambda i: (i, 0)),
            pl.BlockSpec((1,) + mod3.shape[1:], lambda i: (i // tiles_per_seq, 0, 0)),
            pl.BlockSpec((1, d), lambda i: (0, 0)),
            pl.BlockSpec(memory_space=pl.ANY),
        ],
        out_specs=pl.BlockSpec((t, d), lambda i: (i, 0)),
        out_shape=jax.ShapeDtypeStruct((n_tok, d), F32),
        scratch_shapes=[pltpu.VMEM((2, TOP_K, t, d), F32), pltpu.SemaphoreType.DMA((2,))],
        compiler_params=pltpu.CompilerParams(
            dimension_semantics=("arbitrary",), vmem_limit_bytes=VMEM_LIMIT_BYTES),
        name="moe_combine",
    )(pos3, pos3, x1, route, mod3, g_final, y_rows)


def kernel(x, c, w_ada, b_ada, g_norm1, w_in, conv_a, w_a_out, conv_b, b_conv_b, ln_conv_g, ln_conv_b,
           w_b_out, w_o, g_norm2, w_router_g, b_router_g, w_router_e, b_router_e, w_gate, w_up, w_down,
           g_final):
    bsz, seq_len, d = x.shape
    depth = w_ada.shape[0]
    n_tok = bsz * seq_len
    n_assign = n_tok * TOP_K
    assert seq_len % T_MIX == 0 and seq_len % T_MOVE == 0 and d % (2 * LANES) == 0
    assert n_assign % ROW_BLK == 0
    assert N_GROUPS + N_EXPERTS <= ROUTE_LANES
    n_rows = n_assign + N_EXPERTS * ROW_BLK
    n_blocks = n_rows // ROW_BLK

    c_pad = jnp.zeros((SUBLANES, d), F32).at[:bsz].set(c)
    xt = x.reshape(n_tok, d)
    row = lambda a: a.reshape(1, -1)
    for l in range(depth):
        mod = _mod_call(c_pad, w_ada[l], row(b_ada[l]))
        mod3 = mod[:bsz].reshape(bsz, 6, d)
        w_r = jnp.zeros((d, ROUTE_LANES), F32)
        w_r = w_r.at[:, :N_GROUPS].set(w_router_g[l]).at[:, N_GROUPS:N_GROUPS + N_EXPERTS].set(w_router_e[l])
        b_r = jnp.zeros((1, ROUTE_LANES), F32)
        b_r = b_r.at[0, :N_GROUPS].set(b_router_g[l]).at[0, N_GROUPS:N_GROUPS + N_EXPERTS].set(b_router_e[l])
        wr_hi, wr_lo = _split_bf16(w_r)
        x1, h2p, route, cnt = _mixer_call(
            xt, mod3, row(g_norm1[l]), conv_a[l], conv_b[l], row(b_conv_b[l]), row(ln_conv_g[l]),
            row(ln_conv_b[l]), row(g_norm2[l]),
            w_in[l].astype(BF16), w_a_out[l].astype(BF16), w_b_out[l].astype(BF16), w_o[l].astype(BF16),
            wr_hi, wr_lo, b_r, seq_len=seq_len)
        pstart_row, blk_e, n_used = _segment_layout(cnt[0], n_blocks)
        pos = _plan_call(route, pstart_row)
        pos3 = pos[:, :TOP_K].reshape(n_tok // T_MOVE, T_MOVE, TOP_K).transpose(0, 2, 1)
        xs = _scatter_call(pos3, h2p, n_rows)
        y_rows = _expert_call(xs, blk_e, n_used,
                              w_gate[l].astype(BF16), w_up[l].astype(BF16), w_down[l].astype(BF16))
        xt = _combine_call(x1, route, mod3, row(g_final), y_rows, pos3, seq_len=seq_len,
                           final_norm=(l == depth - 1))
    return xt.reshape(bsz, seq_len, d)
```

```python
import functools

import jax
import jax.numpy as jnp
from jax import lax
from jax.experimental import pallas as pl
from jax.experimental.pallas import tpu as pltpu

F32 = jnp.float32
BF16 = jnp.bfloat16
U32 = jnp.uint32

N_GROUPS = 4
EXPERTS_PER_GROUP = 8
N_EXPERTS = N_GROUPS * EXPERTS_PER_GROUP
TOP_K = 2
K_SHORT = 3
K_CONF = 31
RMS_EPS = 1e-6
LN_EPS = 1e-5

LANES = 128
SUBLANES = 8
T_MIX = 256
HALO_B = 16
HALO_A = 8
CONV_ROWS = 64
ROW_BLK = 256
T_MOVE = 256
T_PLAN = 2048
ROUTE_LANES = LANES
REC_E, REC_W, REC_RANK = 0, 2, 4
VMEM_LIMIT_BYTES = 56 * 1024 * 1024


def _sigmoid(v):
    return 1.0 / (1.0 + jnp.exp(-v))


def _split_bf16(v):
    hi = v.astype(BF16)
    lo = (v - hi.astype(F32)).astype(BF16)
    return hi, lo


def _dot(a, b):
    return jnp.dot(a, b, preferred_element_type=F32)


def _const_spec(shape):
    nd = len(shape)
    return pl.BlockSpec(shape, lambda *_: (0,) * nd, pipeline_mode=pl.Buffered(1))


def _lane_ids(shape):
    return lax.broadcasted_iota(jnp.int32, shape, 1).astype(F32)


def _mod_kernel(c_ref, w_ref, b_ref, o_ref):
    c = c_ref[...]
    a_hi, a_lo = _split_bf16(c * _sigmoid(c))
    w_hi, w_lo = _split_bf16(w_ref[...])
    o_ref[...] = _dot(a_hi, w_hi) + _dot(a_lo, w_hi) + _dot(a_hi, w_lo) + b_ref[...]


def _mod_call(c_pad, w_ada, b_ada):
    rows, d = c_pad.shape
    n_out = w_ada.shape[1]
    blk = 1024
    return pl.pallas_call(
        _mod_kernel,
        grid=(n_out // blk,),
        in_specs=[
            pl.BlockSpec((rows, d), lambda j: (0, 0)),
            pl.BlockSpec((d, blk), lambda j: (0, j)),
            pl.BlockSpec((1, blk), lambda j: (0, j)),
        ],
        out_specs=pl.BlockSpec((rows, blk), lambda j: (0, j)),
        out_shape=jax.ShapeDtypeStruct((rows, n_out), F32),
        compiler_params=pltpu.CompilerParams(dimension_semantics=("arbitrary",)),
        name="adaln_mod",
    )(c_pad, w_ada, b_ada)


def _depthwise_lane_block(buf_ref, w_ref, out_ref, phase_ref, n_taps, row0, rows, cb):
    first, stop = rows
    max_off = row0 + n_taps - 1
    n_keep = stop - first + (max_off // SUBLANES) * SUBLANES
    cols = slice(cb * LANES, (cb + 1) * LANES)
    for s in sorted({(row0 + k) % SUBLANES for k in range(n_taps)}):
        phase_ref[s, pl.ds(0, n_keep), :] = buf_ref[pl.ds(first + s, n_keep), cols]
    row_blocks = range(first, stop, CONV_ROWS)
    acc = [None] * len(row_blocks)
    for k in range(n_taps):
        q, s = divmod(row0 + k, SUBLANES)
        w_k = jnp.broadcast_to(w_ref[k:k + 1, cols], (CONV_ROWS, LANES))
        for j, r0 in enumerate(row_blocks):
            term = phase_ref[s, pl.ds(r0 - first + q * SUBLANES, CONV_ROWS), :] * w_k
            acc[j] = term if acc[j] is None else acc[j] + term
    for j, r0 in enumerate(row_blocks):
        out_ref[pl.ds(r0, CONV_ROWS), cols] = acc[j]


def _route(logits, run_ref, live):
    shape = logits.shape
    lane = _lane_ids(shape)
    big = float(4 * ROUTE_LANES)
    neg_inf = -jnp.inf

    def first_argmax(v):
        m = jnp.max(v, axis=-1, keepdims=True)
        return m, jnp.min(jnp.where(v == m, lane, big), axis=-1, keepdims=True)

    is_group = lane < N_GROUPS
    g_max, g_idx = first_argmax(jnp.where(is_group, logits, neg_inf))
    g_prob = 1.0 / jnp.sum(jnp.where(is_group, jnp.exp(logits - g_max), 0.0), axis=-1, keepdims=True)
    lo = N_GROUPS + EXPERTS_PER_GROUP * g_idx
    in_group = (lane >= lo) & (lane < lo + EXPERTS_PER_GROUP)
    e_logits = jnp.where(in_group, logits, neg_inf)
    m1, i1 = first_argmax(e_logits)
    m2, i2 = first_argmax(jnp.where(lane == i1, neg_inf, e_logits))
    t = jnp.exp(m2 - m1)
    p1 = 1.0 / (1.0 + t)
    p2 = t / (1.0 + t)
    e1 = i1 - N_GROUPS
    e2 = i2 - N_GROUPS

    hot1 = lane == e1
    hot2 = lane == e2
    both = jnp.where(hot1 | hot2, 1.0, 0.0)
    n_rows = shape[0]
    earlier = (lax.broadcasted_iota(jnp.int32, (n_rows, n_rows), 0)
               > lax.broadcasted_iota(jnp.int32, (n_rows, n_rows), 1))
    before = _dot(jnp.where(earlier, 1.0, 0.0).astype(BF16), both.astype(BF16)) + run_ref[0:1, :]
    rank1 = jnp.sum(jnp.where(hot1, before, 0.0), axis=-1, keepdims=True)
    rank2 = jnp.sum(jnp.where(hot2, before, 0.0), axis=-1, keepdims=True)
    run_ref[0:1, :] = run_ref[0:1, :] + live * jnp.sum(both, axis=0, keepdims=True)

    rec = jnp.where(lane == REC_E, e1, 0.0)
    rec = jnp.where(lane == REC_E + 1, e2, rec)
    rec = jnp.where(lane == REC_W, g_prob * p1, rec)
    rec = jnp.where(lane == REC_W + 1, g_prob * p2, rec)
    rec = jnp.where(lane == REC_RANK, rank1, rec)
    rec = jnp.where(lane == REC_RANK + 1, rank2, rec)
    return rec


def _pack_bf16_pairs(v_bf16):
    half = v_bf16.shape[1] // 2
    lo = lax.bitcast_convert_type(v_bf16[:, :half].astype(F32), U32)
    hi = lax.bitcast_convert_type(v_bf16[:, half:].astype(F32), U32)
    return (lo >> 16) | hi


def _unpack_bf16_pairs(p_u32):
    lo = lax.bitcast_convert_type(p_u32 << 16, F32).astype(BF16)
    hi = lax.bitcast_convert_type(p_u32 & jnp.uint32(0xFFFF0000), F32).astype(BF16)
    return jnp.concatenate([lo, hi], axis=1)


def _mixer_kernel(x_ref, modn_ref, modp_ref, g1_ref, ca_ref, cb_ref, bcb_ref, lng_ref, lnb_ref, g2_ref,
                  win_ref, wa_ref, wb_ref, wo_ref, wrh_ref, wrl_ref, br_ref,
                  x1_ref, h2p_ref, route_ref, cnt_ref,
                  wu_ref, wcx_ref, ba_ref, sga_ref, sgb_ref, xs_ref, cva_ref, cvb_ref, run_ref, phase_ref,
                  *, n_tiles):
    i = pl.program_id(0)
    t_rows, d = x_ref.shape

    @pl.when(i == 0)
    def _init():
        for ref in (wu_ref, wcx_ref, ba_ref, sga_ref, sgb_ref, xs_ref, run_ref):
            ref[...] = jnp.zeros(ref.shape, ref.dtype)

    x = x_ref[...]
    sh1 = modn_ref[0, 0:1, :]
    sc1 = modn_ref[0, 1:2, :]
    h = (x * lax.rsqrt(jnp.mean(x * x, axis=-1, keepdims=True) + RMS_EPS)) * g1_ref[...]
    hb = (h * (1.0 + sc1) + sh1).astype(BF16)

    def proj(g):
        return _dot(hb, win_ref[:, g * d:(g + 1) * d])

    def conv_work(rows):
        items = []
        for cb in range(d // LANES):
            items.append(functools.partial(_depthwise_lane_block, wcx_ref, ca_ref, cva_ref,
                                           phase_ref.at[2], K_SHORT, HALO_A - K_SHORT // 2, rows, cb))
            items.append(functools.partial(_depthwise_lane_block, wu_ref, cb_ref, cvb_ref,
                                           phase_ref.at[cb % 2], K_CONF, HALO_B - K_CONF // 2, rows, cb))
        return items

    early = conv_work((0, t_rows - CONV_ROWS))
    n_proj = 7
    z = []
    for g in range(n_proj):
        z.append(proj(g))
        for item in early[g * len(early) // n_proj:(g + 1) * len(early) // n_proj]:
            item()
    b_a = z[0]
    cx = z[1] * z[2]
    u = z[3] * _sigmoid(z[4])
    sg_a = _sigmoid(z[5])
    sg_b = _sigmoid(z[6])

    same_seq = ((i % n_tiles) != 0).astype(F32)
    wu_ref[pl.ds(HALO_B + t_rows, HALO_B), :] = u[0:HALO_B] * same_seq
    wcx_ref[pl.ds(HALO_A + t_rows, HALO_A), :] = cx[0:HALO_A] * same_seq

    for item in conv_work((t_rows - CONV_ROWS, t_rows)):
        item()

    y_a = _dot((ba_ref[...] * cva_ref[...]).astype(BF16), wa_ref[...])
    v = cvb_ref[...] + bcb_ref[...]
    mu = jnp.mean(v, axis=-1, keepdims=True)
    vc = v - mu
    var = jnp.mean(vc * vc, axis=-1, keepdims=True)
    v = (vc * lax.rsqrt(var + LN_EPS)) * lng_ref[...] + lnb_ref[...]
    y_b = _dot((v * _sigmoid(v)).astype(BF16), wb_ref[...])
    merged = sga_ref[...] * y_a + sgb_ref[...] * y_b
    mix = _dot(merged.astype(BF16), wo_ref[...])

    gt1 = modp_ref[0, 2:3, :]
    sh2 = modp_ref[0, 3:4, :]
    sc2 = modp_ref[0, 4:5, :]
    x1 = xs_ref[...] + gt1 * mix
    x1_ref[...] = x1
    h2 = (x1 * lax.rsqrt(jnp.mean(x1 * x1, axis=-1, keepdims=True) + RMS_EPS)) * g2_ref[...]
    h2 = h2 * (1.0 + sc2) + sh2
    h2_hi, h2_lo = _split_bf16(h2)
    h2p_ref[...] = _pack_bf16_pairs(h2_hi)
    logits = _dot(h2_hi, wrh_ref[...]) + _dot(h2_lo, wrh_ref[...]) + _dot(h2_hi, wrl_ref[...]) + br_ref[...]
    route_ref[...] = _route(logits, run_ref, (i > 0).astype(F32))
    cnt_ref[...] = jnp.broadcast_to(run_ref[0:1, :], cnt_ref.shape)

    tail_u = wu_ref[pl.ds(t_rows, HALO_B), :]
    wu_ref[pl.ds(0, HALO_B), :] = tail_u * same_seq
    wu_ref[pl.ds(HALO_B, t_rows), :] = u
    tail_cx = wcx_ref[pl.ds(t_rows, HALO_A), :]
    wcx_ref[pl.ds(0, HALO_A), :] = tail_cx * same_seq
    wcx_ref[pl.ds(HALO_A, t_rows), :] = cx
    ba_ref[...] = b_a
    sga_ref[...] = sg_a
    sgb_ref[...] = sg_b
    xs_ref[...] = x


def _mixer_call(x2d, mod3, g1, conv_a, conv_b, b_conv_b, ln_g, ln_b, g2,
                w_in, w_a, w_b, w_o, wr_hi, wr_lo, b_r, *, seq_len):
    n_tok, d = x2d.shape
    t = T_MIX
    n_tiles = seq_len // t
    n_chunks = n_tok // t
    last = n_chunks - 1

    def cur(i):
        return jnp.minimum(i, last)

    def prev(i):
        return jnp.maximum(i - 1, 0)

    row_spec_prev = lambda width: pl.BlockSpec((t, width), lambda i: (prev(i), 0))
    vec = lambda a: _const_spec(a.shape)
    in_specs = [
        pl.BlockSpec((t, d), lambda i: (cur(i), 0)),
        pl.BlockSpec((1,) + mod3.shape[1:], lambda i: (cur(i) // n_tiles, 0, 0)),
        pl.BlockSpec((1,) + mod3.shape[1:], lambda i: (prev(i) // n_tiles, 0, 0)),
        vec(g1), vec(conv_a), vec(conv_b), vec(b_conv_b), vec(ln_g), vec(ln_b), vec(g2),
        vec(w_in), vec(w_a), vec(w_b), vec(w_o), vec(wr_hi), vec(wr_lo), vec(b_r),
    ]
    out_specs = [row_spec_prev(d), row_spec_prev(d // 2), row_spec_prev(ROUTE_LANES),
                 pl.BlockSpec((SUBLANES, ROUTE_LANES), lambda i: (0, 0))]
    out_shape = [
        jax.ShapeDtypeStruct((n_tok, d), F32),
        jax.ShapeDtypeStruct((n_tok, d // 2), U32),
        jax.ShapeDtypeStruct((n_tok, ROUTE_LANES), F32),
        jax.ShapeDtypeStruct((SUBLANES, ROUTE_LANES), F32),
    ]
    scratch = [
        pltpu.VMEM((t + 2 * HALO_B, d), F32),
        pltpu.VMEM((t + 2 * HALO_A, d), F32),
        pltpu.VMEM((t, d), F32), pltpu.VMEM((t, d), F32), pltpu.VMEM((t, d), F32), pltpu.VMEM((t, d), F32),
        pltpu.VMEM((t, d), F32), pltpu.VMEM((t, d), F32),
        pltpu.VMEM((SUBLANES, ROUTE_LANES), F32),
        pltpu.VMEM((3, SUBLANES, t + 2 * HALO_B, LANES), F32),
    ]
    return pl.pallas_call(
        functools.partial(_mixer_kernel, n_tiles=n_tiles),
        grid=(n_chunks + 1,),
        in_specs=in_specs,
        out_specs=out_specs,
        out_shape=out_shape,
        scratch_shapes=scratch,
        compiler_params=pltpu.CompilerParams(
            dimension_semantics=("arbitrary",), vmem_limit_bytes=VMEM_LIMIT_BYTES),
        name="mixer_router",
    )(x2d, mod3, mod3, g1, conv_a, conv_b, b_conv_b, ln_g, ln_b, g2,
      w_in, w_a, w_b, w_o, wr_hi, wr_lo, b_r)


def _segment_layout(counts_row, n_blocks):
    counts = counts_row[:N_EXPERTS].astype(jnp.int32)
    padded = ((counts + ROW_BLK - 1) // ROW_BLK) * ROW_BLK
    pend = jnp.cumsum(padded)
    pstart = pend - padded
    blk_row0 = jnp.arange(n_blocks, dtype=jnp.int32) * ROW_BLK
    blk_e = jnp.minimum(jnp.sum(pend[None, :] <= blk_row0[:, None], axis=1), N_EXPERTS - 1).astype(jnp.int32)
    n_used = (pend[-1] // ROW_BLK).astype(jnp.int32).reshape(1)
    pstart_row = jnp.zeros((1, ROUTE_LANES), F32).at[0, :N_EXPERTS].set(pstart.astype(F32))
    return pstart_row, blk_e, n_used


def _plan_kernel(route_ref, pstart_ref, pos_ref):
    rec = route_ref[...]
    lane = _lane_ids(rec.shape)
    pos = jnp.zeros(rec.shape, F32)
    for k in range(TOP_K):
        e = rec[:, REC_E + k:REC_E + k + 1]
        seg = jnp.sum(jnp.where(lane == e, pstart_ref[...], 0.0), axis=-1, keepdims=True)
        pos = jnp.where(lane == k, seg + rec[:, REC_RANK + k:REC_RANK + k + 1], pos)
    pos_ref[...] = pos.astype(jnp.int32)


def _plan_call(route, pstart_row):
    n_tok = route.shape[0]
    t = T_PLAN
    return pl.pallas_call(
        _plan_kernel,
        grid=(n_tok // t,),
        in_specs=[pl.BlockSpec((t, ROUTE_LANES), lambda i: (i, 0)),
                  pl.BlockSpec((1, ROUTE_LANES), lambda i: (0, 0))],
        out_specs=pl.BlockSpec((t, ROUTE_LANES), lambda i: (i, 0)),
        out_shape=jax.ShapeDtypeStruct((n_tok, ROUTE_LANES), jnp.int32),
        compiler_params=pltpu.CompilerParams(dimension_semantics=("arbitrary",)),
        name="moe_plan",
    )(route, pstart_row)


def _scatter_kernel(pos_ref, rows_ref, buf_in_hbm, buf_hbm, sem):
    del buf_in_hbm
    t_rows = rows_ref.shape[0]

    def body(r, carry):
        for k in range(TOP_K):
            pltpu.make_async_copy(rows_ref.at[pl.ds(r, 1), :],
                                  buf_hbm.at[pl.ds(pos_ref[0, k, r], 1), :], sem).start()
        return carry

    lax.fori_loop(0, t_rows, body, 0, unroll=8)
    for _ in range(TOP_K):
        pltpu.make_async_copy(rows_ref, buf_hbm.at[pl.ds(0, t_rows), :], sem).wait()


def _scatter_call(pos3, rows, n_rows):
    n_tok, width = rows.shape
    t = T_MOVE
    buf0 = jnp.zeros((n_rows, width), rows.dtype)
    return pl.pallas_call(
        _scatter_kernel,
        grid=(n_tok // t,),
        in_specs=[
            pl.BlockSpec((1, TOP_K, t), lambda i: (i, 0, 0), memory_space=pltpu.SMEM),
            pl.BlockSpec((t, width), lambda i: (i, 0)),
            pl.BlockSpec(memory_space=pl.ANY),
        ],
        out_specs=pl.BlockSpec(memory_space=pl.ANY),
        out_shape=jax.ShapeDtypeStruct((n_rows, width), rows.dtype),
        scratch_shapes=[pltpu.SemaphoreType.DMA(())],
        input_output_aliases={2: 0},
        compiler_params=pltpu.CompilerParams(dimension_semantics=("arbitrary",)),
        name="moe_scatter",
    )(pos3, rows, buf0)


def _expert_kernel(blk_e_ref, n_used_ref, xs_ref, wg_ref, wu_ref, wd_ref, y_ref):
    del blk_e_ref
    i = pl.program_id(0)
    n_used = n_used_ref[0]

    @pl.when(i < n_used)
    def _compute():
        xb = _unpack_bf16_pairs(xs_ref[...])
        a = _dot(xb, wg_ref[0])
        u = _dot(xb, wu_ref[0])
        hid = ((a * _sigmoid(a)) * u).astype(BF16)
        y_ref[...] = _dot(hid, wd_ref[0])

    @pl.when(i >= n_used)
    def _unused_block():
        y_ref[...] = jnp.zeros(y_ref.shape, y_ref.dtype)


def _expert_call(xs, blk_e, n_used, w_gate, w_up, w_down):
    n_rows, half = xs.shape
    d = 2 * half
    n_blocks = n_rows // ROW_BLK
    d_e = w_gate.shape[-1]

    def used(i, n_used_ref):
        return jnp.minimum(i, n_used_ref[0] - 1)

    grid_spec = pltpu.PrefetchScalarGridSpec(
        num_scalar_prefetch=2,
        grid=(n_blocks,),
        in_specs=[
            pl.BlockSpec((ROW_BLK, half), lambda i, be, nu: (used(i, nu), 0)),
            pl.BlockSpec((1, d, d_e), lambda i, be, nu: (be[used(i, nu)], 0, 0)),
            pl.BlockSpec((1, d, d_e), lambda i, be, nu: (be[used(i, nu)], 0, 0)),
            pl.BlockSpec((1, d_e, d), lambda i, be, nu: (be[used(i, nu)], 0, 0)),
        ],
        out_specs=pl.BlockSpec((ROW_BLK, d), lambda i, be, nu: (i, 0)),
    )
    return pl.pallas_call(
        _expert_kernel,
        grid_spec=grid_spec,
        out_shape=jax.ShapeDtypeStruct((n_rows, d), F32),
        compiler_params=pltpu.CompilerParams(
            dimension_semantics=("arbitrary",), vmem_limit_bytes=VMEM_LIMIT_BYTES),
        name="moe_experts",
    )(blk_e, n_used, xs, w_gate, w_up, w_down)


def _row_gather_start(src_hbm, idx_of_row, dst, sem, n_rows):
    def body(r, carry):
        pltpu.make_async_copy(src_hbm.at[pl.ds(idx_of_row(r), 1), :], dst.at[pl.ds(r, 1), :], sem).start()
        return carry
    lax.fori_loop(0, n_rows, body, 0, unroll=8)


def _row_gather_wait(src_hbm, dst, sem, n_rows):
    pltpu.make_async_copy(src_hbm.at[pl.ds(0, n_rows), :], dst, sem).wait()


def _combine_kernel(posc_ref, posn_ref, x1_ref, route_ref, mod_ref, gf_ref, y_hbm, o_ref, ybuf, sem,
                    *, n_steps, final_norm):
    i = pl.program_id(0)
    slot = i % 2
    t_rows = x1_ref.shape[0]

    def start(pos_ref, s):
        for k in range(TOP_K):
            _row_gather_start(y_hbm, lambda r, k=k: pos_ref[0, k, r], ybuf.at[s, k], sem.at[s], t_rows)

    @pl.when(i == 0)
    def _prime():
        start(posc_ref, 0)

    @pl.when(i + 1 < n_steps)
    def _prefetch():
        start(posn_ref, 1 - slot)

    for k in range(TOP_K):
        _row_gather_wait(y_hbm, ybuf.at[slot, k], sem.at[slot], t_rows)
    rec = route_ref[...]
    moe = ybuf[slot, 0] * rec[:, REC_W:REC_W + 1] + ybuf[slot, 1] * rec[:, REC_W + 1:REC_W + 2]
    x2 = x1_ref[...] + mod_ref[0, 5:6, :] * moe
    if final_norm:
        x2 = (x2 * lax.rsqrt(jnp.mean(x2 * x2, axis=-1, keepdims=True) + RMS_EPS)) * gf_ref[...]
    o_ref[...] = x2


def _combine_call(x1, route, mod3, g_final, y_rows, pos3, *, seq_len, final_norm):
    n_tok, d = x1.shape
    t = T_MOVE
    n_steps = n_tok // t
    tiles_per_seq = seq_len // t
    last = n_steps - 1
    return pl.pallas_call(
        functools.partial(_combine_kernel, n_steps=n_steps, final_norm=final_norm),
        grid=(n_steps,),
        in_specs=[
            pl.BlockSpec((1, TOP_K, t), lambda i: (i, 0, 0), memory_space=pltpu.SMEM),
            pl.BlockSpec((1, TOP_K, t), lambda i: (jnp.minimum(i + 1, last), 0, 0), memory_space=pltpu.SMEM),
            pl.BlockSpec((t, d), lambda i: (i, 0)),
            pl.BlockSpec((t, ROUTE_LANES), lambda i: (i, 0)),
            pl.BlockSpec((1,) + mod3.shape[1:], lambda i: (i // tiles_per_seq, 0, 0)),
            pl.BlockSpec((1, d), lambda i: (0, 0)),
            pl.BlockSpec(memory_space=pl.ANY),
        ],
        out_specs=pl.BlockSpec((t, d), lambda i: (i, 0)),
        out_shape=jax.ShapeDtypeStruct((n_tok, d), F32),
        scratch_shapes=[pltpu.VMEM((2, TOP_K, t, d), F32), pltpu.SemaphoreType.DMA((2,))],
        compiler_params=pltpu.CompilerParams(
            dimension_semantics=("arbitrary",), vmem_limit_bytes=VMEM_LIMIT_BYTES),
        name="moe_combine",
    )(pos3, pos3, x1, route, mod3, g_final, y_rows)


def kernel(x, c, w_ada, b_ada, g_norm1, w_in, conv_a, w_a_out, conv_b, b_conv_b, ln_conv_g, ln_conv_b,
           w_b_out, w_o, g_norm2, w_router_g, b_router_g, w_router_e, b_router_e, w_gate, w_up, w_down,
           g_final):
    bsz, seq_len, d = x.shape
    depth = w_ada.shape[0]
    n_tok = bsz * seq_len
    n_assign = n_tok * TOP_K
    assert seq_len % T_MIX == 0 and seq_len % T_MOVE == 0 and d % (2 * LANES) == 0
    assert n_tok % T_PLAN == 0
    assert n_assign % ROW_BLK == 0
    assert N_GROUPS + N_EXPERTS <= ROUTE_LANES
    n_rows = n_assign + N_EXPERTS * ROW_BLK
    n_blocks = n_rows // ROW_BLK

    c_pad = jnp.zeros((SUBLANES, d), F32).at[:bsz].set(c)
    xt = x.reshape(n_tok, d)
    row = lambda a: a.reshape(1, -1)
    for l in range(depth):
        mod = _mod_call(c_pad, w_ada[l], row(b_ada[l]))
        mod3 = mod[:bsz].reshape(bsz, 6, d)
        w_r = jnp.zeros((d, ROUTE_LANES), F32)
        w_r = w_r.at[:, :N_GROUPS].set(w_router_g[l]).at[:, N_GROUPS:N_GROUPS + N_EXPERTS].set(w_router_e[l])
        b_r = jnp.zeros((1, ROUTE_LANES), F32)
        b_r = b_r.at[0, :N_GROUPS].set(b_router_g[l]).at[0, N_GROUPS:N_GROUPS + N_EXPERTS].set(b_router_e[l])
        wr_hi, wr_lo = _split_bf16(w_r)
        x1, h2p, route, cnt = _mixer_call(
            xt, mod3, row(g_norm1[l]), conv_a[l], conv_b[l], row(b_conv_b[l]), row(ln_conv_g[l]),
            row(ln_conv_b[l]), row(g_norm2[l]),
            w_in[l].astype(BF16), w_a_out[l].astype(BF16), w_b_out[l].astype(BF16), w_o[l].astype(BF16),
            wr_hi, wr_lo, b_r, seq_len=seq_len)
        pstart_row, blk_e, n_used = _segment_layout(cnt[0], n_blocks)
        pos = _plan_call(route, pstart_row)
        pos3 = pos[:, :TOP_K].reshape(n_tok // T_MOVE, T_MOVE, TOP_K).transpose(0, 2, 1)
        xs = _scatter_call(pos3, h2p, n_rows)
        y_rows = _expert_call(xs, blk_e, n_used,
                              w_gate[l].astype(BF16), w_up[l].astype(BF16), w_down[l].astype(BF16))
        xt = _combine_call(x1, route, mod3, row(g_final), y_rows, pos3, seq_len=seq_len,
                           final_norm=(l == depth - 1))
    return xt.reshape(bsz, seq_len, d)
```

```python
import functools

import jax
import jax.numpy as jnp
from jax import lax
from jax.experimental import pallas as pl
from jax.experimental.pallas import tpu as pltpu

F32 = jnp.float32
BF16 = jnp.bfloat16
U32 = jnp.uint32

N_GROUPS = 4
EXPERTS_PER_GROUP = 8
N_EXPERTS = N_GROUPS * EXPERTS_PER_GROUP
TOP_K = 2
K_SHORT = 3
K_CONF = 31
RMS_EPS = 1e-6
LN_EPS = 1e-5

LANES = 128
SUBLANES = 8
T_MIX = 256
HALO_B = 16
HALO_A = 8
CONV_ROWS = 64
ROW_BLK = 256
T_MOVE = 256
T_PLAN = 2048
W_STAGE_COLS = 512
ROUTE_LANES = LANES
REC_E, REC_W, REC_RANK = 0, 2, 4
VMEM_LIMIT_BYTES = 56 * 1024 * 1024


def _sigmoid(v):
    return 1.0 / (1.0 + jnp.exp(-v))


def _split_bf16(v):
    hi = v.astype(BF16)
    lo = (v - hi.astype(F32)).astype(BF16)
    return hi, lo


def _dot(a, b):
    return jnp.dot(a, b, preferred_element_type=F32)


def _const_spec(shape):
    nd = len(shape)
    return pl.BlockSpec(shape, lambda *_: (0,) * nd, pipeline_mode=pl.Buffered(1))


def _lane_ids(shape):
    return lax.broadcasted_iota(jnp.int32, shape, 1).astype(F32)


def _mod_kernel(c_ref, w_ref, b_ref, o_ref):
    c = c_ref[...]
    a_hi, a_lo = _split_bf16(c * _sigmoid(c))
    w_hi, w_lo = _split_bf16(w_ref[...])
    o_ref[...] = _dot(a_hi, w_hi) + _dot(a_lo, w_hi) + _dot(a_hi, w_lo) + b_ref[...]


def _mod_call(c_pad, w_ada, b_ada):
    rows, d = c_pad.shape
    n_out = w_ada.shape[1]
    blk = 1024
    return pl.pallas_call(
        _mod_kernel,
        grid=(n_out // blk,),
        in_specs=[
            pl.BlockSpec((rows, d), lambda j: (0, 0)),
            pl.BlockSpec((d, blk), lambda j: (0, j)),
            pl.BlockSpec((1, blk), lambda j: (0, j)),
        ],
        out_specs=pl.BlockSpec((rows, blk), lambda j: (0, j)),
        out_shape=jax.ShapeDtypeStruct((rows, n_out), F32),
        compiler_params=pltpu.CompilerParams(dimension_semantics=("arbitrary",)),
        name="adaln_mod",
    )(c_pad, w_ada, b_ada)


def _depthwise_lane_block(buf_ref, w_ref, out_ref, phase_ref, n_taps, row0, rows, cb):
    first, stop = rows
    max_off = row0 + n_taps - 1
    n_keep = stop - first + (max_off // SUBLANES) * SUBLANES
    cols = slice(cb * LANES, (cb + 1) * LANES)
    for s in sorted({(row0 + k) % SUBLANES for k in range(n_taps)}):
        phase_ref[s, pl.ds(0, n_keep), :] = buf_ref[pl.ds(first + s, n_keep), cols]
    row_blocks = range(first, stop, CONV_ROWS)
    acc = [None] * len(row_blocks)
    for k in range(n_taps):
        q, s = divmod(row0 + k, SUBLANES)
        w_k = jnp.broadcast_to(w_ref[k:k + 1, cols], (CONV_ROWS, LANES))
        for j, r0 in enumerate(row_blocks):
            term = phase_ref[s, pl.ds(r0 - first + q * SUBLANES, CONV_ROWS), :] * w_k
            acc[j] = term if acc[j] is None else acc[j] + term
    for j, r0 in enumerate(row_blocks):
        out_ref[pl.ds(r0, CONV_ROWS), cols] = acc[j]


def _route(logits, run_ref, live):
    shape = logits.shape
    lane = _lane_ids(shape)
    big = float(4 * ROUTE_LANES)
    neg_inf = -jnp.inf

    def first_argmax(v):
        m = jnp.max(v, axis=-1, keepdims=True)
        return m, jnp.min(jnp.where(v == m, lane, big), axis=-1, keepdims=True)

    is_group = lane < N_GROUPS
    g_max, g_idx = first_argmax(jnp.where(is_group, logits, neg_inf))
    g_prob = 1.0 / jnp.sum(jnp.where(is_group, jnp.exp(logits - g_max), 0.0), axis=-1, keepdims=True)
    lo = N_GROUPS + EXPERTS_PER_GROUP * g_idx
    in_group = (lane >= lo) & (lane < lo + EXPERTS_PER_GROUP)
    e_logits = jnp.where(in_group, logits, neg_inf)
    m1, i1 = first_argmax(e_logits)
    m2, i2 = first_argmax(jnp.where(lane == i1, neg_inf, e_logits))
    t = jnp.exp(m2 - m1)
    p1 = 1.0 / (1.0 + t)
    p2 = t / (1.0 + t)
    e1 = i1 - N_GROUPS
    e2 = i2 - N_GROUPS

    hot1 = lane == e1
    hot2 = lane == e2
    both = jnp.where(hot1 | hot2, 1.0, 0.0)
    n_rows = shape[0]
    earlier = (lax.broadcasted_iota(jnp.int32, (n_rows, n_rows), 0)
               > lax.broadcasted_iota(jnp.int32, (n_rows, n_rows), 1))
    before = _dot(jnp.where(earlier, 1.0, 0.0).astype(BF16), both.astype(BF16)) + run_ref[0:1, :]
    rank1 = jnp.sum(jnp.where(hot1, before, 0.0), axis=-1, keepdims=True)
    rank2 = jnp.sum(jnp.where(hot2, before, 0.0), axis=-1, keepdims=True)
    run_ref[0:1, :] = run_ref[0:1, :] + live * jnp.sum(both, axis=0, keepdims=True)

    rec = jnp.where(lane == REC_E, e1, 0.0)
    rec = jnp.where(lane == REC_E + 1, e2, rec)
    rec = jnp.where(lane == REC_W, g_prob * p1, rec)
    rec = jnp.where(lane == REC_W + 1, g_prob * p2, rec)
    rec = jnp.where(lane == REC_RANK, rank1, rec)
    rec = jnp.where(lane == REC_RANK + 1, rank2, rec)
    return rec


def _pack_bf16_pairs(v_bf16):
    half = v_bf16.shape[1] // 2
    lo = lax.bitcast_convert_type(v_bf16[:, :half].astype(F32), U32)
    hi = lax.bitcast_convert_type(v_bf16[:, half:].astype(F32), U32)
    return (lo >> 16) | hi


def _unpack_bf16_pairs(p_u32):
    lo = lax.bitcast_convert_type(p_u32 << 16, F32).astype(BF16)
    hi = lax.bitcast_convert_type(p_u32 & jnp.uint32(0xFFFF0000), F32).astype(BF16)
    return jnp.concatenate([lo, hi], axis=1)


def _stage_weight_bf16(src_hbm, dst_ref, stage_ref, sem):
    chunk = stage_ref.shape[2]
    n_chunks = src_hbm.shape[1] // chunk

    def cols(j):
        return pl.ds(pl.multiple_of(j * chunk, chunk), chunk)

    def copy(j, slot):
        return pltpu.make_async_copy(src_hbm.at[:, cols(j)], stage_ref.at[slot], sem.at[slot])

    copy(0, 0).start()

    def body(j, carry):
        slot = j % 2

        @pl.when(j + 1 < n_chunks)
        def _next():
            copy(j + 1, 1 - slot).start()

        copy(j, slot).wait()
        dst_ref[:, cols(j)] = stage_ref[slot].astype(BF16)
        return carry

    lax.fori_loop(0, n_chunks, body, 0)


def _mixer_kernel(x_ref, modn_ref, modp_ref, g1_ref, ca_ref, cb_ref, bcb_ref, lng_ref, lnb_ref, g2_ref,
                  wrh_ref, wrl_ref, br_ref, win_hbm, wa_hbm, wb_hbm, wo_hbm,
                  x1_ref, h2p_ref, route_ref, cnt_ref,
                  wu_ref, wcx_ref, ba_ref, sga_ref, sgb_ref, xs_ref, cva_ref, cvb_ref, run_ref, phase_ref,
                  win_ref, wa_ref, wb_ref, wo_ref, stage_ref, stage_sem,
                  *, n_tiles):
    i = pl.program_id(0)
    t_rows, d = x_ref.shape

    @pl.when(i == 0)
    def _init():
        for ref in (wu_ref, wcx_ref, ba_ref, sga_ref, sgb_ref, xs_ref, run_ref):
            ref[...] = jnp.zeros(ref.shape, ref.dtype)
        for src, dst in ((win_hbm, win_ref), (wa_hbm, wa_ref), (wb_hbm, wb_ref), (wo_hbm, wo_ref)):
            _stage_weight_bf16(src, dst, stage_ref, stage_sem)

    x = x_ref[...]
    sh1 = modn_ref[0, 0:1, :]
    sc1 = modn_ref[0, 1:2, :]
    h = (x * lax.rsqrt(jnp.mean(x * x, axis=-1, keepdims=True) + RMS_EPS)) * g1_ref[...]
    hb = (h * (1.0 + sc1) + sh1).astype(BF16)

    def proj(g):
        return _dot(hb, win_ref[:, g * d:(g + 1) * d])

    def conv_work(rows):
        items = []
        for cb in range(d // LANES):
            items.append(functools.partial(_depthwise_lane_block, wcx_ref, ca_ref, cva_ref,
                                           phase_ref.at[2], K_SHORT, HALO_A - K_SHORT // 2, rows, cb))
            items.append(functools.partial(_depthwise_lane_block, wu_ref, cb_ref, cvb_ref,
                                           phase_ref.at[cb % 2], K_CONF, HALO_B - K_CONF // 2, rows, cb))
        return items

    early = conv_work((0, t_rows - CONV_ROWS))
    n_proj = 7
    z = []
    for g in range(n_proj):
        z.append(proj(g))
        for item in early[g * len(early) // n_proj:(g + 1) * len(early) // n_proj]:
            item()
    b_a = z[0]
    cx = z[1] * z[2]
    u = z[3] * _sigmoid(z[4])
    sg_a = _sigmoid(z[5])
    sg_b = _sigmoid(z[6])

    same_seq = ((i % n_tiles) != 0).astype(F32)
    wu_ref[pl.ds(HALO_B + t_rows, HALO_B), :] = u[0:HALO_B] * same_seq
    wcx_ref[pl.ds(HALO_A + t_rows, HALO_A), :] = cx[0:HALO_A] * same_seq

    for item in conv_work((t_rows - CONV_ROWS, t_rows)):
        item()

    y_a = _dot((ba_ref[...] * cva_ref[...]).astype(BF16), wa_ref[...])
    v = cvb_ref[...] + bcb_ref[...]
    mu = jnp.mean(v, axis=-1, keepdims=True)
    vc = v - mu
    var = jnp.mean(vc * vc, axis=-1, keepdims=True)
    v = (vc * lax.rsqrt(var + LN_EPS)) * lng_ref[...] + lnb_ref[...]
    y_b = _dot((v * _sigmoid(v)).astype(BF16), wb_ref[...])
    merged = sga_ref[...] * y_a + sgb_ref[...] * y_b
    mix = _dot(merged.astype(BF16), wo_ref[...])

    gt1 = modp_ref[0, 2:3, :]
    sh2 = modp_ref[0, 3:4, :]
    sc2 = modp_ref[0, 4:5, :]
    x1 = xs_ref[...] + gt1 * mix
    x1_ref[...] = x1
    h2 = (x1 * lax.rsqrt(jnp.mean(x1 * x1, axis=-1, keepdims=True) + RMS_EPS)) * g2_ref[...]
    h2 = h2 * (1.0 + sc2) + sh2
    h2_hi, h2_lo = _split_bf16(h2)
    h2p_ref[...] = _pack_bf16_pairs(h2_hi)
    logits = _dot(h2_hi, wrh_ref[...]) + _dot(h2_lo, wrh_ref[...]) + _dot(h2_hi, wrl_ref[...]) + br_ref[...]
    route_ref[...] = _route(logits, run_ref, (i > 0).astype(F32))
    cnt_ref[...] = jnp.broadcast_to(run_ref[0:1, :], cnt_ref.shape)

    tail_u = wu_ref[pl.ds(t_rows, HALO_B), :]
    wu_ref[pl.ds(0, HALO_B), :] = tail_u * same_seq
    wu_ref[pl.ds(HALO_B, t_rows), :] = u
    tail_cx = wcx_ref[pl.ds(t_rows, HALO_A), :]
    wcx_ref[pl.ds(0, HALO_A), :] = tail_cx * same_seq
    wcx_ref[pl.ds(HALO_A, t_rows), :] = cx
    ba_ref[...] = b_a
    sga_ref[...] = sg_a
    sgb_ref[...] = sg_b
    xs_ref[...] = x


def _mixer_call(x2d, mod3, g1, conv_a, conv_b, b_conv_b, ln_g, ln_b, g2,
                w_in, w_a, w_b, w_o, wr_hi, wr_lo, b_r, *, seq_len):
    n_tok, d = x2d.shape
    t = T_MIX
    n_tiles = seq_len // t
    n_chunks = n_tok // t
    last = n_chunks - 1

    def cur(i):
        return jnp.minimum(i, last)

    def prev(i):
        return jnp.maximum(i - 1, 0)

    row_spec_prev = lambda width: pl.BlockSpec((t, width), lambda i: (prev(i), 0))
    vec = lambda a: _const_spec(a.shape)
    hbm = pl.BlockSpec(memory_space=pl.ANY)
    in_specs = [
        pl.BlockSpec((t, d), lambda i: (cur(i), 0)),
        pl.BlockSpec((1,) + mod3.shape[1:], lambda i: (cur(i) // n_tiles, 0, 0)),
        pl.BlockSpec((1,) + mod3.shape[1:], lambda i: (prev(i) // n_tiles, 0, 0)),
        vec(g1), vec(conv_a), vec(conv_b), vec(b_conv_b), vec(ln_g), vec(ln_b), vec(g2),
        vec(wr_hi), vec(wr_lo), vec(b_r),
        hbm, hbm, hbm, hbm,
    ]
    out_specs = [row_spec_prev(d), row_spec_prev(d // 2), row_spec_prev(ROUTE_LANES),
                 pl.BlockSpec((SUBLANES, ROUTE_LANES), lambda i: (0, 0))]
    out_shape = [
        jax.ShapeDtypeStruct((n_tok, d), F32),
        jax.ShapeDtypeStruct((n_tok, d // 2), U32),
        jax.ShapeDtypeStruct((n_tok, ROUTE_LANES), F32),
        jax.ShapeDtypeStruct((SUBLANES, ROUTE_LANES), F32),
    ]
    scratch = [
        pltpu.VMEM((t + 2 * HALO_B, d), F32),
        pltpu.VMEM((t + 2 * HALO_A, d), F32),
        pltpu.VMEM((t, d), F32), pltpu.VMEM((t, d), F32), pltpu.VMEM((t, d), F32), pltpu.VMEM((t, d), F32),
        pltpu.VMEM((t, d), F32), pltpu.VMEM((t, d), F32),
        pltpu.VMEM((SUBLANES, ROUTE_LANES), F32),
        pltpu.VMEM((3, SUBLANES, t + 2 * HALO_B, LANES), F32),
        pltpu.VMEM(w_in.shape, BF16), pltpu.VMEM(w_a.shape, BF16), pltpu.VMEM(w_b.shape, BF16),
        pltpu.VMEM(w_o.shape, BF16),
        pltpu.VMEM((2, d, W_STAGE_COLS), F32), pltpu.SemaphoreType.DMA((2,)),
    ]
    return pl.pallas_call(
        functools.partial(_mixer_kernel, n_tiles=n_tiles),
        grid=(n_chunks + 1,),
        in_specs=in_specs,
        out_specs=out_specs,
        out_shape=out_shape,
        scratch_shapes=scratch,
        compiler_params=pltpu.CompilerParams(
            dimension_semantics=("arbitrary",), vmem_limit_bytes=VMEM_LIMIT_BYTES),
        name="mixer_router",
    )(x2d, mod3, mod3, g1, conv_a, conv_b, b_conv_b, ln_g, ln_b, g2,
      wr_hi, wr_lo, b_r, w_in, w_a, w_b, w_o)


def _segment_layout(counts_row, n_blocks):
    counts = counts_row[:N_EXPERTS].astype(jnp.int32)
    padded = ((counts + ROW_BLK - 1) // ROW_BLK) * ROW_BLK
    pend = jnp.cumsum(padded)
    pstart = pend - padded
    blk_row0 = jnp.arange(n_blocks, dtype=jnp.int32) * ROW_BLK
    blk_e = jnp.minimum(jnp.sum(pend[None, :] <= blk_row0[:, None], axis=1), N_EXPERTS - 1).astype(jnp.int32)
    n_used = (pend[-1] // ROW_BLK).astype(jnp.int32).reshape(1)
    pstart_row = jnp.zeros((1, ROUTE_LANES), F32).at[0, :N_EXPERTS].set(pstart.astype(F32))
    return pstart_row, blk_e, n_used


def _plan_kernel(route_ref, pstart_ref, pos_ref):
    rec = route_ref[...]
    lane = _lane_ids(rec.shape)
    pos = jnp.zeros(rec.shape, F32)
    for k in range(TOP_K):
        e = rec[:, REC_E + k:REC_E + k + 1]
        seg = jnp.sum(jnp.where(lane == e, pstart_ref[...], 0.0), axis=-1, keepdims=True)
        pos = jnp.where(lane == k, seg + rec[:, REC_RANK + k:REC_RANK + k + 1], pos)
    pos_ref[...] = pos.astype(jnp.int32)


def _plan_call(route, pstart_row):
    n_tok = route.shape[0]
    t = T_PLAN
    return pl.pallas_call(
        _plan_kernel,
        grid=(n_tok // t,),
        in_specs=[pl.BlockSpec((t, ROUTE_LANES), lambda i: (i, 0)),
                  pl.BlockSpec((1, ROUTE_LANES), lambda i: (0, 0))],
        out_specs=pl.BlockSpec((t, ROUTE_LANES), lambda i: (i, 0)),
        out_shape=jax.ShapeDtypeStruct((n_tok, ROUTE_LANES), jnp.int32),
        compiler_params=pltpu.CompilerParams(dimension_semantics=("arbitrary",)),
        name="moe_plan",
    )(route, pstart_row)


def _scatter_kernel(pos_ref, rows_ref, buf_in_hbm, buf_hbm, sem):
    del buf_in_hbm
    t_rows = rows_ref.shape[0]

    def body(r, carry):
        for k in range(TOP_K):
            pltpu.make_async_copy(rows_ref.at[pl.ds(r, 1), :],
                                  buf_hbm.at[pl.ds(pos_ref[0, k, r], 1), :], sem).start()
        return carry

    lax.fori_loop(0, t_rows, body, 0, unroll=8)
    for _ in range(TOP_K):
        pltpu.make_async_copy(rows_ref, buf_hbm.at[pl.ds(0, t_rows), :], sem).wait()


def _scatter_call(pos3, rows, n_rows):
    n_tok, width = rows.shape
    t = T_MOVE
    buf0 = jnp.zeros((n_rows, width), rows.dtype)
    return pl.pallas_call(
        _scatter_kernel,
        grid=(n_tok // t,),
        in_specs=[
            pl.BlockSpec((1, TOP_K, t), lambda i: (i, 0, 0), memory_space=pltpu.SMEM),
            pl.BlockSpec((t, width), lambda i: (i, 0)),
            pl.BlockSpec(memory_space=pl.ANY),
        ],
        out_specs=pl.BlockSpec(memory_space=pl.ANY),
        out_shape=jax.ShapeDtypeStruct((n_rows, width), rows.dtype),
        scratch_shapes=[pltpu.SemaphoreType.DMA(())],
        input_output_aliases={2: 0},
        compiler_params=pltpu.CompilerParams(dimension_semantics=("arbitrary",)),
        name="moe_scatter",
    )(pos3, rows, buf0)


def _expert_kernel(blk_e_ref, n_used_ref, xs_ref, wg_ref, wu_ref, wd_ref, y_ref, wg_bf, wu_bf, wd_bf):
    i = pl.program_id(0)
    n_used = n_used_ref[0]
    new_expert = jnp.logical_or(i == 0, blk_e_ref[i] != blk_e_ref[jnp.maximum(i - 1, 0)])

    @pl.when(jnp.logical_and(i < n_used, new_expert))
    def _convert_weights():
        wg_bf[...] = wg_ref[0].astype(BF16)
        wu_bf[...] = wu_ref[0].astype(BF16)
        wd_bf[...] = wd_ref[0].astype(BF16)

    @pl.when(i < n_used)
    def _compute():
        xb = _unpack_bf16_pairs(xs_ref[...])
        a = _dot(xb, wg_bf[...])
        u = _dot(xb, wu_bf[...])
        hid = ((a * _sigmoid(a)) * u).astype(BF16)
        y_ref[...] = _dot(hid, wd_bf[...])

    @pl.when(i >= n_used)
    def _unused_block():
        y_ref[...] = jnp.zeros(y_ref.shape, y_ref.dtype)


def _expert_call(xs, blk_e, n_used, w_gate, w_up, w_down):
    n_rows, half = xs.shape
    d = 2 * half
    n_blocks = n_rows // ROW_BLK
    d_e = w_gate.shape[-1]

    def used(i, n_used_ref):
        return jnp.minimum(i, n_used_ref[0] - 1)

    grid_spec = pltpu.PrefetchScalarGridSpec(
        num_scalar_prefetch=2,
        grid=(n_blocks,),
        in_specs=[
            pl.BlockSpec((ROW_BLK, half), lambda i, be, nu: (used(i, nu), 0)),
            pl.BlockSpec((1, d, d_e), lambda i, be, nu: (be[used(i, nu)], 0, 0)),
            pl.BlockSpec((1, d, d_e), lambda i, be, nu: (be[used(i, nu)], 0, 0)),
            pl.BlockSpec((1, d_e, d), lambda i, be, nu: (be[used(i, nu)], 0, 0)),
        ],
        out_specs=pl.BlockSpec((ROW_BLK, d), lambda i, be, nu: (i, 0)),
        scratch_shapes=[pltpu.VMEM((d, d_e), BF16), pltpu.VMEM((d, d_e), BF16), pltpu.VMEM((d_e, d), BF16)],
    )
    return pl.pallas_call(
        _expert_kernel,
        grid_spec=grid_spec,
        out_shape=jax.ShapeDtypeStruct((n_rows, d), F32),
        compiler_params=pltpu.CompilerParams(
            dimension_semantics=("arbitrary",), vmem_limit_bytes=VMEM_LIMIT_BYTES),
        name="moe_experts",
    )(blk_e, n_used, xs, w_gate, w_up, w_down)


def _row_gather_start(src_hbm, idx_of_row, dst, sem, n_rows):
    def body(r, carry):
        pltpu.make_async_copy(src_hbm.at[pl.ds(idx_of_row(r), 1), :], dst.at[pl.ds(r, 1), :], sem).start()
        return carry
    lax.fori_loop(0, n_rows, body, 0, unroll=8)


def _row_gather_wait(src_hbm, dst, sem, n_rows):
    pltpu.make_async_copy(src_hbm.at[pl.ds(0, n_rows), :], dst, sem).wait()


def _combine_kernel(posc_ref, posn_ref, x1_ref, route_ref, mod_ref, gf_ref, y_hbm, o_ref, ybuf, sem,
                    *, n_steps, final_norm):
    i = pl.program_id(0)
    slot = i % 2
    t_rows = x1_ref.shape[0]

    def start(pos_ref, s):
        for k in range(TOP_K):
            _row_gather_start(y_hbm, lambda r, k=k: pos_ref[0, k, r], ybuf.at[s, k], sem.at[s], t_rows)

    @pl.when(i == 0)
    def _prime():
        start(posc_ref, 0)

    @pl.when(i + 1 < n_steps)
    def _prefetch():
        start(posn_ref, 1 - slot)

    for k in range(TOP_K):
        _row_gather_wait(y_hbm, ybuf.at[slot, k], sem.at[slot], t_rows)
    rec = route_ref[...]
    moe = ybuf[slot, 0] * rec[:, REC_W:REC_W + 1] + ybuf[slot, 1] * rec[:, REC_W + 1:REC_W + 2]
    x2 = x1_ref[...] + mod_ref[0, 5:6, :] * moe
    if final_norm:
        x2 = (x2 * lax.rsqrt(jnp.mean(x2 * x2, axis=-1, keepdims=True) + RMS_EPS)) * gf_ref[...]
    o_ref[...] = x2


def _combine_call(x1, route, mod3, g_final, y_rows, pos3, *, seq_len, final_norm):
    n_tok, d = x1.shape
    t = T_MOVE
    n_steps = n_tok // t
    tiles_per_seq = seq_len // t
    last = n_steps - 1
    return pl.pallas_call(
        functools.partial(_combine_kernel, n_steps=n_steps, final_norm=final_norm),
        grid=(n_steps,),
        in_specs=[
            pl.BlockSpec((1, TOP_K, t), lambda i: (i, 0, 0), memory_space=pltpu.SMEM),
            pl.BlockSpec((1, TOP_K, t), lambda i: (jnp.minimum(i + 1, last), 0, 0), memory_space=pltpu.SMEM),
            pl.BlockSpec((t, d), lambda i: (i, 0)),
            pl.BlockSpec((t, ROUTE_LANES), lambda i: (i, 0)),
            pl.BlockSpec((1,) + mod3.shape[1:], lambda i: (i // tiles_per_seq, 0, 0)),
            pl.BlockSpec((1, d), lambda i: (0, 0)),
            pl.BlockSpec(memory_space=pl.ANY),
        ],
        out_specs=pl.BlockSpec((t, d), lambda i: (i, 0)),
        out_shape=jax.ShapeDtypeStruct((n_tok, d), F32),
        scratch_shapes=[pltpu.VMEM((2, TOP_K, t, d), F32), pltpu.SemaphoreType.DMA((2,))],
        compiler_params=pltpu.CompilerParams(
            dimension_semantics=("arbitrary",), vmem_limit_bytes=VMEM_LIMIT_BYTES),
        name="moe_combine",
    )(pos3, pos3, x1, route, mod3, g_final, y_rows)


def kernel(x, c, w_ada, b_ada, g_norm1, w_in, conv_a, w_a_out, conv_b, b_conv_b, ln_conv_g, ln_conv_b,
           w_b_out, w_o, g_norm2, w_router_g, b_router_g, w_router_e, b_router_e, w_gate, w_up, w_down,
           g_final):
    bsz, seq_len, d = x.shape
    depth = w_ada.shape[0]
    n_tok = bsz * seq_len
    n_assign = n_tok * TOP_K
    assert seq_len % T_MIX == 0 and seq_len % T_MOVE == 0 and d % (2 * LANES) == 0
    assert n_tok % T_PLAN == 0
    assert n_assign % ROW_BLK == 0
    assert N_GROUPS + N_EXPERTS <= ROUTE_LANES
    n_rows = n_assign + N_EXPERTS * ROW_BLK
    n_blocks = n_rows // ROW_BLK

    c_pad = jnp.zeros((SUBLANES, d), F32).at[:bsz].set(c)
    xt = x.reshape(n_tok, d)
    row = lambda a: a.reshape(1, -1)
    for l in range(depth):
        mod = _mod_call(c_pad, w_ada[l], row(b_ada[l]))
        mod3 = mod[:bsz].reshape(bsz, 6, d)
        w_r = jnp.zeros((d, ROUTE_LANES), F32)
        w_r = w_r.at[:, :N_GROUPS].set(w_router_g[l]).at[:, N_GROUPS:N_GROUPS + N_EXPERTS].set(w_router_e[l])
        b_r = jnp.zeros((1, ROUTE_LANES), F32)
        b_r = b_r.at[0, :N_GROUPS].set(b_router_g[l]).at[0, N_GROUPS:N_GROUPS + N_EXPERTS].set(b_router_e[l])
        wr_hi, wr_lo = _split_bf16(w_r)
        x1, h2p, route, cnt = _mixer_call(
            xt, mod3, row(g_norm1[l]), conv_a[l], conv_b[l], row(b_conv_b[l]), row(ln_conv_g[l]),
            row(ln_conv_b[l]), row(g_norm2[l]),
            w_in[l], w_a_out[l], w_b_out[l], w_o[l], wr_hi, wr_lo, b_r, seq_len=seq_len)
        pstart_row, blk_e, n_used = _segment_layout(cnt[0], n_blocks)
        pos = _plan_call(route, pstart_row)
        pos3 = pos[:, :TOP_K].reshape(n_tok // T_MOVE, T_MOVE, TOP_K).transpose(0, 2, 1)
        xs = _scatter_call(pos3, h2p, n_rows)
        y_rows = _expert_call(xs, blk_e, n_used, w_gate[l], w_up[l], w_down[l])
        xt = _combine_call(x1, route, mod3, row(g_final), y_rows, pos3, seq_len=seq_len,
                           final_norm=(l == depth - 1))
    return xt.reshape(bsz, seq_len, d)
```

```python
import functools

import jax
import jax.numpy as jnp
from jax import lax
from jax.experimental import pallas as pl
from jax.experimental.pallas import tpu as pltpu

F32 = jnp.float32
BF16 = jnp.bfloat16
U32 = jnp.uint32

N_GROUPS = 4
EXPERTS_PER_GROUP = 8
N_EXPERTS = N_GROUPS * EXPERTS_PER_GROUP
TOP_K = 2
K_SHORT = 3
K_CONF = 31
RMS_EPS = 1e-6
LN_EPS = 1e-5

LANES = 128
SUBLANES = 8
T_MIX = 256
HALO_B = 16
HALO_A = 8
CONV_ROWS = 64
ROW_BLK = 256
T_MOVE = 256
T_PLAN = 2048
PROJ_SLAB = 256
W_STAGE_COLS = 256
ROUTE_LANES = LANES
REC_E, REC_W, REC_RANK = 0, 2, 4
VMEM_LIMIT_BYTES = 58 * 1024 * 1024


def _sigmoid(v):
    return 1.0 / (1.0 + jnp.exp(-v))


def _split_bf16(v):
    hi = v.astype(BF16)
    lo = (v - hi.astype(F32)).astype(BF16)
    return hi, lo


def _dot(a, b):
    return jnp.dot(a, b, preferred_element_type=F32)


def _const_spec(shape):
    nd = len(shape)
    return pl.BlockSpec(shape, lambda *_: (0,) * nd, pipeline_mode=pl.Buffered(1))


def _lane_ids(shape):
    return lax.broadcasted_iota(jnp.int32, shape, 1).astype(F32)


def _mod_kernel(c_ref, w_ref, b_ref, o_ref):
    c = c_ref[...]
    a_hi, a_lo = _split_bf16(c * _sigmoid(c))
    w_hi, w_lo = _split_bf16(w_ref[...])
    o_ref[...] = _dot(a_hi, w_hi) + _dot(a_lo, w_hi) + _dot(a_hi, w_lo) + b_ref[...]


def _mod_call(c_pad, w_ada, b_ada):
    rows, d = c_pad.shape
    n_out = w_ada.shape[1]
    blk = 1024
    return pl.pallas_call(
        _mod_kernel,
        grid=(n_out // blk,),
        in_specs=[
            pl.BlockSpec((rows, d), lambda j: (0, 0)),
            pl.BlockSpec((d, blk), lambda j: (0, j)),
            pl.BlockSpec((1, blk), lambda j: (0, j)),
        ],
        out_specs=pl.BlockSpec((rows, blk), lambda j: (0, j)),
        out_shape=jax.ShapeDtypeStruct((rows, n_out), F32),
        compiler_params=pltpu.CompilerParams(dimension_semantics=("arbitrary",)),
        name="adaln_mod",
    )(c_pad, w_ada, b_ada)


def _depthwise_work(buf_ref, w_ref, out_ref, phase_ref, n_taps, row0, rows, cols):
    first, stop = rows
    max_off = row0 + n_taps - 1
    n_keep = stop - first + (max_off // SUBLANES) * SUBLANES
    phases = sorted({(row0 + k) % SUBLANES for k in range(n_taps)})

    def realign():
        for slot, s in enumerate(phases):
            phase_ref[slot, pl.ds(0, n_keep), :] = buf_ref[pl.ds(first + s, n_keep), cols]

    def row_block(r0):
        acc = None
        for k in range(n_taps):
            q, s = divmod(row0 + k, SUBLANES)
            window = phase_ref[phases.index(s), pl.ds(r0 - first + q * SUBLANES, CONV_ROWS), :]
            term = window * w_ref[k:k + 1, cols]
            acc = term if acc is None else acc + term
        out_ref[pl.ds(r0, CONV_ROWS), cols] = acc

    return [realign] + [functools.partial(row_block, r0) for r0 in range(first, stop, CONV_ROWS)]


def _route(logits, run_ref, live):
    shape = logits.shape
    lane = _lane_ids(shape)
    big = float(4 * ROUTE_LANES)
    neg_inf = -jnp.inf

    def first_argmax(v):
        m = jnp.max(v, axis=-1, keepdims=True)
        return m, jnp.min(jnp.where(v == m, lane, big), axis=-1, keepdims=True)

    is_group = lane < N_GROUPS
    g_max, g_idx = first_argmax(jnp.where(is_group, logits, neg_inf))
    g_prob = 1.0 / jnp.sum(jnp.where(is_group, jnp.exp(logits - g_max), 0.0), axis=-1, keepdims=True)
    lo = N_GROUPS + EXPERTS_PER_GROUP * g_idx
    in_group = (lane >= lo) & (lane < lo + EXPERTS_PER_GROUP)
    e_logits = jnp.where(in_group, logits, neg_inf)
    m1, i1 = first_argmax(e_logits)
    m2, i2 = first_argmax(jnp.where(lane == i1, neg_inf, e_logits))
    t = jnp.exp(m2 - m1)
    p1 = 1.0 / (1.0 + t)
    p2 = t / (1.0 + t)
    e1 = i1 - N_GROUPS
    e2 = i2 - N_GROUPS

    hot1 = lane == e1
    hot2 = lane == e2
    both = jnp.where(hot1 | hot2, 1.0, 0.0)
    n_rows = shape[0]
    earlier = (lax.broadcasted_iota(jnp.int32, (n_rows, n_rows), 0)
               > lax.broadcasted_iota(jnp.int32, (n_rows, n_rows), 1))
    before = _dot(jnp.where(earlier, 1.0, 0.0).astype(BF16), both.astype(BF16)) + run_ref[0:1, :]
    rank1 = jnp.sum(jnp.where(hot1, before, 0.0), axis=-1, keepdims=True)
    rank2 = jnp.sum(jnp.where(hot2, before, 0.0), axis=-1, keepdims=True)
    run_ref[0:1, :] = run_ref[0:1, :] + live * jnp.sum(both, axis=0, keepdims=True)

    rec = jnp.where(lane == REC_E, e1, 0.0)
    rec = jnp.where(lane == REC_E + 1, e2, rec)
    rec = jnp.where(lane == REC_W, g_prob * p1, rec)
    rec = jnp.where(lane == REC_W + 1, g_prob * p2, rec)
    rec = jnp.where(lane == REC_RANK, rank1, rec)
    rec = jnp.where(lane == REC_RANK + 1, rank2, rec)
    return rec


def _pack_bf16_pairs(v_bf16):
    half = v_bf16.shape[1] // 2
    lo = lax.bitcast_convert_type(v_bf16[:, :half].astype(F32), U32)
    hi = lax.bitcast_convert_type(v_bf16[:, half:].astype(F32), U32)
    return (lo >> 16) | hi


def _unpack_bf16_pairs(p_u32):
    lo = lax.bitcast_convert_type(p_u32 << 16, F32).astype(BF16)
    hi = lax.bitcast_convert_type(p_u32 & jnp.uint32(0xFFFF0000), F32).astype(BF16)
    return jnp.concatenate([lo, hi], axis=1)


def _stage_weight_bf16(src_hbm, dst_ref, stage_ref, sem):
    chunk = stage_ref.shape[2]
    n_chunks = src_hbm.shape[1] // chunk

    def cols(j):
        return pl.ds(pl.multiple_of(j * chunk, chunk), chunk)

    def copy(j, slot):
        return pltpu.make_async_copy(src_hbm.at[:, cols(j)], stage_ref.at[slot], sem.at[slot])

    copy(0, 0).start()

    def body(j, carry):
        slot = j % 2

        @pl.when(j + 1 < n_chunks)
        def _next():
            copy(j + 1, 1 - slot).start()

        copy(j, slot).wait()
        dst_ref[:, cols(j)] = stage_ref[slot].astype(BF16)
        return carry

    lax.fori_loop(0, n_chunks, body, 0)


def _mixer_kernel(x_ref, modn_ref, modp_ref, g1_ref, ca_ref, cb_ref, bcb_ref, lng_ref, lnb_ref, g2_ref,
                  wrh_ref, wrl_ref, br_ref, win_hbm, wa_hbm, wb_hbm, wo_hbm,
                  x1_ref, h2p_ref, route_ref, cnt_ref,
                  wu_ref, wcx_ref, ba_ref, sga_ref, sgb_ref, xs_ref, hb_ref, cva_ref, cvb_ref, run_ref,
                  pha_ref, phb_ref, win_ref, wa_ref, wb_ref, wo_ref, stage_ref, stage_sem,
                  *, n_tiles):
    i = pl.program_id(0)
    t_rows, d = x_ref.shape
    new = i % 2
    old = 1 - new

    @pl.when(i == 0)
    def _init():
        for ref in (wu_ref, wcx_ref, ba_ref, sga_ref, sgb_ref, xs_ref, run_ref):
            ref[...] = jnp.zeros(ref.shape, ref.dtype)
        for src, dst in ((win_hbm, win_ref), (wa_hbm, wa_ref), (wb_hbm, wb_ref), (wo_hbm, wo_ref)):
            _stage_weight_bf16(src, dst, stage_ref, stage_sem)

    x = x_ref[...]
    xs_ref[new] = x
    sh1 = modn_ref[0, 0:1, :]
    sc1 = modn_ref[0, 1:2, :]
    h = (x * lax.rsqrt(jnp.mean(x * x, axis=-1, keepdims=True) + RMS_EPS)) * g1_ref[...]
    hb_ref[...] = (h * (1.0 + sc1) + sh1).astype(BF16)

    same_seq = ((i % n_tiles) != 0).astype(F32)
    wu_new, wu_old = wu_ref.at[new], wu_ref.at[old]
    wcx_new, wcx_old = wcx_ref.at[new], wcx_ref.at[old]

    def slab_body(j, carry):
        cols = pl.ds(pl.multiple_of(j * PROJ_SLAB, PROJ_SLAB), PROJ_SLAB)

        def conv_items(rows):
            items = []
            for b in range(PROJ_SLAB // LANES):
                lanes = pl.ds(pl.multiple_of(j * PROJ_SLAB + b * LANES, LANES), LANES)
                short = _depthwise_work(wcx_old, ca_ref, cva_ref, pha_ref.at[b], K_SHORT,
                                        HALO_A - K_SHORT // 2, rows, lanes)
                conf = _depthwise_work(wu_old, cb_ref, cvb_ref, phb_ref.at[b], K_CONF,
                                       HALO_B - K_CONF // 2, rows, lanes)
                items += [lambda f=f, g=g: (f(), g()) for f, g in zip(short, conf)]
            return items

        def proj(g):
            w_cols = pl.ds(pl.multiple_of(g * d + j * PROJ_SLAB, PROJ_SLAB), PROJ_SLAB)
            return _dot(hb_ref[...], win_ref[:, w_cols])

        early = conv_items((0, t_rows - CONV_ROWS))
        n_proj = 7
        z = []
        for g in range(n_proj):
            z.append(proj(g))
            for item in early[g * len(early) // n_proj:(g + 1) * len(early) // n_proj]:
                item()
        ba_ref[new, :, cols] = z[0]
        cx = z[1] * z[2]
        u = z[3] * _sigmoid(z[4])
        sga_ref[new, :, cols] = _sigmoid(z[5])
        sgb_ref[new, :, cols] = _sigmoid(z[6])

        wu_new[pl.ds(HALO_B, t_rows), cols] = u
        wu_new[pl.ds(0, HALO_B), cols] = wu_old[pl.ds(t_rows, HALO_B), cols] * same_seq
        wu_old[pl.ds(HALO_B + t_rows, HALO_B), cols] = u[0:HALO_B] * same_seq
        wcx_new[pl.ds(HALO_A, t_rows), cols] = cx
        wcx_new[pl.ds(0, HALO_A), cols] = wcx_old[pl.ds(t_rows, HALO_A), cols] * same_seq
        wcx_old[pl.ds(HALO_A + t_rows, HALO_A), cols] = cx[0:HALO_A] * same_seq

        for item in conv_items((t_rows - CONV_ROWS, t_rows)):
            item()
        return carry

    lax.fori_loop(0, d // PROJ_SLAB, slab_body, 0)

    y_a = _dot((ba_ref[old] * cva_ref[...]).astype(BF16), wa_ref[...])
    v = cvb_ref[...] + bcb_ref[...]
    mu = jnp.mean(v, axis=-1, keepdims=True)
    vc = v - mu
    var = jnp.mean(vc * vc, axis=-1, keepdims=True)
    v = (vc * lax.rsqrt(var + LN_EPS)) * lng_ref[...] + lnb_ref[...]
    y_b = _dot((v * _sigmoid(v)).astype(BF16), wb_ref[...])
    merged = sga_ref[old] * y_a + sgb_ref[old] * y_b
    mix = _dot(merged.astype(BF16), wo_ref[...])

    gt1 = modp_ref[0, 2:3, :]
    sh2 = modp_ref[0, 3:4, :]
    sc2 = modp_ref[0, 4:5, :]
    x1 = xs_ref[old] + gt1 * mix
    x1_ref[...] = x1
    h2 = (x1 * lax.rsqrt(jnp.mean(x1 * x1, axis=-1, keepdims=True) + RMS_EPS)) * g2_ref[...]
    h2 = h2 * (1.0 + sc2) + sh2
    h2_hi, h2_lo = _split_bf16(h2)
    h2p_ref[...] = _pack_bf16_pairs(h2_hi)
    logits = _dot(h2_hi, wrh_ref[...]) + _dot(h2_lo, wrh_ref[...]) + _dot(h2_hi, wrl_ref[...]) + br_ref[...]
    route_ref[...] = _route(logits, run_ref, (i > 0).astype(F32))
    cnt_ref[...] = jnp.broadcast_to(run_ref[0:1, :], cnt_ref.shape)


def _mixer_call(x2d, mod3, g1, conv_a, conv_b, b_conv_b, ln_g, ln_b, g2,
                w_in, w_a, w_b, w_o, wr_hi, wr_lo, b_r, *, seq_len):
    n_tok, d = x2d.shape
    t = T_MIX
    n_tiles = seq_len // t
    n_chunks = n_tok // t
    last = n_chunks - 1

    def cur(i):
        return jnp.minimum(i, last)

    def prev(i):
        return jnp.maximum(i - 1, 0)

    row_spec_prev = lambda width: pl.BlockSpec((t, width), lambda i: (prev(i), 0))
    vec = lambda a: _const_spec(a.shape)
    hbm = pl.BlockSpec(memory_space=pl.ANY)
    in_specs = [
        pl.BlockSpec((t, d), lambda i: (cur(i), 0)),
        pl.BlockSpec((1,) + mod3.shape[1:], lambda i: (cur(i) // n_tiles, 0, 0)),
        pl.BlockSpec((1,) + mod3.shape[1:], lambda i: (prev(i) // n_tiles, 0, 0)),
        vec(g1), vec(conv_a), vec(conv_b), vec(b_conv_b), vec(ln_g), vec(ln_b), vec(g2),
        vec(wr_hi), vec(wr_lo), vec(b_r),
        hbm, hbm, hbm, hbm,
    ]
    out_specs = [row_spec_prev(d), row_spec_prev(d // 2), row_spec_prev(ROUTE_LANES),
                 pl.BlockSpec((SUBLANES, ROUTE_LANES), lambda i: (0, 0))]
    out_shape = [
        jax.ShapeDtypeStruct((n_tok, d), F32),
        jax.ShapeDtypeStruct((n_tok, d // 2), U32),
        jax.ShapeDtypeStruct((n_tok, ROUTE_LANES), F32),
        jax.ShapeDtypeStruct((SUBLANES, ROUTE_LANES), F32),
    ]
    per_slab = PROJ_SLAB // LANES
    scratch = [
        pltpu.VMEM((2, t + 2 * HALO_B, d), F32),
        pltpu.VMEM((2, t + 2 * HALO_A, d), F32),
        pltpu.VMEM((2, t, d), F32), pltpu.VMEM((2, t, d), F32), pltpu.VMEM((2, t, d), F32),
        pltpu.VMEM((2, t, d), F32),
        pltpu.VMEM((t, d), BF16),
        pltpu.VMEM((t, d), F32), pltpu.VMEM((t, d), F32),
        pltpu.VMEM((SUBLANES, ROUTE_LANES), F32),
        pltpu.VMEM((per_slab, min(K_SHORT, SUBLANES), t + 2 * HALO_A, LANES), F32),
        pltpu.VMEM((per_slab, min(K_CONF, SUBLANES), t + 2 * HALO_B, LANES), F32),
        pltpu.VMEM(w_in.shape, BF16), pltpu.VMEM(w_a.shape, BF16), pltpu.VMEM(w_b.shape, BF16),
        pltpu.VMEM(w_o.shape, BF16),
        pltpu.VMEM((2, d, W_STAGE_COLS), F32), pltpu.SemaphoreType.DMA((2,)),
    ]
    return pl.pallas_call(
        functools.partial(_mixer_kernel, n_tiles=n_tiles),
        grid=(n_chunks + 1,),
        in_specs=in_specs,
        out_specs=out_specs,
        out_shape=out_shape,
        scratch_shapes=scratch,
        compiler_params=pltpu.CompilerParams(
            dimension_semantics=("arbitrary",), vmem_limit_bytes=VMEM_LIMIT_BYTES),
        name="mixer_router",
    )(x2d, mod3, mod3, g1, conv_a, conv_b, b_conv_b, ln_g, ln_b, g2,
      wr_hi, wr_lo, b_r, w_in, w_a, w_b, w_o)


def _segment_layout(counts_row, n_blocks):
    counts = counts_row[:N_EXPERTS].astype(jnp.int32)
    padded = ((counts + ROW_BLK - 1) // ROW_BLK) * ROW_BLK
    pend = jnp.cumsum(padded)
    pstart = pend - padded
    blk_row0 = jnp.arange(n_blocks, dtype=jnp.int32) * ROW_BLK
    blk_e = jnp.minimum(jnp.sum(pend[None, :] <= blk_row0[:, None], axis=1), N_EXPERTS - 1).astype(jnp.int32)
    n_used = (pend[-1] // ROW_BLK).astype(jnp.int32).reshape(1)
    pstart_row = jnp.zeros((1, ROUTE_LANES), F32).at[0, :N_EXPERTS].set(pstart.astype(F32))
    return pstart_row, blk_e, n_used


def _plan_kernel(route_ref, pstart_ref, pos_ref):
    rec = route_ref[...]
    lane = _lane_ids(rec.shape)
    pos = jnp.zeros(rec.shape, F32)
    for k in range(TOP_K):
        e = rec[:, REC_E + k:REC_E + k + 1]
        seg = jnp.sum(jnp.where(lane == e, pstart_ref[...], 0.0), axis=-1, keepdims=True)
        pos = jnp.where(lane == k, seg + rec[:, REC_RANK + k:REC_RANK + k + 1], pos)
    pos_ref[...] = pos.astype(jnp.int32)


def _plan_call(route, pstart_row):
    n_tok = route.shape[0]
    t = T_PLAN
    return pl.pallas_call(
        _plan_kernel,
        grid=(n_tok // t,),
        in_specs=[pl.BlockSpec((t, ROUTE_LANES), lambda i: (i, 0)),
                  pl.BlockSpec((1, ROUTE_LANES), lambda i: (0, 0))],
        out_specs=pl.BlockSpec((t, ROUTE_LANES), lambda i: (i, 0)),
        out_shape=jax.ShapeDtypeStruct((n_tok, ROUTE_LANES), jnp.int32),
        compiler_params=pltpu.CompilerParams(dimension_semantics=("arbitrary",)),
        name="moe_plan",
    )(route, pstart_row)


def _scatter_kernel(pos_ref, rows_ref, buf_in_hbm, buf_hbm, sem):
    del buf_in_hbm
    t_rows = rows_ref.shape[0]

    for r in range(t_rows):
        for k in range(TOP_K):
            pltpu.make_async_copy(rows_ref.at[pl.ds(r, 1), :],
                                  buf_hbm.at[pl.ds(pos_ref[0, k, r], 1), :], sem).start()
    for _ in range(TOP_K):
        pltpu.make_async_copy(rows_ref, buf_hbm.at[pl.ds(0, t_rows), :], sem).wait()


def _scatter_call(pos3, rows, n_rows):
    n_tok, width = rows.shape
    t = T_MOVE
    buf0 = jnp.zeros((n_rows, width), rows.dtype)
    return pl.pallas_call(
        _scatter_kernel,
        grid=(n_tok // t,),
        in_specs=[
            pl.BlockSpec((1, TOP_K, t), lambda i: (i, 0, 0), memory_space=pltpu.SMEM),
            pl.BlockSpec((t, width), lambda i: (i, 0)),
            pl.BlockSpec(memory_space=pl.ANY),
        ],
        out_specs=pl.BlockSpec(memory_space=pl.ANY),
        out_shape=jax.ShapeDtypeStruct((n_rows, width), rows.dtype),
        scratch_shapes=[pltpu.SemaphoreType.DMA(())],
        input_output_aliases={2: 0},
        compiler_params=pltpu.CompilerParams(dimension_semantics=("arbitrary",)),
        name="moe_scatter",
    )(pos3, rows, buf0)


def _expert_kernel(blk_e_ref, n_used_ref, xs_ref, wg_ref, wu_ref, wd_ref, y_ref, wg_bf, wu_bf, wd_bf):
    i = pl.program_id(0)
    n_used = n_used_ref[0]
    new_expert = jnp.logical_or(i == 0, blk_e_ref[i] != blk_e_ref[jnp.maximum(i - 1, 0)])

    @pl.when(jnp.logical_and(i < n_used, new_expert))
    def _convert_weights():
        wg_bf[...] = wg_ref[0].astype(BF16)
        wu_bf[...] = wu_ref[0].astype(BF16)
        wd_bf[...] = wd_ref[0].astype(BF16)

    @pl.when(i < n_used)
    def _compute():
        xb = _unpack_bf16_pairs(xs_ref[...])
        a = _dot(xb, wg_bf[...])
        u = _dot(xb, wu_bf[...])
        hid = ((a * _sigmoid(a)) * u).astype(BF16)
        y_ref[...] = _dot(hid, wd_bf[...])

    @pl.when(i >= n_used)
    def _unused_block():
        y_ref[...] = jnp.zeros(y_ref.shape, y_ref.dtype)


def _expert_call(xs, blk_e, n_used, w_gate, w_up, w_down):
    n_rows, half = xs.shape
    d = 2 * half
    n_blocks = n_rows // ROW_BLK
    d_e = w_gate.shape[-1]

    def used(i, n_used_ref):
        return jnp.minimum(i, n_used_ref[0] - 1)

    grid_spec = pltpu.PrefetchScalarGridSpec(
        num_scalar_prefetch=2,
        grid=(n_blocks,),
        in_specs=[
            pl.BlockSpec((ROW_BLK, half), lambda i, be, nu: (used(i, nu), 0)),
            pl.BlockSpec((1, d, d_e), lambda i, be, nu: (be[used(i, nu)], 0, 0)),
            pl.BlockSpec((1, d, d_e), lambda i, be, nu: (be[used(i, nu)], 0, 0)),
            pl.BlockSpec((1, d_e, d), lambda i, be, nu: (be[used(i, nu)], 0, 0)),
        ],
        out_specs=pl.BlockSpec((ROW_BLK, d), lambda i, be, nu: (i, 0)),
        scratch_shapes=[pltpu.VMEM((d, d_e), BF16), pltpu.VMEM((d, d_e), BF16), pltpu.VMEM((d_e, d), BF16)],
    )
    return pl.pallas_call(
        _expert_kernel,
        grid_spec=grid_spec,
        out_shape=jax.ShapeDtypeStruct((n_rows, d), F32),
        compiler_params=pltpu.CompilerParams(
            dimension_semantics=("arbitrary",), vmem_limit_bytes=VMEM_LIMIT_BYTES),
        name="moe_experts",
    )(blk_e, n_used, xs, w_gate, w_up, w_down)


def _row_gather_start(src_hbm, idx_of_row, dst, sem, n_rows):
    for r in range(n_rows):
        pltpu.make_async_copy(src_hbm.at[pl.ds(idx_of_row(r), 1), :], dst.at[pl.ds(r, 1), :], sem).start()


def _row_gather_wait(src_hbm, dst, sem, n_rows):
    pltpu.make_async_copy(src_hbm.at[pl.ds(0, n_rows), :], dst, sem).wait()


def _combine_kernel(posc_ref, posn_ref, x1_ref, route_ref, mod_ref, gf_ref, y_hbm, o_ref, ybuf, sem,
                    *, n_steps, final_norm):
    i = pl.program_id(0)
    slot = i % 2
    t_rows = x1_ref.shape[0]

    def start(pos_ref, s):
        for k in range(TOP_K):
            _row_gather_start(y_hbm, lambda r, k=k: pos_ref[0, k, r], ybuf.at[s, k], sem.at[s], t_rows)

    @pl.when(i == 0)
    def _prime():
        start(posc_ref, 0)

    @pl.when(i + 1 < n_steps)
    def _prefetch():
        start(posn_ref, 1 - slot)

    for k in range(TOP_K):
        _row_gather_wait(y_hbm, ybuf.at[slot, k], sem.at[slot], t_rows)
    rec = route_ref[...]
    moe = ybuf[slot, 0] * rec[:, REC_W:REC_W + 1] + ybuf[slot, 1] * rec[:, REC_W + 1:REC_W + 2]
    x2 = x1_ref[...] + mod_ref[0, 5:6, :] * moe
    if final_norm:
        x2 = (x2 * lax.rsqrt(jnp.mean(x2 * x2, axis=-1, keepdims=True) + RMS_EPS)) * gf_ref[...]
    o_ref[...] = x2


def _combine_call(x1, route, mod3, g_final, y_rows, pos3, *, seq_len, final_norm):
    n_tok, d = x1.shape
    t = T_MOVE
    n_steps = n_tok // t
    tiles_per_seq = seq_len // t
    last = n_steps - 1
    return pl.pallas_call(
        functools.partial(_combine_kernel, n_steps=n_steps, final_norm=final_norm),
        grid=(n_steps,),
        in_specs=[
            pl.BlockSpec((1, TOP_K, t), lambda i: (i, 0, 0), memory_space=pltpu.SMEM),
            pl.BlockSpec((1, TOP_K, t), lambda i: (jnp.minimum(i + 1, last), 0, 0), memory_space=pltpu.SMEM),
            pl.BlockSpec((t, d), lambda i: (i, 0)),
            pl.BlockSpec((t, ROUTE_LANES), lambda i: (i, 0)),
            pl.BlockSpec((1,) + mod3.shape[1:], lambda i: (i // tiles_per_seq, 0, 0)),
            pl.BlockSpec((1, d), lambda i: (0, 0)),
            pl.BlockSpec(memory_space=pl.ANY),
        ],
        out_specs=pl.BlockSpec((t, d), lambda i: (i, 0)),
        out_shape=jax.ShapeDtypeStruct((n_tok, d), F32),
        scratch_shapes=[pltpu.VMEM((2, TOP_K, t, d), F32), pltpu.SemaphoreType.DMA((2,))],
        compiler_params=pltpu.CompilerParams(
            dimension_semantics=("arbitrary",), vmem_limit_bytes=VMEM_LIMIT_BYTES),
        name="moe_combine",
    )(pos3, pos3, x1, route, mod3, g_final, y_rows)


def kernel(x, c, w_ada, b_ada, g_norm1, w_in, conv_a, w_a_out, conv_b, b_conv_b, ln_conv_g, ln_conv_b,
           w_b_out, w_o, g_norm2, w_router_g, b_router_g, w_router_e, b_router_e, w_gate, w_up, w_down,
           g_final):
    bsz, seq_len, d = x.shape
    depth = w_ada.shape[0]
    n_tok = bsz * seq_len
    n_assign = n_tok * TOP_K
    assert seq_len % T_MIX == 0 and seq_len % T_MOVE == 0 and d % (2 * LANES) == 0
    assert n_tok % T_PLAN == 0
    assert n_assign % ROW_BLK == 0
    assert N_GROUPS + N_EXPERTS <= ROUTE_LANES
    n_rows = n_assign + N_EXPERTS * ROW_BLK
    n_blocks = n_rows // ROW_BLK

    c_pad = jnp.zeros((SUBLANES, d), F32).at[:bsz].set(c)
    xt = x.reshape(n_tok, d)
    row = lambda a: a.reshape(1, -1)
    for l in range(depth):
        mod = _mod_call(c_pad, w_ada[l], row(b_ada[l]))
        mod3 = mod[:bsz].reshape(bsz, 6, d)
        w_r = jnp.zeros((d, ROUTE_LANES), F32)
        w_r = w_r.at[:, :N_GROUPS].set(w_router_g[l]).at[:, N_GROUPS:N_GROUPS + N_EXPERTS].set(w_router_e[l])
        b_r = jnp.zeros((1, ROUTE_LANES), F32)
        b_r = b_r.at[0, :N_GROUPS].set(b_router_g[l]).at[0, N_GROUPS:N_GROUPS + N_EXPERTS].set(b_router_e[l])
        wr_hi, wr_lo = _split_bf16(w_r)
        x1, h2p, route, cnt = _mixer_call(
            xt, mod3, row(g_norm1[l]), conv_a[l], conv_b[l], row(b_conv_b[l]), row(ln_conv_g[l]),
            row(ln_conv_b[l]), row(g_norm2[l]),
            w_in[l], w_a_out[l], w_b_out[l], w_o[l], wr_hi, wr_lo, b_r, seq_len=seq_len)
        pstart_row, blk_e, n_used = _segment_layout(cnt[0], n_blocks)
        pos = _plan_call(route, pstart_row)
        pos3 = pos[:, :TOP_K].reshape(n_tok // T_MOVE, T_MOVE, TOP_K).transpose(0, 2, 1)
        xs = _scatter_call(pos3, h2p, n_rows)
        y_rows = _expert_call(xs, blk_e, n_used, w_gate[l], w_up[l], w_down[l])
        xt = _combine_call(x1, route, mod3, row(g_final), y_rows, pos3, seq_len=seq_len,
                           final_norm=(l == depth - 1))
    return xt.reshape(bsz, seq_len, d)
```

```python
import functools

import jax
import jax.numpy as jnp
from jax import lax
from jax.experimental import pallas as pl
from jax.experimental.pallas import tpu as pltpu

F32 = jnp.float32
BF16 = jnp.bfloat16
U32 = jnp.uint32

N_GROUPS = 4
EXPERTS_PER_GROUP = 8
N_EXPERTS = N_GROUPS * EXPERTS_PER_GROUP
TOP_K = 2
K_SHORT = 3
K_CONF = 31
RMS_EPS = 1e-6
LN_EPS = 1e-5

LANES = 128
SUBLANES = 8
T_MIX = 256
HALO_B = 16
HALO_A = 8
CONV_ROWS = 64
ROW_BLK = 256
T_MOVE = 256
T_PLAN = 2048
W_STAGE_COLS = 512
ROUTE_LANES = LANES
REC_E, REC_W, REC_RANK = 0, 2, 4
VMEM_LIMIT_BYTES = 56 * 1024 * 1024


def _sigmoid(v):
    return 1.0 / (1.0 + jnp.exp(-v))


def _split_bf16(v):
    hi = v.astype(BF16)
    lo = (v - hi.astype(F32)).astype(BF16)
    return hi, lo


def _dot(a, b):
    return jnp.dot(a, b, preferred_element_type=F32)


def _const_spec(shape):
    nd = len(shape)
    return pl.BlockSpec(shape, lambda *_: (0,) * nd, pipeline_mode=pl.Buffered(1))


def _lane_ids(shape):
    return lax.broadcasted_iota(jnp.int32, shape, 1).astype(F32)


def _mod_kernel(c_ref, w_ref, b_ref, o_ref):
    c = c_ref[...]
    a_hi, a_lo = _split_bf16(c * _sigmoid(c))
    w_hi, w_lo = _split_bf16(w_ref[...])
    o_ref[...] = _dot(a_hi, w_hi) + _dot(a_lo, w_hi) + _dot(a_hi, w_lo) + b_ref[...]


def _mod_call(c_pad, w_ada, b_ada):
    rows, d = c_pad.shape
    n_out = w_ada.shape[1]
    blk = 1024
    return pl.pallas_call(
        _mod_kernel,
        grid=(n_out // blk,),
        in_specs=[
            pl.BlockSpec((rows, d), lambda j: (0, 0)),
            pl.BlockSpec((d, blk), lambda j: (0, j)),
            pl.BlockSpec((1, blk), lambda j: (0, j)),
        ],
        out_specs=pl.BlockSpec((rows, blk), lambda j: (0, j)),
        out_shape=jax.ShapeDtypeStruct((rows, n_out), F32),
        compiler_params=pltpu.CompilerParams(dimension_semantics=("arbitrary",)),
        name="adaln_mod",
    )(c_pad, w_ada, b_ada)


def _depthwise_lane_block(buf_ref, w_ref, out_ref, phase_ref, n_taps, row0, rows, cb):
    first, stop = rows
    max_off = row0 + n_taps - 1
    n_keep = stop - first + (max_off // SUBLANES) * SUBLANES
    cols = slice(cb * LANES, (cb + 1) * LANES)
    for s in sorted({(row0 + k) % SUBLANES for k in range(n_taps)}):
        phase_ref[s, pl.ds(0, n_keep), :] = buf_ref[pl.ds(first + s, n_keep), cols]
    row_blocks = range(first, stop, CONV_ROWS)
    acc = [None] * len(row_blocks)
    for k in range(n_taps):
        q, s = divmod(row0 + k, SUBLANES)
        w_k = jnp.broadcast_to(w_ref[k:k + 1, cols], (CONV_ROWS, LANES))
        for j, r0 in enumerate(row_blocks):
            term = phase_ref[s, pl.ds(r0 - first + q * SUBLANES, CONV_ROWS), :] * w_k
            acc[j] = term if acc[j] is None else acc[j] + term
    for j, r0 in enumerate(row_blocks):
        out_ref[pl.ds(r0, CONV_ROWS), cols] = acc[j]


def _route(logits, run_ref, live):
    shape = logits.shape
    lane = _lane_ids(shape)
    big = float(4 * ROUTE_LANES)
    neg_inf = -jnp.inf

    def first_argmax(v):
        m = jnp.max(v, axis=-1, keepdims=True)
        return m, jnp.min(jnp.where(v == m, lane, big), axis=-1, keepdims=True)

    is_group = lane < N_GROUPS
    g_max, g_idx = first_argmax(jnp.where(is_group, logits, neg_inf))
    g_prob = 1.0 / jnp.sum(jnp.where(is_group, jnp.exp(logits - g_max), 0.0), axis=-1, keepdims=True)
    lo = N_GROUPS + EXPERTS_PER_GROUP * g_idx
    in_group = (lane >= lo) & (lane < lo + EXPERTS_PER_GROUP)
    e_logits = jnp.where(in_group, logits, neg_inf)
    m1, i1 = first_argmax(e_logits)
    m2, i2 = first_argmax(jnp.where(lane == i1, neg_inf, e_logits))
    t = jnp.exp(m2 - m1)
    p1 = 1.0 / (1.0 + t)
    p2 = t / (1.0 + t)
    e1 = i1 - N_GROUPS
    e2 = i2 - N_GROUPS

    hot1 = lane == e1
    hot2 = lane == e2
    both = jnp.where(hot1 | hot2, 1.0, 0.0)
    n_rows = shape[0]
    earlier = (lax.broadcasted_iota(jnp.int32, (n_rows, n_rows), 0)
               > lax.broadcasted_iota(jnp.int32, (n_rows, n_rows), 1))
    before = _dot(jnp.where(earlier, 1.0, 0.0).astype(BF16), both.astype(BF16)) + run_ref[0:1, :]
    rank1 = jnp.sum(jnp.where(hot1, before, 0.0), axis=-1, keepdims=True)
    rank2 = jnp.sum(jnp.where(hot2, before, 0.0), axis=-1, keepdims=True)
    run_ref[0:1, :] = run_ref[0:1, :] + live * jnp.sum(both, axis=0, keepdims=True)

    rec = jnp.where(lane == REC_E, e1, 0.0)
    rec = jnp.where(lane == REC_E + 1, e2, rec)
    rec = jnp.where(lane == REC_W, g_prob * p1, rec)
    rec = jnp.where(lane == REC_W + 1, g_prob * p2, rec)
    rec = jnp.where(lane == REC_RANK, rank1, rec)
    rec = jnp.where(lane == REC_RANK + 1, rank2, rec)
    return rec


def _pack_bf16_pairs(v_bf16):
    half = v_bf16.shape[1] // 2
    lo = lax.bitcast_convert_type(v_bf16[:, :half].astype(F32), U32)
    hi = lax.bitcast_convert_type(v_bf16[:, half:].astype(F32), U32)
    return (lo >> 16) | hi


def _unpack_bf16_pairs(p_u32):
    lo = lax.bitcast_convert_type(p_u32 << 16, F32).astype(BF16)
    hi = lax.bitcast_convert_type(p_u32 & jnp.uint32(0xFFFF0000), F32).astype(BF16)
    return jnp.concatenate([lo, hi], axis=1)


def _stage_weight_bf16(src_hbm, dst_ref, stage_ref, sem):
    chunk = stage_ref.shape[2]
    n_chunks = src_hbm.shape[1] // chunk

    def cols(j):
        return pl.ds(pl.multiple_of(j * chunk, chunk), chunk)

    def copy(j, slot):
        return pltpu.make_async_copy(src_hbm.at[:, cols(j)], stage_ref.at[slot], sem.at[slot])

    copy(0, 0).start()

    def body(j, carry):
        slot = j % 2

        @pl.when(j + 1 < n_chunks)
        def _next():
            copy(j + 1, 1 - slot).start()

        copy(j, slot).wait()
        dst_ref[:, cols(j)] = stage_ref[slot].astype(BF16)
        return carry

    lax.fori_loop(0, n_chunks, body, 0)


def _mixer_kernel(x_ref, modn_ref, modp_ref, g1_ref, ca_ref, cb_ref, bcb_ref, lng_ref, lnb_ref, g2_ref,
                  wrh_ref, wrl_ref, br_ref, win_hbm, wa_hbm, wb_hbm, wo_hbm,
                  x1_ref, h2p_ref, route_ref, cnt_ref,
                  wu_ref, wcx_ref, ba_ref, sga_ref, sgb_ref, xs_ref, cva_ref, cvb_ref, run_ref, phase_ref,
                  win_ref, wa_ref, wb_ref, wo_ref, stage_ref, stage_sem,
                  *, n_tiles):
    i = pl.program_id(0)
    t_rows, d = x_ref.shape

    @pl.when(i == 0)
    def _init():
        for ref in (wu_ref, wcx_ref, ba_ref, sga_ref, sgb_ref, xs_ref, run_ref):
            ref[...] = jnp.zeros(ref.shape, ref.dtype)
        for src, dst in ((win_hbm, win_ref), (wa_hbm, wa_ref), (wb_hbm, wb_ref), (wo_hbm, wo_ref)):
            _stage_weight_bf16(src, dst, stage_ref, stage_sem)

    x = x_ref[...]
    sh1 = modn_ref[0, 0:1, :]
    sc1 = modn_ref[0, 1:2, :]
    h = (x * lax.rsqrt(jnp.mean(x * x, axis=-1, keepdims=True) + RMS_EPS)) * g1_ref[...]
    hb = (h * (1.0 + sc1) + sh1).astype(BF16)

    def proj(g):
        return _dot(hb, win_ref[:, g * d:(g + 1) * d])

    def conv_work(rows):
        items = []
        for cb in range(d // LANES):
            items.append(functools.partial(_depthwise_lane_block, wcx_ref, ca_ref, cva_ref,
                                           phase_ref.at[2], K_SHORT, HALO_A - K_SHORT // 2, rows, cb))
            items.append(functools.partial(_depthwise_lane_block, wu_ref, cb_ref, cvb_ref,
                                           phase_ref.at[cb % 2], K_CONF, HALO_B - K_CONF // 2, rows, cb))
        return items

    early = conv_work((0, t_rows - CONV_ROWS))
    n_proj = 7
    z = []
    for g in range(n_proj):
        z.append(proj(g))
        for item in early[g * len(early) // n_proj:(g + 1) * len(early) // n_proj]:
            item()
    b_a = z[0]
    cx = z[1] * z[2]
    u = z[3] * _sigmoid(z[4])
    sg_a = _sigmoid(z[5])
    sg_b = _sigmoid(z[6])

    same_seq = ((i % n_tiles) != 0).astype(F32)
    wu_ref[pl.ds(HALO_B + t_rows, HALO_B), :] = u[0:HALO_B] * same_seq
    wcx_ref[pl.ds(HALO_A + t_rows, HALO_A), :] = cx[0:HALO_A] * same_seq

    for item in conv_work((t_rows - CONV_ROWS, t_rows)):
        item()

    y_a = _dot((ba_ref[...] * cva_ref[...]).astype(BF16), wa_ref[...])
    v = cvb_ref[...] + bcb_ref[...]
    mu = jnp.mean(v, axis=-1, keepdims=True)
    vc = v - mu
    var = jnp.mean(vc * vc, axis=-1, keepdims=True)
    v = (vc * lax.rsqrt(var + LN_EPS)) * lng_ref[...] + lnb_ref[...]
    y_b = _dot((v * _sigmoid(v)).astype(BF16), wb_ref[...])
    merged = sga_ref[...] * y_a + sgb_ref[...] * y_b
    mix = _dot(merged.astype(BF16), wo_ref[...])

    gt1 = modp_ref[0, 2:3, :]
    sh2 = modp_ref[0, 3:4, :]
    sc2 = modp_ref[0, 4:5, :]
    x1 = xs_ref[...] + gt1 * mix
    x1_ref[...] = x1
    h2 = (x1 * lax.rsqrt(jnp.mean(x1 * x1, axis=-1, keepdims=True) + RMS_EPS)) * g2_ref[...]
    h2 = h2 * (1.0 + sc2) + sh2
    h2_hi, h2_lo = _split_bf16(h2)
    h2p_ref[...] = _pack_bf16_pairs(h2_hi)
    logits = _dot(h2_hi, wrh_ref[...]) + _dot(h2_lo, wrh_ref[...]) + _dot(h2_hi, wrl_ref[...]) + br_ref[...]
    route_ref[...] = _route(logits, run_ref, (i > 0).astype(F32))
    cnt_ref[...] = jnp.broadcast_to(run_ref[0:1, :], cnt_ref.shape)

    tail_u = wu_ref[pl.ds(t_rows, HALO_B), :]
    wu_ref[pl.ds(0, HALO_B), :] = tail_u * same_seq
    wu_ref[pl.ds(HALO_B, t_rows), :] = u
    tail_cx = wcx_ref[pl.ds(t_rows, HALO_A), :]
    wcx_ref[pl.ds(0, HALO_A), :] = tail_cx * same_seq
    wcx_ref[pl.ds(HALO_A, t_rows), :] = cx
    ba_ref[...] = b_a
    sga_ref[...] = sg_a
    sgb_ref[...] = sg_b
    xs_ref[...] = x


def _mixer_call(x2d, mod3, g1, conv_a, conv_b, b_conv_b, ln_g, ln_b, g2,
                w_in, w_a, w_b, w_o, wr_hi, wr_lo, b_r, *, seq_len):
    n_tok, d = x2d.shape
    t = T_MIX
    n_tiles = seq_len // t
    n_chunks = n_tok // t
    last = n_chunks - 1

    def cur(i):
        return jnp.minimum(i, last)

    def prev(i):
        return jnp.maximum(i - 1, 0)

    row_spec_prev = lambda width: pl.BlockSpec((t, width), lambda i: (prev(i), 0))
    vec = lambda a: _const_spec(a.shape)
    hbm = pl.BlockSpec(memory_space=pl.ANY)
    in_specs = [
        pl.BlockSpec((t, d), lambda i: (cur(i), 0)),
        pl.BlockSpec((1,) + mod3.shape[1:], lambda i: (cur(i) // n_tiles, 0, 0)),
        pl.BlockSpec((1,) + mod3.shape[1:], lambda i: (prev(i) // n_tiles, 0, 0)),
        vec(g1), vec(conv_a), vec(conv_b), vec(b_conv_b), vec(ln_g), vec(ln_b), vec(g2),
        vec(wr_hi), vec(wr_lo), vec(b_r),
        hbm, hbm, hbm, hbm,
    ]
    out_specs = [row_spec_prev(d), row_spec_prev(d // 2), row_spec_prev(ROUTE_LANES),
                 pl.BlockSpec((SUBLANES, ROUTE_LANES), lambda i: (0, 0))]
    out_shape = [
        jax.ShapeDtypeStruct((n_tok, d), F32),
        jax.ShapeDtypeStruct((n_tok, d // 2), U32),
        jax.ShapeDtypeStruct((n_tok, ROUTE_LANES), F32),
        jax.ShapeDtypeStruct((SUBLANES, ROUTE_LANES), F32),
    ]
    scratch = [
        pltpu.VMEM((t + 2 * HALO_B, d), F32),
        pltpu.VMEM((t + 2 * HALO_A, d), F32),
        pltpu.VMEM((t, d), F32), pltpu.VMEM((t, d), F32), pltpu.VMEM((t, d), F32), pltpu.VMEM((t, d), F32),
        pltpu.VMEM((t, d), F32), pltpu.VMEM((t, d), F32),
        pltpu.VMEM((SUBLANES, ROUTE_LANES), F32),
        pltpu.VMEM((3, SUBLANES, t + 2 * HALO_B, LANES), F32),
        pltpu.VMEM(w_in.shape, BF16), pltpu.VMEM(w_a.shape, BF16), pltpu.VMEM(w_b.shape, BF16),
        pltpu.VMEM(w_o.shape, BF16),
        pltpu.VMEM((2, d, W_STAGE_COLS), F32), pltpu.SemaphoreType.DMA((2,)),
    ]
    return pl.pallas_call(
        functools.partial(_mixer_kernel, n_tiles=n_tiles),
        grid=(n_chunks + 1,),
        in_specs=in_specs,
        out_specs=out_specs,
        out_shape=out_shape,
        scratch_shapes=scratch,
        compiler_params=pltpu.CompilerParams(
            dimension_semantics=("arbitrary",), vmem_limit_bytes=VMEM_LIMIT_BYTES),
        name="mixer_router",
    )(x2d, mod3, mod3, g1, conv_a, conv_b, b_conv_b, ln_g, ln_b, g2,
      wr_hi, wr_lo, b_r, w_in, w_a, w_b, w_o)


def _segment_layout(counts_row):
    counts = counts_row[:N_EXPERTS].astype(jnp.int32)
    nblk = (counts + ROW_BLK - 1) // ROW_BLK
    blk_end = jnp.cumsum(nblk)
    blk0 = blk_end - nblk
    n_used = blk_end[-1:]
    pstart_row = jnp.zeros((1, ROUTE_LANES), F32).at[0, :N_EXPERTS].set((blk0 * ROW_BLK).astype(F32))
    return pstart_row, blk0.astype(jnp.int32), nblk.astype(jnp.int32), n_used.astype(jnp.int32)


def _plan_kernel(route_ref, pstart_ref, pos_ref):
    rec = route_ref[...]
    lane = _lane_ids(rec.shape)
    pos = jnp.zeros(rec.shape, F32)
    for k in range(TOP_K):
        e = rec[:, REC_E + k:REC_E + k + 1]
        seg = jnp.sum(jnp.where(lane == e, pstart_ref[...], 0.0), axis=-1, keepdims=True)
        pos = jnp.where(lane == k, seg + rec[:, REC_RANK + k:REC_RANK + k + 1], pos)
    pos_ref[...] = pos.astype(jnp.int32)


def _plan_call(route, pstart_row):
    n_tok = route.shape[0]
    t = T_PLAN
    return pl.pallas_call(
        _plan_kernel,
        grid=(n_tok // t,),
        in_specs=[pl.BlockSpec((t, ROUTE_LANES), lambda i: (i, 0)),
                  pl.BlockSpec((1, ROUTE_LANES), lambda i: (0, 0))],
        out_specs=pl.BlockSpec((t, ROUTE_LANES), lambda i: (i, 0)),
        out_shape=jax.ShapeDtypeStruct((n_tok, ROUTE_LANES), jnp.int32),
        compiler_params=pltpu.CompilerParams(dimension_semantics=("arbitrary",)),
        name="moe_plan",
    )(route, pstart_row)


def _scatter_kernel(pos_ref, rows_ref, buf_in_hbm, buf_hbm, sem):
    del buf_in_hbm
    t_rows = rows_ref.shape[0]

    for r in range(t_rows):
        for k in range(TOP_K):
            pltpu.make_async_copy(rows_ref.at[pl.ds(r, 1), :],
                                  buf_hbm.at[pl.ds(pos_ref[0, k, r], 1), :], sem).start(priority=k)
    for _ in range(TOP_K):
        pltpu.make_async_copy(rows_ref, buf_hbm.at[pl.ds(0, t_rows), :], sem).wait()


def _scatter_call(pos3, rows, n_rows):
    n_tok, width = rows.shape
    t = T_MOVE
    buf0 = jnp.zeros((n_rows, width), rows.dtype)
    return pl.pallas_call(
        _scatter_kernel,
        grid=(n_tok // t,),
        in_specs=[
            pl.BlockSpec((1, TOP_K, t), lambda i: (i, 0, 0), memory_space=pltpu.SMEM),
            pl.BlockSpec((t, width), lambda i: (i, 0)),
            pl.BlockSpec(memory_space=pl.ANY),
        ],
        out_specs=pl.BlockSpec(memory_space=pl.ANY),
        out_shape=jax.ShapeDtypeStruct((n_rows, width), rows.dtype),
        scratch_shapes=[pltpu.SemaphoreType.DMA(())],
        input_output_aliases={2: 0},
        compiler_params=pltpu.CompilerParams(dimension_semantics=("arbitrary",)),
        name="moe_scatter",
    )(pos3, rows, buf0)


def _expert_kernel(blk0_ref, nblk_ref, n_used_ref, wg_ref, wu_ref, wd_ref, xs_hbm, y_hbm,
                   wg_bf, wu_bf, wd_bf, xbuf, ybuf, in_sem, out_sem):
    e = pl.program_id(0)
    n_blk = nblk_ref[e]
    row0 = blk0_ref[e] * ROW_BLK

    def rows(j):
        return pl.ds(pl.multiple_of(row0 + j * ROW_BLK, ROW_BLK), ROW_BLK)

    def load(j, slot):
        return pltpu.make_async_copy(xs_hbm.at[rows(j), :], xbuf.at[slot], in_sem.at[slot])

    def store(j, slot):
        return pltpu.make_async_copy(ybuf.at[slot], y_hbm.at[rows(j), :], out_sem.at[slot])

    @pl.when(n_blk > 0)
    def _run_expert():
        load(0, 0).start()
        wg_bf[...] = wg_ref[0].astype(BF16)
        wu_bf[...] = wu_ref[0].astype(BF16)
        wd_bf[...] = wd_ref[0].astype(BF16)

        def body(j, carry):
            slot = j % 2

            @pl.when(j + 1 < n_blk)
            def _next():
                load(j + 1, 1 - slot).start()

            load(j, slot).wait()

            @pl.when(j >= 2)
            def _window_free():
                store(j - 2, slot).wait()

            xb = _unpack_bf16_pairs(xbuf[slot])
            a = _dot(xb, wg_bf[...])
            u = _dot(xb, wu_bf[...])
            hid = ((a * _sigmoid(a)) * u).astype(BF16)
            ybuf[slot] = _dot(hid, wd_bf[...])
            store(j, slot).start()
            return carry

        lax.fori_loop(0, n_blk, body, 0)

        @pl.when(n_blk >= 2)
        def _drain_older():
            store(n_blk - 2, n_blk % 2).wait()

        store(n_blk - 1, (n_blk - 1) % 2).wait()

    @pl.when(e == pl.num_programs(0) - 1)
    def _zero_unused_blocks():
        ybuf[0] = jnp.zeros(ybuf.shape[1:], ybuf.dtype)

        def fill(j, carry):
            dst = y_hbm.at[pl.ds(pl.multiple_of(j * ROW_BLK, ROW_BLK), ROW_BLK), :]
            copy = pltpu.make_async_copy(ybuf.at[0], dst, out_sem.at[0])
            copy.start()
            copy.wait()
            return carry

        lax.fori_loop(n_used_ref[0], y_hbm.shape[0] // ROW_BLK, fill, 0)


def _expert_call(xs, blk0, nblk, n_used, w_gate, w_up, w_down):
    n_rows, half = xs.shape
    d = 2 * half
    n_experts, _, d_e = w_gate.shape
    hbm = pl.BlockSpec(memory_space=pl.ANY)
    grid_spec = pltpu.PrefetchScalarGridSpec(
        num_scalar_prefetch=3,
        grid=(n_experts,),
        in_specs=[
            pl.BlockSpec((1, d, d_e), lambda e, *_: (e, 0, 0)),
            pl.BlockSpec((1, d, d_e), lambda e, *_: (e, 0, 0)),
            pl.BlockSpec((1, d_e, d), lambda e, *_: (e, 0, 0)),
            hbm,
        ],
        out_specs=hbm,
        scratch_shapes=[
            pltpu.VMEM((d, d_e), BF16), pltpu.VMEM((d, d_e), BF16), pltpu.VMEM((d_e, d), BF16),
            pltpu.VMEM((2, ROW_BLK, half), xs.dtype), pltpu.VMEM((2, ROW_BLK, d), F32),
            pltpu.SemaphoreType.DMA((2,)), pltpu.SemaphoreType.DMA((2,)),
        ],
    )
    return pl.pallas_call(
        _expert_kernel,
        grid_spec=grid_spec,
        out_shape=jax.ShapeDtypeStruct((n_rows, d), F32),
        compiler_params=pltpu.CompilerParams(
            dimension_semantics=("arbitrary",), vmem_limit_bytes=VMEM_LIMIT_BYTES),
        name="moe_experts",
    )(blk0, nblk, n_used, w_gate, w_up, w_down, xs)


def _row_gather_start(src_hbm, idx_of_row, dst, sem, n_rows):
    for r in range(n_rows):
        pltpu.make_async_copy(src_hbm.at[pl.ds(idx_of_row(r), 1), :], dst.at[pl.ds(r, 1), :],
                              sem).start(priority=r % 2)


def _row_gather_wait(src_hbm, dst, sem, n_rows):
    pltpu.make_async_copy(src_hbm.at[pl.ds(0, n_rows), :], dst, sem).wait()


def _combine_kernel(posc_ref, posn_ref, x1_ref, route_ref, mod_ref, gf_ref, y_hbm, o_ref, ybuf, sem,
                    *, n_steps, final_norm):
    i = pl.program_id(0)
    slot = i % 2
    t_rows = x1_ref.shape[0]

    def start(pos_ref, s):
        for k in range(TOP_K):
            _row_gather_start(y_hbm, lambda r, k=k: pos_ref[0, k, r], ybuf.at[s, k], sem.at[s], t_rows)

    @pl.when(i == 0)
    def _prime():
        start(posc_ref, 0)

    @pl.when(i + 1 < n_steps)
    def _prefetch():
        start(posn_ref, 1 - slot)

    for k in range(TOP_K):
        _row_gather_wait(y_hbm, ybuf.at[slot, k], sem.at[slot], t_rows)
    rec = route_ref[...]
    moe = ybuf[slot, 0] * rec[:, REC_W:REC_W + 1] + ybuf[slot, 1] * rec[:, REC_W + 1:REC_W + 2]
    x2 = x1_ref[...] + mod_ref[0, 5:6, :] * moe
    if final_norm:
        x2 = (x2 * lax.rsqrt(jnp.mean(x2 * x2, axis=-1, keepdims=True) + RMS_EPS)) * gf_ref[...]
    o_ref[...] = x2


def _combine_call(x1, route, mod3, g_final, y_rows, pos3, *, seq_len, final_norm):
    n_tok, d = x1.shape
    t = T_MOVE
    n_steps = n_tok // t
    tiles_per_seq = seq_len // t
    last = n_steps - 1
    return pl.pallas_call(
        functools.partial(_combine_kernel, n_steps=n_steps, final_norm=final_norm),
        grid=(n_steps,),
        in_specs=[
            pl.BlockSpec((1, TOP_K, t), lambda i: (i, 0, 0), memory_space=pltpu.SMEM),
            pl.BlockSpec((1, TOP_K, t), lambda i: (jnp.minimum(i + 1, last), 0, 0), memory_space=pltpu.SMEM),
            pl.BlockSpec((t, d), lambda i: (i, 0)),
            pl.BlockSpec((t, ROUTE_LANES), lambda i: (i, 0)),
            pl.BlockSpec((1,) + mod3.shape[1:], lambda i: (i // tiles_per_seq, 0, 0)),
            pl.BlockSpec((1, d), lambda i: (0, 0)),
            pl.BlockSpec(memory_space=pl.ANY),
        ],
        out_specs=pl.BlockSpec((t, d), lambda i: (i, 0)),
        out_shape=jax.ShapeDtypeStruct((n_tok, d), F32),
        scratch_shapes=[pltpu.VMEM((2, TOP_K, t, d), F32), pltpu.SemaphoreType.DMA((2,))],
        compiler_params=pltpu.CompilerParams(
            dimension_semantics=("arbitrary",), vmem_limit_bytes=VMEM_LIMIT_BYTES),
        name="moe_combine",
    )(pos3, pos3, x1, route, mod3, g_final, y_rows)


def kernel(x, c, w_ada, b_ada, g_norm1, w_in, conv_a, w_a_out, conv_b, b_conv_b, ln_conv_g, ln_conv_b,
           w_b_out, w_o, g_norm2, w_router_g, b_router_g, w_router_e, b_router_e, w_gate, w_up, w_down,
           g_final):
    bsz, seq_len, d = x.shape
    depth = w_ada.shape[0]
    n_tok = bsz * seq_len
    n_assign = n_tok * TOP_K
    assert seq_len % T_MIX == 0 and seq_len % T_MOVE == 0 and d % (2 * LANES) == 0
    assert n_tok % T_PLAN == 0
    assert n_assign % ROW_BLK == 0
    assert N_GROUPS + N_EXPERTS <= ROUTE_LANES
    n_rows = n_assign + N_EXPERTS * ROW_BLK

    c_pad = jnp.zeros((SUBLANES, d), F32).at[:bsz].set(c)
    xt = x.reshape(n_tok, d)
    row = lambda a: a.reshape(1, -1)
    for l in range(depth):
        mod = _mod_call(c_pad, w_ada[l], row(b_ada[l]))
        mod3 = mod[:bsz].reshape(bsz, 6, d)
        w_r = jnp.zeros((d, ROUTE_LANES), F32)
        w_r = w_r.at[:, :N_GROUPS].set(w_router_g[l]).at[:, N_GROUPS:N_GROUPS + N_EXPERTS].set(w_router_e[l])
        b_r = jnp.zeros((1, ROUTE_LANES), F32)
        b_r = b_r.at[0, :N_GROUPS].set(b_router_g[l]).at[0, N_GROUPS:N_GROUPS + N_EXPERTS].set(b_router_e[l])
        wr_hi, wr_lo = _split_bf16(w_r)
        x1, h2p, route, cnt = _mixer_call(
            xt, mod3, row(g_norm1[l]), conv_a[l], conv_b[l], row(b_conv_b[l]), row(ln_conv_g[l]),
            row(ln_conv_b[l]), row(g_norm2[l]),
            w_in[l], w_a_out[l], w_b_out[l], w_o[l], wr_hi, wr_lo, b_r, seq_len=seq_len)
        pstart_row, blk0, nblk, n_used = _segment_layout(cnt[0])
        pos = _plan_call(route, pstart_row)
        pos3 = pos[:, :TOP_K].reshape(n_tok // T_MOVE, T_MOVE, TOP_K).transpose(0, 2, 1)
        xs = _scatter_call(pos3, h2p, n_rows)
        y_rows = _expert_call(xs, blk0, nblk, n_used, w_gate[l], w_up[l], w_down[l])
        xt = _combine_call(x1, route, mod3, row(g_final), y_rows, pos3, seq_len=seq_len,
                           final_norm=(l == depth - 1))
    return xt.reshape(bsz, seq_len, d)
```

```python
import functools

import jax
import jax.numpy as jnp
from jax import lax
from jax.experimental import pallas as pl
from jax.experimental.pallas import tpu as pltpu

F32 = jnp.float32
BF16 = jnp.bfloat16
U32 = jnp.uint32

N_GROUPS = 4
EXPERTS_PER_GROUP = 8
N_EXPERTS = N_GROUPS * EXPERTS_PER_GROUP
TOP_K = 2
K_SHORT = 3
K_CONF = 31
RMS_EPS = 1e-6
LN_EPS = 1e-5

LANES = 128
SUBLANES = 8
T_MIX = 256
HALO_B = 16
HALO_A = 8
CONV_ROWS = 64
ROW_BLK = 256
T_MOVE = 256
T_PLAN = 2048
W_STAGE_COLS = 512
ROUTE_LANES = LANES
REC_E, REC_W, REC_RANK = 0, 2, 4
VMEM_LIMIT_BYTES = 56 * 1024 * 1024


def _sigmoid(v):
    return 1.0 / (1.0 + jnp.exp(-v))


def _split_bf16(v):
    hi = v.astype(BF16)
    lo = (v - hi.astype(F32)).astype(BF16)
    return hi, lo


def _dot(a, b):
    return jnp.dot(a, b, preferred_element_type=F32)


def _const_spec(shape):
    nd = len(shape)
    return pl.BlockSpec(shape, lambda *_: (0,) * nd, pipeline_mode=pl.Buffered(1))


def _lane_ids(shape):
    return lax.broadcasted_iota(jnp.int32, shape, 1).astype(F32)


def _mod_kernel(c_ref, w_ref, b_ref, o_ref):
    c = c_ref[...]
    a_hi, a_lo = _split_bf16(c * _sigmoid(c))
    w_hi, w_lo = _split_bf16(w_ref[...])
    o_ref[...] = _dot(a_hi, w_hi) + _dot(a_lo, w_hi) + _dot(a_hi, w_lo) + b_ref[...]


def _mod_call(c_pad, w_ada, b_ada):
    rows, d = c_pad.shape
    n_out = w_ada.shape[1]
    blk = 1024
    return pl.pallas_call(
        _mod_kernel,
        grid=(n_out // blk,),
        in_specs=[
            pl.BlockSpec((rows, d), lambda j: (0, 0)),
            pl.BlockSpec((d, blk), lambda j: (0, j)),
            pl.BlockSpec((1, blk), lambda j: (0, j)),
        ],
        out_specs=pl.BlockSpec((rows, blk), lambda j: (0, j)),
        out_shape=jax.ShapeDtypeStruct((rows, n_out), F32),
        compiler_params=pltpu.CompilerParams(dimension_semantics=("arbitrary",)),
        name="adaln_mod",
    )(c_pad, w_ada, b_ada)


def _depthwise_lane_block(buf_ref, w_ref, out_ref, phase_ref, n_taps, row0, rows, cb):
    first, stop = rows
    max_off = row0 + n_taps - 1
    n_keep = stop - first + (max_off // SUBLANES) * SUBLANES
    cols = slice(cb * LANES, (cb + 1) * LANES)
    for s in sorted({(row0 + k) % SUBLANES for k in range(n_taps)}):
        phase_ref[s, pl.ds(0, n_keep), :] = buf_ref[pl.ds(first + s, n_keep), cols]
    row_blocks = range(first, stop, CONV_ROWS)
    acc = [None] * len(row_blocks)
    for k in range(n_taps):
        q, s = divmod(row0 + k, SUBLANES)
        w_k = jnp.broadcast_to(w_ref[k:k + 1, cols], (CONV_ROWS, LANES))
        for j, r0 in enumerate(row_blocks):
            term = phase_ref[s, pl.ds(r0 - first + q * SUBLANES, CONV_ROWS), :] * w_k
            acc[j] = term if acc[j] is None else acc[j] + term
    for j, r0 in enumerate(row_blocks):
        out_ref[pl.ds(r0, CONV_ROWS), cols] = acc[j]


def _route(logits, run_ref, live):
    shape = logits.shape
    lane = _lane_ids(shape)
    big = float(4 * ROUTE_LANES)
    neg_inf = -jnp.inf

    def first_argmax(v):
        m = jnp.max(v, axis=-1, keepdims=True)
        return m, jnp.min(jnp.where(v == m, lane, big), axis=-1, keepdims=True)

    is_group = lane < N_GROUPS
    g_max, g_idx = first_argmax(jnp.where(is_group, logits, neg_inf))
    g_prob = 1.0 / jnp.sum(jnp.where(is_group, jnp.exp(logits - g_max), 0.0), axis=-1, keepdims=True)
    lo = N_GROUPS + EXPERTS_PER_GROUP * g_idx
    in_group = (lane >= lo) & (lane < lo + EXPERTS_PER_GROUP)
    e_logits = jnp.where(in_group, logits, neg_inf)
    m1, i1 = first_argmax(e_logits)
    m2, i2 = first_argmax(jnp.where(lane == i1, neg_inf, e_logits))
    t = jnp.exp(m2 - m1)
    p1 = 1.0 / (1.0 + t)
    p2 = t / (1.0 + t)
    e1 = i1 - N_GROUPS
    e2 = i2 - N_GROUPS

    hot1 = lane == e1
    hot2 = lane == e2
    both = jnp.where(hot1 | hot2, 1.0, 0.0)
    n_rows = shape[0]
    earlier = (lax.broadcasted_iota(jnp.int32, (n_rows, n_rows), 0)
               > lax.broadcasted_iota(jnp.int32, (n_rows, n_rows), 1))
    before = _dot(jnp.where(earlier, 1.0, 0.0).astype(BF16), both.astype(BF16)) + run_ref[0:1, :]
    rank1 = jnp.sum(jnp.where(hot1, before, 0.0), axis=-1, keepdims=True)
    rank2 = jnp.sum(jnp.where(hot2, before, 0.0), axis=-1, keepdims=True)
    run_ref[0:1, :] = run_ref[0:1, :] + live * jnp.sum(both, axis=0, keepdims=True)

    rec = jnp.where(lane == REC_E, e1, 0.0)
    rec = jnp.where(lane == REC_E + 1, e2, rec)
    rec = jnp.where(lane == REC_W, g_prob * p1, rec)
    rec = jnp.where(lane == REC_W + 1, g_prob * p2, rec)
    rec = jnp.where(lane == REC_RANK, rank1, rec)
    rec = jnp.where(lane == REC_RANK + 1, rank2, rec)
    return rec


def _pack_bf16_pairs(v_bf16):
    half = v_bf16.shape[1] // 2
    lo = lax.bitcast_convert_type(v_bf16[:, :half].astype(F32), U32)
    hi = lax.bitcast_convert_type(v_bf16[:, half:].astype(F32), U32)
    return (lo >> 16) | hi


def _unpack_bf16_pairs(p_u32):
    lo = lax.bitcast_convert_type(p_u32 << 16, F32).astype(BF16)
    hi = lax.bitcast_convert_type(p_u32 & jnp.uint32(0xFFFF0000), F32).astype(BF16)
    return jnp.concatenate([lo, hi], axis=1)


def _stage_weight_bf16(src_hbm, dst_ref, stage_ref, sem):
    chunk = stage_ref.shape[2]
    n_chunks = src_hbm.shape[1] // chunk

    def cols(j):
        return pl.ds(pl.multiple_of(j * chunk, chunk), chunk)

    def copy(j, slot):
        return pltpu.make_async_copy(src_hbm.at[:, cols(j)], stage_ref.at[slot], sem.at[slot])

    copy(0, 0).start()

    def body(j, carry):
        slot = j % 2

        @pl.when(j + 1 < n_chunks)
        def _next():
            copy(j + 1, 1 - slot).start()

        copy(j, slot).wait()
        dst_ref[:, cols(j)] = stage_ref[slot].astype(BF16)
        return carry

    lax.fori_loop(0, n_chunks, body, 0)


def _mixer_kernel(x_ref, modn_ref, modp_ref, g1_ref, ca_ref, cb_ref, bcb_ref, lng_ref, lnb_ref, g2_ref,
                  wrh_ref, wrl_ref, br_ref, win_hbm, wa_hbm, wb_hbm, wo_hbm,
                  x1_ref, h2p_ref, route_ref, cnt_ref,
                  wu_ref, wcx_ref, ba_ref, sga_ref, sgb_ref, xs_ref, cva_ref, cvb_ref, run_ref, phase_ref,
                  win_ref, wa_ref, wb_ref, wo_ref, stage_ref, stage_sem,
                  *, n_tiles):
    i = pl.program_id(0)
    t_rows, d = x_ref.shape

    @pl.when(i == 0)
    def _init():
        for ref in (wu_ref, wcx_ref, ba_ref, sga_ref, sgb_ref, xs_ref, run_ref):
            ref[...] = jnp.zeros(ref.shape, ref.dtype)
        for src, dst in ((win_hbm, win_ref), (wa_hbm, wa_ref), (wb_hbm, wb_ref), (wo_hbm, wo_ref)):
            _stage_weight_bf16(src, dst, stage_ref, stage_sem)

    x = x_ref[...]
    sh1 = modn_ref[0, 0:1, :]
    sc1 = modn_ref[0, 1:2, :]
    h = (x * lax.rsqrt(jnp.mean(x * x, axis=-1, keepdims=True) + RMS_EPS)) * g1_ref[...]
    hb = (h * (1.0 + sc1) + sh1).astype(BF16)

    def proj(g):
        return _dot(hb, win_ref[:, g * d:(g + 1) * d])

    def conv_work(rows):
        items = []
        for cb in range(d // LANES):
            items.append(functools.partial(_depthwise_lane_block, wcx_ref, ca_ref, cva_ref,
                                           phase_ref.at[2], K_SHORT, HALO_A - K_SHORT // 2, rows, cb))
            items.append(functools.partial(_depthwise_lane_block, wu_ref, cb_ref, cvb_ref,
                                           phase_ref.at[cb % 2], K_CONF, HALO_B - K_CONF // 2, rows, cb))
        return items

    early = conv_work((0, t_rows - CONV_ROWS))
    n_proj = 7
    z = []
    for g in range(n_proj):
        z.append(proj(g))
        for item in early[g * len(early) // n_proj:(g + 1) * len(early) // n_proj]:
            item()
    b_a = z[0]
    cx = z[1] * z[2]
    u = z[3] * _sigmoid(z[4])
    sg_a = _sigmoid(z[5])
    sg_b = _sigmoid(z[6])

    same_seq = ((i % n_tiles) != 0).astype(F32)
    wu_ref[pl.ds(HALO_B + t_rows, HALO_B), :] = u[0:HALO_B] * same_seq
    wcx_ref[pl.ds(HALO_A + t_rows, HALO_A), :] = cx[0:HALO_A] * same_seq

    for item in conv_work((t_rows - CONV_ROWS, t_rows)):
        item()

    y_a = _dot((ba_ref[...] * cva_ref[...]).astype(BF16), wa_ref[...])
    v = cvb_ref[...] + bcb_ref[...]
    mu = jnp.mean(v, axis=-1, keepdims=True)
    vc = v - mu
    var = jnp.mean(vc * vc, axis=-1, keepdims=True)
    v = (vc * lax.rsqrt(var + LN_EPS)) * lng_ref[...] + lnb_ref[...]
    y_b = _dot((v * _sigmoid(v)).astype(BF16), wb_ref[...])
    merged = sga_ref[...] * y_a + sgb_ref[...] * y_b
    mix = _dot(merged.astype(BF16), wo_ref[...])

    gt1 = modp_ref[0, 2:3, :]
    sh2 = modp_ref[0, 3:4, :]
    sc2 = modp_ref[0, 4:5, :]
    x1 = xs_ref[...] + gt1 * mix
    x1_ref[...] = x1
    h2 = (x1 * lax.rsqrt(jnp.mean(x1 * x1, axis=-1, keepdims=True) + RMS_EPS)) * g2_ref[...]
    h2 = h2 * (1.0 + sc2) + sh2
    h2_hi, h2_lo = _split_bf16(h2)
    h2p_ref[...] = _pack_bf16_pairs(h2_hi)
    logits = _dot(h2_hi, wrh_ref[...]) + _dot(h2_lo, wrh_ref[...]) + _dot(h2_hi, wrl_ref[...]) + br_ref[...]
    route_ref[...] = _route(logits, run_ref, (i > 0).astype(F32))
    cnt_ref[...] = jnp.broadcast_to(run_ref[0:1, :], cnt_ref.shape)

    tail_u = wu_ref[pl.ds(t_rows, HALO_B), :]
    wu_ref[pl.ds(0, HALO_B), :] = tail_u * same_seq
    wu_ref[pl.ds(HALO_B, t_rows), :] = u
    tail_cx = wcx_ref[pl.ds(t_rows, HALO_A), :]
    wcx_ref[pl.ds(0, HALO_A), :] = tail_cx * same_seq
    wcx_ref[pl.ds(HALO_A, t_rows), :] = cx
    ba_ref[...] = b_a
    sga_ref[...] = sg_a
    sgb_ref[...] = sg_b
    xs_ref[...] = x


def _mixer_call(x2d, mod3, g1, conv_a, conv_b, b_conv_b, ln_g, ln_b, g2,
                w_in, w_a, w_b, w_o, wr_hi, wr_lo, b_r, *, seq_len):
    n_tok, d = x2d.shape
    t = T_MIX
    n_tiles = seq_len // t
    n_chunks = n_tok // t
    last = n_chunks - 1

    def cur(i):
        return jnp.minimum(i, last)

    def prev(i):
        return jnp.maximum(i - 1, 0)

    row_spec_prev = lambda width: pl.BlockSpec((t, width), lambda i: (prev(i), 0))
    vec = lambda a: _const_spec(a.shape)
    hbm = pl.BlockSpec(memory_space=pl.ANY)
    in_specs = [
        pl.BlockSpec((t, d), lambda i: (cur(i), 0)),
        pl.BlockSpec((1,) + mod3.shape[1:], lambda i: (cur(i) // n_tiles, 0, 0)),
        pl.BlockSpec((1,) + mod3.shape[1:], lambda i: (prev(i) // n_tiles, 0, 0)),
        vec(g1), vec(conv_a), vec(conv_b), vec(b_conv_b), vec(ln_g), vec(ln_b), vec(g2),
        vec(wr_hi), vec(wr_lo), vec(b_r),
        hbm, hbm, hbm, hbm,
    ]
    out_specs = [row_spec_prev(d), row_spec_prev(d // 2), row_spec_prev(ROUTE_LANES),
                 pl.BlockSpec((SUBLANES, ROUTE_LANES), lambda i: (0, 0))]
    out_shape = [
        jax.ShapeDtypeStruct((n_tok, d), F32),
        jax.ShapeDtypeStruct((n_tok, d // 2), U32),
        jax.ShapeDtypeStruct((n_tok, ROUTE_LANES), F32),
        jax.ShapeDtypeStruct((SUBLANES, ROUTE_LANES), F32),
    ]
    scratch = [
        pltpu.VMEM((t + 2 * HALO_B, d), F32),
        pltpu.VMEM((t + 2 * HALO_A, d), F32),
        pltpu.VMEM((t, d), F32), pltpu.VMEM((t, d), F32), pltpu.VMEM((t, d), F32), pltpu.VMEM((t, d), F32),
        pltpu.VMEM((t, d), F32), pltpu.VMEM((t, d), F32),
        pltpu.VMEM((SUBLANES, ROUTE_LANES), F32),
        pltpu.VMEM((3, SUBLANES, t + 2 * HALO_B, LANES), F32),
        pltpu.VMEM(w_in.shape, BF16), pltpu.VMEM(w_a.shape, BF16), pltpu.VMEM(w_b.shape, BF16),
        pltpu.VMEM(w_o.shape, BF16),
        pltpu.VMEM((2, d, W_STAGE_COLS), F32), pltpu.SemaphoreType.DMA((2,)),
    ]
    return pl.pallas_call(
        functools.partial(_mixer_kernel, n_tiles=n_tiles),
        grid=(n_chunks + 1,),
        in_specs=in_specs,
        out_specs=out_specs,
        out_shape=out_shape,
        scratch_shapes=scratch,
        compiler_params=pltpu.CompilerParams(
            dimension_semantics=("arbitrary",), vmem_limit_bytes=VMEM_LIMIT_BYTES),
        name="mixer_router",
    )(x2d, mod3, mod3, g1, conv_a, conv_b, b_conv_b, ln_g, ln_b, g2,
      wr_hi, wr_lo, b_r, w_in, w_a, w_b, w_o)


def _segment_layout(counts_row, n_blocks):
    counts = counts_row[:N_EXPERTS].astype(jnp.int32)
    nblk = (counts + ROW_BLK - 1) // ROW_BLK
    blk_end = jnp.cumsum(nblk)
    blk0 = blk_end - nblk
    n_used = blk_end[-1:].astype(jnp.int32)
    pstart_row = jnp.zeros((1, ROUTE_LANES), F32).at[0, :N_EXPERTS].set((blk0 * ROW_BLK).astype(F32))
    ids = jnp.arange(N_EXPERTS, dtype=jnp.int32)
    blk_e = jnp.minimum(jnp.sum(blk_end[None, :] <= jnp.arange(n_blocks, dtype=jnp.int32)[:, None], axis=1),
                        N_EXPERTS - 1).astype(jnp.int32)
    later_used = (ids[None, :] > ids[:, None]) & (nblk[None, :] > 0)
    next_used = jnp.min(jnp.where(later_used, ids[None, :], N_EXPERTS), axis=1)
    next_used = jnp.where(next_used == N_EXPERTS, -1, next_used).astype(jnp.int32)
    return pstart_row, blk_e, next_used[blk_e], n_used


def _plan_kernel(route_ref, pstart_ref, pos_ref):
    rec = route_ref[...]
    lane = _lane_ids(rec.shape)
    pos = jnp.zeros(rec.shape, F32)
    for k in range(TOP_K):
        e = rec[:, REC_E + k:REC_E + k + 1]
        seg = jnp.sum(jnp.where(lane == e, pstart_ref[...], 0.0), axis=-1, keepdims=True)
        pos = jnp.where(lane == k, seg + rec[:, REC_RANK + k:REC_RANK + k + 1], pos)
    pos_ref[...] = pos.astype(jnp.int32)


def _plan_call(route, pstart_row):
    n_tok = route.shape[0]
    t = T_PLAN
    return pl.pallas_call(
        _plan_kernel,
        grid=(n_tok // t,),
        in_specs=[pl.BlockSpec((t, ROUTE_LANES), lambda i: (i, 0)),
                  pl.BlockSpec((1, ROUTE_LANES), lambda i: (0, 0))],
        out_specs=pl.BlockSpec((t, ROUTE_LANES), lambda i: (i, 0)),
        out_shape=jax.ShapeDtypeStruct((n_tok, ROUTE_LANES), jnp.int32),
        compiler_params=pltpu.CompilerParams(dimension_semantics=("arbitrary",)),
        name="moe_plan",
    )(route, pstart_row)


def _scatter_kernel(pos_ref, rows_ref, buf_in_hbm, buf_hbm, sem):
    del buf_in_hbm
    t_rows = rows_ref.shape[0]

    for r in range(t_rows):
        for k in range(TOP_K):
            pltpu.make_async_copy(rows_ref.at[pl.ds(r, 1), :],
                                  buf_hbm.at[pl.ds(pos_ref[0, k, r], 1), :], sem).start(priority=k)
    for _ in range(TOP_K):
        pltpu.make_async_copy(rows_ref, buf_hbm.at[pl.ds(0, t_rows), :], sem).wait()


def _scatter_call(pos3, rows, n_rows):
    n_tok, width = rows.shape
    t = T_MOVE
    buf0 = jnp.zeros((n_rows, width), rows.dtype)
    return pl.pallas_call(
        _scatter_kernel,
        grid=(n_tok // t,),
        in_specs=[
            pl.BlockSpec((1, TOP_K, t), lambda i: (i, 0, 0), memory_space=pltpu.SMEM),
            pl.BlockSpec((t, width), lambda i: (i, 0)),
            pl.BlockSpec(memory_space=pl.ANY),
        ],
        out_specs=pl.BlockSpec(memory_space=pl.ANY),
        out_shape=jax.ShapeDtypeStruct((n_rows, width), rows.dtype),
        scratch_shapes=[pltpu.SemaphoreType.DMA(())],
        input_output_aliases={2: 0},
        compiler_params=pltpu.CompilerParams(dimension_semantics=("arbitrary",)),
        name="moe_scatter",
    )(pos3, rows, buf0)


def _expert_kernel(blk_e_ref, next_e_ref, n_used_ref, xs_hbm, wg_hbm, wu_hbm, wd_hbm, y_hbm,
                   wg_st, wu_st, wd_st, wg_bf, wu_bf, wd_bf, xbuf, ybuf, w_sem, in_sem, out_sem):
    n_used = n_used_ref[0]

    def rows(b):
        return pl.ds(pl.multiple_of(b * ROW_BLK, ROW_BLK), ROW_BLK)

    def load(b, slot):
        return pltpu.make_async_copy(xs_hbm.at[rows(b), :], xbuf.at[slot], in_sem.at[slot])

    def store(b, slot):
        return pltpu.make_async_copy(ybuf.at[slot], y_hbm.at[rows(b), :], out_sem.at[slot])

    def weight_copies(e, wslot):
        return [pltpu.make_async_copy(src.at[e], dst.at[wslot], w_sem.at[wslot, n])
                for n, (src, dst) in enumerate(((wg_hbm, wg_st), (wu_hbm, wu_st), (wd_hbm, wd_st)))]

    for copy in weight_copies(blk_e_ref[0], 0):
        copy.start()
    load(0, 0).start()

    def body(b, wslot):
        e = blk_e_ref[b]
        new_expert = jnp.logical_or(b == 0, e != blk_e_ref[jnp.maximum(b - 1, 0)])
        slot = b % 2

        @pl.when(new_expert)
        def _switch_expert():
            for copy in weight_copies(e, wslot):
                copy.wait()
            wg_bf[...] = wg_st[wslot].astype(BF16)
            wu_bf[...] = wu_st[wslot].astype(BF16)
            wd_bf[...] = wd_st[wslot].astype(BF16)
            nxt = next_e_ref[b]

            @pl.when(nxt >= 0)
            def _prefetch_weights():
                for copy in weight_copies(nxt, 1 - wslot):
                    copy.start()

        @pl.when(b + 1 < n_used)
        def _prefetch_rows():
            load(b + 1, 1 - slot).start()

        load(b, slot).wait()

        @pl.when(b >= 2)
        def _window_free():
            store(b - 2, slot).wait()

        xb = _unpack_bf16_pairs(xbuf[slot])
        a = _dot(xb, wg_bf[...])
        u = _dot(xb, wu_bf[...])
        hid = ((a * _sigmoid(a)) * u).astype(BF16)
        ybuf[slot] = _dot(hid, wd_bf[...])
        store(b, slot).start()
        return jnp.where(new_expert, 1 - wslot, wslot)

    lax.fori_loop(0, n_used, body, jnp.int32(0))

    @pl.when(n_used >= 2)
    def _drain_older():
        store(n_used - 2, n_used % 2).wait()

    store(n_used - 1, (n_used - 1) % 2).wait()

    ybuf[0] = jnp.zeros(ybuf.shape[1:], ybuf.dtype)

    def fill(b, carry):
        copy = pltpu.make_async_copy(ybuf.at[0], y_hbm.at[rows(b), :], out_sem.at[0])
        copy.start()
        copy.wait()
        return carry

    lax.fori_loop(n_used, y_hbm.shape[0] // ROW_BLK, fill, 0)


def _expert_call(xs, blk_e, next_e, n_used, w_gate, w_up, w_down):
    n_rows, half = xs.shape
    d = 2 * half
    d_e = w_gate.shape[-1]
    hbm = pl.BlockSpec(memory_space=pl.ANY)
    grid_spec = pltpu.PrefetchScalarGridSpec(
        num_scalar_prefetch=3,
        grid=(1,),
        in_specs=[hbm, hbm, hbm, hbm],
        out_specs=hbm,
        scratch_shapes=[
            pltpu.VMEM((2, d, d_e), F32), pltpu.VMEM((2, d, d_e), F32), pltpu.VMEM((2, d_e, d), F32),
            pltpu.VMEM((d, d_e), BF16), pltpu.VMEM((d, d_e), BF16), pltpu.VMEM((d_e, d), BF16),
            pltpu.VMEM((2, ROW_BLK, half), xs.dtype), pltpu.VMEM((2, ROW_BLK, d), F32),
            pltpu.SemaphoreType.DMA((2, 3)), pltpu.SemaphoreType.DMA((2,)), pltpu.SemaphoreType.DMA((2,)),
        ],
    )
    return pl.pallas_call(
        _expert_kernel,
        grid_spec=grid_spec,
        out_shape=jax.ShapeDtypeStruct((n_rows, d), F32),
        compiler_params=pltpu.CompilerParams(
            dimension_semantics=("arbitrary",), vmem_limit_bytes=VMEM_LIMIT_BYTES),
        name="moe_experts",
    )(blk_e, next_e, n_used, xs, w_gate, w_up, w_down)


def _row_gather_start(src_hbm, idx_of_row, dst, sem, n_rows):
    for r in range(n_rows):
        pltpu.make_async_copy(src_hbm.at[pl.ds(idx_of_row(r), 1), :], dst.at[pl.ds(r, 1), :],
                              sem).start(priority=r % 2)


def _row_gather_wait(src_hbm, dst, sem, n_rows):
    pltpu.make_async_copy(src_hbm.at[pl.ds(0, n_rows), :], dst, sem).wait()


def _combine_kernel(posc_ref, posn_ref, x1_ref, route_ref, mod_ref, gf_ref, y_hbm, o_ref, ybuf, sem,
                    *, n_steps, final_norm):
    i = pl.program_id(0)
    slot = i % 2
    t_rows = x1_ref.shape[0]

    def start(pos_ref, s):
        for k in range(TOP_K):
            _row_gather_start(y_hbm, lambda r, k=k: pos_ref[0, k, r], ybuf.at[s, k], sem.at[s], t_rows)

    @pl.when(i == 0)
    def _prime():
        start(posc_ref, 0)

    @pl.when(i + 1 < n_steps)
    def _prefetch():
        start(posn_ref, 1 - slot)

    for k in range(TOP_K):
        _row_gather_wait(y_hbm, ybuf.at[slot, k], sem.at[slot], t_rows)
    rec = route_ref[...]
    moe = ybuf[slot, 0] * rec[:, REC_W:REC_W + 1] + ybuf[slot, 1] * rec[:, REC_W + 1:REC_W + 2]
    x2 = x1_ref[...] + mod_ref[0, 5:6, :] * moe
    if final_norm:
        x2 = (x2 * lax.rsqrt(jnp.mean(x2 * x2, axis=-1, keepdims=True) + RMS_EPS)) * gf_ref[...]
    o_ref[...] = x2


def _combine_call(x1, route, mod3, g_final, y_rows, pos3, *, seq_len, final_norm):
    n_tok, d = x1.shape
    t = T_MOVE
    n_steps = n_tok // t
    tiles_per_seq = seq_len // t
    last = n_steps - 1
    return pl.pallas_call(
        functools.partial(_combine_kernel, n_steps=n_steps, final_norm=final_norm),
        grid=(n_steps,),
        in_specs=[
            pl.BlockSpec((1, TOP_K, t), lambda i: (i, 0, 0), memory_space=pltpu.SMEM),
            pl.BlockSpec((1, TOP_K, t), lambda i: (jnp.minimum(i + 1, last), 0, 0), memory_space=pltpu.SMEM),
            pl.BlockSpec((t, d), lambda i: (i, 0)),
            pl.BlockSpec((t, ROUTE_LANES), lambda i: (i, 0)),
            pl.BlockSpec((1,) + mod3.shape[1:], lambda i: (i // tiles_per_seq, 0, 0)),
            pl.BlockSpec((1, d), lambda i: (0, 0)),
            pl.BlockSpec(memory_space=pl.ANY),
        ],
        out_specs=pl.BlockSpec((t, d), lambda i: (i, 0)),
        out_shape=jax.ShapeDtypeStruct((n_tok, d), F32),
        scratch_shapes=[pltpu.VMEM((2, TOP_K, t, d), F32), pltpu.SemaphoreType.DMA((2,))],
        compiler_params=pltpu.CompilerParams(
            dimension_semantics=("arbitrary",), vmem_limit_bytes=VMEM_LIMIT_BYTES),
        name="moe_combine",
    )(pos3, pos3, x1, route, mod3, g_final, y_rows)


def kernel(x, c, w_ada, b_ada, g_norm1, w_in, conv_a, w_a_out, conv_b, b_conv_b, ln_conv_g, ln_conv_b,
           w_b_out, w_o, g_norm2, w_router_g, b_router_g, w_router_e, b_router_e, w_gate, w_up, w_down,
           g_final):
    bsz, seq_len, d = x.shape
    depth = w_ada.shape[0]
    n_tok = bsz * seq_len
    n_assign = n_tok * TOP_K
    assert seq_len % T_MIX == 0 and seq_len % T_MOVE == 0 and d % (2 * LANES) == 0
    assert n_tok % T_PLAN == 0
    assert n_assign % ROW_BLK == 0
    assert N_GROUPS + N_EXPERTS <= ROUTE_LANES
    n_rows = n_assign + N_EXPERTS * ROW_BLK

    c_pad = jnp.zeros((SUBLANES, d), F32).at[:bsz].set(c)
    xt = x.reshape(n_tok, d)
    row = lambda a: a.reshape(1, -1)
    for l in range(depth):
        mod = _mod_call(c_pad, w_ada[l], row(b_ada[l]))
        mod3 = mod[:bsz].reshape(bsz, 6, d)
        w_r = jnp.zeros((d, ROUTE_LANES), F32)
        w_r = w_r.at[:, :N_GROUPS].set(w_router_g[l]).at[:, N_GROUPS:N_GROUPS + N_EXPERTS].set(w_router_e[l])
        b_r = jnp.zeros((1, ROUTE_LANES), F32)
        b_r = b_r.at[0, :N_GROUPS].set(b_router_g[l]).at[0, N_GROUPS:N_GROUPS + N_EXPERTS].set(b_router_e[l])
        wr_hi, wr_lo = _split_bf16(w_r)
        x1, h2p, route, cnt = _mixer_call(
            xt, mod3, row(g_norm1[l]), conv_a[l], conv_b[l], row(b_conv_b[l]), row(ln_conv_g[l]),
            row(ln_conv_b[l]), row(g_norm2[l]),
            w_in[l], w_a_out[l], w_b_out[l], w_o[l], wr_hi, wr_lo, b_r, seq_len=seq_len)
        pstart_row, blk_e, next_e, n_used = _segment_layout(cnt[0], n_rows // ROW_BLK)
        pos = _plan_call(route, pstart_row)
        pos3 = pos[:, :TOP_K].reshape(n_tok // T_MOVE, T_MOVE, TOP_K).transpose(0, 2, 1)
        xs = _scatter_call(pos3, h2p, n_rows)
        y_rows = _expert_call(xs, blk_e, next_e, n_used, w_gate[l], w_up[l], w_down[l])
        xt = _combine_call(x1, route, mod3, row(g_final), y_rows, pos3, seq_len=seq_len,
                           final_norm=(l == depth - 1))
    return xt.reshape(bsz, seq_len, d)
```

```python
import functools

import jax
import jax.numpy as jnp
from jax import lax
from jax.experimental import pallas as pl
from jax.experimental.pallas import tpu as pltpu

F32 = jnp.float32
BF16 = jnp.bfloat16
U32 = jnp.uint32

N_GROUPS = 4
EXPERTS_PER_GROUP = 8
N_EXPERTS = N_GROUPS * EXPERTS_PER_GROUP
TOP_K = 2
K_SHORT = 3
K_CONF = 31
RMS_EPS = 1e-6
LN_EPS = 1e-5

LANES = 128
SUBLANES = 8
T_MIX = 256
HALO_B = 16
HALO_A = 8
CONV_ROWS = 64
ROW_BLK = 256
T_MOVE = 256
T_PLAN = 2048
W_STAGE_COLS = 512
ROUTE_LANES = LANES
REC_E, REC_W, REC_RANK = 0, 2, 4
VMEM_LIMIT_BYTES = 56 * 1024 * 1024


def _sigmoid(v):
    return 1.0 / (1.0 + jnp.exp(-v))


def _split_bf16(v):
    hi = v.astype(BF16)
    lo = (v - hi.astype(F32)).astype(BF16)
    return hi, lo


def _dot(a, b):
    return jnp.dot(a, b, preferred_element_type=F32)


def _const_spec(shape):
    nd = len(shape)
    return pl.BlockSpec(shape, lambda *_: (0,) * nd, pipeline_mode=pl.Buffered(1))


def _lane_ids(shape):
    return lax.broadcasted_iota(jnp.int32, shape, 1).astype(F32)


def _mod_kernel(c_ref, w_ref, b_ref, o_ref):
    c = c_ref[...]
    a_hi, a_lo = _split_bf16(c * _sigmoid(c))
    w_hi, w_lo = _split_bf16(w_ref[...])
    o_ref[...] = _dot(a_hi, w_hi) + _dot(a_lo, w_hi) + _dot(a_hi, w_lo) + b_ref[...]


def _mod_call(c_pad, w_ada, b_ada):
    rows, d = c_pad.shape
    n_out = w_ada.shape[1]
    blk = 1024
    return pl.pallas_call(
        _mod_kernel,
        grid=(n_out // blk,),
        in_specs=[
            pl.BlockSpec((rows, d), lambda j: (0, 0)),
            pl.BlockSpec((d, blk), lambda j: (0, j)),
            pl.BlockSpec((1, blk), lambda j: (0, j)),
        ],
        out_specs=pl.BlockSpec((rows, blk), lambda j: (0, j)),
        out_shape=jax.ShapeDtypeStruct((rows, n_out), F32),
        compiler_params=pltpu.CompilerParams(dimension_semantics=("arbitrary",)),
        name="adaln_mod",
    )(c_pad, w_ada, b_ada)


def _depthwise_lane_block(buf_ref, w_ref, out_ref, phase_ref, n_taps, row0, rows, cb):
    first, stop = rows
    max_off = row0 + n_taps - 1
    n_keep = stop - first + (max_off // SUBLANES) * SUBLANES
    cols = slice(cb * LANES, (cb + 1) * LANES)
    for s in sorted({(row0 + k) % SUBLANES for k in range(n_taps)}):
        phase_ref[s, pl.ds(0, n_keep), :] = buf_ref[pl.ds(first + s, n_keep), cols]
    row_blocks = range(first, stop, CONV_ROWS)
    acc = [None] * len(row_blocks)
    for k in range(n_taps):
        q, s = divmod(row0 + k, SUBLANES)
        w_k = jnp.broadcast_to(w_ref[k:k + 1, cols], (CONV_ROWS, LANES))
        for j, r0 in enumerate(row_blocks):
            term = phase_ref[s, pl.ds(r0 - first + q * SUBLANES, CONV_ROWS), :] * w_k
            acc[j] = term if acc[j] is None else acc[j] + term
    for j, r0 in enumerate(row_blocks):
        out_ref[pl.ds(r0, CONV_ROWS), cols] = acc[j]


def _route(logits, run_ref, live):
    shape = logits.shape
    lane = _lane_ids(shape)
    big = float(4 * ROUTE_LANES)
    neg_inf = -jnp.inf

    def first_argmax(v):
        m = jnp.max(v, axis=-1, keepdims=True)
        return m, jnp.min(jnp.where(v == m, lane, big), axis=-1, keepdims=True)

    is_group = lane < N_GROUPS
    g_max, g_idx = first_argmax(jnp.where(is_group, logits, neg_inf))
    g_prob = 1.0 / jnp.sum(jnp.where(is_group, jnp.exp(logits - g_max), 0.0), axis=-1, keepdims=True)
    lo = N_GROUPS + EXPERTS_PER_GROUP * g_idx
    in_group = (lane >= lo) & (lane < lo + EXPERTS_PER_GROUP)
    e_logits = jnp.where(in_group, logits, neg_inf)
    m1, i1 = first_argmax(e_logits)
    m2, i2 = first_argmax(jnp.where(lane == i1, neg_inf, e_logits))
    t = jnp.exp(m2 - m1)
    p1 = 1.0 / (1.0 + t)
    p2 = t / (1.0 + t)
    e1 = i1 - N_GROUPS
    e2 = i2 - N_GROUPS

    hot1 = lane == e1
    hot2 = lane == e2
    both = jnp.where(hot1 | hot2, 1.0, 0.0)
    n_rows = shape[0]
    earlier = (lax.broadcasted_iota(jnp.int32, (n_rows, n_rows), 0)
               > lax.broadcasted_iota(jnp.int32, (n_rows, n_rows), 1))
    before = _dot(jnp.where(earlier, 1.0, 0.0).astype(BF16), both.astype(BF16)) + run_ref[0:1, :]
    rank1 = jnp.sum(jnp.where(hot1, before, 0.0), axis=-1, keepdims=True)
    rank2 = jnp.sum(jnp.where(hot2, before, 0.0), axis=-1, keepdims=True)
    run_ref[0:1, :] = run_ref[0:1, :] + live * jnp.sum(both, axis=0, keepdims=True)

    rec = jnp.where(lane == REC_E, e1, 0.0)
    rec = jnp.where(lane == REC_E + 1, e2, rec)
    rec = jnp.where(lane == REC_W, g_prob * p1, rec)
    rec = jnp.where(lane == REC_W + 1, g_prob * p2, rec)
    rec = jnp.where(lane == REC_RANK, rank1, rec)
    rec = jnp.where(lane == REC_RANK + 1, rank2, rec)
    return rec


def _pack_bf16_pairs(v_bf16):
    half = v_bf16.shape[1] // 2
    lo = lax.bitcast_convert_type(v_bf16[:, :half].astype(F32), U32)
    hi = lax.bitcast_convert_type(v_bf16[:, half:].astype(F32), U32)
    return (lo >> 16) | hi


def _unpack_bf16_pairs(p_u32, dtype=BF16):
    lo = lax.bitcast_convert_type(p_u32 << 16, F32).astype(dtype)
    hi = lax.bitcast_convert_type(p_u32 & jnp.uint32(0xFFFF0000), F32).astype(dtype)
    return jnp.concatenate([lo, hi], axis=1)


def _stage_weight_bf16(src_hbm, dst_ref, stage_ref, sem):
    chunk = stage_ref.shape[2]
    n_chunks = src_hbm.shape[1] // chunk

    def cols(j):
        return pl.ds(pl.multiple_of(j * chunk, chunk), chunk)

    def copy(j, slot):
        return pltpu.make_async_copy(src_hbm.at[:, cols(j)], stage_ref.at[slot], sem.at[slot])

    copy(0, 0).start()

    def body(j, carry):
        slot = j % 2

        @pl.when(j + 1 < n_chunks)
        def _next():
            copy(j + 1, 1 - slot).start()

        copy(j, slot).wait()
        dst_ref[:, cols(j)] = stage_ref[slot].astype(BF16)
        return carry

    lax.fori_loop(0, n_chunks, body, 0)


def _mixer_kernel(x_ref, modn_ref, modp_ref, g1_ref, ca_ref, cb_ref, bcb_ref, lng_ref, lnb_ref, g2_ref,
                  wrh_ref, wrl_ref, br_ref, win_hbm, wa_hbm, wb_hbm, wo_hbm,
                  x1_ref, h2p_ref, route_ref, cnt_ref,
                  wu_ref, wcx_ref, ba_ref, sga_ref, sgb_ref, xs_ref, cva_ref, cvb_ref, run_ref, phase_ref,
                  win_ref, wa_ref, wb_ref, wo_ref, stage_ref, stage_sem,
                  *, n_tiles):
    i = pl.program_id(0)
    t_rows, d = x_ref.shape

    @pl.when(i == 0)
    def _init():
        for ref in (wu_ref, wcx_ref, ba_ref, sga_ref, sgb_ref, xs_ref, run_ref):
            ref[...] = jnp.zeros(ref.shape, ref.dtype)
        for src, dst in ((win_hbm, win_ref), (wa_hbm, wa_ref), (wb_hbm, wb_ref), (wo_hbm, wo_ref)):
            _stage_weight_bf16(src, dst, stage_ref, stage_sem)

    x = x_ref[...]
    sh1 = modn_ref[0, 0:1, :]
    sc1 = modn_ref[0, 1:2, :]
    h = (x * lax.rsqrt(jnp.mean(x * x, axis=-1, keepdims=True) + RMS_EPS)) * g1_ref[...]
    hb = (h * (1.0 + sc1) + sh1).astype(BF16)

    def proj(g):
        return _dot(hb, win_ref[:, g * d:(g + 1) * d])

    def conv_work(rows):
        items = []
        for cb in range(d // LANES):
            items.append(functools.partial(_depthwise_lane_block, wcx_ref, ca_ref, cva_ref,
                                           phase_ref.at[2], K_SHORT, HALO_A - K_SHORT // 2, rows, cb))
            items.append(functools.partial(_depthwise_lane_block, wu_ref, cb_ref, cvb_ref,
                                           phase_ref.at[cb % 2], K_CONF, HALO_B - K_CONF // 2, rows, cb))
        return items

    early = conv_work((0, t_rows - CONV_ROWS))
    n_proj = 7
    z = []
    for g in range(n_proj):
        z.append(proj(g))
        for item in early[g * len(early) // n_proj:(g + 1) * len(early) // n_proj]:
            item()
    b_a = z[0]
    cx = z[1] * z[2]
    u = z[3] * _sigmoid(z[4])
    sg_a = _sigmoid(z[5])
    sg_b = _sigmoid(z[6])

    same_seq = ((i % n_tiles) != 0).astype(F32)
    wu_ref[pl.ds(HALO_B + t_rows, HALO_B), :] = u[0:HALO_B] * same_seq
    wcx_ref[pl.ds(HALO_A + t_rows, HALO_A), :] = cx[0:HALO_A] * same_seq

    for item in conv_work((t_rows - CONV_ROWS, t_rows)):
        item()

    y_a = _dot((ba_ref[...] * cva_ref[...]).astype(BF16), wa_ref[...])
    v = cvb_ref[...] + bcb_ref[...]
    mu = jnp.mean(v, axis=-1, keepdims=True)
    vc = v - mu
    var = jnp.mean(vc * vc, axis=-1, keepdims=True)
    v = (vc * lax.rsqrt(var + LN_EPS)) * lng_ref[...] + lnb_ref[...]
    y_b = _dot((v * _sigmoid(v)).astype(BF16), wb_ref[...])
    merged = sga_ref[...] * y_a + sgb_ref[...] * y_b
    mix = _dot(merged.astype(BF16), wo_ref[...])

    gt1 = modp_ref[0, 2:3, :]
    sh2 = modp_ref[0, 3:4, :]
    sc2 = modp_ref[0, 4:5, :]
    x1 = xs_ref[...] + gt1 * mix
    x1_ref[...] = x1
    h2 = (x1 * lax.rsqrt(jnp.mean(x1 * x1, axis=-1, keepdims=True) + RMS_EPS)) * g2_ref[...]
    h2 = h2 * (1.0 + sc2) + sh2
    h2_hi, h2_lo = _split_bf16(h2)
    h2p_ref[...] = _pack_bf16_pairs(h2_hi)
    logits = _dot(h2_hi, wrh_ref[...]) + _dot(h2_lo, wrh_ref[...]) + _dot(h2_hi, wrl_ref[...]) + br_ref[...]
    route_ref[...] = _route(logits, run_ref, (i > 0).astype(F32))
    cnt_ref[...] = jnp.broadcast_to(run_ref[0:1, :], cnt_ref.shape)

    tail_u = wu_ref[pl.ds(t_rows, HALO_B), :]
    wu_ref[pl.ds(0, HALO_B), :] = tail_u * same_seq
    wu_ref[pl.ds(HALO_B, t_rows), :] = u
    tail_cx = wcx_ref[pl.ds(t_rows, HALO_A), :]
    wcx_ref[pl.ds(0, HALO_A), :] = tail_cx * same_seq
    wcx_ref[pl.ds(HALO_A, t_rows), :] = cx
    ba_ref[...] = b_a
    sga_ref[...] = sg_a
    sgb_ref[...] = sg_b
    xs_ref[...] = x


def _mixer_call(x2d, mod3, g1, conv_a, conv_b, b_conv_b, ln_g, ln_b, g2,
                w_in, w_a, w_b, w_o, wr_hi, wr_lo, b_r, *, seq_len):
    n_tok, d = x2d.shape
    t = T_MIX
    n_tiles = seq_len // t
    n_chunks = n_tok // t
    last = n_chunks - 1

    def cur(i):
        return jnp.minimum(i, last)

    def prev(i):
        return jnp.maximum(i - 1, 0)

    row_spec_prev = lambda width: pl.BlockSpec((t, width), lambda i: (prev(i), 0))
    vec = lambda a: _const_spec(a.shape)
    hbm = pl.BlockSpec(memory_space=pl.ANY)
    in_specs = [
        pl.BlockSpec((t, d), lambda i: (cur(i), 0)),
        pl.BlockSpec((1,) + mod3.shape[1:], lambda i: (cur(i) // n_tiles, 0, 0)),
        pl.BlockSpec((1,) + mod3.shape[1:], lambda i: (prev(i) // n_tiles, 0, 0)),
        vec(g1), vec(conv_a), vec(conv_b), vec(b_conv_b), vec(ln_g), vec(ln_b), vec(g2),
        vec(wr_hi), vec(wr_lo), vec(b_r),
        hbm, hbm, hbm, hbm,
    ]
    out_specs = [row_spec_prev(d), row_spec_prev(d // 2), row_spec_prev(ROUTE_LANES),
                 pl.BlockSpec((SUBLANES, ROUTE_LANES), lambda i: (0, 0))]
    out_shape = [
        jax.ShapeDtypeStruct((n_tok, d), F32),
        jax.ShapeDtypeStruct((n_tok, d // 2), U32),
        jax.ShapeDtypeStruct((n_tok, ROUTE_LANES), F32),
        jax.ShapeDtypeStruct((SUBLANES, ROUTE_LANES), F32),
    ]
    scratch = [
        pltpu.VMEM((t + 2 * HALO_B, d), F32),
        pltpu.VMEM((t + 2 * HALO_A, d), F32),
        pltpu.VMEM((t, d), F32), pltpu.VMEM((t, d), F32), pltpu.VMEM((t, d), F32), pltpu.VMEM((t, d), F32),
        pltpu.VMEM((t, d), F32), pltpu.VMEM((t, d), F32),
        pltpu.VMEM((SUBLANES, ROUTE_LANES), F32),
        pltpu.VMEM((3, SUBLANES, t + 2 * HALO_B, LANES), F32),
        pltpu.VMEM(w_in.shape, BF16), pltpu.VMEM(w_a.shape, BF16), pltpu.VMEM(w_b.shape, BF16),
        pltpu.VMEM(w_o.shape, BF16),
        pltpu.VMEM((2, d, W_STAGE_COLS), F32), pltpu.SemaphoreType.DMA((2,)),
    ]
    return pl.pallas_call(
        functools.partial(_mixer_kernel, n_tiles=n_tiles),
        grid=(n_chunks + 1,),
        in_specs=in_specs,
        out_specs=out_specs,
        out_shape=out_shape,
        scratch_shapes=scratch,
        compiler_params=pltpu.CompilerParams(
            dimension_semantics=("arbitrary",), vmem_limit_bytes=VMEM_LIMIT_BYTES),
        name="mixer_router",
    )(x2d, mod3, mod3, g1, conv_a, conv_b, b_conv_b, ln_g, ln_b, g2,
      wr_hi, wr_lo, b_r, w_in, w_a, w_b, w_o)


def _segment_layout(counts_row, n_blocks):
    counts = counts_row[:N_EXPERTS].astype(jnp.int32)
    nblk = (counts + ROW_BLK - 1) // ROW_BLK
    blk_end = jnp.cumsum(nblk)
    blk0 = blk_end - nblk
    n_used = blk_end[-1:].astype(jnp.int32)
    pstart_row = jnp.zeros((1, ROUTE_LANES), F32).at[0, :N_EXPERTS].set((blk0 * ROW_BLK).astype(F32))
    ids = jnp.arange(N_EXPERTS, dtype=jnp.int32)
    blk_e = jnp.minimum(jnp.sum(blk_end[None, :] <= jnp.arange(n_blocks, dtype=jnp.int32)[:, None], axis=1),
                        N_EXPERTS - 1).astype(jnp.int32)
    later_used = (ids[None, :] > ids[:, None]) & (nblk[None, :] > 0)
    next_used = jnp.min(jnp.where(later_used, ids[None, :], N_EXPERTS), axis=1)
    next_used = jnp.where(next_used == N_EXPERTS, -1, next_used).astype(jnp.int32)
    return pstart_row, blk_e, next_used[blk_e], n_used


def _plan_kernel(route_ref, pstart_ref, pos_ref):
    rec = route_ref[...]
    lane = _lane_ids(rec.shape)
    pos = jnp.zeros(rec.shape, F32)
    for k in range(TOP_K):
        e = rec[:, REC_E + k:REC_E + k + 1]
        seg = jnp.sum(jnp.where(lane == e, pstart_ref[...], 0.0), axis=-1, keepdims=True)
        pos = jnp.where(lane == k, seg + rec[:, REC_RANK + k:REC_RANK + k + 1], pos)
    pos_ref[...] = pos.astype(jnp.int32)


def _plan_call(route, pstart_row):
    n_tok = route.shape[0]
    t = T_PLAN
    return pl.pallas_call(
        _plan_kernel,
        grid=(n_tok // t,),
        in_specs=[pl.BlockSpec((t, ROUTE_LANES), lambda i: (i, 0)),
                  pl.BlockSpec((1, ROUTE_LANES), lambda i: (0, 0))],
        out_specs=pl.BlockSpec((t, ROUTE_LANES), lambda i: (i, 0)),
        out_shape=jax.ShapeDtypeStruct((n_tok, ROUTE_LANES), jnp.int32),
        compiler_params=pltpu.CompilerParams(dimension_semantics=("arbitrary",)),
        name="moe_plan",
    )(route, pstart_row)


def _scatter_kernel(pos_ref, rows_ref, buf_in_hbm, buf_hbm, sem):
    del buf_in_hbm
    t_rows = rows_ref.shape[0]

    for r in range(t_rows):
        for k in range(TOP_K):
            pltpu.make_async_copy(rows_ref.at[pl.ds(r, 1), :],
                                  buf_hbm.at[pl.ds(pos_ref[0, k, r], 1), :], sem).start(priority=k)
    for _ in range(TOP_K):
        pltpu.make_async_copy(rows_ref, buf_hbm.at[pl.ds(0, t_rows), :], sem).wait()


def _scatter_call(pos3, rows, n_rows):
    n_tok, width = rows.shape
    t = T_MOVE
    buf0 = jnp.zeros((n_rows, width), rows.dtype)
    return pl.pallas_call(
        _scatter_kernel,
        grid=(n_tok // t,),
        in_specs=[
            pl.BlockSpec((1, TOP_K, t), lambda i: (i, 0, 0), memory_space=pltpu.SMEM),
            pl.BlockSpec((t, width), lambda i: (i, 0)),
            pl.BlockSpec(memory_space=pl.ANY),
        ],
        out_specs=pl.BlockSpec(memory_space=pl.ANY),
        out_shape=jax.ShapeDtypeStruct((n_rows, width), rows.dtype),
        scratch_shapes=[pltpu.SemaphoreType.DMA(())],
        input_output_aliases={2: 0},
        compiler_params=pltpu.CompilerParams(dimension_semantics=("arbitrary",)),
        name="moe_scatter",
    )(pos3, rows, buf0)


def _expert_kernel(blk_e_ref, next_e_ref, n_used_ref, xs_hbm, wg_hbm, wu_hbm, wd_hbm, y_hbm,
                   wg_st, wu_st, wd_st, wg_bf, wu_bf, wd_bf, xbuf, ybuf, w_sem, in_sem, out_sem):
    n_used = n_used_ref[0]

    def rows(b):
        return pl.ds(pl.multiple_of(b * ROW_BLK, ROW_BLK), ROW_BLK)

    def load(b, slot):
        return pltpu.make_async_copy(xs_hbm.at[rows(b), :], xbuf.at[slot], in_sem.at[slot])

    def store(b, slot):
        return pltpu.make_async_copy(ybuf.at[slot], y_hbm.at[rows(b), :], out_sem.at[slot])

    def weight_copies(e, wslot):
        return [pltpu.make_async_copy(src.at[e], dst.at[wslot], w_sem.at[wslot, n])
                for n, (src, dst) in enumerate(((wg_hbm, wg_st), (wu_hbm, wu_st), (wd_hbm, wd_st)))]

    for copy in weight_copies(blk_e_ref[0], 0):
        copy.start()
    load(0, 0).start()

    def body(b, wslot):
        e = blk_e_ref[b]
        new_expert = jnp.logical_or(b == 0, e != blk_e_ref[jnp.maximum(b - 1, 0)])
        slot = b % 2

        @pl.when(new_expert)
        def _switch_expert():
            for copy in weight_copies(e, wslot):
                copy.wait()
            wg_bf[...] = wg_st[wslot].astype(BF16)
            wu_bf[...] = wu_st[wslot].astype(BF16)
            wd_bf[...] = wd_st[wslot].astype(BF16)
            nxt = next_e_ref[b]

            @pl.when(nxt >= 0)
            def _prefetch_weights():
                for copy in weight_copies(nxt, 1 - wslot):
                    copy.start()

        @pl.when(b + 1 < n_used)
        def _prefetch_rows():
            load(b + 1, 1 - slot).start()

        load(b, slot).wait()

        @pl.when(b >= 2)
        def _window_free():
            store(b - 2, slot).wait()

        xb = _unpack_bf16_pairs(xbuf[slot])
        a = _dot(xb, wg_bf[...])
        u = _dot(xb, wu_bf[...])
        hid = ((a * _sigmoid(a)) * u).astype(BF16)
        ybuf[slot] = _pack_bf16_pairs(_dot(hid, wd_bf[...]).astype(BF16))
        store(b, slot).start()
        return jnp.where(new_expert, 1 - wslot, wslot)

    lax.fori_loop(0, n_used, body, jnp.int32(0))

    @pl.when(n_used >= 2)
    def _drain_older():
        store(n_used - 2, n_used % 2).wait()

    store(n_used - 1, (n_used - 1) % 2).wait()

    ybuf[0] = jnp.zeros(ybuf.shape[1:], ybuf.dtype)

    def fill(b, carry):
        copy = pltpu.make_async_copy(ybuf.at[0], y_hbm.at[rows(b), :], out_sem.at[0])
        copy.start()
        copy.wait()
        return carry

    lax.fori_loop(n_used, y_hbm.shape[0] // ROW_BLK, fill, 0)


def _expert_call(xs, blk_e, next_e, n_used, w_gate, w_up, w_down):
    n_rows, half = xs.shape
    d = 2 * half
    d_e = w_gate.shape[-1]
    hbm = pl.BlockSpec(memory_space=pl.ANY)
    grid_spec = pltpu.PrefetchScalarGridSpec(
        num_scalar_prefetch=3,
        grid=(1,),
        in_specs=[hbm, hbm, hbm, hbm],
        out_specs=hbm,
        scratch_shapes=[
            pltpu.VMEM((2, d, d_e), F32), pltpu.VMEM((2, d, d_e), F32), pltpu.VMEM((2, d_e, d), F32),
            pltpu.VMEM((d, d_e), BF16), pltpu.VMEM((d, d_e), BF16), pltpu.VMEM((d_e, d), BF16),
            pltpu.VMEM((2, ROW_BLK, half), xs.dtype), pltpu.VMEM((2, ROW_BLK, half), xs.dtype),
            pltpu.SemaphoreType.DMA((2, 3)), pltpu.SemaphoreType.DMA((2,)), pltpu.SemaphoreType.DMA((2,)),
        ],
    )
    return pl.pallas_call(
        _expert_kernel,
        grid_spec=grid_spec,
        out_shape=jax.ShapeDtypeStruct((n_rows, half), xs.dtype),
        compiler_params=pltpu.CompilerParams(
            dimension_semantics=("arbitrary",), vmem_limit_bytes=VMEM_LIMIT_BYTES),
        name="moe_experts",
    )(blk_e, next_e, n_used, xs, w_gate, w_up, w_down)


def _row_gather_start(src_hbm, idx_of_row, dst, sem, n_rows):
    for r in range(n_rows):
        pltpu.make_async_copy(src_hbm.at[pl.ds(idx_of_row(r), 1), :], dst.at[pl.ds(r, 1), :],
                              sem).start(priority=r % 2)


def _row_gather_wait(src_hbm, dst, sem, n_rows):
    pltpu.make_async_copy(src_hbm.at[pl.ds(0, n_rows), :], dst, sem).wait()


def _combine_kernel(posc_ref, posn_ref, x1_ref, route_ref, mod_ref, gf_ref, y_hbm, o_ref, ybuf, sem,
                    *, n_steps, final_norm):
    i = pl.program_id(0)
    slot = i % 2
    t_rows = x1_ref.shape[0]

    def start(pos_ref, s):
        for k in range(TOP_K):
            _row_gather_start(y_hbm, lambda r, k=k: pos_ref[0, k, r], ybuf.at[s, k], sem.at[s], t_rows)

    @pl.when(i == 0)
    def _prime():
        start(posc_ref, 0)

    @pl.when(i + 1 < n_steps)
    def _prefetch():
        start(posn_ref, 1 - slot)

    for k in range(TOP_K):
        _row_gather_wait(y_hbm, ybuf.at[slot, k], sem.at[slot], t_rows)
    rec = route_ref[...]
    y0 = _unpack_bf16_pairs(ybuf[slot, 0], F32)
    y1 = _unpack_bf16_pairs(ybuf[slot, 1], F32)
    moe = y0 * rec[:, REC_W:REC_W + 1] + y1 * rec[:, REC_W + 1:REC_W + 2]
    x2 = x1_ref[...] + mod_ref[0, 5:6, :] * moe
    if final_norm:
        x2 = (x2 * lax.rsqrt(jnp.mean(x2 * x2, axis=-1, keepdims=True) + RMS_EPS)) * gf_ref[...]
    o_ref[...] = x2


def _combine_call(x1, route, mod3, g_final, y_rows, pos3, *, seq_len, final_norm):
    n_tok, d = x1.shape
    t = T_MOVE
    n_steps = n_tok // t
    tiles_per_seq = seq_len // t
    last = n_steps - 1
    return pl.pallas_call(
        functools.partial(_combine_kernel, n_steps=n_steps, final_norm=final_norm),
        grid=(n_steps,),
        in_specs=[
            pl.BlockSpec((1, TOP_K, t), lambda i: (i, 0, 0), memory_space=pltpu.SMEM),
            pl.BlockSpec((1, TOP_K, t), lambda i: (jnp.minimum(i + 1, last), 0, 0), memory_space=pltpu.SMEM),
            pl.BlockSpec((t, d), lambda i: (i, 0)),
            pl.BlockSpec((t, ROUTE_LANES), lambda i: (i, 0)),
            pl.BlockSpec((1,) + mod3.shape[1:], lambda i: (i // tiles_per_seq, 0, 0)),
            pl.BlockSpec((1, d), lambda i: (0, 0)),
            pl.BlockSpec(memory_space=pl.ANY),
        ],
        out_specs=pl.BlockSpec((t, d), lambda i: (i, 0)),
        out_shape=jax.ShapeDtypeStruct((n_tok, d), F32),
        scratch_shapes=[pltpu.VMEM((2, TOP_K, t, y_rows.shape[1]), y_rows.dtype),
                        pltpu.SemaphoreType.DMA((2,))],
        compiler_params=pltpu.CompilerParams(
            dimension_semantics=("arbitrary",), vmem_limit_bytes=VMEM_LIMIT_BYTES),
        name="moe_combine",
    )(pos3, pos3, x1, route, mod3, g_final, y_rows)


def kernel(x, c, w_ada, b_ada, g_norm1, w_in, conv_a, w_a_out, conv_b, b_conv_b, ln_conv_g, ln_conv_b,
           w_b_out, w_o, g_norm2, w_router_g, b_router_g, w_router_e, b_router_e, w_gate, w_up, w_down,
           g_final):
    bsz, seq_len, d = x.shape
    depth = w_ada.shape[0]
    n_tok = bsz * seq_len
    n_assign = n_tok * TOP_K
    assert seq_len % T_MIX == 0 and seq_len % T_MOVE == 0 and d % (2 * LANES) == 0
    assert n_tok % T_PLAN == 0
    assert n_assign % ROW_BLK == 0
    assert N_GROUPS + N_EXPERTS <= ROUTE_LANES
    n_rows = n_assign + N_EXPERTS * ROW_BLK

    c_pad = jnp.zeros((SUBLANES, d), F32).at[:bsz].set(c)
    xt = x.reshape(n_tok, d)
    row = lambda a: a.reshape(1, -1)
    for l in range(depth):
        mod = _mod_call(c_pad, w_ada[l], row(b_ada[l]))
        mod3 = mod[:bsz].reshape(bsz, 6, d)
        w_r = jnp.zeros((d, ROUTE_LANES), F32)
        w_r = w_r.at[:, :N_GROUPS].set(w_router_g[l]).at[:, N_GROUPS:N_GROUPS + N_EXPERTS].set(w_router_e[l])
        b_r = jnp.zeros((1, ROUTE_LANES), F32)
        b_r = b_r.at[0, :N_GROUPS].set(b_router_g[l]).at[0, N_GROUPS:N_GROUPS + N_EXPERTS].set(b_router_e[l])
        wr_hi, wr_lo = _split_bf16(w_r)
        x1, h2p, route, cnt = _mixer_call(
            xt, mod3, row(g_norm1[l]), conv_a[l], conv_b[l], row(b_conv_b[l]), row(ln_conv_g[l]),
            row(ln_conv_b[l]), row(g_norm2[l]),
            w_in[l], w_a_out[l], w_b_out[l], w_o[l], wr_hi, wr_lo, b_r, seq_len=seq_len)
        pstart_row, blk_e, next_e, n_used = _segment_layout(cnt[0], n_rows // ROW_BLK)
        pos = _plan_call(route, pstart_row)
        pos3 = pos[:, :TOP_K].reshape(n_tok // T_MOVE, T_MOVE, TOP_K).transpose(0, 2, 1)
        xs = _scatter_call(pos3, h2p, n_rows)
        y_rows = _expert_call(xs, blk_e, next_e, n_used, w_gate[l], w_up[l], w_down[l])
        xt = _combine_call(x1, route, mod3, row(g_final), y_rows, pos3, seq_len=seq_len,
                           final_norm=(l == depth - 1))
    return xt.reshape(bsz, seq_len, d)
```

```python
import functools

import jax
import jax.numpy as jnp
from jax import lax
from jax.experimental import pallas as pl
from jax.experimental.pallas import tpu as pltpu

F32 = jnp.float32
BF16 = jnp.bfloat16
U32 = jnp.uint32

N_GROUPS = 4
EXPERTS_PER_GROUP = 8
N_EXPERTS = N_GROUPS * EXPERTS_PER_GROUP
TOP_K = 2
K_SHORT = 3
K_CONF = 31
RMS_EPS = 1e-6
LN_EPS = 1e-5

LANES = 128
SUBLANES = 8
T_MIX = 256
HALO_B = 16
HALO_A = 8
CONV_ROWS = 64
ROW_BLK = 512
T_MOVE = 256
T_PLAN = 2048
W_STAGE_COLS = 512
ROUTE_LANES = LANES
REC_E, REC_W, REC_RANK = 0, 2, 4
VMEM_LIMIT_BYTES = 56 * 1024 * 1024


def _sigmoid(v):
    return 1.0 / (1.0 + jnp.exp(-v))


def _split_bf16(v):
    hi = v.astype(BF16)
    lo = (v - hi.astype(F32)).astype(BF16)
    return hi, lo


def _dot(a, b):
    return jnp.dot(a, b, preferred_element_type=F32)


def _const_spec(shape):
    nd = len(shape)
    return pl.BlockSpec(shape, lambda *_: (0,) * nd, pipeline_mode=pl.Buffered(1))


def _lane_ids(shape):
    return lax.broadcasted_iota(jnp.int32, shape, 1).astype(F32)


def _mod_kernel(c_ref, w_ref, b_ref, o_ref):
    c = c_ref[...]
    a_hi, a_lo = _split_bf16(c * _sigmoid(c))
    w_hi, w_lo = _split_bf16(w_ref[...])
    o_ref[...] = _dot(a_hi, w_hi) + _dot(a_lo, w_hi) + _dot(a_hi, w_lo) + b_ref[...]


def _mod_call(c_pad, w_ada, b_ada):
    rows, d = c_pad.shape
    n_out = w_ada.shape[1]
    blk = 1024
    return pl.pallas_call(
        _mod_kernel,
        grid=(n_out // blk,),
        in_specs=[
            pl.BlockSpec((rows, d), lambda j: (0, 0)),
            pl.BlockSpec((d, blk), lambda j: (0, j)),
            pl.BlockSpec((1, blk), lambda j: (0, j)),
        ],
        out_specs=pl.BlockSpec((rows, blk), lambda j: (0, j)),
        out_shape=jax.ShapeDtypeStruct((rows, n_out), F32),
        compiler_params=pltpu.CompilerParams(dimension_semantics=("arbitrary",)),
        name="adaln_mod",
    )(c_pad, w_ada, b_ada)


def _depthwise_lane_block(buf_ref, w_ref, out_ref, phase_ref, n_taps, row0, rows, cb):
    first, stop = rows
    max_off = row0 + n_taps - 1
    n_keep = stop - first + (max_off // SUBLANES) * SUBLANES
    cols = slice(cb * LANES, (cb + 1) * LANES)
    for s in sorted({(row0 + k) % SUBLANES for k in range(n_taps)}):
        phase_ref[s, pl.ds(0, n_keep), :] = buf_ref[pl.ds(first + s, n_keep), cols]
    row_blocks = range(first, stop, CONV_ROWS)
    acc = [None] * len(row_blocks)
    for k in range(n_taps):
        q, s = divmod(row0 + k, SUBLANES)
        w_k = jnp.broadcast_to(w_ref[k:k + 1, cols], (CONV_ROWS, LANES))
        for j, r0 in enumerate(row_blocks):
            term = phase_ref[s, pl.ds(r0 - first + q * SUBLANES, CONV_ROWS), :] * w_k
            acc[j] = term if acc[j] is None else acc[j] + term
    for j, r0 in enumerate(row_blocks):
        out_ref[pl.ds(r0, CONV_ROWS), cols] = acc[j]


def _route(logits, run_ref, live):
    shape = logits.shape
    lane = _lane_ids(shape)
    big = float(4 * ROUTE_LANES)
    neg_inf = -jnp.inf

    def first_argmax(v):
        m = jnp.max(v, axis=-1, keepdims=True)
        return m, jnp.min(jnp.where(v == m, lane, big), axis=-1, keepdims=True)

    is_group = lane < N_GROUPS
    g_max, g_idx = first_argmax(jnp.where(is_group, logits, neg_inf))
    g_prob = 1.0 / jnp.sum(jnp.where(is_group, jnp.exp(logits - g_max), 0.0), axis=-1, keepdims=True)
    lo = N_GROUPS + EXPERTS_PER_GROUP * g_idx
    in_group = (lane >= lo) & (lane < lo + EXPERTS_PER_GROUP)
    e_logits = jnp.where(in_group, logits, neg_inf)
    m1, i1 = first_argmax(e_logits)
    m2, i2 = first_argmax(jnp.where(lane == i1, neg_inf, e_logits))
    t = jnp.exp(m2 - m1)
    p1 = 1.0 / (1.0 + t)
    p2 = t / (1.0 + t)
    e1 = i1 - N_GROUPS
    e2 = i2 - N_GROUPS

    hot1 = lane == e1
    hot2 = lane == e2
    both = jnp.where(hot1 | hot2, 1.0, 0.0)
    n_rows = shape[0]
    earlier = (lax.broadcasted_iota(jnp.int32, (n_rows, n_rows), 0)
               > lax.broadcasted_iota(jnp.int32, (n_rows, n_rows), 1))
    before = _dot(jnp.where(earlier, 1.0, 0.0).astype(BF16), both.astype(BF16)) + run_ref[0:1, :]
    rank1 = jnp.sum(jnp.where(hot1, before, 0.0), axis=-1, keepdims=True)
    rank2 = jnp.sum(jnp.where(hot2, before, 0.0), axis=-1, keepdims=True)
    run_ref[0:1, :] = run_ref[0:1, :] + live * jnp.sum(both, axis=0, keepdims=True)

    rec = jnp.where(lane == REC_E, e1, 0.0)
    rec = jnp.where(lane == REC_E + 1, e2, rec)
    rec = jnp.where(lane == REC_W, g_prob * p1, rec)
    rec = jnp.where(lane == REC_W + 1, g_prob * p2, rec)
    rec = jnp.where(lane == REC_RANK, rank1, rec)
    rec = jnp.where(lane == REC_RANK + 1, rank2, rec)
    return rec


def _pack_bf16_pairs(v_bf16):
    half = v_bf16.shape[1] // 2
    lo = lax.bitcast_convert_type(v_bf16[:, :half].astype(F32), U32)
    hi = lax.bitcast_convert_type(v_bf16[:, half:].astype(F32), U32)
    return (lo >> 16) | hi


def _unpack_bf16_pairs(p_u32, dtype=BF16):
    lo = lax.bitcast_convert_type(p_u32 << 16, F32).astype(dtype)
    hi = lax.bitcast_convert_type(p_u32 & jnp.uint32(0xFFFF0000), F32).astype(dtype)
    return jnp.concatenate([lo, hi], axis=1)


def _stage_weight_bf16(src_hbm, dst_ref, stage_ref, sem):
    chunk = stage_ref.shape[2]
    n_chunks = src_hbm.shape[1] // chunk

    def cols(j):
        return pl.ds(pl.multiple_of(j * chunk, chunk), chunk)

    def copy(j, slot):
        return pltpu.make_async_copy(src_hbm.at[:, cols(j)], stage_ref.at[slot], sem.at[slot])

    copy(0, 0).start()

    def body(j, carry):
        slot = j % 2

        @pl.when(j + 1 < n_chunks)
        def _next():
            copy(j + 1, 1 - slot).start()

        copy(j, slot).wait()
        dst_ref[:, cols(j)] = stage_ref[slot].astype(BF16)
        return carry

    lax.fori_loop(0, n_chunks, body, 0)


def _mixer_kernel(x_ref, modn_ref, modp_ref, g1_ref, ca_ref, cb_ref, bcb_ref, lng_ref, lnb_ref, g2_ref,
                  wrh_ref, wrl_ref, br_ref, win_hbm, wa_hbm, wb_hbm, wo_hbm,
                  x1_ref, h2p_ref, route_ref, cnt_ref,
                  wu_ref, wcx_ref, ba_ref, sga_ref, sgb_ref, xs_ref, cva_ref, cvb_ref, run_ref, phase_ref,
                  win_ref, wa_ref, wb_ref, wo_ref, stage_ref, stage_sem,
                  *, n_tiles):
    i = pl.program_id(0)
    t_rows, d = x_ref.shape

    @pl.when(i == 0)
    def _init():
        for ref in (wu_ref, wcx_ref, ba_ref, sga_ref, sgb_ref, xs_ref, run_ref):
            ref[...] = jnp.zeros(ref.shape, ref.dtype)
        for src, dst in ((win_hbm, win_ref), (wa_hbm, wa_ref), (wb_hbm, wb_ref), (wo_hbm, wo_ref)):
            _stage_weight_bf16(src, dst, stage_ref, stage_sem)

    x = x_ref[...]
    sh1 = modn_ref[0, 0:1, :]
    sc1 = modn_ref[0, 1:2, :]
    h = (x * lax.rsqrt(jnp.mean(x * x, axis=-1, keepdims=True) + RMS_EPS)) * g1_ref[...]
    hb = (h * (1.0 + sc1) + sh1).astype(BF16)

    def proj(g):
        return _dot(hb, win_ref[:, g * d:(g + 1) * d])

    def conv_work(rows):
        items = []
        for cb in range(d // LANES):
            items.append(functools.partial(_depthwise_lane_block, wcx_ref, ca_ref, cva_ref,
                                           phase_ref.at[2], K_SHORT, HALO_A - K_SHORT // 2, rows, cb))
            items.append(functools.partial(_depthwise_lane_block, wu_ref, cb_ref, cvb_ref,
                                           phase_ref.at[cb % 2], K_CONF, HALO_B - K_CONF // 2, rows, cb))
        return items

    early = conv_work((0, t_rows - CONV_ROWS))
    n_proj = 7
    z = []
    for g in range(n_proj):
        z.append(proj(g))
        for item in early[g * len(early) // n_proj:(g + 1) * len(early) // n_proj]:
            item()
    b_a = z[0]
    cx = z[1] * z[2]
    u = z[3] * _sigmoid(z[4])
    sg_a = _sigmoid(z[5])
    sg_b = _sigmoid(z[6])

    same_seq = ((i % n_tiles) != 0).astype(F32)
    wu_ref[pl.ds(HALO_B + t_rows, HALO_B), :] = u[0:HALO_B] * same_seq
    wcx_ref[pl.ds(HALO_A + t_rows, HALO_A), :] = cx[0:HALO_A] * same_seq

    for item in conv_work((t_rows - CONV_ROWS, t_rows)):
        item()

    y_a = _dot((ba_ref[...] * cva_ref[...]).astype(BF16), wa_ref[...])
    v = cvb_ref[...] + bcb_ref[...]
    mu = jnp.mean(v, axis=-1, keepdims=True)
    vc = v - mu
    var = jnp.mean(vc * vc, axis=-1, keepdims=True)
    v = (vc * lax.rsqrt(var + LN_EPS)) * lng_ref[...] + lnb_ref[...]
    y_b = _dot((v * _sigmoid(v)).astype(BF16), wb_ref[...])
    merged = sga_ref[...] * y_a + sgb_ref[...] * y_b
    mix = _dot(merged.astype(BF16), wo_ref[...])

    gt1 = modp_ref[0, 2:3, :]
    sh2 = modp_ref[0, 3:4, :]
    sc2 = modp_ref[0, 4:5, :]
    x1 = xs_ref[...] + gt1 * mix
    x1_ref[...] = x1
    h2 = (x1 * lax.rsqrt(jnp.mean(x1 * x1, axis=-1, keepdims=True) + RMS_EPS)) * g2_ref[...]
    h2 = h2 * (1.0 + sc2) + sh2
    h2_hi, h2_lo = _split_bf16(h2)
    h2p_ref[...] = _pack_bf16_pairs(h2_hi)
    logits = _dot(h2_hi, wrh_ref[...]) + _dot(h2_lo, wrh_ref[...]) + _dot(h2_hi, wrl_ref[...]) + br_ref[...]
    route_ref[...] = _route(logits, run_ref, (i > 0).astype(F32))
    cnt_ref[...] = jnp.broadcast_to(run_ref[0:1, :], cnt_ref.shape)

    tail_u = wu_ref[pl.ds(t_rows, HALO_B), :]
    wu_ref[pl.ds(0, HALO_B), :] = tail_u * same_seq
    wu_ref[pl.ds(HALO_B, t_rows), :] = u
    tail_cx = wcx_ref[pl.ds(t_rows, HALO_A), :]
    wcx_ref[pl.ds(0, HALO_A), :] = tail_cx * same_seq
    wcx_ref[pl.ds(HALO_A, t_rows), :] = cx
    ba_ref[...] = b_a
    sga_ref[...] = sg_a
    sgb_ref[...] = sg_b
    xs_ref[...] = x


def _mixer_call(x2d, mod3, g1, conv_a, conv_b, b_conv_b, ln_g, ln_b, g2,
                w_in, w_a, w_b, w_o, wr_hi, wr_lo, b_r, *, seq_len):
    n_tok, d = x2d.shape
    t = T_MIX
    n_tiles = seq_len // t
    n_chunks = n_tok // t
    last = n_chunks - 1

    def cur(i):
        return jnp.minimum(i, last)

    def prev(i):
        return jnp.maximum(i - 1, 0)

    row_spec_prev = lambda width: pl.BlockSpec((t, width), lambda i: (prev(i), 0))
    vec = lambda a: _const_spec(a.shape)
    hbm = pl.BlockSpec(memory_space=pl.ANY)
    in_specs = [
        pl.BlockSpec((t, d), lambda i: (cur(i), 0)),
        pl.BlockSpec((1,) + mod3.shape[1:], lambda i: (cur(i) // n_tiles, 0, 0)),
        pl.BlockSpec((1,) + mod3.shape[1:], lambda i: (prev(i) // n_tiles, 0, 0)),
        vec(g1), vec(conv_a), vec(conv_b), vec(b_conv_b), vec(ln_g), vec(ln_b), vec(g2),
        vec(wr_hi), vec(wr_lo), vec(b_r),
        hbm, hbm, hbm, hbm,
    ]
    out_specs = [row_spec_prev(d), row_spec_prev(d // 2), row_spec_prev(ROUTE_LANES),
                 pl.BlockSpec((SUBLANES, ROUTE_LANES), lambda i: (0, 0))]
    out_shape = [
        jax.ShapeDtypeStruct((n_tok, d), F32),
        jax.ShapeDtypeStruct((n_tok, d // 2), U32),
        jax.ShapeDtypeStruct((n_tok, ROUTE_LANES), F32),
        jax.ShapeDtypeStruct((SUBLANES, ROUTE_LANES), F32),
    ]
    scratch = [
        pltpu.VMEM((t + 2 * HALO_B, d), F32),
        pltpu.VMEM((t + 2 * HALO_A, d), F32),
        pltpu.VMEM((t, d), F32), pltpu.VMEM((t, d), F32), pltpu.VMEM((t, d), F32), pltpu.VMEM((t, d), F32),
        pltpu.VMEM((t, d), F32), pltpu.VMEM((t, d), F32),
        pltpu.VMEM((SUBLANES, ROUTE_LANES), F32),
        pltpu.VMEM((3, SUBLANES, t + 2 * HALO_B, LANES), F32),
        pltpu.VMEM(w_in.shape, BF16), pltpu.VMEM(w_a.shape, BF16), pltpu.VMEM(w_b.shape, BF16),
        pltpu.VMEM(w_o.shape, BF16),
        pltpu.VMEM((2, d, W_STAGE_COLS), F32), pltpu.SemaphoreType.DMA((2,)),
    ]
    return pl.pallas_call(
        functools.partial(_mixer_kernel, n_tiles=n_tiles),
        grid=(n_chunks + 1,),
        in_specs=in_specs,
        out_specs=out_specs,
        out_shape=out_shape,
        scratch_shapes=scratch,
        compiler_params=pltpu.CompilerParams(
            dimension_semantics=("arbitrary",), vmem_limit_bytes=VMEM_LIMIT_BYTES),
        name="mixer_router",
    )(x2d, mod3, mod3, g1, conv_a, conv_b, b_conv_b, ln_g, ln_b, g2,
      wr_hi, wr_lo, b_r, w_in, w_a, w_b, w_o)


def _segment_layout(counts_row, n_blocks):
    counts = counts_row[:N_EXPERTS].astype(jnp.int32)
    nblk = (counts + ROW_BLK - 1) // ROW_BLK
    blk_end = jnp.cumsum(nblk)
    blk0 = blk_end - nblk
    n_used = blk_end[-1:].astype(jnp.int32)
    pstart_row = jnp.zeros((1, ROUTE_LANES), F32).at[0, :N_EXPERTS].set((blk0 * ROW_BLK).astype(F32))
    ids = jnp.arange(N_EXPERTS, dtype=jnp.int32)
    blk_e = jnp.minimum(jnp.sum(blk_end[None, :] <= jnp.arange(n_blocks, dtype=jnp.int32)[:, None], axis=1),
                        N_EXPERTS - 1).astype(jnp.int32)
    later_used = (ids[None, :] > ids[:, None]) & (nblk[None, :] > 0)
    next_used = jnp.min(jnp.where(later_used, ids[None, :], N_EXPERTS), axis=1)
    next_used = jnp.where(next_used == N_EXPERTS, -1, next_used).astype(jnp.int32)
    return pstart_row, blk_e, next_used[blk_e], n_used


def _plan_kernel(route_ref, pstart_ref, pos_ref):
    rec = route_ref[...]
    lane = _lane_ids(rec.shape)
    pos = jnp.zeros(rec.shape, F32)
    for k in range(TOP_K):
        e = rec[:, REC_E + k:REC_E + k + 1]
        seg = jnp.sum(jnp.where(lane == e, pstart_ref[...], 0.0), axis=-1, keepdims=True)
        pos = jnp.where(lane == k, seg + rec[:, REC_RANK + k:REC_RANK + k + 1], pos)
    pos_ref[...] = pos.astype(jnp.int32)


def _plan_call(route, pstart_row):
    n_tok = route.shape[0]
    t = T_PLAN
    return pl.pallas_call(
        _plan_kernel,
        grid=(n_tok // t,),
        in_specs=[pl.BlockSpec((t, ROUTE_LANES), lambda i: (i, 0)),
                  pl.BlockSpec((1, ROUTE_LANES), lambda i: (0, 0))],
        out_specs=pl.BlockSpec((t, ROUTE_LANES), lambda i: (i, 0)),
        out_shape=jax.ShapeDtypeStruct((n_tok, ROUTE_LANES), jnp.int32),
        compiler_params=pltpu.CompilerParams(dimension_semantics=("arbitrary",)),
        name="moe_plan",
    )(route, pstart_row)


def _scatter_kernel(pos_ref, rows_ref, buf_in_hbm, buf_hbm, sem):
    del buf_in_hbm
    t_rows = rows_ref.shape[0]

    for r in range(t_rows):
        for k in range(TOP_K):
            pltpu.make_async_copy(rows_ref.at[pl.ds(r, 1), :],
                                  buf_hbm.at[pl.ds(pos_ref[0, k, r], 1), :], sem).start(priority=k)
    for _ in range(TOP_K):
        pltpu.make_async_copy(rows_ref, buf_hbm.at[pl.ds(0, t_rows), :], sem).wait()


def _scatter_call(pos3, rows, n_rows):
    n_tok, width = rows.shape
    t = T_MOVE
    buf0 = jnp.zeros((n_rows, width), rows.dtype)
    return pl.pallas_call(
        _scatter_kernel,
        grid=(n_tok // t,),
        in_specs=[
            pl.BlockSpec((1, TOP_K, t), lambda i: (i, 0, 0), memory_space=pltpu.SMEM),
            pl.BlockSpec((t, width), lambda i: (i, 0)),
            pl.BlockSpec(memory_space=pl.ANY),
        ],
        out_specs=pl.BlockSpec(memory_space=pl.ANY),
        out_shape=jax.ShapeDtypeStruct((n_rows, width), rows.dtype),
        scratch_shapes=[pltpu.SemaphoreType.DMA(())],
        input_output_aliases={2: 0},
        compiler_params=pltpu.CompilerParams(dimension_semantics=("arbitrary",)),
        name="moe_scatter",
    )(pos3, rows, buf0)


def _expert_kernel(blk_e_ref, next_e_ref, n_used_ref, xs_hbm, wg_hbm, wu_hbm, wd_hbm, y_hbm,
                   wg_st, wu_st, wd_st, wg_bf, wu_bf, wd_bf, xbuf, ybuf, w_sem, in_sem, out_sem):
    n_used = n_used_ref[0]

    def rows(b):
        return pl.ds(pl.multiple_of(b * ROW_BLK, ROW_BLK), ROW_BLK)

    def load(b, slot):
        return pltpu.make_async_copy(xs_hbm.at[rows(b), :], xbuf.at[slot], in_sem.at[slot])

    def store(b, slot):
        return pltpu.make_async_copy(ybuf.at[slot], y_hbm.at[rows(b), :], out_sem.at[slot])

    def weight_copies(e, wslot):
        return [pltpu.make_async_copy(src.at[e], dst.at[wslot], w_sem.at[wslot, n])
                for n, (src, dst) in enumerate(((wg_hbm, wg_st), (wu_hbm, wu_st), (wd_hbm, wd_st)))]

    for copy in weight_copies(blk_e_ref[0], 0):
        copy.start(priority=1)
    load(0, 0).start()

    def body(b, wslot):
        e = blk_e_ref[b]
        new_expert = jnp.logical_or(b == 0, e != blk_e_ref[jnp.maximum(b - 1, 0)])
        slot = b % 2

        @pl.when(new_expert)
        def _switch_expert():
            for copy in weight_copies(e, wslot):
                copy.wait()
            wg_bf[...] = wg_st[wslot].astype(BF16)
            wu_bf[...] = wu_st[wslot].astype(BF16)
            wd_bf[...] = wd_st[wslot].astype(BF16)
            nxt = next_e_ref[b]

            @pl.when(nxt >= 0)
            def _prefetch_weights():
                for copy in weight_copies(nxt, 1 - wslot):
                    copy.start(priority=1)

        @pl.when(b + 1 < n_used)
        def _prefetch_rows():
            load(b + 1, 1 - slot).start()

        load(b, slot).wait()

        @pl.when(b >= 2)
        def _window_free():
            store(b - 2, slot).wait()

        xb = _unpack_bf16_pairs(xbuf[slot])
        a = _dot(xb, wg_bf[...])
        u = _dot(xb, wu_bf[...])
        hid = ((a * _sigmoid(a)) * u).astype(BF16)
        ybuf[slot] = _pack_bf16_pairs(_dot(hid, wd_bf[...]).astype(BF16))
        store(b, slot).start()
        return jnp.where(new_expert, 1 - wslot, wslot)

    lax.fori_loop(0, n_used, body, jnp.int32(0))

    @pl.when(n_used >= 2)
    def _drain_older():
        store(n_used - 2, n_used % 2).wait()

    store(n_used - 1, (n_used - 1) % 2).wait()

    ybuf[0] = jnp.zeros(ybuf.shape[1:], ybuf.dtype)

    def fill(b, carry):
        copy = pltpu.make_async_copy(ybuf.at[0], y_hbm.at[rows(b), :], out_sem.at[0])
        copy.start()
        copy.wait()
        return carry

    lax.fori_loop(n_used, y_hbm.shape[0] // ROW_BLK, fill, 0)


def _expert_call(xs, blk_e, next_e, n_used, w_gate, w_up, w_down):
    n_rows, half = xs.shape
    d = 2 * half
    d_e = w_gate.shape[-1]
    hbm = pl.BlockSpec(memory_space=pl.ANY)
    grid_spec = pltpu.PrefetchScalarGridSpec(
        num_scalar_prefetch=3,
        grid=(1,),
        in_specs=[hbm, hbm, hbm, hbm],
        out_specs=hbm,
        scratch_shapes=[
            pltpu.VMEM((2, d, d_e), F32), pltpu.VMEM((2, d, d_e), F32), pltpu.VMEM((2, d_e, d), F32),
            pltpu.VMEM((d, d_e), BF16), pltpu.VMEM((d, d_e), BF16), pltpu.VMEM((d_e, d), BF16),
            pltpu.VMEM((2, ROW_BLK, half), xs.dtype), pltpu.VMEM((2, ROW_BLK, half), xs.dtype),
            pltpu.SemaphoreType.DMA((2, 3)), pltpu.SemaphoreType.DMA((2,)), pltpu.SemaphoreType.DMA((2,)),
        ],
    )
    return pl.pallas_call(
        _expert_kernel,
        grid_spec=grid_spec,
        out_shape=jax.ShapeDtypeStruct((n_rows, half), xs.dtype),
        compiler_params=pltpu.CompilerParams(
            dimension_semantics=("arbitrary",), vmem_limit_bytes=VMEM_LIMIT_BYTES),
        name="moe_experts",
    )(blk_e, next_e, n_used, xs, w_gate, w_up, w_down)


def _row_gather_start(src_hbm, idx_of_row, dst, sem, n_rows):
    for r in range(n_rows):
        pltpu.make_async_copy(src_hbm.at[pl.ds(idx_of_row(r), 1), :], dst.at[pl.ds(r, 1), :],
                              sem).start(priority=r % 2)


def _row_gather_wait(src_hbm, dst, sem, n_rows):
    pltpu.make_async_copy(src_hbm.at[pl.ds(0, n_rows), :], dst, sem).wait()


def _combine_kernel(posc_ref, posn_ref, x1_ref, route_ref, mod_ref, gf_ref, y_hbm, o_ref, ybuf, sem,
                    *, n_steps, final_norm):
    i = pl.program_id(0)
    slot = i % 2
    t_rows = x1_ref.shape[0]

    def start(pos_ref, s):
        for k in range(TOP_K):
            _row_gather_start(y_hbm, lambda r, k=k: pos_ref[0, k, r], ybuf.at[s, k], sem.at[s], t_rows)

    @pl.when(i == 0)
    def _prime():
        start(posc_ref, 0)

    @pl.when(i + 1 < n_steps)
    def _prefetch():
        start(posn_ref, 1 - slot)

    for k in range(TOP_K):
        _row_gather_wait(y_hbm, ybuf.at[slot, k], sem.at[slot], t_rows)
    rec = route_ref[...]
    y0 = _unpack_bf16_pairs(ybuf[slot, 0], F32)
    y1 = _unpack_bf16_pairs(ybuf[slot, 1], F32)
    moe = y0 * rec[:, REC_W:REC_W + 1] + y1 * rec[:, REC_W + 1:REC_W + 2]
    x2 = x1_ref[...] + mod_ref[0, 5:6, :] * moe
    if final_norm:
        x2 = (x2 * lax.rsqrt(jnp.mean(x2 * x2, axis=-1, keepdims=True) + RMS_EPS)) * gf_ref[...]
    o_ref[...] = x2


def _combine_call(x1, route, mod3, g_final, y_rows, pos3, *, seq_len, final_norm):
    n_tok, d = x1.shape
    t = T_MOVE
    n_steps = n_tok // t
    tiles_per_seq = seq_len // t
    last = n_steps - 1
    return pl.pallas_call(
        functools.partial(_combine_kernel, n_steps=n_steps, final_norm=final_norm),
        grid=(n_steps,),
        in_specs=[
            pl.BlockSpec((1, TOP_K, t), lambda i: (i, 0, 0), memory_space=pltpu.SMEM),
            pl.BlockSpec((1, TOP_K, t), lambda i: (jnp.minimum(i + 1, last), 0, 0), memory_space=pltpu.SMEM),
            pl.BlockSpec((t, d), lambda i: (i, 0)),
            pl.BlockSpec((t, ROUTE_LANES), lambda i: (i, 0)),
            pl.BlockSpec((1,) + mod3.shape[1:], lambda i: (i // tiles_per_seq, 0, 0)),
            pl.BlockSpec((1, d), lambda i: (0, 0)),
            pl.BlockSpec(memory_space=pl.ANY),
        ],
        out_specs=pl.BlockSpec((t, d), lambda i: (i, 0)),
        out_shape=jax.ShapeDtypeStruct((n_tok, d), F32),
        scratch_shapes=[pltpu.VMEM((2, TOP_K, t, y_rows.shape[1]), y_rows.dtype),
                        pltpu.SemaphoreType.DMA((2,))],
        compiler_params=pltpu.CompilerParams(
            dimension_semantics=("arbitrary",), vmem_limit_bytes=VMEM_LIMIT_BYTES),
        name="moe_combine",
    )(pos3, pos3, x1, route, mod3, g_final, y_rows)


def kernel(x, c, w_ada, b_ada, g_norm1, w_in, conv_a, w_a_out, conv_b, b_conv_b, ln_conv_g, ln_conv_b,
           w_b_out, w_o, g_norm2, w_router_g, b_router_g, w_router_e, b_router_e, w_gate, w_up, w_down,
           g_final):
    bsz, seq_len, d = x.shape
    depth = w_ada.shape[0]
    n_tok = bsz * seq_len
    n_assign = n_tok * TOP_K
    assert seq_len % T_MIX == 0 and seq_len % T_MOVE == 0 and d % (2 * LANES) == 0
    assert n_tok % T_PLAN == 0
    assert n_assign % ROW_BLK == 0
    assert N_GROUPS + N_EXPERTS <= ROUTE_LANES
    n_rows = n_assign + N_EXPERTS * ROW_BLK

    c_pad = jnp.zeros((SUBLANES, d), F32).at[:bsz].set(c)
    xt = x.reshape(n_tok, d)
    row = lambda a: a.reshape(1, -1)
    for l in range(depth):
        mod = _mod_call(c_pad, w_ada[l], row(b_ada[l]))
        mod3 = mod[:bsz].reshape(bsz, 6, d)
        w_r = jnp.zeros((d, ROUTE_LANES), F32)
        w_r = w_r.at[:, :N_GROUPS].set(w_router_g[l]).at[:, N_GROUPS:N_GROUPS + N_EXPERTS].set(w_router_e[l])
        b_r = jnp.zeros((1, ROUTE_LANES), F32)
        b_r = b_r.at[0, :N_GROUPS].set(b_router_g[l]).at[0, N_GROUPS:N_GROUPS + N_EXPERTS].set(b_router_e[l])
        wr_hi, wr_lo = _split_bf16(w_r)
        x1, h2p, route, cnt = _mixer_call(
            xt, mod3, row(g_norm1[l]), conv_a[l], conv_b[l], row(b_conv_b[l]), row(ln_conv_g[l]),
            row(ln_conv_b[l]), row(g_norm2[l]),
            w_in[l], w_a_out[l], w_b_out[l], w_o[l], wr_hi, wr_lo, b_r, seq_len=seq_len)
        pstart_row, blk_e, next_e, n_used = _segment_layout(cnt[0], n_rows // ROW_BLK)
        pos = _plan_call(route, pstart_row)
        pos3 = pos[:, :TOP_K].reshape(n_tok // T_MOVE, T_MOVE, TOP_K).transpose(0, 2, 1)
        xs = _scatter_call(pos3, h2p, n_rows)
        y_rows = _expert_call(xs, blk_e, next_e, n_used, w_gate[l], w_up[l], w_down[l])
        xt = _combine_call(x1, route, mod3, row(g_final), y_rows, pos3, seq_len=seq_len,
                           final_norm=(l == depth - 1))
    return xt.reshape(bsz, seq_len, d)
```

```python
import functools

import jax
import jax.numpy as jnp
from jax import lax
from jax.experimental import pallas as pl
from jax.experimental.pallas import tpu as pltpu

F32 = jnp.float32
BF16 = jnp.bfloat16
U32 = jnp.uint32

N_GROUPS = 4
EXPERTS_PER_GROUP = 8
N_EXPERTS = N_GROUPS * EXPERTS_PER_GROUP
TOP_K = 2
K_SHORT = 3
K_CONF = 31
RMS_EPS = 1e-6
LN_EPS = 1e-5

LANES = 128
SUBLANES = 8
T_MIX = 256
HALO_B = 16
HALO_A = 8
CONV_ROWS = 64
ROW_BLK = 256
T_MOVE = 512
W_STAGE_COLS = 512
ROUTE_LANES = LANES
REC_E, REC_W, REC_RANK = 0, 2, 4
VMEM_LIMIT_BYTES = 56 * 1024 * 1024


def _sigmoid(v):
    return 1.0 / (1.0 + jnp.exp(-v))


def _split_bf16(v):
    hi = v.astype(BF16)
    lo = (v - hi.astype(F32)).astype(BF16)
    return hi, lo


def _dot(a, b):
    return jnp.dot(a, b, preferred_element_type=F32)


def _const_spec(shape):
    nd = len(shape)
    return pl.BlockSpec(shape, lambda *_: (0,) * nd, pipeline_mode=pl.Buffered(1))


def _lane_ids(shape):
    return lax.broadcasted_iota(jnp.int32, shape, 1).astype(F32)


def _mod_kernel(c_ref, w_ref, b_ref, o_ref):
    c = c_ref[...]
    a_hi, a_lo = _split_bf16(c * _sigmoid(c))
    w_hi, w_lo = _split_bf16(w_ref[...])
    o_ref[...] = _dot(a_hi, w_hi) + _dot(a_lo, w_hi) + _dot(a_hi, w_lo) + b_ref[...]


def _mod_call(c_pad, w_ada, b_ada):
    rows, d = c_pad.shape
    n_out = w_ada.shape[1]
    blk = 1024
    return pl.pallas_call(
        _mod_kernel,
        grid=(n_out // blk,),
        in_specs=[
            pl.BlockSpec((rows, d), lambda j: (0, 0)),
            pl.BlockSpec((d, blk), lambda j: (0, j)),
            pl.BlockSpec((1, blk), lambda j: (0, j)),
        ],
        out_specs=pl.BlockSpec((rows, blk), lambda j: (0, j)),
        out_shape=jax.ShapeDtypeStruct((rows, n_out), F32),
        compiler_params=pltpu.CompilerParams(dimension_semantics=("arbitrary",)),
        name="adaln_mod",
    )(c_pad, w_ada, b_ada)


def _depthwise_lane_block(buf_ref, w_ref, out_ref, phase_ref, n_taps, row0, rows, cb):
    first, stop = rows
    max_off = row0 + n_taps - 1
    n_keep = stop - first + (max_off // SUBLANES) * SUBLANES
    cols = slice(cb * LANES, (cb + 1) * LANES)
    for s in sorted({(row0 + k) % SUBLANES for k in range(n_taps)}):
        phase_ref[s, pl.ds(0, n_keep), :] = buf_ref[pl.ds(first + s, n_keep), cols]
    row_blocks = range(first, stop, CONV_ROWS)
    acc = [None] * len(row_blocks)
    for k in range(n_taps):
        q, s = divmod(row0 + k, SUBLANES)
        w_k = jnp.broadcast_to(w_ref[k:k + 1, cols], (CONV_ROWS, LANES))
        for j, r0 in enumerate(row_blocks):
            term = phase_ref[s, pl.ds(r0 - first + q * SUBLANES, CONV_ROWS), :] * w_k
            acc[j] = term if acc[j] is None else acc[j] + term
    for j, r0 in enumerate(row_blocks):
        out_ref[pl.ds(r0, CONV_ROWS), cols] = acc[j]


def _route(logits, run_ref, live):
    shape = logits.shape
    lane = _lane_ids(shape)
    big = float(4 * ROUTE_LANES)
    neg_inf = -jnp.inf

    def first_argmax(v):
        m = jnp.max(v, axis=-1, keepdims=True)
        return m, jnp.min(jnp.where(v == m, lane, big), axis=-1, keepdims=True)

    is_group = lane < N_GROUPS
    g_max, g_idx = first_argmax(jnp.where(is_group, logits, neg_inf))
    g_prob = 1.0 / jnp.sum(jnp.where(is_group, jnp.exp(logits - g_max), 0.0), axis=-1, keepdims=True)
    lo = N_GROUPS + EXPERTS_PER_GROUP * g_idx
    in_group = (lane >= lo) & (lane < lo + EXPERTS_PER_GROUP)
    e_logits = jnp.where(in_group, logits, neg_inf)
    m1, i1 = first_argmax(e_logits)
    m2, i2 = first_argmax(jnp.where(lane == i1, neg_inf, e_logits))
    t = jnp.exp(m2 - m1)
    p1 = 1.0 / (1.0 + t)
    p2 = t / (1.0 + t)
    e1 = i1 - N_GROUPS
    e2 = i2 - N_GROUPS

    hot1 = lane == e1
    hot2 = lane == e2
    both = jnp.where(hot1 | hot2, 1.0, 0.0)
    n_rows = shape[0]
    earlier = (lax.broadcasted_iota(jnp.int32, (n_rows, n_rows), 0)
               > lax.broadcasted_iota(jnp.int32, (n_rows, n_rows), 1))
    before = _dot(jnp.where(earlier, 1.0, 0.0).astype(BF16), both.astype(BF16)) + run_ref[0:1, :]
    rank1 = jnp.sum(jnp.where(hot1, before, 0.0), axis=-1, keepdims=True)
    rank2 = jnp.sum(jnp.where(hot2, before, 0.0), axis=-1, keepdims=True)
    run_ref[0:1, :] = run_ref[0:1, :] + live * jnp.sum(both, axis=0, keepdims=True)

    rec = jnp.where(lane == REC_E, e1, 0.0)
    rec = jnp.where(lane == REC_E + 1, e2, rec)
    rec = jnp.where(lane == REC_W, g_prob * p1, rec)
    rec = jnp.where(lane == REC_W + 1, g_prob * p2, rec)
    rec = jnp.where(lane == REC_RANK, rank1, rec)
    rec = jnp.where(lane == REC_RANK + 1, rank2, rec)
    return rec


def _pack_bf16_pairs(v_bf16):
    half = v_bf16.shape[1] // 2
    lo = lax.bitcast_convert_type(v_bf16[:, :half].astype(F32), U32)
    hi = lax.bitcast_convert_type(v_bf16[:, half:].astype(F32), U32)
    return (lo >> 16) | hi


def _unpack_bf16_pairs(p_u32, dtype=BF16):
    lo = lax.bitcast_convert_type(p_u32 << 16, F32).astype(dtype)
    hi = lax.bitcast_convert_type(p_u32 & jnp.uint32(0xFFFF0000), F32).astype(dtype)
    return jnp.concatenate([lo, hi], axis=1)


def _stage_weight_bf16(src_hbm, dst_ref, stage_ref, sem):
    chunk = stage_ref.shape[2]
    n_chunks = src_hbm.shape[1] // chunk

    def cols(j):
        return pl.ds(pl.multiple_of(j * chunk, chunk), chunk)

    def copy(j, slot):
        return pltpu.make_async_copy(src_hbm.at[:, cols(j)], stage_ref.at[slot], sem.at[slot])

    copy(0, 0).start()

    def body(j, carry):
        slot = j % 2

        @pl.when(j + 1 < n_chunks)
        def _next():
            copy(j + 1, 1 - slot).start()

        copy(j, slot).wait()
        dst_ref[:, cols(j)] = stage_ref[slot].astype(BF16)
        return carry

    lax.fori_loop(0, n_chunks, body, 0)


def _mixer_kernel(x_ref, modn_ref, modp_ref, g1_ref, ca_ref, cb_ref, bcb_ref, lng_ref, lnb_ref, g2_ref,
                  wrh_ref, wrl_ref, br_ref, win_hbm, wa_hbm, wb_hbm, wo_hbm,
                  x1_ref, h2p_ref, route_ref, cnt_ref,
                  wu_ref, wcx_ref, ba_ref, sga_ref, sgb_ref, xs_ref, cva_ref, cvb_ref, run_ref, phase_ref,
                  win_ref, wa_ref, wb_ref, wo_ref, stage_ref, stage_sem,
                  *, n_tiles):
    i = pl.program_id(0)
    t_rows, d = x_ref.shape

    @pl.when(i == 0)
    def _init():
        for ref in (wu_ref, wcx_ref, ba_ref, sga_ref, sgb_ref, xs_ref, run_ref):
            ref[...] = jnp.zeros(ref.shape, ref.dtype)
        for src, dst in ((win_hbm, win_ref), (wa_hbm, wa_ref), (wb_hbm, wb_ref), (wo_hbm, wo_ref)):
            _stage_weight_bf16(src, dst, stage_ref, stage_sem)

    x = x_ref[...]
    sh1 = modn_ref[0, 0:1, :]
    sc1 = modn_ref[0, 1:2, :]
    h = (x * lax.rsqrt(jnp.mean(x * x, axis=-1, keepdims=True) + RMS_EPS)) * g1_ref[...]
    hb = (h * (1.0 + sc1) + sh1).astype(BF16)

    def proj(g):
        return _dot(hb, win_ref[:, g * d:(g + 1) * d])

    def conv_work(rows):
        items = []
        for cb in range(d // LANES):
            items.append(functools.partial(_depthwise_lane_block, wcx_ref, ca_ref, cva_ref,
                                           phase_ref.at[2], K_SHORT, HALO_A - K_SHORT // 2, rows, cb))
            items.append(functools.partial(_depthwise_lane_block, wu_ref, cb_ref, cvb_ref,
                                           phase_ref.at[cb % 2], K_CONF, HALO_B - K_CONF // 2, rows, cb))
        return items

    early = conv_work((0, t_rows - CONV_ROWS))
    n_proj = 7
    z = []
    for g in range(n_proj):
        z.append(proj(g))
        for item in early[g * len(early) // n_proj:(g + 1) * len(early) // n_proj]:
            item()
    b_a = z[0]
    cx = z[1] * z[2]
    u = z[3] * _sigmoid(z[4])
    sg_a = _sigmoid(z[5])
    sg_b = _sigmoid(z[6])

    same_seq = ((i % n_tiles) != 0).astype(F32)
    wu_ref[pl.ds(HALO_B + t_rows, HALO_B), :] = u[0:HALO_B] * same_seq
    wcx_ref[pl.ds(HALO_A + t_rows, HALO_A), :] = cx[0:HALO_A] * same_seq

    for item in conv_work((t_rows - CONV_ROWS, t_rows)):
        item()

    y_a = _dot((ba_ref[...] * cva_ref[...]).astype(BF16), wa_ref[...])
    v = cvb_ref[...] + bcb_ref[...]
    mu = jnp.mean(v, axis=-1, keepdims=True)
    vc = v - mu
    var = jnp.mean(vc * vc, axis=-1, keepdims=True)
    v = (vc * lax.rsqrt(var + LN_EPS)) * lng_ref[...] + lnb_ref[...]
    y_b = _dot((v * _sigmoid(v)).astype(BF16), wb_ref[...])
    merged = sga_ref[...] * y_a + sgb_ref[...] * y_b
    mix = _dot(merged.astype(BF16), wo_ref[...])

    gt1 = modp_ref[0, 2:3, :]
    sh2 = modp_ref[0, 3:4, :]
    sc2 = modp_ref[0, 4:5, :]
    x1 = xs_ref[...] + gt1 * mix
    x1_ref[...] = x1
    h2 = (x1 * lax.rsqrt(jnp.mean(x1 * x1, axis=-1, keepdims=True) + RMS_EPS)) * g2_ref[...]
    h2 = h2 * (1.0 + sc2) + sh2
    h2_hi, h2_lo = _split_bf16(h2)
    h2p_ref[...] = _pack_bf16_pairs(h2_hi)
    logits = _dot(h2_hi, wrh_ref[...]) + _dot(h2_lo, wrh_ref[...]) + _dot(h2_hi, wrl_ref[...]) + br_ref[...]
    route_ref[...] = _route(logits, run_ref, (i > 0).astype(F32))
    cnt_ref[...] = jnp.broadcast_to(run_ref[0:1, :], cnt_ref.shape)

    tail_u = wu_ref[pl.ds(t_rows, HALO_B), :]
    wu_ref[pl.ds(0, HALO_B), :] = tail_u * same_seq
    wu_ref[pl.ds(HALO_B, t_rows), :] = u
    tail_cx = wcx_ref[pl.ds(t_rows, HALO_A), :]
    wcx_ref[pl.ds(0, HALO_A), :] = tail_cx * same_seq
    wcx_ref[pl.ds(HALO_A, t_rows), :] = cx
    ba_ref[...] = b_a
    sga_ref[...] = sg_a
    sgb_ref[...] = sg_b
    xs_ref[...] = x


def _mixer_call(x2d, mod3, g1, conv_a, conv_b, b_conv_b, ln_g, ln_b, g2,
                w_in, w_a, w_b, w_o, wr_hi, wr_lo, b_r, *, seq_len):
    n_tok, d = x2d.shape
    t = T_MIX
    n_tiles = seq_len // t
    n_chunks = n_tok // t
    last = n_chunks - 1

    def cur(i):
        return jnp.minimum(i, last)

    def prev(i):
        return jnp.maximum(i - 1, 0)

    row_spec_prev = lambda width: pl.BlockSpec((t, width), lambda i: (prev(i), 0))
    vec = lambda a: _const_spec(a.shape)
    hbm = pl.BlockSpec(memory_space=pl.ANY)
    in_specs = [
        pl.BlockSpec((t, d), lambda i: (cur(i), 0)),
        pl.BlockSpec((1,) + mod3.shape[1:], lambda i: (cur(i) // n_tiles, 0, 0)),
        pl.BlockSpec((1,) + mod3.shape[1:], lambda i: (prev(i) // n_tiles, 0, 0)),
        vec(g1), vec(conv_a), vec(conv_b), vec(b_conv_b), vec(ln_g), vec(ln_b), vec(g2),
        vec(wr_hi), vec(wr_lo), vec(b_r),
        hbm, hbm, hbm, hbm,
    ]
    out_specs = [row_spec_prev(d), row_spec_prev(d // 2), row_spec_prev(ROUTE_LANES),
                 pl.BlockSpec((SUBLANES, ROUTE_LANES), lambda i: (0, 0))]
    out_shape = [
        jax.ShapeDtypeStruct((n_tok, d), F32),
        jax.ShapeDtypeStruct((n_tok, d // 2), U32),
        jax.ShapeDtypeStruct((n_tok, ROUTE_LANES), F32),
        jax.ShapeDtypeStruct((SUBLANES, ROUTE_LANES), F32),
    ]
    scratch = [
        pltpu.VMEM((t + 2 * HALO_B, d), F32),
        pltpu.VMEM((t + 2 * HALO_A, d), F32),
        pltpu.VMEM((t, d), F32), pltpu.VMEM((t, d), F32), pltpu.VMEM((t, d), F32), pltpu.VMEM((t, d), F32),
        pltpu.VMEM((t, d), F32), pltpu.VMEM((t, d), F32),
        pltpu.VMEM((SUBLANES, ROUTE_LANES), F32),
        pltpu.VMEM((3, SUBLANES, t + 2 * HALO_B, LANES), F32),
        pltpu.VMEM(w_in.shape, BF16), pltpu.VMEM(w_a.shape, BF16), pltpu.VMEM(w_b.shape, BF16),
        pltpu.VMEM(w_o.shape, BF16),
        pltpu.VMEM((2, d, W_STAGE_COLS), F32), pltpu.SemaphoreType.DMA((2,)),
    ]
    return pl.pallas_call(
        functools.partial(_mixer_kernel, n_tiles=n_tiles),
        grid=(n_chunks + 1,),
        in_specs=in_specs,
        out_specs=out_specs,
        out_shape=out_shape,
        scratch_shapes=scratch,
        compiler_params=pltpu.CompilerParams(
            dimension_semantics=("arbitrary",), vmem_limit_bytes=VMEM_LIMIT_BYTES),
        name="mixer_router",
    )(x2d, mod3, mod3, g1, conv_a, conv_b, b_conv_b, ln_g, ln_b, g2,
      wr_hi, wr_lo, b_r, w_in, w_a, w_b, w_o)


def _segment_layout(counts_row, n_blocks):
    counts = counts_row[:N_EXPERTS].astype(jnp.int32)
    nblk = (counts + ROW_BLK - 1) // ROW_BLK
    blk_end = jnp.cumsum(nblk)
    blk0 = blk_end - nblk
    n_used = blk_end[-1:].astype(jnp.int32)
    pstart_row = jnp.zeros((1, ROUTE_LANES), F32).at[0, :N_EXPERTS].set((blk0 * ROW_BLK).astype(F32))
    ids = jnp.arange(N_EXPERTS, dtype=jnp.int32)
    blk_e = jnp.minimum(jnp.sum(blk_end[None, :] <= jnp.arange(n_blocks, dtype=jnp.int32)[:, None], axis=1),
                        N_EXPERTS - 1).astype(jnp.int32)
    later_used = (ids[None, :] > ids[:, None]) & (nblk[None, :] > 0)
    next_used = jnp.min(jnp.where(later_used, ids[None, :], N_EXPERTS), axis=1)
    next_used = jnp.where(next_used == N_EXPERTS, -1, next_used).astype(jnp.int32)
    return pstart_row, blk_e, next_used[blk_e], n_used


def _plan_kernel(route_ref, pstart_ref, pos_ref):
    rec = route_ref[...]
    lane = _lane_ids(rec.shape)
    pos = jnp.zeros(rec.shape, F32)
    for k in range(TOP_K):
        e = rec[:, REC_E + k:REC_E + k + 1]
        seg = jnp.sum(jnp.where(lane == e, pstart_ref[...], 0.0), axis=-1, keepdims=True)
        pos = jnp.where(lane == k, seg + rec[:, REC_RANK + k:REC_RANK + k + 1], pos)
    pos_ref[0] = pos.T[0:SUBLANES, :].astype(jnp.int32)


def _plan_call(route, pstart_row):
    n_tok = route.shape[0]
    t = T_MOVE
    return pl.pallas_call(
        _plan_kernel,
        grid=(n_tok // t,),
        in_specs=[pl.BlockSpec((t, ROUTE_LANES), lambda i: (i, 0)),
                  pl.BlockSpec((1, ROUTE_LANES), lambda i: (0, 0))],
        out_specs=pl.BlockSpec((1, SUBLANES, t), lambda i: (i, 0, 0)),
        out_shape=jax.ShapeDtypeStruct((n_tok // t, SUBLANES, t), jnp.int32),
        compiler_params=pltpu.CompilerParams(dimension_semantics=("arbitrary",)),
        name="moe_plan",
    )(route, pstart_row)


def _scatter_kernel(pos_ref, rows_ref, buf_in_hbm, buf_hbm, sem):
    del buf_in_hbm
    t_rows = rows_ref.shape[0]

    for r in range(t_rows):
        for k in range(TOP_K):
            pltpu.make_async_copy(rows_ref.at[pl.ds(r, 1), :],
                                  buf_hbm.at[pl.ds(pos_ref[0, k, r], 1), :], sem).start(priority=k)
    for _ in range(TOP_K):
        pltpu.make_async_copy(rows_ref, buf_hbm.at[pl.ds(0, t_rows), :], sem).wait()


def _scatter_call(pos3, rows, n_rows):
    n_tok, width = rows.shape
    t = T_MOVE
    buf0 = jnp.zeros((n_rows, width), rows.dtype)
    return pl.pallas_call(
        _scatter_kernel,
        grid=(n_tok // t,),
        in_specs=[
            pl.BlockSpec((1, SUBLANES, t), lambda i: (i, 0, 0), memory_space=pltpu.SMEM),
            pl.BlockSpec((t, width), lambda i: (i, 0)),
            pl.BlockSpec(memory_space=pl.ANY),
        ],
        out_specs=pl.BlockSpec(memory_space=pl.ANY),
        out_shape=jax.ShapeDtypeStruct((n_rows, width), rows.dtype),
        scratch_shapes=[pltpu.SemaphoreType.DMA(())],
        input_output_aliases={2: 0},
        compiler_params=pltpu.CompilerParams(dimension_semantics=("arbitrary",)),
        name="moe_scatter",
    )(pos3, rows, buf0)


def _expert_kernel(blk_e_ref, next_e_ref, n_used_ref, xs_hbm, wg_hbm, wu_hbm, wd_hbm, y_hbm,
                   wg_st, wu_st, wd_st, wg_bf, wu_bf, wd_bf, xbuf, ybuf, w_sem, in_sem, out_sem):
    n_used = n_used_ref[0]

    def rows(b):
        return pl.ds(pl.multiple_of(b * ROW_BLK, ROW_BLK), ROW_BLK)

    def load(b, slot):
        return pltpu.make_async_copy(xs_hbm.at[rows(b), :], xbuf.at[slot], in_sem.at[slot])

    def store(b, slot):
        return pltpu.make_async_copy(ybuf.at[slot], y_hbm.at[rows(b), :], out_sem.at[slot])

    def weight_copies(e, wslot):
        return [pltpu.make_async_copy(src.at[e], dst.at[wslot], w_sem.at[wslot, n])
                for n, (src, dst) in enumerate(((wg_hbm, wg_st), (wu_hbm, wu_st), (wd_hbm, wd_st)))]

    for copy in weight_copies(blk_e_ref[0], 0):
        copy.start(priority=1)
    load(0, 0).start()

    def body(b, wslot):
        e = blk_e_ref[b]
        new_expert = jnp.logical_or(b == 0, e != blk_e_ref[jnp.maximum(b - 1, 0)])
        slot = b % 2

        @pl.when(new_expert)
        def _switch_expert():
            for copy in weight_copies(e, wslot):
                copy.wait()
            wg_bf[...] = wg_st[wslot].astype(BF16)
            wu_bf[...] = wu_st[wslot].astype(BF16)
            wd_bf[...] = wd_st[wslot].astype(BF16)
            nxt = next_e_ref[b]

            @pl.when(nxt >= 0)
            def _prefetch_weights():
                for copy in weight_copies(nxt, 1 - wslot):
                    copy.start(priority=1)

        @pl.when(b + 1 < n_used)
        def _prefetch_rows():
            load(b + 1, 1 - slot).start()

        load(b, slot).wait()

        @pl.when(b >= 2)
        def _window_free():
            store(b - 2, slot).wait()

        xb = _unpack_bf16_pairs(xbuf[slot])
        a = _dot(xb, wg_bf[...])
        u = _dot(xb, wu_bf[...])
        hid = ((a * _sigmoid(a)) * u).astype(BF16)
        ybuf[slot] = _pack_bf16_pairs(_dot(hid, wd_bf[...]).astype(BF16))
        store(b, slot).start()
        return jnp.where(new_expert, 1 - wslot, wslot)

    lax.fori_loop(0, n_used, body, jnp.int32(0))

    @pl.when(n_used >= 2)
    def _drain_older():
        store(n_used - 2, n_used % 2).wait()

    store(n_used - 1, (n_used - 1) % 2).wait()

    ybuf[0] = jnp.zeros(ybuf.shape[1:], ybuf.dtype)

    def fill(b, carry):
        copy = pltpu.make_async_copy(ybuf.at[0], y_hbm.at[rows(b), :], out_sem.at[0])
        copy.start()
        copy.wait()
        return carry

    lax.fori_loop(n_used, y_hbm.shape[0] // ROW_BLK, fill, 0)


def _expert_call(xs, blk_e, next_e, n_used, w_gate, w_up, w_down):
    n_rows, half = xs.shape
    d = 2 * half
    d_e = w_gate.shape[-1]
    hbm = pl.BlockSpec(memory_space=pl.ANY)
    grid_spec = pltpu.PrefetchScalarGridSpec(
        num_scalar_prefetch=3,
        grid=(1,),
        in_specs=[hbm, hbm, hbm, hbm],
        out_specs=hbm,
        scratch_shapes=[
            pltpu.VMEM((2, d, d_e), F32), pltpu.VMEM((2, d, d_e), F32), pltpu.VMEM((2, d_e, d), F32),
            pltpu.VMEM((d, d_e), BF16), pltpu.VMEM((d, d_e), BF16), pltpu.VMEM((d_e, d), BF16),
            pltpu.VMEM((2, ROW_BLK, half), xs.dtype), pltpu.VMEM((2, ROW_BLK, half), xs.dtype),
            pltpu.SemaphoreType.DMA((2, 3)), pltpu.SemaphoreType.DMA((2,)), pltpu.SemaphoreType.DMA((2,)),
        ],
    )
    return pl.pallas_call(
        _expert_kernel,
        grid_spec=grid_spec,
        out_shape=jax.ShapeDtypeStruct((n_rows, half), xs.dtype),
        compiler_params=pltpu.CompilerParams(
            dimension_semantics=("arbitrary",), vmem_limit_bytes=VMEM_LIMIT_BYTES),
        name="moe_experts",
    )(blk_e, next_e, n_used, xs, w_gate, w_up, w_down)


def _row_gather_start(src_hbm, idx_of_row, dst, sem, n_rows):
    for r in range(n_rows):
        pltpu.make_async_copy(src_hbm.at[pl.ds(idx_of_row(r), 1), :], dst.at[pl.ds(r, 1), :],
                              sem).start(priority=r % 2)


def _row_gather_wait(src_hbm, dst, sem, n_rows):
    pltpu.make_async_copy(src_hbm.at[pl.ds(0, n_rows), :], dst, sem).wait()


def _combine_kernel(posc_ref, posn_ref, x1_ref, route_ref, mod_ref, gf_ref, y_hbm, o_ref, ybuf, sem,
                    *, n_steps, final_norm):
    i = pl.program_id(0)
    slot = i % 2
    t_rows = x1_ref.shape[0]

    def start(pos_ref, s):
        for k in range(TOP_K):
            _row_gather_start(y_hbm, lambda r, k=k: pos_ref[0, k, r], ybuf.at[s, k], sem.at[s], t_rows)

    @pl.when(i == 0)
    def _prime():
        start(posc_ref, 0)

    @pl.when(i + 1 < n_steps)
    def _prefetch():
        start(posn_ref, 1 - slot)

    for k in range(TOP_K):
        _row_gather_wait(y_hbm, ybuf.at[slot, k], sem.at[slot], t_rows)
    rec = route_ref[...]
    y0 = _unpack_bf16_pairs(ybuf[slot, 0], F32)
    y1 = _unpack_bf16_pairs(ybuf[slot, 1], F32)
    moe = y0 * rec[:, REC_W:REC_W + 1] + y1 * rec[:, REC_W + 1:REC_W + 2]
    x2 = x1_ref[...] + mod_ref[0, 5:6, :] * moe
    if final_norm:
        x2 = (x2 * lax.rsqrt(jnp.mean(x2 * x2, axis=-1, keepdims=True) + RMS_EPS)) * gf_ref[...]
    o_ref[...] = x2


def _combine_call(x1, route, mod3, g_final, y_rows, pos3, *, seq_len, final_norm):
    n_tok, d = x1.shape
    t = T_MOVE
    n_steps = n_tok // t
    tiles_per_seq = seq_len // t
    last = n_steps - 1
    return pl.pallas_call(
        functools.partial(_combine_kernel, n_steps=n_steps, final_norm=final_norm),
        grid=(n_steps,),
        in_specs=[
            pl.BlockSpec((1, SUBLANES, t), lambda i: (i, 0, 0), memory_space=pltpu.SMEM),
            pl.BlockSpec((1, SUBLANES, t), lambda i: (jnp.minimum(i + 1, last), 0, 0),
                         memory_space=pltpu.SMEM),
            pl.BlockSpec((t, d), lambda i: (i, 0)),
            pl.BlockSpec((t, ROUTE_LANES), lambda i: (i, 0)),
            pl.BlockSpec((1,) + mod3.shape[1:], lambda i: (i // tiles_per_seq, 0, 0)),
            pl.BlockSpec((1, d), lambda i: (0, 0)),
            pl.BlockSpec(memory_space=pl.ANY),
        ],
        out_specs=pl.BlockSpec((t, d), lambda i: (i, 0)),
        out_shape=jax.ShapeDtypeStruct((n_tok, d), F32),
        scratch_shapes=[pltpu.VMEM((2, TOP_K, t, y_rows.shape[1]), y_rows.dtype),
                        pltpu.SemaphoreType.DMA((2,))],
        compiler_params=pltpu.CompilerParams(
            dimension_semantics=("arbitrary",), vmem_limit_bytes=VMEM_LIMIT_BYTES),
        name="moe_combine",
    )(pos3, pos3, x1, route, mod3, g_final, y_rows)


def kernel(x, c, w_ada, b_ada, g_norm1, w_in, conv_a, w_a_out, conv_b, b_conv_b, ln_conv_g, ln_conv_b,
           w_b_out, w_o, g_norm2, w_router_g, b_router_g, w_router_e, b_router_e, w_gate, w_up, w_down,
           g_final):
    bsz, seq_len, d = x.shape
    depth = w_ada.shape[0]
    n_tok = bsz * seq_len
    n_assign = n_tok * TOP_K
    assert seq_len % T_MIX == 0 and seq_len % T_MOVE == 0 and d % (2 * LANES) == 0
    assert n_assign % ROW_BLK == 0
    assert N_GROUPS + N_EXPERTS <= ROUTE_LANES
    n_rows = n_assign + N_EXPERTS * ROW_BLK

    c_pad = jnp.zeros((SUBLANES, d), F32).at[:bsz].set(c)
    xt = x.reshape(n_tok, d)
    row = lambda a: a.reshape(1, -1)
    for l in range(depth):
        mod = _mod_call(c_pad, w_ada[l], row(b_ada[l]))
        mod3 = mod[:bsz].reshape(bsz, 6, d)
        w_r = jnp.zeros((d, ROUTE_LANES), F32)
        w_r = w_r.at[:, :N_GROUPS].set(w_router_g[l]).at[:, N_GROUPS:N_GROUPS + N_EXPERTS].set(w_router_e[l])
        b_r = jnp.zeros((1, ROUTE_LANES), F32)
        b_r = b_r.at[0, :N_GROUPS].set(b_router_g[l]).at[0, N_GROUPS:N_GROUPS + N_EXPERTS].set(b_router_e[l])
        wr_hi, wr_lo = _split_bf16(w_r)
        x1, h2p, route, cnt = _mixer_call(
            xt, mod3, row(g_norm1[l]), conv_a[l], conv_b[l], row(b_conv_b[l]), row(ln_conv_g[l]),
            row(ln_conv_b[l]), row(g_norm2[l]),
            w_in[l], w_a_out[l], w_b_out[l], w_o[l], wr_hi, wr_lo, b_r, seq_len=seq_len)
        pstart_row, blk_e, next_e, n_used = _segment_layout(cnt[0], n_rows // ROW_BLK)
        pos3 = _plan_call(route, pstart_row)
        xs = _scatter_call(pos3, h2p, n_rows)
        y_rows = _expert_call(xs, blk_e, next_e, n_used, w_gate[l], w_up[l], w_down[l])
        xt = _combine_call(x1, route, mod3, row(g_final), y_rows, pos3, seq_len=seq_len,
                           final_norm=(l == depth - 1))
    return xt.reshape(bsz, seq_len, d)
```

```python
import functools

import jax
import jax.numpy as jnp
from jax import lax
from jax.experimental import pallas as pl
from jax.experimental.pallas import tpu as pltpu

F32 = jnp.float32
BF16 = jnp.bfloat16
U32 = jnp.uint32

N_GROUPS = 4
EXPERTS_PER_GROUP = 8
N_EXPERTS = N_GROUPS * EXPERTS_PER_GROUP
TOP_K = 2
K_SHORT = 3
K_CONF = 31
RMS_EPS = 1e-6
LN_EPS = 1e-5

LANES = 128
SUBLANES = 8
T_MIX = 256
HALO_B = 16
HALO_A = 8
CONV_ROWS = 64
ROW_BLK = 512
T_MOVE = 512
T_PLAN = 2048
W_STAGE_COLS = 512
ROUTE_LANES = LANES
REC_E, REC_W, REC_RANK = 0, 2, 4
VMEM_LIMIT_BYTES = 56 * 1024 * 1024


def _sigmoid(v):
    return 1.0 / (1.0 + jnp.exp(-v))


def _split_bf16(v):
    hi = v.astype(BF16)
    lo = (v - hi.astype(F32)).astype(BF16)
    return hi, lo


def _dot(a, b):
    return jnp.dot(a, b, preferred_element_type=F32)


def _const_spec(shape):
    nd = len(shape)
    return pl.BlockSpec(shape, lambda *_: (0,) * nd, pipeline_mode=pl.Buffered(1))


def _lane_ids(shape):
    return lax.broadcasted_iota(jnp.int32, shape, 1).astype(F32)


def _mod_kernel(c_ref, w_ref, b_ref, o_ref):
    c = c_ref[...]
    a_hi, a_lo = _split_bf16(c * _sigmoid(c))
    w_hi, w_lo = _split_bf16(w_ref[...])
    o_ref[...] = _dot(a_hi, w_hi) + _dot(a_lo, w_hi) + _dot(a_hi, w_lo) + b_ref[...]


def _mod_call(c_pad, w_ada, b_ada):
    rows, d = c_pad.shape
    n_out = w_ada.shape[1]
    blk = 1024
    return pl.pallas_call(
        _mod_kernel,
        grid=(n_out // blk,),
        in_specs=[
            pl.BlockSpec((rows, d), lambda j: (0, 0)),
            pl.BlockSpec((d, blk), lambda j: (0, j)),
            pl.BlockSpec((1, blk), lambda j: (0, j)),
        ],
        out_specs=pl.BlockSpec((rows, blk), lambda j: (0, j)),
        out_shape=jax.ShapeDtypeStruct((rows, n_out), F32),
        compiler_params=pltpu.CompilerParams(dimension_semantics=("arbitrary",)),
        name="adaln_mod",
    )(c_pad, w_ada, b_ada)


def _zero_bits_of(v):
    return (lax.bitcast_convert_type(v, U32) >> 16) >> 16


def _depthwise_lane_block(buf_ref, w_ref, out_ref, phase_ref, n_taps, row0, rows, cb, after=None):
    first, stop = rows
    max_off = row0 + n_taps - 1
    n_keep = stop - first + (max_off // SUBLANES) * SUBLANES
    cols = slice(cb * LANES, (cb + 1) * LANES)
    for s in sorted({(row0 + k) % SUBLANES for k in range(n_taps)}):
        phase_ref[s, pl.ds(0, n_keep), :] = buf_ref[pl.ds(first + s, n_keep), cols]
    row_blocks = range(first, stop, CONV_ROWS)
    acc = [None] * len(row_blocks)
    for k in range(n_taps):
        q, s = divmod(row0 + k, SUBLANES)
        w_row = w_ref[k:k + 1, cols]
        if after is not None:
            w_row = lax.bitcast_convert_type(lax.bitcast_convert_type(w_row, U32) | after, F32)
        w_k = jnp.broadcast_to(w_row, (CONV_ROWS, LANES))
        for j, r0 in enumerate(row_blocks):
            term = phase_ref[s, pl.ds(r0 - first + q * SUBLANES, CONV_ROWS), :] * w_k
            acc[j] = term if acc[j] is None else acc[j] + term
    for j, r0 in enumerate(row_blocks):
        out_ref[pl.ds(r0, CONV_ROWS), cols] = acc[j]
    return _zero_bits_of(acc[-1][0:1, :])


def _route(logits, run_ref, live):
    shape = logits.shape
    lane = _lane_ids(shape)
    big = float(4 * ROUTE_LANES)
    neg_inf = -jnp.inf

    def first_argmax(v):
        m = jnp.max(v, axis=-1, keepdims=True)
        return m, jnp.min(jnp.where(v == m, lane, big), axis=-1, keepdims=True)

    is_group = lane < N_GROUPS
    g_max, g_idx = first_argmax(jnp.where(is_group, logits, neg_inf))
    g_prob = 1.0 / jnp.sum(jnp.where(is_group, jnp.exp(logits - g_max), 0.0), axis=-1, keepdims=True)
    lo = N_GROUPS + EXPERTS_PER_GROUP * g_idx
    in_group = (lane >= lo) & (lane < lo + EXPERTS_PER_GROUP)
    e_logits = jnp.where(in_group, logits, neg_inf)
    m1, i1 = first_argmax(e_logits)
    m2, i2 = first_argmax(jnp.where(lane == i1, neg_inf, e_logits))
    t = jnp.exp(m2 - m1)
    p1 = 1.0 / (1.0 + t)
    p2 = t / (1.0 + t)
    e1 = i1 - N_GROUPS
    e2 = i2 - N_GROUPS

    hot1 = lane == e1
    hot2 = lane == e2
    both = jnp.where(hot1 | hot2, 1.0, 0.0)
    n_rows = shape[0]
    earlier = (lax.broadcasted_iota(jnp.int32, (n_rows, n_rows), 0)
               > lax.broadcasted_iota(jnp.int32, (n_rows, n_rows), 1))
    before = _dot(jnp.where(earlier, 1.0, 0.0).astype(BF16), both.astype(BF16)) + run_ref[0:1, :]
    rank1 = jnp.sum(jnp.where(hot1, before, 0.0), axis=-1, keepdims=True)
    rank2 = jnp.sum(jnp.where(hot2, before, 0.0), axis=-1, keepdims=True)
    run_ref[0:1, :] = run_ref[0:1, :] + live * jnp.sum(both, axis=0, keepdims=True)

    rec = jnp.where(lane == REC_E, e1, 0.0)
    rec = jnp.where(lane == REC_E + 1, e2, rec)
    rec = jnp.where(lane == REC_W, g_prob * p1, rec)
    rec = jnp.where(lane == REC_W + 1, g_prob * p2, rec)
    rec = jnp.where(lane == REC_RANK, rank1, rec)
    rec = jnp.where(lane == REC_RANK + 1, rank2, rec)
    return rec


def _pack_bf16_pairs(v_bf16):
    half = v_bf16.shape[1] // 2
    lo = lax.bitcast_convert_type(v_bf16[:, :half].astype(F32), U32)
    hi = lax.bitcast_convert_type(v_bf16[:, half:].astype(F32), U32)
    return (lo >> 16) | hi


def _unpack_bf16_pairs(p_u32, dtype=BF16):
    lo = lax.bitcast_convert_type(p_u32 << 16, F32).astype(dtype)
    hi = lax.bitcast_convert_type(p_u32 & jnp.uint32(0xFFFF0000), F32).astype(dtype)
    return jnp.concatenate([lo, hi], axis=1)


def _stage_weight_bf16(src_hbm, dst_ref, stage_ref, sem):
    chunk = stage_ref.shape[2]
    n_chunks = src_hbm.shape[1] // chunk

    def cols(j):
        return pl.ds(pl.multiple_of(j * chunk, chunk), chunk)

    def copy(j, slot):
        return pltpu.make_async_copy(src_hbm.at[:, cols(j)], stage_ref.at[slot], sem.at[slot])

    copy(0, 0).start()

    def body(j, carry):
        slot = j % 2

        @pl.when(j + 1 < n_chunks)
        def _next():
            copy(j + 1, 1 - slot).start()

        copy(j, slot).wait()
        dst_ref[:, cols(j)] = stage_ref[slot].astype(BF16)
        return carry

    lax.fori_loop(0, n_chunks, body, 0)


def _mixer_kernel(x_ref, modn_ref, modp_ref, g1_ref, ca_ref, cb_ref, bcb_ref, lng_ref, lnb_ref, g2_ref,
                  wrh_ref, wrl_ref, br_ref, win_hbm, wa_hbm, wb_hbm, wo_hbm,
                  x1_ref, h2p_ref, route_ref, cnt_ref,
                  wu_ref, wcx_ref, ba_ref, sga_ref, sgb_ref, xs_ref, cva_ref, cvb_ref, run_ref, phase_ref,
                  win_ref, wa_ref, wb_ref, wo_ref, stage_ref, stage_sem,
                  *, n_tiles):
    i = pl.program_id(0)
    t_rows, d = x_ref.shape

    @pl.when(i == 0)
    def _init():
        for ref in (wu_ref, wcx_ref, ba_ref, sga_ref, sgb_ref, xs_ref, run_ref):
            ref[...] = jnp.zeros(ref.shape, ref.dtype)
        for src, dst in ((win_hbm, win_ref), (wa_hbm, wa_ref), (wb_hbm, wb_ref), (wo_hbm, wo_ref)):
            _stage_weight_bf16(src, dst, stage_ref, stage_sem)

    x = x_ref[...]
    sh1 = modn_ref[0, 0:1, :]
    sc1 = modn_ref[0, 1:2, :]
    h = (x * lax.rsqrt(jnp.mean(x * x, axis=-1, keepdims=True) + RMS_EPS)) * g1_ref[...]
    hb = (h * (1.0 + sc1) + sh1).astype(BF16)

    def proj(g, after):
        lhs = hb
        if after is not None:
            bits = pltpu.bitcast(hb, U32)
            lhs = pltpu.bitcast(bits | jnp.broadcast_to(after[0:1, 0:1], bits.shape), BF16)
        return _dot(lhs, win_ref[:, g * d:(g + 1) * d])

    def conv_work(rows):
        items = []
        for cb in range(d // LANES):
            items.append(functools.partial(_depthwise_lane_block, wcx_ref, ca_ref, cva_ref,
                                           phase_ref.at[2], K_SHORT, HALO_A - K_SHORT // 2, rows, cb))
            items.append(functools.partial(_depthwise_lane_block, wu_ref, cb_ref, cvb_ref,
                                           phase_ref.at[cb % 2], K_CONF, HALO_B - K_CONF // 2, rows, cb))
        return items

    early = conv_work((0, t_rows - CONV_ROWS))
    n_proj = 7
    z = []
    conv_done = None
    for g in range(n_proj):
        z.append(proj(g, conv_done))
        released = _zero_bits_of(z[g - 1][0:1, 0:LANES]) if g >= 1 else None
        for item in early[g * len(early) // n_proj:(g + 1) * len(early) // n_proj]:
            conv_done = item(after=released)
    b_a = z[0]
    cx = z[1] * z[2]
    u = z[3] * _sigmoid(z[4])
    sg_a = _sigmoid(z[5])
    sg_b = _sigmoid(z[6])

    same_seq = ((i % n_tiles) != 0).astype(F32)
    wu_ref[pl.ds(HALO_B + t_rows, HALO_B), :] = u[0:HALO_B] * same_seq
    wcx_ref[pl.ds(HALO_A + t_rows, HALO_A), :] = cx[0:HALO_A] * same_seq

    for item in conv_work((t_rows - CONV_ROWS, t_rows)):
        item()

    y_a = _dot((ba_ref[...] * cva_ref[...]).astype(BF16), wa_ref[...])
    v = cvb_ref[...] + bcb_ref[...]
    mu = jnp.mean(v, axis=-1, keepdims=True)
    vc = v - mu
    var = jnp.mean(vc * vc, axis=-1, keepdims=True)
    v = (vc * lax.rsqrt(var + LN_EPS)) * lng_ref[...] + lnb_ref[...]
    y_b = _dot((v * _sigmoid(v)).astype(BF16), wb_ref[...])
    merged = sga_ref[...] * y_a + sgb_ref[...] * y_b
    mix = _dot(merged.astype(BF16), wo_ref[...])

    gt1 = modp_ref[0, 2:3, :]
    sh2 = modp_ref[0, 3:4, :]
    sc2 = modp_ref[0, 4:5, :]
    x1 = xs_ref[...] + gt1 * mix
    x1_ref[...] = x1
    h2 = (x1 * lax.rsqrt(jnp.mean(x1 * x1, axis=-1, keepdims=True) + RMS_EPS)) * g2_ref[...]
    h2 = h2 * (1.0 + sc2) + sh2
    h2_hi, h2_lo = _split_bf16(h2)
    h2p_ref[...] = _pack_bf16_pairs(h2_hi)
    logits = _dot(h2_hi, wrh_ref[...]) + _dot(h2_lo, wrh_ref[...]) + _dot(h2_hi, wrl_ref[...]) + br_ref[...]
    route_ref[...] = _route(logits, run_ref, (i > 0).astype(F32))
    cnt_ref[...] = jnp.broadcast_to(run_ref[0:1, :], cnt_ref.shape)

    tail_u = wu_ref[pl.ds(t_rows, HALO_B), :]
    wu_ref[pl.ds(0, HALO_B), :] = tail_u * same_seq
    wu_ref[pl.ds(HALO_B, t_rows), :] = u
    tail_cx = wcx_ref[pl.ds(t_rows, HALO_A), :]
    wcx_ref[pl.ds(0, HALO_A), :] = tail_cx * same_seq
    wcx_ref[pl.ds(HALO_A, t_rows), :] = cx
    ba_ref[...] = b_a
    sga_ref[...] = sg_a
    sgb_ref[...] = sg_b
    xs_ref[...] = x


def _mixer_call(x2d, mod3, g1, conv_a, conv_b, b_conv_b, ln_g, ln_b, g2,
                w_in, w_a, w_b, w_o, wr_hi, wr_lo, b_r, *, seq_len):
    n_tok, d = x2d.shape
    t = T_MIX
    n_tiles = seq_len // t
    n_chunks = n_tok // t
    last = n_chunks - 1

    def cur(i):
        return jnp.minimum(i, last)

    def prev(i):
        return jnp.maximum(i - 1, 0)

    row_spec_prev = lambda width: pl.BlockSpec((t, width), lambda i: (prev(i), 0))
    vec = lambda a: _const_spec(a.shape)
    hbm = pl.BlockSpec(memory_space=pl.ANY)
    in_specs = [
        pl.BlockSpec((t, d), lambda i: (cur(i), 0)),
        pl.BlockSpec((1,) + mod3.shape[1:], lambda i: (cur(i) // n_tiles, 0, 0)),
        pl.BlockSpec((1,) + mod3.shape[1:], lambda i: (prev(i) // n_tiles, 0, 0)),
        vec(g1), vec(conv_a), vec(conv_b), vec(b_conv_b), vec(ln_g), vec(ln_b), vec(g2),
        vec(wr_hi), vec(wr_lo), vec(b_r),
        hbm, hbm, hbm, hbm,
    ]
    out_specs = [row_spec_prev(d), row_spec_prev(d // 2), row_spec_prev(ROUTE_LANES),
                 pl.BlockSpec((SUBLANES, ROUTE_LANES), lambda i: (0, 0))]
    out_shape = [
        jax.ShapeDtypeStruct((n_tok, d), F32),
        jax.ShapeDtypeStruct((n_tok, d // 2), U32),
        jax.ShapeDtypeStruct((n_tok, ROUTE_LANES), F32),
        jax.ShapeDtypeStruct((SUBLANES, ROUTE_LANES), F32),
    ]
    scratch = [
        pltpu.VMEM((t + 2 * HALO_B, d), F32),
        pltpu.VMEM((t + 2 * HALO_A, d), F32),
        pltpu.VMEM((t, d), F32), pltpu.VMEM((t, d), F32), pltpu.VMEM((t, d), F32), pltpu.VMEM((t, d), F32),
        pltpu.VMEM((t, d), F32), pltpu.VMEM((t, d), F32),
        pltpu.VMEM((SUBLANES, ROUTE_LANES), F32),
        pltpu.VMEM((3, SUBLANES, t + 2 * HALO_B, LANES), F32),
        pltpu.VMEM(w_in.shape, BF16), pltpu.VMEM(w_a.shape, BF16), pltpu.VMEM(w_b.shape, BF16),
        pltpu.VMEM(w_o.shape, BF16),
        pltpu.VMEM((2, d, W_STAGE_COLS), F32), pltpu.SemaphoreType.DMA((2,)),
    ]
    return pl.pallas_call(
        functools.partial(_mixer_kernel, n_tiles=n_tiles),
        grid=(n_chunks + 1,),
        in_specs=in_specs,
        out_specs=out_specs,
        out_shape=out_shape,
        scratch_shapes=scratch,
        compiler_params=pltpu.CompilerParams(
            dimension_semantics=("arbitrary",), vmem_limit_bytes=VMEM_LIMIT_BYTES),
        name="mixer_router",
    )(x2d, mod3, mod3, g1, conv_a, conv_b, b_conv_b, ln_g, ln_b, g2,
      wr_hi, wr_lo, b_r, w_in, w_a, w_b, w_o)


def _segment_layout(counts_row, n_blocks):
    counts = counts_row[:N_EXPERTS].astype(jnp.int32)
    nblk = (counts + ROW_BLK - 1) // ROW_BLK
    blk_end = jnp.cumsum(nblk)
    blk0 = blk_end - nblk
    n_used = blk_end[-1:].astype(jnp.int32)
    pstart_row = jnp.zeros((1, ROUTE_LANES), F32).at[0, :N_EXPERTS].set((blk0 * ROW_BLK).astype(F32))
    blk = jnp.arange(n_blocks, dtype=jnp.int32)[:, None]
    blk_e = jnp.minimum(jnp.sum(blk_end[None, :] <= blk, axis=1), N_EXPERTS - 1).astype(jnp.int32)
    seg_end = jnp.min(jnp.where(blk_end[None, :] > blk, blk_end[None, :], n_blocks), axis=1)
    next_e = jnp.sum(blk_end[None, :] <= seg_end[:, None], axis=1).astype(jnp.int32)
    next_e = jnp.where(seg_end >= n_used[0], -1, next_e)
    return pstart_row, blk_e, next_e, n_used


def _plan_kernel(route_ref, pstart_ref, pos_ref):
    rec = route_ref[...]
    lane = _lane_ids(rec.shape)
    pos = jnp.zeros(rec.shape, F32)
    for k in range(TOP_K):
        e = rec[:, REC_E + k:REC_E + k + 1]
        seg = jnp.sum(jnp.where(lane == e, pstart_ref[...], 0.0), axis=-1, keepdims=True)
        pos = jnp.where(lane == k, seg + rec[:, REC_RANK + k:REC_RANK + k + 1], pos)
    pos_ref[...] = pos.astype(jnp.int32)


def _plan_call(route, pstart_row):
    n_tok = route.shape[0]
    t = T_PLAN
    return pl.pallas_call(
        _plan_kernel,
        grid=(n_tok // t,),
        in_specs=[pl.BlockSpec((t, ROUTE_LANES), lambda i: (i, 0)),
                  pl.BlockSpec((1, ROUTE_LANES), lambda i: (0, 0))],
        out_specs=pl.BlockSpec((t, ROUTE_LANES), lambda i: (i, 0)),
        out_shape=jax.ShapeDtypeStruct((n_tok, ROUTE_LANES), jnp.int32),
        compiler_params=pltpu.CompilerParams(dimension_semantics=("arbitrary",)),
        name="moe_plan",
    )(route, pstart_row)


def _scatter_kernel(pos_ref, rows_ref, buf_in_hbm, buf_hbm, sem):
    del buf_in_hbm
    t_rows = rows_ref.shape[0]

    for r in range(t_rows):
        for k in range(TOP_K):
            pltpu.make_async_copy(rows_ref.at[pl.ds(r, 1), :],
                                  buf_hbm.at[pl.ds(pos_ref[0, k, r], 1), :], sem).start(priority=k)
    for _ in range(TOP_K):
        pltpu.make_async_copy(rows_ref, buf_hbm.at[pl.ds(0, t_rows), :], sem).wait()


def _scatter_call(pos3, rows, n_rows):
    n_tok, width = rows.shape
    t = T_MOVE
    buf0 = jnp.zeros((n_rows, width), rows.dtype)
    return pl.pallas_call(
        _scatter_kernel,
        grid=(n_tok // t,),
        in_specs=[
            pl.BlockSpec((1, TOP_K, t), lambda i: (i, 0, 0), memory_space=pltpu.SMEM),
            pl.BlockSpec((t, width), lambda i: (i, 0)),
            pl.BlockSpec(memory_space=pl.ANY),
        ],
        out_specs=pl.BlockSpec(memory_space=pl.ANY),
        out_shape=jax.ShapeDtypeStruct((n_rows, width), rows.dtype),
        scratch_shapes=[pltpu.SemaphoreType.DMA(())],
        input_output_aliases={2: 0},
        compiler_params=pltpu.CompilerParams(dimension_semantics=("arbitrary",)),
        name="moe_scatter",
    )(pos3, rows, buf0)


def _expert_kernel(blk_e_ref, next_e_ref, n_used_ref, xs_hbm, wg_hbm, wu_hbm, wd_hbm, y_hbm,
                   wg_st, wu_st, wd_st, wg_bf, wu_bf, wd_bf, xbuf, ybuf, w_sem, in_sem, out_sem):
    n_used = n_used_ref[0]

    def rows(b):
        return pl.ds(pl.multiple_of(b * ROW_BLK, ROW_BLK), ROW_BLK)

    def load(b, slot):
        return pltpu.make_async_copy(xs_hbm.at[rows(b), :], xbuf.at[slot], in_sem.at[slot])

    def store(b, slot):
        return pltpu.make_async_copy(ybuf.at[slot], y_hbm.at[rows(b), :], out_sem.at[slot])

    def weight_copies(e, wslot):
        return [pltpu.make_async_copy(src.at[e], dst.at[wslot], w_sem.at[wslot, n])
                for n, (src, dst) in enumerate(((wg_hbm, wg_st), (wu_hbm, wu_st), (wd_hbm, wd_st)))]

    for copy in weight_copies(blk_e_ref[0], 0):
        copy.start(priority=1)
    load(0, 0).start()

    def body(b, wslot):
        e = blk_e_ref[b]
        new_expert = jnp.logical_or(b == 0, e != blk_e_ref[jnp.maximum(b - 1, 0)])
        slot = b % 2

        @pl.when(new_expert)
        def _switch_expert():
            for copy in weight_copies(e, wslot):
                copy.wait()
            wg_bf[...] = wg_st[wslot].astype(BF16)
            wu_bf[...] = wu_st[wslot].astype(BF16)
            wd_bf[...] = wd_st[wslot].astype(BF16)
            nxt = next_e_ref[b]

            @pl.when(nxt >= 0)
            def _prefetch_weights():
                for copy in weight_copies(nxt, 1 - wslot):
                    copy.start(priority=1)

        @pl.when(b + 1 < n_used)
        def _prefetch_rows():
            load(b + 1, 1 - slot).start()

        load(b, slot).wait()

        @pl.when(b >= 2)
        def _window_free():
            store(b - 2, slot).wait()

        xb = _unpack_bf16_pairs(xbuf[slot])
        a = _dot(xb, wg_bf[...])
        u = _dot(xb, wu_bf[...])
        hid = ((a * _sigmoid(a)) * u).astype(BF16)
        ybuf[slot] = _pack_bf16_pairs(_dot(hid, wd_bf[...]).astype(BF16))
        store(b, slot).start()
        return jnp.where(new_expert, 1 - wslot, wslot)

    lax.fori_loop(0, n_used, body, jnp.int32(0))

    @pl.when(n_used >= 2)
    def _drain_older():
        store(n_used - 2, n_used % 2).wait()

    store(n_used - 1, (n_used - 1) % 2).wait()

    ybuf[0] = jnp.zeros(ybuf.shape[1:], ybuf.dtype)

    def fill(b, carry):
        copy = pltpu.make_async_copy(ybuf.at[0], y_hbm.at[rows(b), :], out_sem.at[0])
        copy.start()
        copy.wait()
        return carry

    lax.fori_loop(n_used, y_hbm.shape[0] // ROW_BLK, fill, 0)


def _expert_call(xs, blk_e, next_e, n_used, w_gate, w_up, w_down):
    n_rows, half = xs.shape
    d = 2 * half
    d_e = w_gate.shape[-1]
    hbm = pl.BlockSpec(memory_space=pl.ANY)
    grid_spec = pltpu.PrefetchScalarGridSpec(
        num_scalar_prefetch=3,
        grid=(1,),
        in_specs=[hbm, hbm, hbm, hbm],
        out_specs=hbm,
        scratch_shapes=[
            pltpu.VMEM((2, d, d_e), F32), pltpu.VMEM((2, d, d_e), F32), pltpu.VMEM((2, d_e, d), F32),
            pltpu.VMEM((d, d_e), BF16), pltpu.VMEM((d, d_e), BF16), pltpu.VMEM((d_e, d), BF16),
            pltpu.VMEM((2, ROW_BLK, half), xs.dtype), pltpu.VMEM((2, ROW_BLK, half), xs.dtype),
            pltpu.SemaphoreType.DMA((2, 3)), pltpu.SemaphoreType.DMA((2,)), pltpu.SemaphoreType.DMA((2,)),
        ],
    )
    return pl.pallas_call(
        _expert_kernel,
        grid_spec=grid_spec,
        out_shape=jax.ShapeDtypeStruct((n_rows, half), xs.dtype),
        compiler_params=pltpu.CompilerParams(
            dimension_semantics=("arbitrary",), vmem_limit_bytes=VMEM_LIMIT_BYTES),
        name="moe_experts",
    )(blk_e, next_e, n_used, xs, w_gate, w_up, w_down)


def _row_gather_start(src_hbm, idx_of_row, dst, sem, n_rows):
    for r in range(n_rows):
        pltpu.make_async_copy(src_hbm.at[pl.ds(idx_of_row(r), 1), :], dst.at[pl.ds(r, 1), :],
                              sem).start(priority=r % 2)


def _row_gather_wait(src_hbm, dst, sem, n_rows):
    pltpu.make_async_copy(src_hbm.at[pl.ds(0, n_rows), :], dst, sem).wait()


def _combine_kernel(posc_ref, posn_ref, x1_ref, route_ref, mod_ref, gf_ref, y_hbm, o_ref, ybuf, sem,
                    *, n_steps, final_norm):
    i = pl.program_id(0)
    slot = i % 2
    t_rows = x1_ref.shape[0]

    def start(pos_ref, s):
        for k in range(TOP_K):
            _row_gather_start(y_hbm, lambda r, k=k: pos_ref[0, k, r], ybuf.at[s, k], sem.at[s], t_rows)

    @pl.when(i == 0)
    def _prime():
        start(posc_ref, 0)

    @pl.when(i + 1 < n_steps)
    def _prefetch():
        start(posn_ref, 1 - slot)

    for k in range(TOP_K):
        _row_gather_wait(y_hbm, ybuf.at[slot, k], sem.at[slot], t_rows)
    rec = route_ref[...]
    y0 = _unpack_bf16_pairs(ybuf[slot, 0], F32)
    y1 = _unpack_bf16_pairs(ybuf[slot, 1], F32)
    moe = y0 * rec[:, REC_W:REC_W + 1] + y1 * rec[:, REC_W + 1:REC_W + 2]
    x2 = x1_ref[...] + mod_ref[0, 5:6, :] * moe
    if final_norm:
        x2 = (x2 * lax.rsqrt(jnp.mean(x2 * x2, axis=-1, keepdims=True) + RMS_EPS)) * gf_ref[...]
    o_ref[...] = x2


def _combine_call(x1, route, mod3, g_final, y_rows, pos3, *, seq_len, final_norm):
    n_tok, d = x1.shape
    t = T_MOVE
    n_steps = n_tok // t
    tiles_per_seq = seq_len // t
    last = n_steps - 1
    return pl.pallas_call(
        functools.partial(_combine_kernel, n_steps=n_steps, final_norm=final_norm),
        grid=(n_steps,),
        in_specs=[
            pl.BlockSpec((1, TOP_K, t), lambda i: (i, 0, 0), memory_space=pltpu.SMEM),
            pl.BlockSpec((1, TOP_K, t), lambda i: (jnp.minimum(i + 1, last), 0, 0), memory_space=pltpu.SMEM),
            pl.BlockSpec((t, d), lambda i: (i, 0)),
            pl.BlockSpec((t, ROUTE_LANES), lambda i: (i, 0)),
            pl.BlockSpec((1,) + mod3.shape[1:], lambda i: (i // tiles_per_seq, 0, 0)),
            pl.BlockSpec((1, d), lambda i: (0, 0)),
            pl.BlockSpec(memory_space=pl.ANY),
        ],
        out_specs=pl.BlockSpec((t, d), lambda i: (i, 0)),
        out_shape=jax.ShapeDtypeStruct((n_tok, d), F32),
        scratch_shapes=[pltpu.VMEM((2, TOP_K, t, y_rows.shape[1]), y_rows.dtype),
                        pltpu.SemaphoreType.DMA((2,))],
        compiler_params=pltpu.CompilerParams(
            dimension_semantics=("arbitrary",), vmem_limit_bytes=VMEM_LIMIT_BYTES),
        name="moe_combine",
    )(pos3, pos3, x1, route, mod3, g_final, y_rows)


def kernel(x, c, w_ada, b_ada, g_norm1, w_in, conv_a, w_a_out, conv_b, b_conv_b, ln_conv_g, ln_conv_b,
           w_b_out, w_o, g_norm2, w_router_g, b_router_g, w_router_e, b_router_e, w_gate, w_up, w_down,
           g_final):
    bsz, seq_len, d = x.shape
    depth = w_ada.shape[0]
    n_tok = bsz * seq_len
    n_assign = n_tok * TOP_K
    assert seq_len % T_MIX == 0 and seq_len % T_MOVE == 0 and d % (2 * LANES) == 0
    assert n_tok % T_PLAN == 0
    assert n_assign % ROW_BLK == 0
    assert N_GROUPS + N_EXPERTS <= ROUTE_LANES
    n_rows = n_assign + N_EXPERTS * ROW_BLK

    c_pad = jnp.zeros((SUBLANES, d), F32).at[:bsz].set(c)
    xt = x.reshape(n_tok, d)
    row = lambda a: a.reshape(1, -1)
    for l in range(depth):
        mod = _mod_call(c_pad, w_ada[l], row(b_ada[l]))
        mod3 = mod[:bsz].reshape(bsz, 6, d)
        w_r = jnp.zeros((d, ROUTE_LANES), F32)
        w_r = w_r.at[:, :N_GROUPS].set(w_router_g[l]).at[:, N_GROUPS:N_GROUPS + N_EXPERTS].set(w_router_e[l])
        b_r = jnp.zeros((1, ROUTE_LANES), F32)
        b_r = b_r.at[0, :N_GROUPS].set(b_router_g[l]).at[0, N_GROUPS:N_GROUPS + N_EXPERTS].set(b_router_e[l])
        wr_hi, wr_lo = _split_bf16(w_r)
        x1, h2p, route, cnt = _mixer_call(
            xt, mod3, row(g_norm1[l]), conv_a[l], conv_b[l], row(b_conv_b[l]), row(ln_conv_g[l]),
            row(ln_conv_b[l]), row(g_norm2[l]),
            w_in[l], w_a_out[l], w_b_out[l], w_o[l], wr_hi, wr_lo, b_r, seq_len=seq_len)
        pstart_row, blk_e, next_e, n_used = _segment_layout(cnt[0], n_rows // ROW_BLK)
        pos = _plan_call(route, pstart_row)
        pos3 = pos[:, :TOP_K].reshape(n_tok // T_MOVE, T_MOVE, TOP_K).transpose(0, 2, 1)
        xs = _scatter_call(pos3, h2p, n_rows)
        y_rows = _expert_call(xs, blk_e, next_e, n_used, w_gate[l], w_up[l], w_down[l])
        xt = _combine_call(x1, route, mod3, row(g_final), y_rows, pos3, seq_len=seq_len,
                           final_norm=(l == depth - 1))
    return xt.reshape(bsz, seq_len, d)
```

```python
import functools

import jax
import jax.numpy as jnp
from jax import lax
from jax.experimental import pallas as pl
from jax.experimental.pallas import tpu as pltpu

F32 = jnp.float32
BF16 = jnp.bfloat16
U32 = jnp.uint32

N_GROUPS = 4
EXPERTS_PER_GROUP = 8
N_EXPERTS = N_GROUPS * EXPERTS_PER_GROUP
TOP_K = 2
K_SHORT = 3
K_CONF = 31
RMS_EPS = 1e-6
LN_EPS = 1e-5

LANES = 128
SUBLANES = 8
T_MIX = 256
HALO_B = 16
HALO_A = 8
CONV_ROWS = 64
PAIR_LAG = 1
ROUTE_AFTER_GROUP = 1
ROW_BLK = 512
T_MOVE = 512
T_PLAN = 2048
W_STAGE_COLS = 512
ROUTE_LANES = LANES
REC_E, REC_W, REC_RANK = 0, 2, 4
VMEM_LIMIT_BYTES = 56 * 1024 * 1024


def _sigmoid(v):
    return 1.0 / (1.0 + jnp.exp(-v))


def _split_bf16(v):
    hi = v.astype(BF16)
    lo = (v - hi.astype(F32)).astype(BF16)
    return hi, lo


def _dot(a, b):
    return jnp.dot(a, b, preferred_element_type=F32)


def _const_spec(shape):
    nd = len(shape)
    return pl.BlockSpec(shape, lambda *_: (0,) * nd, pipeline_mode=pl.Buffered(1))


def _lane_ids(shape):
    return lax.broadcasted_iota(jnp.int32, shape, 1).astype(F32)


def _mod_kernel(c_ref, w_ref, b_ref, o_ref):
    c = c_ref[...]
    a_hi, a_lo = _split_bf16(c * _sigmoid(c))
    w_hi, w_lo = _split_bf16(w_ref[...])
    o_ref[...] = _dot(a_hi, w_hi) + _dot(a_lo, w_hi) + _dot(a_hi, w_lo) + b_ref[...]


def _mod_call(c_pad, w_ada, b_ada):
    rows, d = c_pad.shape
    n_out = w_ada.shape[1]
    blk = 1024
    return pl.pallas_call(
        _mod_kernel,
        grid=(n_out // blk,),
        in_specs=[
            pl.BlockSpec((rows, d), lambda j: (0, 0)),
            pl.BlockSpec((d, blk), lambda j: (0, j)),
            pl.BlockSpec((1, blk), lambda j: (0, j)),
        ],
        out_specs=pl.BlockSpec((rows, blk), lambda j: (0, j)),
        out_shape=jax.ShapeDtypeStruct((rows, n_out), F32),
        compiler_params=pltpu.CompilerParams(dimension_semantics=("arbitrary",)),
        name="adaln_mod",
    )(c_pad, w_ada, b_ada)


def _zero_bits_of(v):
    return (lax.bitcast_convert_type(v, U32) >> 16) >> 16


def _depthwise_lane_block(buf_ref, w_ref, out_ref, phase_ref, n_taps, row0, rows, cb, after=None):
    first, stop = rows
    max_off = row0 + n_taps - 1
    n_keep = stop - first + (max_off // SUBLANES) * SUBLANES
    cols = slice(cb * LANES, (cb + 1) * LANES)
    for s in sorted({(row0 + k) % SUBLANES for k in range(n_taps)}):
        phase_ref[s, pl.ds(0, n_keep), :] = buf_ref[pl.ds(first + s, n_keep), cols]
    row_blocks = range(first, stop, CONV_ROWS)
    acc = [None] * len(row_blocks)
    for k in range(n_taps):
        q, s = divmod(row0 + k, SUBLANES)
        w_row = w_ref[k:k + 1, cols]
        if after is not None:
            w_row = lax.bitcast_convert_type(lax.bitcast_convert_type(w_row, U32) | after, F32)
        w_k = jnp.broadcast_to(w_row, (CONV_ROWS, LANES))
        for j, r0 in enumerate(row_blocks):
            term = phase_ref[s, pl.ds(r0 - first + q * SUBLANES, CONV_ROWS), :] * w_k
            acc[j] = term if acc[j] is None else acc[j] + term
    for j, r0 in enumerate(row_blocks):
        out_ref[pl.ds(r0, CONV_ROWS), cols] = acc[j]
    return _zero_bits_of(acc[-1][0:1, :])


def _route(logits, run_ref, live):
    shape = logits.shape
    lane = _lane_ids(shape)
    big = float(4 * ROUTE_LANES)
    neg_inf = -jnp.inf

    def first_argmax(v):
        m = jnp.max(v, axis=-1, keepdims=True)
        return m, jnp.min(jnp.where(v == m, lane, big), axis=-1, keepdims=True)

    is_group = lane < N_GROUPS
    g_max, g_idx = first_argmax(jnp.where(is_group, logits, neg_inf))
    g_prob = 1.0 / jnp.sum(jnp.where(is_group, jnp.exp(logits - g_max), 0.0), axis=-1, keepdims=True)
    lo = N_GROUPS + EXPERTS_PER_GROUP * g_idx
    in_group = (lane >= lo) & (lane < lo + EXPERTS_PER_GROUP)
    e_logits = jnp.where(in_group, logits, neg_inf)
    m1, i1 = first_argmax(e_logits)
    m2, i2 = first_argmax(jnp.where(lane == i1, neg_inf, e_logits))
    t = jnp.exp(m2 - m1)
    p1 = 1.0 / (1.0 + t)
    p2 = t / (1.0 + t)
    e1 = i1 - N_GROUPS
    e2 = i2 - N_GROUPS

    hot1 = lane == e1
    hot2 = lane == e2
    both = jnp.where(hot1 | hot2, 1.0, 0.0)
    n_rows = shape[0]
    earlier = (lax.broadcasted_iota(jnp.int32, (n_rows, n_rows), 0)
               > lax.broadcasted_iota(jnp.int32, (n_rows, n_rows), 1))
    before = _dot(jnp.where(earlier, 1.0, 0.0).astype(BF16), both.astype(BF16)) + run_ref[0:1, :]
    rank1 = jnp.sum(jnp.where(hot1, before, 0.0), axis=-1, keepdims=True)
    rank2 = jnp.sum(jnp.where(hot2, before, 0.0), axis=-1, keepdims=True)
    run_ref[0:1, :] = run_ref[0:1, :] + live * jnp.sum(both, axis=0, keepdims=True)

    rec = jnp.where(lane == REC_E, e1, 0.0)
    rec = jnp.where(lane == REC_E + 1, e2, rec)
    rec = jnp.where(lane == REC_W, g_prob * p1, rec)
    rec = jnp.where(lane == REC_W + 1, g_prob * p2, rec)
    rec = jnp.where(lane == REC_RANK, rank1, rec)
    rec = jnp.where(lane == REC_RANK + 1, rank2, rec)
    return rec


def _pack_bf16_pairs(v_bf16):
    half = v_bf16.shape[1] // 2
    lo = lax.bitcast_convert_type(v_bf16[:, :half].astype(F32), U32)
    hi = lax.bitcast_convert_type(v_bf16[:, half:].astype(F32), U32)
    return (lo >> 16) | hi


def _unpack_bf16_pairs(p_u32, dtype=BF16):
    lo = lax.bitcast_convert_type(p_u32 << 16, F32).astype(dtype)
    hi = lax.bitcast_convert_type(p_u32 & jnp.uint32(0xFFFF0000), F32).astype(dtype)
    return jnp.concatenate([lo, hi], axis=1)


def _stage_weight_bf16(src_hbm, dst_ref, stage_ref, sem):
    chunk = stage_ref.shape[2]
    n_chunks = src_hbm.shape[1] // chunk

    def cols(j):
        return pl.ds(pl.multiple_of(j * chunk, chunk), chunk)

    def copy(j, slot):
        return pltpu.make_async_copy(src_hbm.at[:, cols(j)], stage_ref.at[slot], sem.at[slot])

    copy(0, 0).start()

    def body(j, carry):
        slot = j % 2

        @pl.when(j + 1 < n_chunks)
        def _next():
            copy(j + 1, 1 - slot).start()

        copy(j, slot).wait()
        dst_ref[:, cols(j)] = stage_ref[slot].astype(BF16)
        return carry

    lax.fori_loop(0, n_chunks, body, 0)


def _mixer_kernel(x_ref, modn_ref, modp_ref, g1_ref, ca_ref, cb_ref, bcb_ref, lng_ref, lnb_ref, g2_ref,
                  wrh_ref, wrl_ref, br_ref, win_hbm, wa_hbm, wb_hbm, wo_hbm,
                  x1_ref, h2p_ref, route_ref, cnt_ref,
                  wu_ref, wcx_ref, ba_ref, sga_ref, sgb_ref, xs_ref, cva_ref, cvb_ref, run_ref, logit_ref,
                  phase_ref, win_ref, wa_ref, wb_ref, wo_ref, stage_ref, stage_sem,
                  *, n_tiles):
    i = pl.program_id(0)
    t_rows, d = x_ref.shape

    @pl.when(i == 0)
    def _init():
        for ref in (wu_ref, wcx_ref, ba_ref, sga_ref, sgb_ref, xs_ref, run_ref, logit_ref):
            ref[...] = jnp.zeros(ref.shape, ref.dtype)
        for src, dst in ((win_hbm, win_ref), (wa_hbm, wa_ref), (wb_hbm, wb_ref), (wo_hbm, wo_ref)):
            _stage_weight_bf16(src, dst, stage_ref, stage_sem)

    x = x_ref[...]
    sh1 = modn_ref[0, 0:1, :]
    sc1 = modn_ref[0, 1:2, :]
    h = (x * lax.rsqrt(jnp.mean(x * x, axis=-1, keepdims=True) + RMS_EPS)) * g1_ref[...]
    hb = (h * (1.0 + sc1) + sh1).astype(BF16)

    def proj(g, after):
        lhs = hb
        if after is not None:
            bits = pltpu.bitcast(hb, U32)
            lhs = pltpu.bitcast(bits | jnp.broadcast_to(after[0:1, 0:1], bits.shape), BF16)
        return _dot(lhs, win_ref[:, g * d:(g + 1) * d])

    def conv_work(rows):
        items = []
        for cb in range(d // LANES):
            items.append(functools.partial(_depthwise_lane_block, wcx_ref, ca_ref, cva_ref,
                                           phase_ref.at[2], K_SHORT, HALO_A - K_SHORT // 2, rows, cb))
            items.append(functools.partial(_depthwise_lane_block, wu_ref, cb_ref, cvb_ref,
                                           phase_ref.at[cb % 2], K_CONF, HALO_B - K_CONF // 2, rows, cb))
        return items

    early = conv_work((0, t_rows - CONV_ROWS))
    n_proj = 7
    z = []
    conv_done = None
    for g in range(n_proj):
        z.append(proj(g, conv_done))
        released = _zero_bits_of(z[g - PAIR_LAG][0:1, 0:LANES]) if g >= PAIR_LAG else None
        for item in early[g * len(early) // n_proj:(g + 1) * len(early) // n_proj]:
            conv_done = item(after=released)
        if g == ROUTE_AFTER_GROUP:
            route_ref[...] = _route(logit_ref[...], run_ref, (i > 1).astype(F32))
            cnt_ref[...] = jnp.broadcast_to(run_ref[0:1, :], cnt_ref.shape)
    b_a = z[0]
    cx = z[1] * z[2]
    u = z[3] * _sigmoid(z[4])
    sg_a = _sigmoid(z[5])
    sg_b = _sigmoid(z[6])

    same_seq = ((i % n_tiles) != 0).astype(F32)
    wu_ref[pl.ds(HALO_B + t_rows, HALO_B), :] = u[0:HALO_B] * same_seq
    wcx_ref[pl.ds(HALO_A + t_rows, HALO_A), :] = cx[0:HALO_A] * same_seq

    for item in conv_work((t_rows - CONV_ROWS, t_rows)):
        item()
    y_a = _dot((ba_ref[...] * cva_ref[...]).astype(BF16), wa_ref[...])
    v = cvb_ref[...] + bcb_ref[...]
    mu = jnp.mean(v, axis=-1, keepdims=True)
    vc = v - mu
    var = jnp.mean(vc * vc, axis=-1, keepdims=True)
    v = (vc * lax.rsqrt(var + LN_EPS)) * lng_ref[...] + lnb_ref[...]
    y_b = _dot((v * _sigmoid(v)).astype(BF16), wb_ref[...])
    merged = sga_ref[...] * y_a + sgb_ref[...] * y_b
    mix = _dot(merged.astype(BF16), wo_ref[...])

    gt1 = modp_ref[0, 2:3, :]
    sh2 = modp_ref[0, 3:4, :]
    sc2 = modp_ref[0, 4:5, :]
    x1 = xs_ref[...] + gt1 * mix
    h2 = (x1 * lax.rsqrt(jnp.mean(x1 * x1, axis=-1, keepdims=True) + RMS_EPS)) * g2_ref[...]
    h2 = h2 * (1.0 + sc2) + sh2
    h2_hi, h2_lo = _split_bf16(h2)
    h2p = _pack_bf16_pairs(h2_hi)
    hi_part = _dot(h2_hi, wrl_ref[...])
    logit_ref[...] = (hi_part[:, :ROUTE_LANES] + _dot(h2_lo, wrh_ref[...]) + hi_part[:, ROUTE_LANES:]
                      + br_ref[...])

    @pl.when(i < pl.num_programs(0) - 1)
    def _emit_rows():
        x1_ref[...] = x1
        h2p_ref[...] = h2p

    tail_u = wu_ref[pl.ds(t_rows, HALO_B), :]
    wu_ref[pl.ds(0, HALO_B), :] = tail_u * same_seq
    wu_ref[pl.ds(HALO_B, t_rows), :] = u
    tail_cx = wcx_ref[pl.ds(t_rows, HALO_A), :]
    wcx_ref[pl.ds(0, HALO_A), :] = tail_cx * same_seq
    wcx_ref[pl.ds(HALO_A, t_rows), :] = cx
    ba_ref[...] = b_a
    sga_ref[...] = sg_a
    sgb_ref[...] = sg_b
    xs_ref[...] = x


def _mixer_call(x2d, mod3, g1, conv_a, conv_b, b_conv_b, ln_g, ln_b, g2,
                w_in, w_a, w_b, w_o, wr_hi, wr_lo, b_r, *, seq_len):
    n_tok, d = x2d.shape
    t = T_MIX
    n_tiles = seq_len // t
    n_chunks = n_tok // t
    last = n_chunks - 1

    def cur(i):
        return jnp.minimum(i, last)

    def prev(i):
        return jnp.clip(i - 1, 0, last)

    def prev2(i):
        return jnp.clip(i - 2, 0, last)

    row_spec_prev = lambda width: pl.BlockSpec((t, width), lambda i: (prev(i), 0))
    vec = lambda a: _const_spec(a.shape)
    hbm = pl.BlockSpec(memory_space=pl.ANY)
    in_specs = [
        pl.BlockSpec((t, d), lambda i: (cur(i), 0)),
        pl.BlockSpec((1,) + mod3.shape[1:], lambda i: (cur(i) // n_tiles, 0, 0)),
        pl.BlockSpec((1,) + mod3.shape[1:], lambda i: (prev(i) // n_tiles, 0, 0)),
        vec(g1), vec(conv_a), vec(conv_b), vec(b_conv_b), vec(ln_g), vec(ln_b), vec(g2),
        vec(wr_hi), vec(wr_lo), vec(b_r),
        hbm, hbm, hbm, hbm,
    ]
    out_specs = [row_spec_prev(d), row_spec_prev(d // 2),
                 pl.BlockSpec((t, ROUTE_LANES), lambda i: (prev2(i), 0)),
                 pl.BlockSpec((SUBLANES, ROUTE_LANES), lambda i: (0, 0))]
    out_shape = [
        jax.ShapeDtypeStruct((n_tok, d), F32),
        jax.ShapeDtypeStruct((n_tok, d // 2), U32),
        jax.ShapeDtypeStruct((n_tok, ROUTE_LANES), F32),
        jax.ShapeDtypeStruct((SUBLANES, ROUTE_LANES), F32),
    ]
    scratch = [
        pltpu.VMEM((t + 2 * HALO_B, d), F32),
        pltpu.VMEM((t + 2 * HALO_A, d), F32),
        pltpu.VMEM((t, d), F32), pltpu.VMEM((t, d), F32), pltpu.VMEM((t, d), F32), pltpu.VMEM((t, d), F32),
        pltpu.VMEM((t, d), F32), pltpu.VMEM((t, d), F32),
        pltpu.VMEM((SUBLANES, ROUTE_LANES), F32),
        pltpu.VMEM((t, ROUTE_LANES), F32),
        pltpu.VMEM((3, SUBLANES, t + 2 * HALO_B, LANES), F32),
        pltpu.VMEM(w_in.shape, BF16), pltpu.VMEM(w_a.shape, BF16), pltpu.VMEM(w_b.shape, BF16),
        pltpu.VMEM(w_o.shape, BF16),
        pltpu.VMEM((2, d, W_STAGE_COLS), F32), pltpu.SemaphoreType.DMA((2,)),
    ]
    return pl.pallas_call(
        functools.partial(_mixer_kernel, n_tiles=n_tiles),
        grid=(n_chunks + 2,),
        in_specs=in_specs,
        out_specs=out_specs,
        out_shape=out_shape,
        scratch_shapes=scratch,
        compiler_params=pltpu.CompilerParams(
            dimension_semantics=("arbitrary",), vmem_limit_bytes=VMEM_LIMIT_BYTES),
        name="mixer_router",
    )(x2d, mod3, mod3, g1, conv_a, conv_b, b_conv_b, ln_g, ln_b, g2,
      wr_hi, wr_lo, b_r, w_in, w_a, w_b, w_o)


def _segment_layout(counts_row, n_blocks):
    counts = counts_row[:N_EXPERTS].astype(jnp.int32)
    nblk = (counts + ROW_BLK - 1) // ROW_BLK
    blk_end = jnp.cumsum(nblk)
    blk0 = blk_end - nblk
    n_used = blk_end[-1:].astype(jnp.int32)
    pstart_row = jnp.zeros((1, ROUTE_LANES), F32).at[0, :N_EXPERTS].set((blk0 * ROW_BLK).astype(F32))
    blk = jnp.arange(n_blocks, dtype=jnp.int32)[:, None]
    blk_e = jnp.minimum(jnp.sum(blk_end[None, :] <= blk, axis=1), N_EXPERTS - 1).astype(jnp.int32)
    seg_end = jnp.min(jnp.where(blk_end[None, :] > blk, blk_end[None, :], n_blocks), axis=1)
    next_e = jnp.sum(blk_end[None, :] <= seg_end[:, None], axis=1).astype(jnp.int32)
    next_e = jnp.where(seg_end >= n_used[0], -1, next_e)
    return pstart_row, blk_e, next_e, n_used


def _plan_kernel(route_ref, pstart_ref, pos_ref):
    rec = route_ref[...]
    lane = _lane_ids(rec.shape)
    pos = jnp.zeros(rec.shape, F32)
    for k in range(TOP_K):
        e = rec[:, REC_E + k:REC_E + k + 1]
        seg = jnp.sum(jnp.where(lane == e, pstart_ref[...], 0.0), axis=-1, keepdims=True)
        pos = jnp.where(lane == k, seg + rec[:, REC_RANK + k:REC_RANK + k + 1], pos)
    pos_ref[...] = pos.astype(jnp.int32)


def _plan_call(route, pstart_row):
    n_tok = route.shape[0]
    t = T_PLAN
    return pl.pallas_call(
        _plan_kernel,
        grid=(n_tok // t,),
        in_specs=[pl.BlockSpec((t, ROUTE_LANES), lambda i: (i, 0)),
                  pl.BlockSpec((1, ROUTE_LANES), lambda i: (0, 0))],
        out_specs=pl.BlockSpec((t, ROUTE_LANES), lambda i: (i, 0)),
        out_shape=jax.ShapeDtypeStruct((n_tok, ROUTE_LANES), jnp.int32),
        compiler_params=pltpu.CompilerParams(dimension_semantics=("arbitrary",)),
        name="moe_plan",
    )(route, pstart_row)


def _scatter_kernel(pos_ref, rows_ref, buf_in_hbm, buf_hbm, sem):
    del buf_in_hbm
    t_rows = rows_ref.shape[0]

    for r in range(t_rows):
        for k in range(TOP_K):
            pltpu.make_async_copy(rows_ref.at[pl.ds(r, 1), :],
                                  buf_hbm.at[pl.ds(pos_ref[0, k, r], 1), :], sem).start(priority=k)
    for _ in range(TOP_K):
        pltpu.make_async_copy(rows_ref, buf_hbm.at[pl.ds(0, t_rows), :], sem).wait()


def _scatter_call(pos3, rows, n_rows):
    n_tok, width = rows.shape
    t = T_MOVE
    buf0 = jnp.zeros((n_rows, width), rows.dtype)
    return pl.pallas_call(
        _scatter_kernel,
        grid=(n_tok // t,),
        in_specs=[
            pl.BlockSpec((1, TOP_K, t), lambda i: (i, 0, 0), memory_space=pltpu.SMEM),
            pl.BlockSpec((t, width), lambda i: (i, 0)),
            pl.BlockSpec(memory_space=pl.ANY),
        ],
        out_specs=pl.BlockSpec(memory_space=pl.ANY),
        out_shape=jax.ShapeDtypeStruct((n_rows, width), rows.dtype),
        scratch_shapes=[pltpu.SemaphoreType.DMA(())],
        input_output_aliases={2: 0},
        compiler_params=pltpu.CompilerParams(dimension_semantics=("arbitrary",)),
        name="moe_scatter",
    )(pos3, rows, buf0)


def _expert_kernel(blk_e_ref, next_e_ref, n_used_ref, xs_hbm, wg_hbm, wu_hbm, wd_hbm, y_hbm,
                   wg_st, wu_st, wd_st, wg_bf, wu_bf, wd_bf, xbuf, ybuf, w_sem, in_sem, out_sem):
    n_used = n_used_ref[0]

    def rows(b):
        return pl.ds(pl.multiple_of(b * ROW_BLK, ROW_BLK), ROW_BLK)

    def load(b, slot):
        return pltpu.make_async_copy(xs_hbm.at[rows(b), :], xbuf.at[slot], in_sem.at[slot])

    def store(b, slot):
        return pltpu.make_async_copy(ybuf.at[slot], y_hbm.at[rows(b), :], out_sem.at[slot])

    def weight_copies(e, wslot):
        return [pltpu.make_async_copy(src.at[e], dst.at[wslot], w_sem.at[wslot, n])
                for n, (src, dst) in enumerate(((wg_hbm, wg_st), (wu_hbm, wu_st), (wd_hbm, wd_st)))]

    for copy in weight_copies(blk_e_ref[0], 0):
        copy.start(priority=1)
    load(0, 0).start()

    def body(b, wslot):
        e = blk_e_ref[b]
        new_expert = jnp.logical_or(b == 0, e != blk_e_ref[jnp.maximum(b - 1, 0)])
        slot = b % 2

        @pl.when(new_expert)
        def _switch_expert():
            for copy in weight_copies(e, wslot):
                copy.wait()
            wg_bf[...] = wg_st[wslot].astype(BF16)
            wu_bf[...] = wu_st[wslot].astype(BF16)
            wd_bf[...] = wd_st[wslot].astype(BF16)
            nxt = next_e_ref[b]

            @pl.when(nxt >= 0)
            def _prefetch_weights():
                for copy in weight_copies(nxt, 1 - wslot):
                    copy.start(priority=1)

        @pl.when(b + 1 < n_used)
        def _prefetch_rows():
            load(b + 1, 1 - slot).start()

        load(b, slot).wait()

        @pl.when(b >= 2)
        def _window_free():
            store(b - 2, slot).wait()

        xb = _unpack_bf16_pairs(xbuf[slot])
        a = _dot(xb, wg_bf[...])
        u = _dot(xb, wu_bf[...])
        hid = ((a * _sigmoid(a)) * u).astype(BF16)
        ybuf[slot] = _pack_bf16_pairs(_dot(hid, wd_bf[...]).astype(BF16))
        store(b, slot).start()
        return jnp.where(new_expert, 1 - wslot, wslot)

    lax.fori_loop(0, n_used, body, jnp.int32(0))

    @pl.when(n_used >= 2)
    def _drain_older():
        store(n_used - 2, n_used % 2).wait()

    store(n_used - 1, (n_used - 1) % 2).wait()

    ybuf[0] = jnp.zeros(ybuf.shape[1:], ybuf.dtype)

    def fill(b, carry):
        copy = pltpu.make_async_copy(ybuf.at[0], y_hbm.at[rows(b), :], out_sem.at[0])
        copy.start()
        copy.wait()
        return carry

    lax.fori_loop(n_used, y_hbm.shape[0] // ROW_BLK, fill, 0)


def _expert_call(xs, blk_e, next_e, n_used, w_gate, w_up, w_down):
    n_rows, half = xs.shape
    d = 2 * half
    d_e = w_gate.shape[-1]
    hbm = pl.BlockSpec(memory_space=pl.ANY)
    grid_spec = pltpu.PrefetchScalarGridSpec(
        num_scalar_prefetch=3,
        grid=(1,),
        in_specs=[hbm, hbm, hbm, hbm],
        out_specs=hbm,
        scratch_shapes=[
            pltpu.VMEM((2, d, d_e), F32), pltpu.VMEM((2, d, d_e), F32), pltpu.VMEM((2, d_e, d), F32),
            pltpu.VMEM((d, d_e), BF16), pltpu.VMEM((d, d_e), BF16), pltpu.VMEM((d_e, d), BF16),
            pltpu.VMEM((2, ROW_BLK, half), xs.dtype), pltpu.VMEM((2, ROW_BLK, half), xs.dtype),
            pltpu.SemaphoreType.DMA((2, 3)), pltpu.SemaphoreType.DMA((2,)), pltpu.SemaphoreType.DMA((2,)),
        ],
    )
    return pl.pallas_call(
        _expert_kernel,
        grid_spec=grid_spec,
        out_shape=jax.ShapeDtypeStruct((n_rows, half), xs.dtype),
        compiler_params=pltpu.CompilerParams(
            dimension_semantics=("arbitrary",), vmem_limit_bytes=VMEM_LIMIT_BYTES),
        name="moe_experts",
    )(blk_e, next_e, n_used, xs, w_gate, w_up, w_down)


def _row_gather_start(src_hbm, idx_of_row, dst, sem, n_rows):
    for r in range(n_rows):
        pltpu.make_async_copy(src_hbm.at[pl.ds(idx_of_row(r), 1), :], dst.at[pl.ds(r, 1), :],
                              sem).start(priority=r % 2)


def _row_gather_wait(src_hbm, dst, sem, n_rows):
    pltpu.make_async_copy(src_hbm.at[pl.ds(0, n_rows), :], dst, sem).wait()


def _combine_kernel(posc_ref, posn_ref, x1_ref, route_ref, mod_ref, gf_ref, y_hbm, o_ref, ybuf, sem,
                    *, n_steps, final_norm):
    i = pl.program_id(0)
    slot = i % 2
    t_rows = x1_ref.shape[0]

    def start(pos_ref, s):
        for k in range(TOP_K):
            _row_gather_start(y_hbm, lambda r, k=k: pos_ref[0, k, r], ybuf.at[s, k], sem.at[s], t_rows)

    @pl.when(i == 0)
    def _prime():
        start(posc_ref, 0)

    @pl.when(i + 1 < n_steps)
    def _prefetch():
        start(posn_ref, 1 - slot)

    for k in range(TOP_K):
        _row_gather_wait(y_hbm, ybuf.at[slot, k], sem.at[slot], t_rows)
    rec = route_ref[...]
    y0 = _unpack_bf16_pairs(ybuf[slot, 0], F32)
    y1 = _unpack_bf16_pairs(ybuf[slot, 1], F32)
    moe = y0 * rec[:, REC_W:REC_W + 1] + y1 * rec[:, REC_W + 1:REC_W + 2]
    x2 = x1_ref[...] + mod_ref[0, 5:6, :] * moe
    if final_norm:
        x2 = (x2 * lax.rsqrt(jnp.mean(x2 * x2, axis=-1, keepdims=True) + RMS_EPS)) * gf_ref[...]
    o_ref[...] = x2


def _combine_call(x1, route, mod3, g_final, y_rows, pos3, *, seq_len, final_norm):
    n_tok, d = x1.shape
    t = T_MOVE
    n_steps = n_tok // t
    tiles_per_seq = seq_len // t
    last = n_steps - 1
    return pl.pallas_call(
        functools.partial(_combine_kernel, n_steps=n_steps, final_norm=final_norm),
        grid=(n_steps,),
        in_specs=[
            pl.BlockSpec((1, TOP_K, t), lambda i: (i, 0, 0), memory_space=pltpu.SMEM),
            pl.BlockSpec((1, TOP_K, t), lambda i: (jnp.minimum(i + 1, last), 0, 0), memory_space=pltpu.SMEM),
            pl.BlockSpec((t, d), lambda i: (i, 0)),
            pl.BlockSpec((t, ROUTE_LANES), lambda i: (i, 0)),
            pl.BlockSpec((1,) + mod3.shape[1:], lambda i: (i // tiles_per_seq, 0, 0)),
            pl.BlockSpec((1, d), lambda i: (0, 0)),
            pl.BlockSpec(memory_space=pl.ANY),
        ],
        out_specs=pl.BlockSpec((t, d), lambda i: (i, 0)),
        out_shape=jax.ShapeDtypeStruct((n_tok, d), F32),
        scratch_shapes=[pltpu.VMEM((2, TOP_K, t, y_rows.shape[1]), y_rows.dtype),
                        pltpu.SemaphoreType.DMA((2,))],
        compiler_params=pltpu.CompilerParams(
            dimension_semantics=("arbitrary",), vmem_limit_bytes=VMEM_LIMIT_BYTES),
        name="moe_combine",
    )(pos3, pos3, x1, route, mod3, g_final, y_rows)


def kernel(x, c, w_ada, b_ada, g_norm1, w_in, conv_a, w_a_out, conv_b, b_conv_b, ln_conv_g, ln_conv_b,
           w_b_out, w_o, g_norm2, w_router_g, b_router_g, w_router_e, b_router_e, w_gate, w_up, w_down,
           g_final):
    bsz, seq_len, d = x.shape
    depth = w_ada.shape[0]
    n_tok = bsz * seq_len
    n_assign = n_tok * TOP_K
    assert seq_len % T_MIX == 0 and seq_len % T_MOVE == 0 and d % (2 * LANES) == 0
    assert n_tok % T_PLAN == 0
    assert n_assign % ROW_BLK == 0
    assert N_GROUPS + N_EXPERTS <= ROUTE_LANES
    n_rows = n_assign + N_EXPERTS * ROW_BLK

    c_pad = jnp.zeros((SUBLANES, d), F32).at[:bsz].set(c)
    xt = x.reshape(n_tok, d)
    row = lambda a: a.reshape(1, -1)
    for l in range(depth):
        mod = _mod_call(c_pad, w_ada[l], row(b_ada[l]))
        mod3 = mod[:bsz].reshape(bsz, 6, d)
        w_r = jnp.zeros((d, ROUTE_LANES), F32)
        w_r = w_r.at[:, :N_GROUPS].set(w_router_g[l]).at[:, N_GROUPS:N_GROUPS + N_EXPERTS].set(w_router_e[l])
        b_r = jnp.zeros((1, ROUTE_LANES), F32)
        b_r = b_r.at[0, :N_GROUPS].set(b_router_g[l]).at[0, N_GROUPS:N_GROUPS + N_EXPERTS].set(b_router_e[l])
        wr_hi, wr_lo = _split_bf16(w_r)
        wr_both = jnp.concatenate([wr_hi, wr_lo], axis=1)
        x1, h2p, route, cnt = _mixer_call(
            xt, mod3, row(g_norm1[l]), conv_a[l], conv_b[l], row(b_conv_b[l]), row(ln_conv_g[l]),
            row(ln_conv_b[l]), row(g_norm2[l]),
            w_in[l], w_a_out[l], w_b_out[l], w_o[l], wr_hi, wr_both, b_r, seq_len=seq_len)
        pstart_row, blk_e, next_e, n_used = _segment_layout(cnt[0], n_rows // ROW_BLK)
        pos = _plan_call(route, pstart_row)
        pos3 = pos[:, :TOP_K].reshape(n_tok // T_MOVE, T_MOVE, TOP_K).transpose(0, 2, 1)
        xs = _scatter_call(pos3, h2p, n_rows)
        y_rows = _expert_call(xs, blk_e, next_e, n_used, w_gate[l], w_up[l], w_down[l])
        xt = _combine_call(x1, route, mod3, row(g_final), y_rows, pos3, seq_len=seq_len,
                           final_norm=(l == depth - 1))
    return xt.reshape(bsz, seq_len, d)
```

```python
import functools

import jax
import jax.numpy as jnp
from jax import lax
from jax.experimental import pallas as pl
from jax.experimental.pallas import tpu as pltpu

F32 = jnp.float32
BF16 = jnp.bfloat16
U32 = jnp.uint32

N_GROUPS = 4
EXPERTS_PER_GROUP = 8
N_EXPERTS = N_GROUPS * EXPERTS_PER_GROUP
TOP_K = 2
K_SHORT = 3
K_CONF = 31
RMS_EPS = 1e-6
LN_EPS = 1e-5

LANES = 128
SUBLANES = 8
T_MIX = 256
HALO_B = 16
HALO_A = 8
CONV_ROWS = 64
COMBINE_ROWS = 64
PAIR_LAG = 1
ROUTE_AFTER_GROUP = 1
ROW_BLK = 512
T_MOVE = 1024
T_PLAN = 2048
W_STAGE_COLS = 512
ROUTE_LANES = LANES
REC_E, REC_W, REC_RANK = 0, 2, 4
VMEM_LIMIT_BYTES = 56 * 1024 * 1024


def _sigmoid(v):
    return 1.0 / (1.0 + jnp.exp(-v))


def _split_bf16(v):
    hi = v.astype(BF16)
    lo = (v - hi.astype(F32)).astype(BF16)
    return hi, lo


def _dot(a, b):
    return jnp.dot(a, b, preferred_element_type=F32)


def _const_spec(shape):
    nd = len(shape)
    return pl.BlockSpec(shape, lambda *_: (0,) * nd, pipeline_mode=pl.Buffered(1))


def _lane_ids(shape):
    return lax.broadcasted_iota(jnp.int32, shape, 1).astype(F32)


def _mod_kernel(c_ref, w_ref, b_ref, o_ref):
    c = c_ref[...]
    a_hi, a_lo = _split_bf16(c * _sigmoid(c))
    w_hi, w_lo = _split_bf16(w_ref[...])
    o_ref[...] = _dot(a_hi, w_hi) + _dot(a_lo, w_hi) + _dot(a_hi, w_lo) + b_ref[...]


def _mod_call(c_pad, w_ada, b_ada):
    rows, d = c_pad.shape
    n_out = w_ada.shape[1]
    blk = 1024
    return pl.pallas_call(
        _mod_kernel,
        grid=(n_out // blk,),
        in_specs=[
            pl.BlockSpec((rows, d), lambda j: (0, 0)),
            pl.BlockSpec((d, blk), lambda j: (0, j)),
            pl.BlockSpec((1, blk), lambda j: (0, j)),
        ],
        out_specs=pl.BlockSpec((rows, blk), lambda j: (0, j)),
        out_shape=jax.ShapeDtypeStruct((rows, n_out), F32),
        compiler_params=pltpu.CompilerParams(dimension_semantics=("arbitrary",)),
        name="adaln_mod",
    )(c_pad, w_ada, b_ada)


def _zero_bits_of(v):
    return (lax.bitcast_convert_type(v, U32) >> 16) >> 16


def _depthwise_lane_block(buf_ref, w_ref, out_ref, phase_ref, n_taps, row0, rows, cb, after=None):
    first, stop = rows
    max_off = row0 + n_taps - 1
    n_keep = stop - first + (max_off // SUBLANES) * SUBLANES
    cols = slice(cb * LANES, (cb + 1) * LANES)
    for s in sorted({(row0 + k) % SUBLANES for k in range(n_taps)}):
        phase_ref[s, pl.ds(0, n_keep), :] = buf_ref[pl.ds(first + s, n_keep), cols]
    row_blocks = range(first, stop, CONV_ROWS)
    acc = [None] * len(row_blocks)
    for k in range(n_taps):
        q, s = divmod(row0 + k, SUBLANES)
        w_row = w_ref[k:k + 1, cols]
        if after is not None:
            w_row = lax.bitcast_convert_type(lax.bitcast_convert_type(w_row, U32) | after, F32)
        w_k = jnp.broadcast_to(w_row, (CONV_ROWS, LANES))
        for j, r0 in enumerate(row_blocks):
            term = phase_ref[s, pl.ds(r0 - first + q * SUBLANES, CONV_ROWS), :] * w_k
            acc[j] = term if acc[j] is None else acc[j] + term
    for j, r0 in enumerate(row_blocks):
        out_ref[pl.ds(r0, CONV_ROWS), cols] = acc[j]
    return _zero_bits_of(acc[-1][0:1, :])


def _route(logits, run_ref, live):
    shape = logits.shape
    lane = _lane_ids(shape)
    big = float(4 * ROUTE_LANES)
    neg_inf = -jnp.inf

    def first_argmax(v):
        m = jnp.max(v, axis=-1, keepdims=True)
        return m, jnp.min(jnp.where(v == m, lane, big), axis=-1, keepdims=True)

    is_group = lane < N_GROUPS
    g_max, g_idx = first_argmax(jnp.where(is_group, logits, neg_inf))
    g_prob = 1.0 / jnp.sum(jnp.where(is_group, jnp.exp(logits - g_max), 0.0), axis=-1, keepdims=True)
    lo = N_GROUPS + EXPERTS_PER_GROUP * g_idx
    in_group = (lane >= lo) & (lane < lo + EXPERTS_PER_GROUP)
    e_logits = jnp.where(in_group, logits, neg_inf)
    m1, i1 = first_argmax(e_logits)
    m2, i2 = first_argmax(jnp.where(lane == i1, neg_inf, e_logits))
    t = jnp.exp(m2 - m1)
    p1 = 1.0 / (1.0 + t)
    p2 = t / (1.0 + t)
    e1 = i1 - N_GROUPS
    e2 = i2 - N_GROUPS

    hot1 = lane == e1
    hot2 = lane == e2
    both = jnp.where(hot1 | hot2, 1.0, 0.0)
    n_rows = shape[0]
    earlier = (lax.broadcasted_iota(jnp.int32, (n_rows, n_rows), 0)
               > lax.broadcasted_iota(jnp.int32, (n_rows, n_rows), 1))
    before = _dot(jnp.where(earlier, 1.0, 0.0).astype(BF16), both.astype(BF16)) + run_ref[0:1, :]
    rank1 = jnp.sum(jnp.where(hot1, before, 0.0), axis=-1, keepdims=True)
    rank2 = jnp.sum(jnp.where(hot2, before, 0.0), axis=-1, keepdims=True)
    run_ref[0:1, :] = run_ref[0:1, :] + live * jnp.sum(both, axis=0, keepdims=True)

    rec = jnp.where(lane == REC_E, e1, 0.0)
    rec = jnp.where(lane == REC_E + 1, e2, rec)
    rec = jnp.where(lane == REC_W, g_prob * p1, rec)
    rec = jnp.where(lane == REC_W + 1, g_prob * p2, rec)
    rec = jnp.where(lane == REC_RANK, rank1, rec)
    rec = jnp.where(lane == REC_RANK + 1, rank2, rec)
    return rec


def _pack_bf16_pairs(v_bf16):
    half = v_bf16.shape[1] // 2
    lo = lax.bitcast_convert_type(v_bf16[:, :half].astype(F32), U32)
    hi = lax.bitcast_convert_type(v_bf16[:, half:].astype(F32), U32)
    return (lo >> 16) | hi


def _unpack_bf16_pairs(p_u32, dtype=BF16):
    lo = lax.bitcast_convert_type(p_u32 << 16, F32).astype(dtype)
    hi = lax.bitcast_convert_type(p_u32 & jnp.uint32(0xFFFF0000), F32).astype(dtype)
    return jnp.concatenate([lo, hi], axis=1)


def _stage_weight_bf16(src_hbm, dst_ref, stage_ref, sem):
    chunk = stage_ref.shape[2]
    n_chunks = src_hbm.shape[1] // chunk

    def cols(j):
        return pl.ds(pl.multiple_of(j * chunk, chunk), chunk)

    def copy(j, slot):
        return pltpu.make_async_copy(src_hbm.at[:, cols(j)], stage_ref.at[slot], sem.at[slot])

    copy(0, 0).start()

    def body(j, carry):
        slot = j % 2

        @pl.when(j + 1 < n_chunks)
        def _next():
            copy(j + 1, 1 - slot).start()

        copy(j, slot).wait()
        dst_ref[:, cols(j)] = stage_ref[slot].astype(BF16)
        return carry

    lax.fori_loop(0, n_chunks, body, 0)


def _mixer_kernel(x_ref, modn_ref, modp_ref, g1_ref, ca_ref, cb_ref, bcb_ref, lng_ref, lnb_ref, g2_ref,
                  wrh_ref, wrl_ref, br_ref, win_hbm, wa_hbm, wb_hbm, wo_hbm,
                  x1_ref, h2p_ref, route_ref, cnt_ref,
                  wu_ref, wcx_ref, ba_ref, sga_ref, sgb_ref, xs_ref, cva_ref, cvb_ref, run_ref, logit_ref,
                  phase_ref, win_ref, wa_ref, wb_ref, wo_ref, stage_ref, stage_sem,
                  *, n_tiles):
    i = pl.program_id(0)
    t_rows, d = x_ref.shape

    @pl.when(i == 0)
    def _init():
        for ref in (wu_ref, wcx_ref, ba_ref, sga_ref, sgb_ref, xs_ref, run_ref, logit_ref):
            ref[...] = jnp.zeros(ref.shape, ref.dtype)
        for src, dst in ((win_hbm, win_ref), (wa_hbm, wa_ref), (wb_hbm, wb_ref), (wo_hbm, wo_ref)):
            _stage_weight_bf16(src, dst, stage_ref, stage_sem)

    x = x_ref[...]
    sh1 = modn_ref[0, 0:1, :]
    sc1 = modn_ref[0, 1:2, :]
    h = (x * lax.rsqrt(jnp.mean(x * x, axis=-1, keepdims=True) + RMS_EPS)) * g1_ref[...]
    hb = (h * (1.0 + sc1) + sh1).astype(BF16)

    def proj(g, after):
        lhs = hb
        if after is not None:
            bits = pltpu.bitcast(hb, U32)
            lhs = pltpu.bitcast(bits | jnp.broadcast_to(after[0:1, 0:1], bits.shape), BF16)
        return _dot(lhs, win_ref[:, g * d:(g + 1) * d])

    def conv_work(rows):
        items = []
        for cb in range(d // LANES):
            items.append(functools.partial(_depthwise_lane_block, wcx_ref, ca_ref, cva_ref,
                                           phase_ref.at[2], K_SHORT, HALO_A - K_SHORT // 2, rows, cb))
            items.append(functools.partial(_depthwise_lane_block, wu_ref, cb_ref, cvb_ref,
                                           phase_ref.at[cb % 2], K_CONF, HALO_B - K_CONF // 2, rows, cb))
        return items

    early = conv_work((0, t_rows - CONV_ROWS))
    n_proj = 7
    z = []
    conv_done = None
    for g in range(n_proj):
        z.append(proj(g, conv_done))
        released = _zero_bits_of(z[g - PAIR_LAG][0:1, 0:LANES]) if g >= PAIR_LAG else None
        for item in early[g * len(early) // n_proj:(g + 1) * len(early) // n_proj]:
            conv_done = item(after=released)
        if g == ROUTE_AFTER_GROUP:
            route_ref[...] = _route(logit_ref[...], run_ref, (i > 1).astype(F32))
            cnt_ref[...] = jnp.broadcast_to(run_ref[0:1, :], cnt_ref.shape)
    b_a = z[0]
    cx = z[1] * z[2]
    u = z[3] * _sigmoid(z[4])
    sg_a = _sigmoid(z[5])
    sg_b = _sigmoid(z[6])

    same_seq = ((i % n_tiles) != 0).astype(F32)
    wu_ref[pl.ds(HALO_B + t_rows, HALO_B), :] = u[0:HALO_B] * same_seq
    wcx_ref[pl.ds(HALO_A + t_rows, HALO_A), :] = cx[0:HALO_A] * same_seq

    for item in conv_work((t_rows - CONV_ROWS, t_rows)):
        item()
    y_a = _dot((ba_ref[...] * cva_ref[...]).astype(BF16), wa_ref[...])
    v = cvb_ref[...] + bcb_ref[...]
    mu = jnp.mean(v, axis=-1, keepdims=True)
    vc = v - mu
    var = jnp.mean(vc * vc, axis=-1, keepdims=True)
    v = (vc * lax.rsqrt(var + LN_EPS)) * lng_ref[...] + lnb_ref[...]
    y_b = _dot((v * _sigmoid(v)).astype(BF16), wb_ref[...])
    merged = sga_ref[...] * y_a + sgb_ref[...] * y_b
    mix = _dot(merged.astype(BF16), wo_ref[...])

    gt1 = modp_ref[0, 2:3, :]
    sh2 = modp_ref[0, 3:4, :]
    sc2 = modp_ref[0, 4:5, :]
    x1 = xs_ref[...] + gt1 * mix
    h2 = (x1 * lax.rsqrt(jnp.mean(x1 * x1, axis=-1, keepdims=True) + RMS_EPS)) * g2_ref[...]
    h2 = h2 * (1.0 + sc2) + sh2
    h2_hi, h2_lo = _split_bf16(h2)
    h2p = _pack_bf16_pairs(h2_hi)
    hi_part = _dot(h2_hi, wrl_ref[...])
    logit_ref[...] = (hi_part[:, :ROUTE_LANES] + _dot(h2_lo, wrh_ref[...]) + hi_part[:, ROUTE_LANES:]
                      + br_ref[...])

    @pl.when(i < pl.num_programs(0) - 1)
    def _emit_rows():
        x1_ref[...] = x1
        h2p_ref[...] = h2p

    tail_u = wu_ref[pl.ds(t_rows, HALO_B), :]
    wu_ref[pl.ds(0, HALO_B), :] = tail_u * same_seq
    wu_ref[pl.ds(HALO_B, t_rows), :] = u
    tail_cx = wcx_ref[pl.ds(t_rows, HALO_A), :]
    wcx_ref[pl.ds(0, HALO_A), :] = tail_cx * same_seq
    wcx_ref[pl.ds(HALO_A, t_rows), :] = cx
    ba_ref[...] = b_a
    sga_ref[...] = sg_a
    sgb_ref[...] = sg_b
    xs_ref[...] = x


def _mixer_call(x2d, mod3, g1, conv_a, conv_b, b_conv_b, ln_g, ln_b, g2,
                w_in, w_a, w_b, w_o, wr_hi, wr_lo, b_r, *, seq_len):
    n_tok, d = x2d.shape
    t = T_MIX
    n_tiles = seq_len // t
    n_chunks = n_tok // t
    last = n_chunks - 1

    def cur(i):
        return jnp.minimum(i, last)

    def prev(i):
        return jnp.clip(i - 1, 0, last)

    def prev2(i):
        return jnp.clip(i - 2, 0, last)

    row_spec_prev = lambda width: pl.BlockSpec((t, width), lambda i: (prev(i), 0))
    vec = lambda a: _const_spec(a.shape)
    hbm = pl.BlockSpec(memory_space=pl.ANY)
    in_specs = [
        pl.BlockSpec((t, d), lambda i: (cur(i), 0)),
        pl.BlockSpec((1,) + mod3.shape[1:], lambda i: (cur(i) // n_tiles, 0, 0)),
        pl.BlockSpec((1,) + mod3.shape[1:], lambda i: (prev(i) // n_tiles, 0, 0)),
        vec(g1), vec(conv_a), vec(conv_b), vec(b_conv_b), vec(ln_g), vec(ln_b), vec(g2),
        vec(wr_hi), vec(wr_lo), vec(b_r),
        hbm, hbm, hbm, hbm,
    ]
    out_specs = [row_spec_prev(d), row_spec_prev(d // 2),
                 pl.BlockSpec((t, ROUTE_LANES), lambda i: (prev2(i), 0)),
                 pl.BlockSpec((SUBLANES, ROUTE_LANES), lambda i: (0, 0))]
    out_shape = [
        jax.ShapeDtypeStruct((n_tok, d), F32),
        jax.ShapeDtypeStruct((n_tok, d // 2), U32),
        jax.ShapeDtypeStruct((n_tok, ROUTE_LANES), F32),
        jax.ShapeDtypeStruct((SUBLANES, ROUTE_LANES), F32),
    ]
    scratch = [
        pltpu.VMEM((t + 2 * HALO_B, d), F32),
        pltpu.VMEM((t + 2 * HALO_A, d), F32),
        pltpu.VMEM((t, d), F32), pltpu.VMEM((t, d), F32), pltpu.VMEM((t, d), F32), pltpu.VMEM((t, d), F32),
        pltpu.VMEM((t, d), F32), pltpu.VMEM((t, d), F32),
        pltpu.VMEM((SUBLANES, ROUTE_LANES), F32),
        pltpu.VMEM((t, ROUTE_LANES), F32),
        pltpu.VMEM((3, SUBLANES, t + 2 * HALO_B, LANES), F32),
        pltpu.VMEM(w_in.shape, BF16), pltpu.VMEM(w_a.shape, BF16), pltpu.VMEM(w_b.shape, BF16),
        pltpu.VMEM(w_o.shape, BF16),
        pltpu.VMEM((2, d, W_STAGE_COLS), F32), pltpu.SemaphoreType.DMA((2,)),
    ]
    return pl.pallas_call(
        functools.partial(_mixer_kernel, n_tiles=n_tiles),
        grid=(n_chunks + 2,),
        in_specs=in_specs,
        out_specs=out_specs,
        out_shape=out_shape,
        scratch_shapes=scratch,
        compiler_params=pltpu.CompilerParams(
            dimension_semantics=("arbitrary",), vmem_limit_bytes=VMEM_LIMIT_BYTES),
        name="mixer_router",
    )(x2d, mod3, mod3, g1, conv_a, conv_b, b_conv_b, ln_g, ln_b, g2,
      wr_hi, wr_lo, b_r, w_in, w_a, w_b, w_o)


def _segment_layout(counts_row, n_blocks):
    counts = counts_row[:N_EXPERTS].astype(jnp.int32)
    nblk = (counts + ROW_BLK - 1) // ROW_BLK
    blk_end = jnp.cumsum(nblk)
    blk0 = blk_end - nblk
    n_used = blk_end[-1:].astype(jnp.int32)
    pstart_row = jnp.zeros((1, ROUTE_LANES), F32).at[0, :N_EXPERTS].set((blk0 * ROW_BLK).astype(F32))
    blk = jnp.arange(n_blocks, dtype=jnp.int32)[:, None]
    blk_e = jnp.minimum(jnp.sum(blk_end[None, :] <= blk, axis=1), N_EXPERTS - 1).astype(jnp.int32)
    seg_end = jnp.min(jnp.where(blk_end[None, :] > blk, blk_end[None, :], n_blocks), axis=1)
    next_e = jnp.sum(blk_end[None, :] <= seg_end[:, None], axis=1).astype(jnp.int32)
    next_e = jnp.where(seg_end >= n_used[0], -1, next_e)
    return pstart_row, blk_e, next_e, n_used


def _plan_kernel(route_ref, pstart_ref, pos_ref):
    rec = route_ref[...]
    lane = _lane_ids(rec.shape)
    pos = jnp.zeros(rec.shape, F32)
    for k in range(TOP_K):
        e = rec[:, REC_E + k:REC_E + k + 1]
        seg = jnp.sum(jnp.where(lane == e, pstart_ref[...], 0.0), axis=-1, keepdims=True)
        pos = jnp.where(lane == k, seg + rec[:, REC_RANK + k:REC_RANK + k + 1], pos)
    pos_ref[...] = pos.astype(jnp.int32)


def _plan_call(route, pstart_row):
    n_tok = route.shape[0]
    t = T_PLAN
    return pl.pallas_call(
        _plan_kernel,
        grid=(n_tok // t,),
        in_specs=[pl.BlockSpec((t, ROUTE_LANES), lambda i: (i, 0)),
                  pl.BlockSpec((1, ROUTE_LANES), lambda i: (0, 0))],
        out_specs=pl.BlockSpec((t, ROUTE_LANES), lambda i: (i, 0)),
        out_shape=jax.ShapeDtypeStruct((n_tok, ROUTE_LANES), jnp.int32),
        compiler_params=pltpu.CompilerParams(dimension_semantics=("arbitrary",)),
        name="moe_plan",
    )(route, pstart_row)


def _scatter_kernel(pos_ref, rows_ref, buf_in_hbm, buf_hbm, sem):
    del buf_in_hbm
    t_rows = rows_ref.shape[0]

    for r in range(t_rows):
        for k in range(TOP_K):
            pltpu.make_async_copy(rows_ref.at[pl.ds(r, 1), :],
                                  buf_hbm.at[pl.ds(pos_ref[0, k, r], 1), :], sem).start(priority=k)
    for _ in range(TOP_K):
        pltpu.make_async_copy(rows_ref, buf_hbm.at[pl.ds(0, t_rows), :], sem).wait()


def _scatter_call(pos3, rows, n_rows):
    n_tok, width = rows.shape
    t = T_MOVE
    buf0 = jnp.zeros((n_rows, width), rows.dtype)
    return pl.pallas_call(
        _scatter_kernel,
        grid=(n_tok // t,),
        in_specs=[
            pl.BlockSpec((1, TOP_K, t), lambda i: (i, 0, 0), memory_space=pltpu.SMEM),
            pl.BlockSpec((t, width), lambda i: (i, 0)),
            pl.BlockSpec(memory_space=pl.ANY),
        ],
        out_specs=pl.BlockSpec(memory_space=pl.ANY),
        out_shape=jax.ShapeDtypeStruct((n_rows, width), rows.dtype),
        scratch_shapes=[pltpu.SemaphoreType.DMA(())],
        input_output_aliases={2: 0},
        compiler_params=pltpu.CompilerParams(dimension_semantics=("arbitrary",)),
        name="moe_scatter",
    )(pos3, rows, buf0)


def _expert_kernel(blk_e_ref, next_e_ref, n_used_ref, xs_hbm, wg_hbm, wu_hbm, wd_hbm, y_hbm,
                   wg_st, wu_st, wd_st, wg_bf, wu_bf, wd_bf, xbuf, ybuf, w_sem, in_sem, out_sem):
    n_used = n_used_ref[0]

    def rows(b):
        return pl.ds(pl.multiple_of(b * ROW_BLK, ROW_BLK), ROW_BLK)

    def load(b, slot):
        return pltpu.make_async_copy(xs_hbm.at[rows(b), :], xbuf.at[slot], in_sem.at[slot])

    def store(b, slot):
        return pltpu.make_async_copy(ybuf.at[slot], y_hbm.at[rows(b), :], out_sem.at[slot])

    def weight_copies(e, wslot):
        return [pltpu.make_async_copy(src.at[e], dst.at[wslot], w_sem.at[wslot, n])
                for n, (src, dst) in enumerate(((wg_hbm, wg_st), (wu_hbm, wu_st), (wd_hbm, wd_st)))]

    for copy in weight_copies(blk_e_ref[0], 0):
        copy.start(priority=1)
    load(0, 0).start()

    def body(b, wslot):
        e = blk_e_ref[b]
        new_expert = jnp.logical_or(b == 0, e != blk_e_ref[jnp.maximum(b - 1, 0)])
        slot = b % 2

        @pl.when(new_expert)
        def _switch_expert():
            for copy in weight_copies(e, wslot):
                copy.wait()
            wg_bf[...] = wg_st[wslot].astype(BF16)
            wu_bf[...] = wu_st[wslot].astype(BF16)
            wd_bf[...] = wd_st[wslot].astype(BF16)
            nxt = next_e_ref[b]

            @pl.when(nxt >= 0)
            def _prefetch_weights():
                for copy in weight_copies(nxt, 1 - wslot):
                    copy.start(priority=1)

        @pl.when(b + 1 < n_used)
        def _prefetch_rows():
            load(b + 1, 1 - slot).start()

        load(b, slot).wait()

        @pl.when(b >= 2)
        def _window_free():
            store(b - 2, slot).wait()

        xb = _unpack_bf16_pairs(xbuf[slot])
        a = _dot(xb, wg_bf[...])
        u = _dot(xb, wu_bf[...])
        hid = ((a * _sigmoid(a)) * u).astype(BF16)
        ybuf[slot] = _pack_bf16_pairs(_dot(hid, wd_bf[...]).astype(BF16))
        store(b, slot).start()
        return jnp.where(new_expert, 1 - wslot, wslot)

    lax.fori_loop(0, n_used, body, jnp.int32(0))

    @pl.when(n_used >= 2)
    def _drain_older():
        store(n_used - 2, n_used % 2).wait()

    store(n_used - 1, (n_used - 1) % 2).wait()

    ybuf[0] = jnp.zeros(ybuf.shape[1:], ybuf.dtype)

    def fill(b, carry):
        copy = pltpu.make_async_copy(ybuf.at[0], y_hbm.at[rows(b), :], out_sem.at[0])
        copy.start()
        copy.wait()
        return carry

    lax.fori_loop(n_used, y_hbm.shape[0] // ROW_BLK, fill, 0)


def _expert_call(xs, blk_e, next_e, n_used, w_gate, w_up, w_down):
    n_rows, half = xs.shape
    d = 2 * half
    d_e = w_gate.shape[-1]
    hbm = pl.BlockSpec(memory_space=pl.ANY)
    grid_spec = pltpu.PrefetchScalarGridSpec(
        num_scalar_prefetch=3,
        grid=(1,),
        in_specs=[hbm, hbm, hbm, hbm],
        out_specs=hbm,
        scratch_shapes=[
            pltpu.VMEM((2, d, d_e), F32), pltpu.VMEM((2, d, d_e), F32), pltpu.VMEM((2, d_e, d), F32),
            pltpu.VMEM((d, d_e), BF16), pltpu.VMEM((d, d_e), BF16), pltpu.VMEM((d_e, d), BF16),
            pltpu.VMEM((2, ROW_BLK, half), xs.dtype), pltpu.VMEM((2, ROW_BLK, half), xs.dtype),
            pltpu.SemaphoreType.DMA((2, 3)), pltpu.SemaphoreType.DMA((2,)), pltpu.SemaphoreType.DMA((2,)),
        ],
    )
    return pl.pallas_call(
        _expert_kernel,
        grid_spec=grid_spec,
        out_shape=jax.ShapeDtypeStruct((n_rows, half), xs.dtype),
        compiler_params=pltpu.CompilerParams(
            dimension_semantics=("arbitrary",), vmem_limit_bytes=VMEM_LIMIT_BYTES),
        name="moe_experts",
    )(blk_e, next_e, n_used, xs, w_gate, w_up, w_down)


def _row_gather_wait(src_hbm, dst, sem, n_rows):
    pltpu.make_async_copy(src_hbm.at[pl.ds(0, n_rows), :], dst, sem).wait()


def _combine_kernel(posc_ref, posn_ref, x1_ref, route_ref, mod_ref, gf_ref, y_hbm, o_ref, ybuf, sem,
                    *, n_steps, final_norm):
    i = pl.program_id(0)
    slot = i % 2
    t_rows = x1_ref.shape[0]

    def gather_rows(pos_ref, s, lo, hi):
        for k in range(TOP_K):
            for r in range(lo, hi):
                pltpu.make_async_copy(y_hbm.at[pl.ds(pos_ref[0, k, r], 1), :],
                                      ybuf.at[s, k, pl.ds(r, 1), :], sem.at[s]).start(priority=r % 2)

    def wait_tile(s):
        for k in range(TOP_K):
            _row_gather_wait(y_hbm, ybuf.at[s, k], sem.at[s], t_rows)

    @pl.when(i == 0)
    def _prime():
        gather_rows(posc_ref, 0, 0, t_rows)

    wait_tile(slot)
    gt2 = mod_ref[0, 5:6, :]
    for lo in range(0, t_rows, COMBINE_ROWS):
        rows = pl.ds(lo, COMBINE_ROWS)
        rec = route_ref[rows, :]
        y0 = _unpack_bf16_pairs(ybuf[slot, 0, rows, :], F32)
        y1 = _unpack_bf16_pairs(ybuf[slot, 1, rows, :], F32)
        moe = y0 * rec[:, REC_W:REC_W + 1] + y1 * rec[:, REC_W + 1:REC_W + 2]
        x2 = x1_ref[rows, :] + gt2 * moe
        if final_norm:
            x2 = (x2 * lax.rsqrt(jnp.mean(x2 * x2, axis=-1, keepdims=True) + RMS_EPS)) * gf_ref[...]
        o_ref[rows, :] = x2
        gather_rows(posn_ref, 1 - slot, lo, lo + COMBINE_ROWS)

    @pl.when(i == n_steps - 1)
    def _drain():
        wait_tile(1 - slot)


def _combine_call(x1, route, mod3, g_final, y_rows, pos3, *, seq_len, final_norm):
    n_tok, d = x1.shape
    t = T_MOVE
    n_steps = n_tok // t
    tiles_per_seq = seq_len // t
    last = n_steps - 1
    return pl.pallas_call(
        functools.partial(_combine_kernel, n_steps=n_steps, final_norm=final_norm),
        grid=(n_steps,),
        in_specs=[
            pl.BlockSpec((1, TOP_K, t), lambda i: (i, 0, 0), memory_space=pltpu.SMEM),
            pl.BlockSpec((1, TOP_K, t), lambda i: (jnp.minimum(i + 1, last), 0, 0), memory_space=pltpu.SMEM),
            pl.BlockSpec((t, d), lambda i: (i, 0)),
            pl.BlockSpec((t, ROUTE_LANES), lambda i: (i, 0)),
            pl.BlockSpec((1,) + mod3.shape[1:], lambda i: (i // tiles_per_seq, 0, 0)),
            pl.BlockSpec((1, d), lambda i: (0, 0)),
            pl.BlockSpec(memory_space=pl.ANY),
        ],
        out_specs=pl.BlockSpec((t, d), lambda i: (i, 0)),
        out_shape=jax.ShapeDtypeStruct((n_tok, d), F32),
        scratch_shapes=[pltpu.VMEM((2, TOP_K, t, y_rows.shape[1]), y_rows.dtype),
                        pltpu.SemaphoreType.DMA((2,))],
        compiler_params=pltpu.CompilerParams(
            dimension_semantics=("arbitrary",), vmem_limit_bytes=VMEM_LIMIT_BYTES),
        name="moe_combine",
    )(pos3, pos3, x1, route, mod3, g_final, y_rows)


def kernel(x, c, w_ada, b_ada, g_norm1, w_in, conv_a, w_a_out, conv_b, b_conv_b, ln_conv_g, ln_conv_b,
           w_b_out, w_o, g_norm2, w_router_g, b_router_g, w_router_e, b_router_e, w_gate, w_up, w_down,
           g_final):
    bsz, seq_len, d = x.shape
    depth = w_ada.shape[0]
    n_tok = bsz * seq_len
    n_assign = n_tok * TOP_K
    assert seq_len % T_MIX == 0 and seq_len % T_MOVE == 0 and d % (2 * LANES) == 0
    assert n_tok % T_PLAN == 0
    assert n_assign % ROW_BLK == 0
    assert N_GROUPS + N_EXPERTS <= ROUTE_LANES
    n_rows = n_assign + N_EXPERTS * ROW_BLK

    c_pad = jnp.zeros((SUBLANES, d), F32).at[:bsz].set(c)
    xt = x.reshape(n_tok, d)
    row = lambda a: a.reshape(1, -1)
    for l in range(depth):
        mod = _mod_call(c_pad, w_ada[l], row(b_ada[l]))
        mod3 = mod[:bsz].reshape(bsz, 6, d)
        w_r = jnp.zeros((d, ROUTE_LANES), F32)
        w_r = w_r.at[:, :N_GROUPS].set(w_router_g[l]).at[:, N_GROUPS:N_GROUPS + N_EXPERTS].set(w_router_e[l])
        b_r = jnp.zeros((1, ROUTE_LANES), F32)
        b_r = b_r.at[0, :N_GROUPS].set(b_router_g[l]).at[0, N_GROUPS:N_GROUPS + N_EXPERTS].set(b_router_e[l])
        wr_hi, wr_lo = _split_bf16(w_r)
        wr_both = jnp.concatenate([wr_hi, wr_lo], axis=1)
        x1, h2p, route, cnt = _mixer_call(
            xt, mod3, row(g_norm1[l]), conv_a[l], conv_b[l], row(b_conv_b[l]), row(ln_conv_g[l]),
            row(ln_conv_b[l]), row(g_norm2[l]),
            w_in[l], w_a_out[l], w_b_out[l], w_o[l], wr_hi, wr_both, b_r, seq_len=seq_len)
        pstart_row, blk_e, next_e, n_used = _segment_layout(cnt[0], n_rows // ROW_BLK)
        pos = _plan_call(route, pstart_row)
        pos3 = pos[:, :TOP_K].reshape(n_tok // T_MOVE, T_MOVE, TOP_K).transpose(0, 2, 1)
        xs = _scatter_call(pos3, h2p, n_rows)
        y_rows = _expert_call(xs, blk_e, next_e, n_used, w_gate[l], w_up[l], w_down[l])
        xt = _combine_call(x1, route, mod3, row(g_final), y_rows, pos3, seq_len=seq_len,
                           final_norm=(l == depth - 1))
    return xt.reshape(bsz, seq_len, d)
```

```python
import functools

import jax
import jax.numpy as jnp
from jax import lax
from jax.experimental import pallas as pl
from jax.experimental.pallas import tpu as pltpu

F32 = jnp.float32
BF16 = jnp.bfloat16
U32 = jnp.uint32

N_GROUPS = 4
EXPERTS_PER_GROUP = 8
N_EXPERTS = N_GROUPS * EXPERTS_PER_GROUP
TOP_K = 2
K_SHORT = 3
K_CONF = 31
RMS_EPS = 1e-6
LN_EPS = 1e-5

LANES = 128
SUBLANES = 8
T_MIX = 256
HALO_B = 16
HALO_A = 8
CONV_ROWS = 64
CONV_BLOCKS_LIVE = 3
PAIR_LAG = 1
ROUTE_AFTER_GROUP = 1
ROW_BLK = 512
T_MOVE = 1024
T_PLAN = 2048
W_STAGE_COLS = 512
ROUTE_LANES = LANES
REC_E, REC_W, REC_RANK = 0, 2, 4
VMEM_LIMIT_BYTES = 56 * 1024 * 1024


def _sigmoid(v):
    return 1.0 / (1.0 + jnp.exp(-v))


def _split_bf16(v):
    hi = v.astype(BF16)
    lo = (v - hi.astype(F32)).astype(BF16)
    return hi, lo


def _dot(a, b):
    return jnp.dot(a, b, preferred_element_type=F32)


def _const_spec(shape):
    nd = len(shape)
    return pl.BlockSpec(shape, lambda *_: (0,) * nd, pipeline_mode=pl.Buffered(1))


def _lane_ids(shape):
    return lax.broadcasted_iota(jnp.int32, shape, 1).astype(F32)


def _mod_kernel(c_ref, w_ref, b_ref, o_ref):
    c = c_ref[...]
    a_hi, a_lo = _split_bf16(c * _sigmoid(c))
    w_hi, w_lo = _split_bf16(w_ref[...])
    o_ref[...] = _dot(a_hi, w_hi) + _dot(a_lo, w_hi) + _dot(a_hi, w_lo) + b_ref[...]


def _mod_call(c_pad, w_ada, b_ada):
    rows, d = c_pad.shape
    n_out = w_ada.shape[1]
    blk = 1024
    return pl.pallas_call(
        _mod_kernel,
        grid=(n_out // blk,),
        in_specs=[
            pl.BlockSpec((rows, d), lambda j: (0, 0)),
            pl.BlockSpec((d, blk), lambda j: (0, j)),
            pl.BlockSpec((1, blk), lambda j: (0, j)),
        ],
        out_specs=pl.BlockSpec((rows, blk), lambda j: (0, j)),
        out_shape=jax.ShapeDtypeStruct((rows, n_out), F32),
        compiler_params=pltpu.CompilerParams(dimension_semantics=("arbitrary",)),
        name="adaln_mod",
    )(c_pad, w_ada, b_ada)


def _zero_bits_of(v):
    return (lax.bitcast_convert_type(v, U32) >> 16) >> 16


def _depthwise_lane_block(buf_ref, w_ref, out_ref, phase_ref, n_taps, row0, rows, cb, after=None):
    first, stop = rows
    max_off = row0 + n_taps - 1
    n_keep = stop - first + (max_off // SUBLANES) * SUBLANES
    cols = slice(cb * LANES, (cb + 1) * LANES)
    for s in sorted({(row0 + k) % SUBLANES for k in range(n_taps)}):
        phase_ref[s, pl.ds(0, n_keep), :] = buf_ref[pl.ds(first + s, n_keep), cols]
    all_blocks = list(range(first, stop, CONV_ROWS))
    for g0 in range(0, len(all_blocks), CONV_BLOCKS_LIVE):
        row_blocks = all_blocks[g0:g0 + CONV_BLOCKS_LIVE]
        acc = [None] * len(row_blocks)
        for k in range(n_taps):
            q, s = divmod(row0 + k, SUBLANES)
            w_row = w_ref[k:k + 1, cols]
            if after is not None:
                w_row = lax.bitcast_convert_type(lax.bitcast_convert_type(w_row, U32) | after, F32)
            w_k = jnp.broadcast_to(w_row, (CONV_ROWS, LANES))
            for j, r0 in enumerate(row_blocks):
                term = phase_ref[s, pl.ds(r0 - first + q * SUBLANES, CONV_ROWS), :] * w_k
                acc[j] = term if acc[j] is None else acc[j] + term
        for j, r0 in enumerate(row_blocks):
            out_ref[pl.ds(r0, CONV_ROWS), cols] = acc[j]
    return _zero_bits_of(acc[-1][0:1, :])


def _route(logits, run_ref, live):
    shape = logits.shape
    lane = _lane_ids(shape)
    big = float(4 * ROUTE_LANES)
    neg_inf = -jnp.inf

    def first_argmax(v):
        m = jnp.max(v, axis=-1, keepdims=True)
        return m, jnp.min(jnp.where(v == m, lane, big), axis=-1, keepdims=True)

    is_group = lane < N_GROUPS
    g_max, g_idx = first_argmax(jnp.where(is_group, logits, neg_inf))
    g_prob = 1.0 / jnp.sum(jnp.where(is_group, jnp.exp(logits - g_max), 0.0), axis=-1, keepdims=True)
    lo = N_GROUPS + EXPERTS_PER_GROUP * g_idx
    in_group = (lane >= lo) & (lane < lo + EXPERTS_PER_GROUP)
    e_logits = jnp.where(in_group, logits, neg_inf)
    m1, i1 = first_argmax(e_logits)
    m2, i2 = first_argmax(jnp.where(lane == i1, neg_inf, e_logits))
    t = jnp.exp(m2 - m1)
    p1 = 1.0 / (1.0 + t)
    p2 = t / (1.0 + t)
    e1 = i1 - N_GROUPS
    e2 = i2 - N_GROUPS

    hot1 = lane == e1
    hot2 = lane == e2
    both = jnp.where(hot1 | hot2, 1.0, 0.0)
    n_rows = shape[0]
    earlier = (lax.broadcasted_iota(jnp.int32, (n_rows, n_rows), 0)
               > lax.broadcasted_iota(jnp.int32, (n_rows, n_rows), 1))
    before = _dot(jnp.where(earlier, 1.0, 0.0).astype(BF16), both.astype(BF16)) + run_ref[0:1, :]
    rank1 = jnp.sum(jnp.where(hot1, before, 0.0), axis=-1, keepdims=True)
    rank2 = jnp.sum(jnp.where(hot2, before, 0.0), axis=-1, keepdims=True)
    run_ref[0:1, :] = run_ref[0:1, :] + live * jnp.sum(both, axis=0, keepdims=True)

    rec = jnp.where(lane == REC_E, e1, 0.0)
    rec = jnp.where(lane == REC_E + 1, e2, rec)
    rec = jnp.where(lane == REC_W, g_prob * p1, rec)
    rec = jnp.where(lane == REC_W + 1, g_prob * p2, rec)
    rec = jnp.where(lane == REC_RANK, rank1, rec)
    rec = jnp.where(lane == REC_RANK + 1, rank2, rec)
    return rec


def _pack_bf16_pairs(v_bf16):
    half = v_bf16.shape[1] // 2
    lo = lax.bitcast_convert_type(v_bf16[:, :half].astype(F32), U32)
    hi = lax.bitcast_convert_type(v_bf16[:, half:].astype(F32), U32)
    return (lo >> 16) | hi


def _unpack_bf16_pairs(p_u32, dtype=BF16):
    lo = lax.bitcast_convert_type(p_u32 << 16, F32).astype(dtype)
    hi = lax.bitcast_convert_type(p_u32 & jnp.uint32(0xFFFF0000), F32).astype(dtype)
    return jnp.concatenate([lo, hi], axis=1)


def _stage_weight_bf16(src_hbm, dst_ref, stage_ref, sem):
    chunk = stage_ref.shape[2]
    n_chunks = src_hbm.shape[1] // chunk

    def cols(j):
        return pl.ds(pl.multiple_of(j * chunk, chunk), chunk)

    def copy(j, slot):
        return pltpu.make_async_copy(src_hbm.at[:, cols(j)], stage_ref.at[slot], sem.at[slot])

    copy(0, 0).start()

    def body(j, carry):
        slot = j % 2

        @pl.when(j + 1 < n_chunks)
        def _next():
            copy(j + 1, 1 - slot).start()

        copy(j, slot).wait()
        dst_ref[:, cols(j)] = stage_ref[slot].astype(BF16)
        return carry

    lax.fori_loop(0, n_chunks, body, 0)


def _mixer_kernel(x_ref, modn_ref, modp_ref, g1_ref, ca_ref, cb_ref, bcb_ref, lng_ref, lnb_ref, g2_ref,
                  wrh_ref, wrl_ref, br_ref, win_hbm, wa_hbm, wb_hbm, wo_hbm,
                  x1_ref, h2p_ref, route_ref, cnt_ref,
                  wu_ref, wcx_ref, ba_ref, sga_ref, sgb_ref, xs_ref, cva_ref, cvb_ref, run_ref, logit_ref,
                  phase_ref, win_ref, wa_ref, wb_ref, wo_ref, stage_ref, stage_sem,
                  *, n_tiles):
    i = pl.program_id(0)
    t_rows, d = x_ref.shape

    @pl.when(i == 0)
    def _init():
        for ref in (wu_ref, wcx_ref, ba_ref, sga_ref, sgb_ref, xs_ref, run_ref, logit_ref):
            ref[...] = jnp.zeros(ref.shape, ref.dtype)
        for src, dst in ((win_hbm, win_ref), (wa_hbm, wa_ref), (wb_hbm, wb_ref), (wo_hbm, wo_ref)):
            _stage_weight_bf16(src, dst, stage_ref, stage_sem)

    x = x_ref[...]
    sh1 = modn_ref[0, 0:1, :]
    sc1 = modn_ref[0, 1:2, :]
    h = (x * lax.rsqrt(jnp.mean(x * x, axis=-1, keepdims=True) + RMS_EPS)) * g1_ref[...]
    hb = (h * (1.0 + sc1) + sh1).astype(BF16)

    def proj(g, after):
        lhs = hb
        if after is not None:
            bits = pltpu.bitcast(hb, U32)
            lhs = pltpu.bitcast(bits | jnp.broadcast_to(after[0:1, 0:1], bits.shape), BF16)
        return _dot(lhs, win_ref[:, g * d:(g + 1) * d])

    def conv_work(rows):
        items = []
        for cb in range(d // LANES):
            items.append(functools.partial(_depthwise_lane_block, wcx_ref, ca_ref, cva_ref,
                                           phase_ref.at[2], K_SHORT, HALO_A - K_SHORT // 2, rows, cb))
            items.append(functools.partial(_depthwise_lane_block, wu_ref, cb_ref, cvb_ref,
                                           phase_ref.at[cb % 2], K_CONF, HALO_B - K_CONF // 2, rows, cb))
        return items

    early = conv_work((0, t_rows - CONV_ROWS))
    n_proj = 7
    z = []
    conv_done = None
    for g in range(n_proj):
        z.append(proj(g, conv_done))
        released = _zero_bits_of(z[g - PAIR_LAG][0:1, 0:LANES]) if g >= PAIR_LAG else None
        for item in early[g * len(early) // n_proj:(g + 1) * len(early) // n_proj]:
            conv_done = item(after=released)
        if g == ROUTE_AFTER_GROUP:
            route_ref[...] = _route(logit_ref[...], run_ref, (i > 1).astype(F32))
            cnt_ref[...] = jnp.broadcast_to(run_ref[0:1, :], cnt_ref.shape)
    b_a = z[0]
    cx = z[1] * z[2]
    u = z[3] * _sigmoid(z[4])
    sg_a = _sigmoid(z[5])
    sg_b = _sigmoid(z[6])

    same_seq = ((i % n_tiles) != 0).astype(F32)
    wu_ref[pl.ds(HALO_B + t_rows, HALO_B), :] = u[0:HALO_B] * same_seq
    wcx_ref[pl.ds(HALO_A + t_rows, HALO_A), :] = cx[0:HALO_A] * same_seq

    for item in conv_work((t_rows - CONV_ROWS, t_rows)):
        item()
    y_a = _dot((ba_ref[...] * cva_ref[...]).astype(BF16), wa_ref[...])
    v = cvb_ref[...] + bcb_ref[...]
    mu = jnp.mean(v, axis=-1, keepdims=True)
    vc = v - mu
    var = jnp.mean(vc * vc, axis=-1, keepdims=True)
    v = (vc * lax.rsqrt(var + LN_EPS)) * lng_ref[...] + lnb_ref[...]
    y_b = _dot((v * _sigmoid(v)).astype(BF16), wb_ref[...])
    merged = sga_ref[...] * y_a + sgb_ref[...] * y_b
    mix = _dot(merged.astype(BF16), wo_ref[...])

    gt1 = modp_ref[0, 2:3, :]
    sh2 = modp_ref[0, 3:4, :]
    sc2 = modp_ref[0, 4:5, :]
    x1 = xs_ref[...] + gt1 * mix
    h2 = (x1 * lax.rsqrt(jnp.mean(x1 * x1, axis=-1, keepdims=True) + RMS_EPS)) * g2_ref[...]
    h2 = h2 * (1.0 + sc2) + sh2
    h2_hi, h2_lo = _split_bf16(h2)
    h2p = _pack_bf16_pairs(h2_hi)
    hi_part = _dot(h2_hi, wrl_ref[...])
    logit_ref[...] = (hi_part[:, :ROUTE_LANES] + _dot(h2_lo, wrh_ref[...]) + hi_part[:, ROUTE_LANES:]
                      + br_ref[...])

    @pl.when(i < pl.num_programs(0) - 1)
    def _emit_rows():
        x1_ref[...] = x1
        h2p_ref[...] = h2p

    tail_u = wu_ref[pl.ds(t_rows, HALO_B), :]
    wu_ref[pl.ds(0, HALO_B), :] = tail_u * same_seq
    wu_ref[pl.ds(HALO_B, t_rows), :] = u
    tail_cx = wcx_ref[pl.ds(t_rows, HALO_A), :]
    wcx_ref[pl.ds(0, HALO_A), :] = tail_cx * same_seq
    wcx_ref[pl.ds(HALO_A, t_rows), :] = cx
    ba_ref[...] = b_a
    sga_ref[...] = sg_a
    sgb_ref[...] = sg_b
    xs_ref[...] = x


def _mixer_call(x2d, mod3, g1, conv_a, conv_b, b_conv_b, ln_g, ln_b, g2,
                w_in, w_a, w_b, w_o, wr_hi, wr_lo, b_r, *, seq_len):
    n_tok, d = x2d.shape
    t = T_MIX
    n_tiles = seq_len // t
    n_chunks = n_tok // t
    last = n_chunks - 1

    def cur(i):
        return jnp.minimum(i, last)

    def prev(i):
        return jnp.clip(i - 1, 0, last)

    def prev2(i):
        return jnp.clip(i - 2, 0, last)

    row_spec_prev = lambda width: pl.BlockSpec((t, width), lambda i: (prev(i), 0))
    vec = lambda a: _const_spec(a.shape)
    hbm = pl.BlockSpec(memory_space=pl.ANY)
    in_specs = [
        pl.BlockSpec((t, d), lambda i: (cur(i), 0)),
        pl.BlockSpec((1,) + mod3.shape[1:], lambda i: (cur(i) // n_tiles, 0, 0)),
        pl.BlockSpec((1,) + mod3.shape[1:], lambda i: (prev(i) // n_tiles, 0, 0)),
        vec(g1), vec(conv_a), vec(conv_b), vec(b_conv_b), vec(ln_g), vec(ln_b), vec(g2),
        vec(wr_hi), vec(wr_lo), vec(b_r),
        hbm, hbm, hbm, hbm,
    ]
    out_specs = [row_spec_prev(d), row_spec_prev(d // 2),
                 pl.BlockSpec((t, ROUTE_LANES), lambda i: (prev2(i), 0)),
                 pl.BlockSpec((SUBLANES, ROUTE_LANES), lambda i: (0, 0))]
    out_shape = [
        jax.ShapeDtypeStruct((n_tok, d), F32),
        jax.ShapeDtypeStruct((n_tok, d // 2), U32),
        jax.ShapeDtypeStruct((n_tok, ROUTE_LANES), F32),
        jax.ShapeDtypeStruct((SUBLANES, ROUTE_LANES), F32),
    ]
    scratch = [
        pltpu.VMEM((t + 2 * HALO_B, d), F32),
        pltpu.VMEM((t + 2 * HALO_A, d), F32),
        pltpu.VMEM((t, d), F32), pltpu.VMEM((t, d), F32), pltpu.VMEM((t, d), F32), pltpu.VMEM((t, d), F32),
        pltpu.VMEM((t, d), F32), pltpu.VMEM((t, d), F32),
        pltpu.VMEM((SUBLANES, ROUTE_LANES), F32),
        pltpu.VMEM((t, ROUTE_LANES), F32),
        pltpu.VMEM((3, SUBLANES, t + 2 * HALO_B, LANES), F32),
        pltpu.VMEM(w_in.shape, BF16), pltpu.VMEM(w_a.shape, BF16), pltpu.VMEM(w_b.shape, BF16),
        pltpu.VMEM(w_o.shape, BF16),
        pltpu.VMEM((2, d, W_STAGE_COLS), F32), pltpu.SemaphoreType.DMA((2,)),
    ]
    return pl.pallas_call(
        functools.partial(_mixer_kernel, n_tiles=n_tiles),
        grid=(n_chunks + 2,),
        in_specs=in_specs,
        out_specs=out_specs,
        out_shape=out_shape,
        scratch_shapes=scratch,
        compiler_params=pltpu.CompilerParams(
            dimension_semantics=("arbitrary",), vmem_limit_bytes=VMEM_LIMIT_BYTES),
        name="mixer_router",
    )(x2d, mod3, mod3, g1, conv_a, conv_b, b_conv_b, ln_g, ln_b, g2,
      wr_hi, wr_lo, b_r, w_in, w_a, w_b, w_o)


def _segment_layout(counts_row, n_blocks):
    counts = counts_row[:N_EXPERTS].astype(jnp.int32)
    nblk = (counts + ROW_BLK - 1) // ROW_BLK
    blk_end = jnp.cumsum(nblk)
    blk0 = blk_end - nblk
    n_used = blk_end[-1:].astype(jnp.int32)
    pstart_row = jnp.zeros((1, ROUTE_LANES), F32).at[0, :N_EXPERTS].set((blk0 * ROW_BLK).astype(F32))
    blk = jnp.arange(n_blocks, dtype=jnp.int32)[:, None]
    blk_e = jnp.minimum(jnp.sum(blk_end[None, :] <= blk, axis=1), N_EXPERTS - 1).astype(jnp.int32)
    seg_end = jnp.min(jnp.where(blk_end[None, :] > blk, blk_end[None, :], n_blocks), axis=1)
    next_e = jnp.sum(blk_end[None, :] <= seg_end[:, None], axis=1).astype(jnp.int32)
    next_e = jnp.where(seg_end >= n_used[0], -1, next_e)
    last_blk = jnp.where(nblk > 0, blk_end - 1, -1).astype(jnp.int32)
    return pstart_row, blk_e, next_e, last_blk, n_used


def _plan_kernel(route_ref, pstart_ref, pos_ref):
    rec = route_ref[...]
    lane = _lane_ids(rec.shape)
    pos = jnp.zeros(rec.shape, F32)
    for k in range(TOP_K):
        e = rec[:, REC_E + k:REC_E + k + 1]
        seg = jnp.sum(jnp.where(lane == e, pstart_ref[...], 0.0), axis=-1, keepdims=True)
        pos = jnp.where(lane == k, seg + rec[:, REC_RANK + k:REC_RANK + k + 1], pos)
    pos_ref[...] = pos.astype(jnp.int32)


def _plan_call(route, pstart_row):
    n_tok = route.shape[0]
    t = T_PLAN
    return pl.pallas_call(
        _plan_kernel,
        grid=(n_tok // t,),
        in_specs=[pl.BlockSpec((t, ROUTE_LANES), lambda i: (i, 0)),
                  pl.BlockSpec((1, ROUTE_LANES), lambda i: (0, 0))],
        out_specs=pl.BlockSpec((t, ROUTE_LANES), lambda i: (i, 0)),
        out_shape=jax.ShapeDtypeStruct((n_tok, ROUTE_LANES), jnp.int32),
        compiler_params=pltpu.CompilerParams(dimension_semantics=("arbitrary",)),
        name="moe_plan",
    )(route, pstart_row)


def _scatter_kernel(last_blk_ref, n_used_ref, pos_ref, rows_ref, buf_hbm, zeros_ref, sem, zero_sem):
    t_rows = rows_ref.shape[0]
    n_blocks = buf_hbm.shape[0] // ROW_BLK

    @pl.when(pl.program_id(0) == 0)
    def _define_padding():
        zeros_ref[...] = jnp.zeros(zeros_ref.shape, zeros_ref.dtype)

        def block_copy(b):
            dst = buf_hbm.at[pl.ds(pl.multiple_of(b * ROW_BLK, ROW_BLK), ROW_BLK), :]
            return pltpu.make_async_copy(zeros_ref, dst, zero_sem)

        def for_padded_blocks(action):
            def per_expert(e, carry):
                @pl.when(last_blk_ref[e] >= 0)
                def _():
                    action(last_blk_ref[e])
                return carry

            def per_unused(b, carry):
                action(b)
                return carry

            lax.fori_loop(0, last_blk_ref.shape[0], per_expert, 0)
            lax.fori_loop(n_used_ref[0], n_blocks, per_unused, 0)

        for_padded_blocks(lambda b: block_copy(b).start())
        for_padded_blocks(lambda b: block_copy(b).wait())

    for r in range(t_rows):
        for k in range(TOP_K):
            pltpu.make_async_copy(rows_ref.at[pl.ds(r, 1), :],
                                  buf_hbm.at[pl.ds(pos_ref[0, k, r], 1), :], sem).start(priority=k)
    for _ in range(TOP_K):
        pltpu.make_async_copy(rows_ref, buf_hbm.at[pl.ds(0, t_rows), :], sem).wait()


def _scatter_call(pos3, rows, last_blk, n_used, n_rows):
    n_tok, width = rows.shape
    t = T_MOVE
    grid_spec = pltpu.PrefetchScalarGridSpec(
        num_scalar_prefetch=2,
        grid=(n_tok // t,),
        in_specs=[
            pl.BlockSpec((1, TOP_K, t), lambda i, *_: (i, 0, 0), memory_space=pltpu.SMEM),
            pl.BlockSpec((t, width), lambda i, *_: (i, 0)),
        ],
        out_specs=pl.BlockSpec(memory_space=pl.ANY),
        scratch_shapes=[pltpu.VMEM((ROW_BLK, width), rows.dtype),
                        pltpu.SemaphoreType.DMA(()), pltpu.SemaphoreType.DMA(())],
    )
    return pl.pallas_call(
        _scatter_kernel,
        grid_spec=grid_spec,
        out_shape=jax.ShapeDtypeStruct((n_rows, width), rows.dtype),
        compiler_params=pltpu.CompilerParams(dimension_semantics=("arbitrary",)),
        name="moe_scatter",
    )(last_blk, n_used, pos3, rows)


def _expert_kernel(blk_e_ref, next_e_ref, n_used_ref, xs_hbm, wg_hbm, wu_hbm, wd_hbm, y_hbm,
                   wg_st, wu_st, wd_st, wg_bf, wu_bf, wd_bf, xbuf, ybuf, w_sem, in_sem, out_sem):
    n_used = n_used_ref[0]

    def rows(b):
        return pl.ds(pl.multiple_of(b * ROW_BLK, ROW_BLK), ROW_BLK)

    def load(b, slot):
        return pltpu.make_async_copy(xs_hbm.at[rows(b), :], xbuf.at[slot], in_sem.at[slot])

    def store(b, slot):
        return pltpu.make_async_copy(ybuf.at[slot], y_hbm.at[rows(b), :], out_sem.at[slot])

    def weight_copies(e, wslot):
        return [pltpu.make_async_copy(src.at[e], dst.at[wslot], w_sem.at[wslot, n])
                for n, (src, dst) in enumerate(((wg_hbm, wg_st), (wu_hbm, wu_st), (wd_hbm, wd_st)))]

    for copy in weight_copies(blk_e_ref[0], 0):
        copy.start(priority=1)
    load(0, 0).start()

    def body(b, wslot):
        e = blk_e_ref[b]
        new_expert = jnp.logical_or(b == 0, e != blk_e_ref[jnp.maximum(b - 1, 0)])
        slot = b % 2

        @pl.when(new_expert)
        def _switch_expert():
            for copy in weight_copies(e, wslot):
                copy.wait()
            wg_bf[...] = wg_st[wslot].astype(BF16)
            wu_bf[...] = wu_st[wslot].astype(BF16)
            wd_bf[...] = wd_st[wslot].astype(BF16)
            nxt = next_e_ref[b]

            @pl.when(nxt >= 0)
            def _prefetch_weights():
                for copy in weight_copies(nxt, 1 - wslot):
                    copy.start(priority=1)

        @pl.when(b + 1 < n_used)
        def _prefetch_rows():
            load(b + 1, 1 - slot).start()

        load(b, slot).wait()

        @pl.when(b >= 2)
        def _window_free():
            store(b - 2, slot).wait()

        xb = _unpack_bf16_pairs(xbuf[slot])
        a = _dot(xb, wg_bf[...])
        u = _dot(xb, wu_bf[...])
        hid = ((a * _sigmoid(a)) * u).astype(BF16)
        ybuf[slot] = _pack_bf16_pairs(_dot(hid, wd_bf[...]).astype(BF16))
        store(b, slot).start()
        return jnp.where(new_expert, 1 - wslot, wslot)

    lax.fori_loop(0, n_used, body, jnp.int32(0))

    @pl.when(n_used >= 2)
    def _drain_older():
        store(n_used - 2, n_used % 2).wait()

    store(n_used - 1, (n_used - 1) % 2).wait()

    ybuf[0] = jnp.zeros(ybuf.shape[1:], ybuf.dtype)

    def fill(b, carry):
        copy = pltpu.make_async_copy(ybuf.at[0], y_hbm.at[rows(b), :], out_sem.at[0])
        copy.start()
        copy.wait()
        return carry

    lax.fori_loop(n_used, y_hbm.shape[0] // ROW_BLK, fill, 0)


def _expert_call(xs, blk_e, next_e, n_used, w_gate, w_up, w_down):
    n_rows, half = xs.shape
    d = 2 * half
    d_e = w_gate.shape[-1]
    hbm = pl.BlockSpec(memory_space=pl.ANY)
    grid_spec = pltpu.PrefetchScalarGridSpec(
        num_scalar_prefetch=3,
        grid=(1,),
        in_specs=[hbm, hbm, hbm, hbm],
        out_specs=hbm,
        scratch_shapes=[
            pltpu.VMEM((2, d, d_e), F32), pltpu.VMEM((2, d, d_e), F32), pltpu.VMEM((2, d_e, d), F32),
            pltpu.VMEM((d, d_e), BF16), pltpu.VMEM((d, d_e), BF16), pltpu.VMEM((d_e, d), BF16),
            pltpu.VMEM((2, ROW_BLK, half), xs.dtype), pltpu.VMEM((2, ROW_BLK, half), xs.dtype),
            pltpu.SemaphoreType.DMA((2, 3)), pltpu.SemaphoreType.DMA((2,)), pltpu.SemaphoreType.DMA((2,)),
        ],
    )
    return pl.pallas_call(
        _expert_kernel,
        grid_spec=grid_spec,
        out_shape=jax.ShapeDtypeStruct((n_rows, half), xs.dtype),
        compiler_params=pltpu.CompilerParams(
            dimension_semantics=("arbitrary",), vmem_limit_bytes=VMEM_LIMIT_BYTES),
        name="moe_experts",
    )(blk_e, next_e, n_used, xs, w_gate, w_up, w_down)


def _row_gather_wait(src_hbm, dst, sem, n_rows):
    pltpu.make_async_copy(src_hbm.at[pl.ds(0, n_rows), :], dst, sem).wait()


def _combine_kernel(posc_ref, posn_ref, x1_ref, route_ref, mod_ref, gf_ref, y_hbm, o_ref, ybuf, sem,
                    *, n_steps, final_norm):
    i = pl.program_id(0)
    slot = i % 2
    t_rows = x1_ref.shape[0]

    def start(pos_ref, s):
        for k in range(TOP_K):
            for r in range(t_rows):
                pltpu.make_async_copy(y_hbm.at[pl.ds(pos_ref[0, k, r], 1), :],
                                      ybuf.at[s, k, pl.ds(r, 1), :], sem.at[s]).start(priority=r % 2)

    @pl.when(i == 0)
    def _prime():
        start(posc_ref, 0)

    @pl.when(i + 1 < n_steps)
    def _prefetch():
        start(posn_ref, 1 - slot)

    for k in range(TOP_K):
        _row_gather_wait(y_hbm, ybuf.at[slot, k], sem.at[slot], t_rows)
    rec = route_ref[...]
    y0 = _unpack_bf16_pairs(ybuf[slot, 0], F32)
    y1 = _unpack_bf16_pairs(ybuf[slot, 1], F32)
    moe = y0 * rec[:, REC_W:REC_W + 1] + y1 * rec[:, REC_W + 1:REC_W + 2]
    x2 = x1_ref[...] + mod_ref[0, 5:6, :] * moe
    if final_norm:
        x2 = (x2 * lax.rsqrt(jnp.mean(x2 * x2, axis=-1, keepdims=True) + RMS_EPS)) * gf_ref[...]
    o_ref[...] = x2


def _combine_call(x1, route, mod3, g_final, y_rows, pos3, *, seq_len, final_norm):
    n_tok, d = x1.shape
    t = T_MOVE
    n_steps = n_tok // t
    tiles_per_seq = seq_len // t
    last = n_steps - 1
    return pl.pallas_call(
        functools.partial(_combine_kernel, n_steps=n_steps, final_norm=final_norm),
        grid=(n_steps,),
        in_specs=[
            pl.BlockSpec((1, TOP_K, t), lambda i: (i, 0, 0), memory_space=pltpu.SMEM),
            pl.BlockSpec((1, TOP_K, t), lambda i: (jnp.minimum(i + 1, last), 0, 0), memory_space=pltpu.SMEM),
            pl.BlockSpec((t, d), lambda i: (i, 0)),
            pl.BlockSpec((t, ROUTE_LANES), lambda i: (i, 0)),
            pl.BlockSpec((1,) + mod3.shape[1:], lambda i: (i // tiles_per_seq, 0, 0)),
            pl.BlockSpec((1, d), lambda i: (0, 0)),
            pl.BlockSpec(memory_space=pl.ANY),
        ],
        out_specs=pl.BlockSpec((t, d), lambda i: (i, 0)),
        out_shape=jax.ShapeDtypeStruct((n_tok, d), F32),
        scratch_shapes=[pltpu.VMEM((2, TOP_K, t, y_rows.shape[1]), y_rows.dtype),
                        pltpu.SemaphoreType.DMA((2,))],
        compiler_params=pltpu.CompilerParams(
            dimension_semantics=("arbitrary",), vmem_limit_bytes=VMEM_LIMIT_BYTES),
        name="moe_combine",
    )(pos3, pos3, x1, route, mod3, g_final, y_rows)


def kernel(x, c, w_ada, b_ada, g_norm1, w_in, conv_a, w_a_out, conv_b, b_conv_b, ln_conv_g, ln_conv_b,
           w_b_out, w_o, g_norm2, w_router_g, b_router_g, w_router_e, b_router_e, w_gate, w_up, w_down,
           g_final):
    bsz, seq_len, d = x.shape
    depth = w_ada.shape[0]
    n_tok = bsz * seq_len
    n_assign = n_tok * TOP_K
    assert seq_len % T_MIX == 0 and seq_len % T_MOVE == 0 and d % (2 * LANES) == 0
    assert n_tok % T_PLAN == 0
    assert n_assign % ROW_BLK == 0
    assert N_GROUPS + N_EXPERTS <= ROUTE_LANES
    n_rows = n_assign + N_EXPERTS * ROW_BLK

    c_pad = jnp.zeros((SUBLANES, d), F32).at[:bsz].set(c)
    xt = x.reshape(n_tok, d)
    row = lambda a: a.reshape(1, -1)
    for l in range(depth):
        mod = _mod_call(c_pad, w_ada[l], row(b_ada[l]))
        mod3 = mod[:bsz].reshape(bsz, 6, d)
        w_r = jnp.zeros((d, ROUTE_LANES), F32)
        w_r = w_r.at[:, :N_GROUPS].set(w_router_g[l]).at[:, N_GROUPS:N_GROUPS + N_EXPERTS].set(w_router_e[l])
        b_r = jnp.zeros((1, ROUTE_LANES), F32)
        b_r = b_r.at[0, :N_GROUPS].set(b_router_g[l]).at[0, N_GROUPS:N_GROUPS + N_EXPERTS].set(b_router_e[l])
        wr_hi, wr_lo = _split_bf16(w_r)
        wr_both = jnp.concatenate([wr_hi, wr_lo], axis=1)
        x1, h2p, route, cnt = _mixer_call(
            xt, mod3, row(g_norm1[l]), conv_a[l], conv_b[l], row(b_conv_b[l]), row(ln_conv_g[l]),
            row(ln_conv_b[l]), row(g_norm2[l]),
            w_in[l], w_a_out[l], w_b_out[l], w_o[l], wr_hi, wr_both, b_r, seq_len=seq_len)
        pstart_row, blk_e, next_e, last_blk, n_used = _segment_layout(cnt[0], n_rows // ROW_BLK)
        pos = _plan_call(route, pstart_row)
        pos3 = pos[:, :TOP_K].reshape(n_tok // T_MOVE, T_MOVE, TOP_K).transpose(0, 2, 1)
        xs = _scatter_call(pos3, h2p, last_blk, n_used, n_rows)
        y_rows = _expert_call(xs, blk_e, next_e, n_used, w_gate[l], w_up[l], w_down[l])
        xt = _combine_call(x1, route, mod3, row(g_final), y_rows, pos3, seq_len=seq_len,
                           final_norm=(l == depth - 1))
    return xt.reshape(bsz, seq_len, d)
```

```python
import functools

import jax
import jax.numpy as jnp
from jax import lax
from jax.experimental import pallas as pl
from jax.experimental.pallas import tpu as pltpu

F32 = jnp.float32
BF16 = jnp.bfloat16
U32 = jnp.uint32

N_GROUPS = 4
EXPERTS_PER_GROUP = 8
N_EXPERTS = N_GROUPS * EXPERTS_PER_GROUP
TOP_K = 2
K_SHORT = 3
K_CONF = 31
RMS_EPS = 1e-6
LN_EPS = 1e-5

LANES = 128
SUBLANES = 8
T_MIX = 256
HALO_B = 16
HALO_A = 8
CONV_ROWS = 64
CONV_BLOCKS_LIVE = 3
PAIR_LAG = 1
ROUTE_AFTER_GROUP = 1
ROW_BLK = 512
T_SCATTER = 1024
T_COMBINE = 512
T_PLAN = 2048
W_STAGE_COLS = 512
ROUTE_LANES = LANES
REC_E, REC_W, REC_RANK = 0, 2, 4
VMEM_LIMIT_BYTES = 56 * 1024 * 1024


def _sigmoid(v):
    return 1.0 / (1.0 + jnp.exp(-v))


def _split_bf16(v):
    hi = v.astype(BF16)
    lo = (v - hi.astype(F32)).astype(BF16)
    return hi, lo


def _dot(a, b):
    return jnp.dot(a, b, preferred_element_type=F32)


def _const_spec(shape):
    nd = len(shape)
    return pl.BlockSpec(shape, lambda *_: (0,) * nd, pipeline_mode=pl.Buffered(1))


def _lane_ids(shape):
    return lax.broadcasted_iota(jnp.int32, shape, 1).astype(F32)


def _mod_kernel(c_ref, w_ref, b_ref, o_ref):
    c = c_ref[...]
    a_hi, a_lo = _split_bf16(c * _sigmoid(c))
    w_hi, w_lo = _split_bf16(w_ref[...])
    o_ref[...] = _dot(a_hi, w_hi) + _dot(a_lo, w_hi) + _dot(a_hi, w_lo) + b_ref[...]


def _mod_call(c_pad, w_ada, b_ada):
    rows, d = c_pad.shape
    n_out = w_ada.shape[1]
    blk = 1024
    return pl.pallas_call(
        _mod_kernel,
        grid=(n_out // blk,),
        in_specs=[
            pl.BlockSpec((rows, d), lambda j: (0, 0)),
            pl.BlockSpec((d, blk), lambda j: (0, j)),
            pl.BlockSpec((1, blk), lambda j: (0, j)),
        ],
        out_specs=pl.BlockSpec((rows, blk), lambda j: (0, j)),
        out_shape=jax.ShapeDtypeStruct((rows, n_out), F32),
        compiler_params=pltpu.CompilerParams(dimension_semantics=("arbitrary",)),
        name="adaln_mod",
    )(c_pad, w_ada, b_ada)


def _zero_bits_of(v):
    return (lax.bitcast_convert_type(v, U32) >> 16) >> 16


def _depthwise_lane_block(buf_ref, w_ref, out_ref, phase_ref, n_taps, row0, rows, cb, after=None):
    first, stop = rows
    max_off = row0 + n_taps - 1
    n_keep = stop - first + (max_off // SUBLANES) * SUBLANES
    cols = slice(cb * LANES, (cb + 1) * LANES)
    for s in sorted({(row0 + k) % SUBLANES for k in range(n_taps)}):
        phase_ref[s, pl.ds(0, n_keep), :] = buf_ref[pl.ds(first + s, n_keep), cols]
    all_blocks = list(range(first, stop, CONV_ROWS))
    for g0 in range(0, len(all_blocks), CONV_BLOCKS_LIVE):
        row_blocks = all_blocks[g0:g0 + CONV_BLOCKS_LIVE]
        acc = [None] * len(row_blocks)
        for k in range(n_taps):
            q, s = divmod(row0 + k, SUBLANES)
            w_row = w_ref[k:k + 1, cols]
            if after is not None:
                w_row = lax.bitcast_convert_type(lax.bitcast_convert_type(w_row, U32) | after, F32)
            w_k = jnp.broadcast_to(w_row, (CONV_ROWS, LANES))
            for j, r0 in enumerate(row_blocks):
                term = phase_ref[s, pl.ds(r0 - first + q * SUBLANES, CONV_ROWS), :] * w_k
                acc[j] = term if acc[j] is None else acc[j] + term
        for j, r0 in enumerate(row_blocks):
            out_ref[pl.ds(r0, CONV_ROWS), cols] = acc[j]
    return _zero_bits_of(acc[-1][0:1, :])


def _route(logits, run_ref, live):
    shape = logits.shape
    lane = _lane_ids(shape)
    big = float(4 * ROUTE_LANES)
    neg_inf = -jnp.inf

    def first_argmax(v):
        m = jnp.max(v, axis=-1, keepdims=True)
        return m, jnp.min(jnp.where(v == m, lane, big), axis=-1, keepdims=True)

    is_group = lane < N_GROUPS
    g_max, g_idx = first_argmax(jnp.where(is_group, logits, neg_inf))
    g_prob = 1.0 / jnp.sum(jnp.where(is_group, jnp.exp(logits - g_max), 0.0), axis=-1, keepdims=True)
    lo = N_GROUPS + EXPERTS_PER_GROUP * g_idx
    in_group = (lane >= lo) & (lane < lo + EXPERTS_PER_GROUP)
    e_logits = jnp.where(in_group, logits, neg_inf)
    m1, i1 = first_argmax(e_logits)
    m2, i2 = first_argmax(jnp.where(lane == i1, neg_inf, e_logits))
    t = jnp.exp(m2 - m1)
    p1 = 1.0 / (1.0 + t)
    p2 = t / (1.0 + t)
    e1 = i1 - N_GROUPS
    e2 = i2 - N_GROUPS

    hot1 = lane == e1
    hot2 = lane == e2
    both = jnp.where(hot1 | hot2, 1.0, 0.0)
    n_rows = shape[0]
    earlier = (lax.broadcasted_iota(jnp.int32, (n_rows, n_rows), 0)
               > lax.broadcasted_iota(jnp.int32, (n_rows, n_rows), 1))
    before = _dot(jnp.where(earlier, 1.0, 0.0).astype(BF16), both.astype(BF16)) + run_ref[0:1, :]
    rank1 = jnp.sum(jnp.where(hot1, before, 0.0), axis=-1, keepdims=True)
    rank2 = jnp.sum(jnp.where(hot2, before, 0.0), axis=-1, keepdims=True)
    run_ref[0:1, :] = run_ref[0:1, :] + live * jnp.sum(both, axis=0, keepdims=True)

    rec = jnp.where(lane == REC_E, e1, 0.0)
    rec = jnp.where(lane == REC_E + 1, e2, rec)
    rec = jnp.where(lane == REC_W, g_prob * p1, rec)
    rec = jnp.where(lane == REC_W + 1, g_prob * p2, rec)
    rec = jnp.where(lane == REC_RANK, rank1, rec)
    rec = jnp.where(lane == REC_RANK + 1, rank2, rec)
    return rec


def _pack_bf16_pairs(v_bf16):
    half = v_bf16.shape[1] // 2
    lo = lax.bitcast_convert_type(v_bf16[:, :half].astype(F32), U32)
    hi = lax.bitcast_convert_type(v_bf16[:, half:].astype(F32), U32)
    return (lo >> 16) | hi


def _unpack_bf16_pairs(p_u32, dtype=BF16):
    lo = lax.bitcast_convert_type(p_u32 << 16, F32).astype(dtype)
    hi = lax.bitcast_convert_type(p_u32 & jnp.uint32(0xFFFF0000), F32).astype(dtype)
    return jnp.concatenate([lo, hi], axis=1)


def _stage_weight_bf16(src_hbm, dst_ref, stage_ref, sem):
    chunk = stage_ref.shape[2]
    n_chunks = src_hbm.shape[1] // chunk

    def cols(j):
        return pl.ds(pl.multiple_of(j * chunk, chunk), chunk)

    def copy(j, slot):
        return pltpu.make_async_copy(src_hbm.at[:, cols(j)], stage_ref.at[slot], sem.at[slot])

    copy(0, 0).start()

    def body(j, carry):
        slot = j % 2

        @pl.when(j + 1 < n_chunks)
        def _next():
            copy(j + 1, 1 - slot).start()

        copy(j, slot).wait()
        dst_ref[:, cols(j)] = stage_ref[slot].astype(BF16)
        return carry

    lax.fori_loop(0, n_chunks, body, 0)


def _mixer_kernel(x_ref, modn_ref, modp_ref, g1_ref, ca_ref, cb_ref, bcb_ref, lng_ref, lnb_ref, g2_ref,
                  wrh_ref, wrl_ref, br_ref, win_hbm, wa_hbm, wb_hbm, wo_hbm,
                  x1_ref, h2p_ref, route_ref, cnt_ref,
                  wu_ref, wcx_ref, ba_ref, sga_ref, sgb_ref, xs_ref, cva_ref, cvb_ref, run_ref, logit_ref,
                  phase_ref, win_ref, wa_ref, wb_ref, wo_ref, stage_ref, stage_sem,
                  *, n_tiles):
    i = pl.program_id(0)
    t_rows, d = x_ref.shape

    @pl.when(i == 0)
    def _init():
        for ref in (wu_ref, wcx_ref, ba_ref, sga_ref, sgb_ref, xs_ref, run_ref, logit_ref):
            ref[...] = jnp.zeros(ref.shape, ref.dtype)
        for src, dst in ((win_hbm, win_ref), (wa_hbm, wa_ref), (wb_hbm, wb_ref), (wo_hbm, wo_ref)):
            _stage_weight_bf16(src, dst, stage_ref, stage_sem)

    x = x_ref[...]
    sh1 = modn_ref[0, 0:1, :]
    sc1 = modn_ref[0, 1:2, :]
    h = (x * lax.rsqrt(jnp.mean(x * x, axis=-1, keepdims=True) + RMS_EPS)) * g1_ref[...]
    hb = (h * (1.0 + sc1) + sh1).astype(BF16)

    def proj(g, after):
        lhs = hb
        if after is not None:
            bits = pltpu.bitcast(hb, U32)
            lhs = pltpu.bitcast(bits | jnp.broadcast_to(after[0:1, 0:1], bits.shape), BF16)
        return _dot(lhs, win_ref[:, g * d:(g + 1) * d])

    def conv_work(rows):
        items = []
        for cb in range(d // LANES):
            items.append(functools.partial(_depthwise_lane_block, wcx_ref, ca_ref, cva_ref,
                                           phase_ref.at[2], K_SHORT, HALO_A - K_SHORT // 2, rows, cb))
            items.append(functools.partial(_depthwise_lane_block, wu_ref, cb_ref, cvb_ref,
                                           phase_ref.at[cb % 2], K_CONF, HALO_B - K_CONF // 2, rows, cb))
        return items

    early = conv_work((0, t_rows - CONV_ROWS))
    n_proj = 7
    z = []
    conv_done = None
    for g in range(n_proj):
        z.append(proj(g, conv_done))
        released = _zero_bits_of(z[g - PAIR_LAG][0:1, 0:LANES]) if g >= PAIR_LAG else None
        for item in early[g * len(early) // n_proj:(g + 1) * len(early) // n_proj]:
            conv_done = item(after=released)
        if g == ROUTE_AFTER_GROUP:
            route_ref[...] = _route(logit_ref[...], run_ref, (i > 1).astype(F32))
            cnt_ref[...] = jnp.broadcast_to(run_ref[0:1, :], cnt_ref.shape)
    b_a = z[0]
    cx = z[1] * z[2]
    u = z[3] * _sigmoid(z[4])
    sg_a = _sigmoid(z[5])
    sg_b = _sigmoid(z[6])

    same_seq = ((i % n_tiles) != 0).astype(F32)
    wu_ref[pl.ds(HALO_B + t_rows, HALO_B), :] = u[0:HALO_B] * same_seq
    wcx_ref[pl.ds(HALO_A + t_rows, HALO_A), :] = cx[0:HALO_A] * same_seq

    for item in conv_work((t_rows - CONV_ROWS, t_rows)):
        item()
    y_a = _dot((ba_ref[...] * cva_ref[...]).astype(BF16), wa_ref[...])
    v = cvb_ref[...] + bcb_ref[...]
    mu = jnp.mean(v, axis=-1, keepdims=True)
    vc = v - mu
    var = jnp.mean(vc * vc, axis=-1, keepdims=True)
    v = (vc * lax.rsqrt(var + LN_EPS)) * lng_ref[...] + lnb_ref[...]
    y_b = _dot((v * _sigmoid(v)).astype(BF16), wb_ref[...])
    merged = sga_ref[...] * y_a + sgb_ref[...] * y_b
    mix = _dot(merged.astype(BF16), wo_ref[...])

    gt1 = modp_ref[0, 2:3, :]
    sh2 = modp_ref[0, 3:4, :]
    sc2 = modp_ref[0, 4:5, :]
    x1 = xs_ref[...] + gt1 * mix
    h2 = (x1 * lax.rsqrt(jnp.mean(x1 * x1, axis=-1, keepdims=True) + RMS_EPS)) * g2_ref[...]
    h2 = h2 * (1.0 + sc2) + sh2
    h2_hi, h2_lo = _split_bf16(h2)
    h2p = _pack_bf16_pairs(h2_hi)
    hi_part = _dot(h2_hi, wrl_ref[...])
    logit_ref[...] = (hi_part[:, :ROUTE_LANES] + _dot(h2_lo, wrh_ref[...]) + hi_part[:, ROUTE_LANES:]
                      + br_ref[...])

    @pl.when(i < pl.num_programs(0) - 1)
    def _emit_rows():
        x1_ref[...] = x1
        h2p_ref[...] = h2p

    tail_u = wu_ref[pl.ds(t_rows, HALO_B), :]
    wu_ref[pl.ds(0, HALO_B), :] = tail_u * same_seq
    wu_ref[pl.ds(HALO_B, t_rows), :] = u
    tail_cx = wcx_ref[pl.ds(t_rows, HALO_A), :]
    wcx_ref[pl.ds(0, HALO_A), :] = tail_cx * same_seq
    wcx_ref[pl.ds(HALO_A, t_rows), :] = cx
    ba_ref[...] = b_a
    sga_ref[...] = sg_a
    sgb_ref[...] = sg_b
    xs_ref[...] = x


def _mixer_call(x2d, mod3, g1, conv_a, conv_b, b_conv_b, ln_g, ln_b, g2,
                w_in, w_a, w_b, w_o, wr_hi, wr_lo, b_r, *, seq_len):
    n_tok, d = x2d.shape
    t = T_MIX
    n_tiles = seq_len // t
    n_chunks = n_tok // t
    last = n_chunks - 1

    def cur(i):
        return jnp.minimum(i, last)

    def prev(i):
        return jnp.clip(i - 1, 0, last)

    def prev2(i):
        return jnp.clip(i - 2, 0, last)

    row_spec_prev = lambda width: pl.BlockSpec((t, width), lambda i: (prev(i), 0))
    vec = lambda a: _const_spec(a.shape)
    hbm = pl.BlockSpec(memory_space=pl.ANY)
    in_specs = [
        pl.BlockSpec((t, d), lambda i: (cur(i), 0)),
        pl.BlockSpec((1,) + mod3.shape[1:], lambda i: (cur(i) // n_tiles, 0, 0)),
        pl.BlockSpec((1,) + mod3.shape[1:], lambda i: (prev(i) // n_tiles, 0, 0)),
        vec(g1), vec(conv_a), vec(conv_b), vec(b_conv_b), vec(ln_g), vec(ln_b), vec(g2),
        vec(wr_hi), vec(wr_lo), vec(b_r),
        hbm, hbm, hbm, hbm,
    ]
    out_specs = [row_spec_prev(d), row_spec_prev(d // 2),
                 pl.BlockSpec((t, ROUTE_LANES), lambda i: (prev2(i), 0)),
                 pl.BlockSpec((SUBLANES, ROUTE_LANES), lambda i: (0, 0))]
    out_shape = [
        jax.ShapeDtypeStruct((n_tok, d), F32),
        jax.ShapeDtypeStruct((n_tok, d // 2), U32),
        jax.ShapeDtypeStruct((n_tok, ROUTE_LANES), F32),
        jax.ShapeDtypeStruct((SUBLANES, ROUTE_LANES), F32),
    ]
    scratch = [
        pltpu.VMEM((t + 2 * HALO_B, d), F32),
        pltpu.VMEM((t + 2 * HALO_A, d), F32),
        pltpu.VMEM((t, d), F32), pltpu.VMEM((t, d), F32), pltpu.VMEM((t, d), F32), pltpu.VMEM((t, d), F32),
        pltpu.VMEM((t, d), F32), pltpu.VMEM((t, d), F32),
        pltpu.VMEM((SUBLANES, ROUTE_LANES), F32),
        pltpu.VMEM((t, ROUTE_LANES), F32),
        pltpu.VMEM((3, SUBLANES, t + 2 * HALO_B, LANES), F32),
        pltpu.VMEM(w_in.shape, BF16), pltpu.VMEM(w_a.shape, BF16), pltpu.VMEM(w_b.shape, BF16),
        pltpu.VMEM(w_o.shape, BF16),
        pltpu.VMEM((2, d, W_STAGE_COLS), F32), pltpu.SemaphoreType.DMA((2,)),
    ]
    return pl.pallas_call(
        functools.partial(_mixer_kernel, n_tiles=n_tiles),
        grid=(n_chunks + 2,),
        in_specs=in_specs,
        out_specs=out_specs,
        out_shape=out_shape,
        scratch_shapes=scratch,
        compiler_params=pltpu.CompilerParams(
            dimension_semantics=("arbitrary",), vmem_limit_bytes=VMEM_LIMIT_BYTES),
        name="mixer_router",
    )(x2d, mod3, mod3, g1, conv_a, conv_b, b_conv_b, ln_g, ln_b, g2,
      wr_hi, wr_lo, b_r, w_in, w_a, w_b, w_o)


def _segment_layout(counts_row, n_blocks):
    counts = counts_row[:N_EXPERTS].astype(jnp.int32)
    nblk = (counts + ROW_BLK - 1) // ROW_BLK
    blk_end = jnp.cumsum(nblk)
    blk0 = blk_end - nblk
    n_used = blk_end[-1:].astype(jnp.int32)
    pstart_row = jnp.zeros((1, ROUTE_LANES), F32).at[0, :N_EXPERTS].set((blk0 * ROW_BLK).astype(F32))
    blk = jnp.arange(n_blocks, dtype=jnp.int32)[:, None]
    blk_e = jnp.minimum(jnp.sum(blk_end[None, :] <= blk, axis=1), N_EXPERTS - 1).astype(jnp.int32)
    seg_end = jnp.min(jnp.where(blk_end[None, :] > blk, blk_end[None, :], n_blocks), axis=1)
    next_e = jnp.sum(blk_end[None, :] <= seg_end[:, None], axis=1).astype(jnp.int32)
    next_e = jnp.where(seg_end >= n_used[0], -1, next_e)
    last_blk = jnp.where(nblk > 0, blk_end - 1, -1).astype(jnp.int32)
    return pstart_row, blk_e, next_e, last_blk, n_used


def _plan_kernel(route_ref, pstart_ref, pos_ref):
    rec = route_ref[...]
    lane = _lane_ids(rec.shape)
    pos = jnp.zeros(rec.shape, F32)
    for k in range(TOP_K):
        e = rec[:, REC_E + k:REC_E + k + 1]
        seg = jnp.sum(jnp.where(lane == e, pstart_ref[...], 0.0), axis=-1, keepdims=True)
        pos = jnp.where(lane == k, seg + rec[:, REC_RANK + k:REC_RANK + k + 1], pos)
    pos_ref[...] = pos.astype(jnp.int32)


def _plan_call(route, pstart_row):
    n_tok = route.shape[0]
    t = T_PLAN
    return pl.pallas_call(
        _plan_kernel,
        grid=(n_tok // t,),
        in_specs=[pl.BlockSpec((t, ROUTE_LANES), lambda i: (i, 0)),
                  pl.BlockSpec((1, ROUTE_LANES), lambda i: (0, 0))],
        out_specs=pl.BlockSpec((t, ROUTE_LANES), lambda i: (i, 0)),
        out_shape=jax.ShapeDtypeStruct((n_tok, ROUTE_LANES), jnp.int32),
        compiler_params=pltpu.CompilerParams(dimension_semantics=("arbitrary",)),
        name="moe_plan",
    )(route, pstart_row)


def _scatter_kernel(last_blk_ref, n_used_ref, pos_ref, rows_ref, buf_hbm, zeros_ref, sem, zero_sem):
    t_rows = rows_ref.shape[0]
    n_blocks = buf_hbm.shape[0] // ROW_BLK

    @pl.when(pl.program_id(0) == 0)
    def _define_padding():
        zeros_ref[...] = jnp.zeros(zeros_ref.shape, zeros_ref.dtype)

        def block_copy(b):
            dst = buf_hbm.at[pl.ds(pl.multiple_of(b * ROW_BLK, ROW_BLK), ROW_BLK), :]
            return pltpu.make_async_copy(zeros_ref, dst, zero_sem)

        def for_padded_blocks(action):
            def per_expert(e, carry):
                @pl.when(last_blk_ref[e] >= 0)
                def _():
                    action(last_blk_ref[e])
                return carry

            def per_unused(b, carry):
                action(b)
                return carry

            lax.fori_loop(0, last_blk_ref.shape[0], per_expert, 0)
            lax.fori_loop(n_used_ref[0], n_blocks, per_unused, 0)

        for_padded_blocks(lambda b: block_copy(b).start())
        for_padded_blocks(lambda b: block_copy(b).wait())

    for r in range(t_rows):
        for k in range(TOP_K):
            pltpu.make_async_copy(rows_ref.at[pl.ds(r, 1), :],
                                  buf_hbm.at[pl.ds(pos_ref[0, k, r], 1), :], sem).start(priority=k)
    for _ in range(TOP_K):
        pltpu.make_async_copy(rows_ref, buf_hbm.at[pl.ds(0, t_rows), :], sem).wait()


def _scatter_call(pos3, rows, last_blk, n_used, n_rows):
    n_tok, width = rows.shape
    t = T_SCATTER
    grid_spec = pltpu.PrefetchScalarGridSpec(
        num_scalar_prefetch=2,
        grid=(n_tok // t,),
        in_specs=[
            pl.BlockSpec((1, TOP_K, t), lambda i, *_: (i, 0, 0), memory_space=pltpu.SMEM),
            pl.BlockSpec((t, width), lambda i, *_: (i, 0)),
        ],
        out_specs=pl.BlockSpec(memory_space=pl.ANY),
        scratch_shapes=[pltpu.VMEM((ROW_BLK, width), rows.dtype),
                        pltpu.SemaphoreType.DMA(()), pltpu.SemaphoreType.DMA(())],
    )
    return pl.pallas_call(
        _scatter_kernel,
        grid_spec=grid_spec,
        out_shape=jax.ShapeDtypeStruct((n_rows, width), rows.dtype),
        compiler_params=pltpu.CompilerParams(dimension_semantics=("arbitrary",)),
        name="moe_scatter",
    )(last_blk, n_used, pos3, rows)


def _expert_kernel(blk_e_ref, next_e_ref, n_used_ref, xs_hbm, wg_hbm, wu_hbm, wd_hbm, y_hbm,
                   wg_st, wu_st, wd_st, wg_bf, wu_bf, wd_bf, xbuf, ybuf, w_sem, in_sem, out_sem):
    n_used = n_used_ref[0]

    def rows(b):
        return pl.ds(pl.multiple_of(b * ROW_BLK, ROW_BLK), ROW_BLK)

    def load(b, slot):
        return pltpu.make_async_copy(xs_hbm.at[rows(b), :], xbuf.at[slot], in_sem.at[slot])

    def store(b, slot):
        return pltpu.make_async_copy(ybuf.at[slot], y_hbm.at[rows(b), :], out_sem.at[slot])

    def weight_copies(e, wslot):
        return [pltpu.make_async_copy(src.at[e], dst.at[wslot], w_sem.at[wslot, n])
                for n, (src, dst) in enumerate(((wg_hbm, wg_st), (wu_hbm, wu_st), (wd_hbm, wd_st)))]

    for copy in weight_copies(blk_e_ref[0], 0):
        copy.start(priority=1)
    load(0, 0).start()

    def body(b, wslot):
        e = blk_e_ref[b]
        new_expert = jnp.logical_or(b == 0, e != blk_e_ref[jnp.maximum(b - 1, 0)])
        slot = b % 2

        @pl.when(new_expert)
        def _switch_expert():
            for copy in weight_copies(e, wslot):
                copy.wait()
            wg_bf[...] = wg_st[wslot].astype(BF16)
            wu_bf[...] = wu_st[wslot].astype(BF16)
            wd_bf[...] = wd_st[wslot].astype(BF16)
            nxt = next_e_ref[b]

            @pl.when(nxt >= 0)
            def _prefetch_weights():
                for copy in weight_copies(nxt, 1 - wslot):
                    copy.start(priority=1)

        @pl.when(b + 1 < n_used)
        def _prefetch_rows():
            load(b + 1, 1 - slot).start()

        load(b, slot).wait()

        @pl.when(b >= 2)
        def _window_free():
            store(b - 2, slot).wait()

        xb = _unpack_bf16_pairs(xbuf[slot])
        a = _dot(xb, wg_bf[...])
        u = _dot(xb, wu_bf[...])
        hid = ((a * _sigmoid(a)) * u).astype(BF16)
        ybuf[slot] = _pack_bf16_pairs(_dot(hid, wd_bf[...]).astype(BF16))
        store(b, slot).start()
        return jnp.where(new_expert, 1 - wslot, wslot)

    lax.fori_loop(0, n_used, body, jnp.int32(0))

    @pl.when(n_used >= 2)
    def _drain_older():
        store(n_used - 2, n_used % 2).wait()

    store(n_used - 1, (n_used - 1) % 2).wait()

    ybuf[0] = jnp.zeros(ybuf.shape[1:], ybuf.dtype)

    def fill(b, carry):
        copy = pltpu.make_async_copy(ybuf.at[0], y_hbm.at[rows(b), :], out_sem.at[0])
        copy.start()
        copy.wait()
        return carry

    lax.fori_loop(n_used, y_hbm.shape[0] // ROW_BLK, fill, 0)


def _expert_call(xs, blk_e, next_e, n_used, w_gate, w_up, w_down):
    n_rows, half = xs.shape
    d = 2 * half
    d_e = w_gate.shape[-1]
    hbm = pl.BlockSpec(memory_space=pl.ANY)
    grid_spec = pltpu.PrefetchScalarGridSpec(
        num_scalar_prefetch=3,
        grid=(1,),
        in_specs=[hbm, hbm, hbm, hbm],
        out_specs=hbm,
        scratch_shapes=[
            pltpu.VMEM((2, d, d_e), F32), pltpu.VMEM((2, d, d_e), F32), pltpu.VMEM((2, d_e, d), F32),
            pltpu.VMEM((d, d_e), BF16), pltpu.VMEM((d, d_e), BF16), pltpu.VMEM((d_e, d), BF16),
            pltpu.VMEM((2, ROW_BLK, half), xs.dtype), pltpu.VMEM((2, ROW_BLK, half), xs.dtype),
            pltpu.SemaphoreType.DMA((2, 3)), pltpu.SemaphoreType.DMA((2,)), pltpu.SemaphoreType.DMA((2,)),
        ],
    )
    return pl.pallas_call(
        _expert_kernel,
        grid_spec=grid_spec,
        out_shape=jax.ShapeDtypeStruct((n_rows, half), xs.dtype),
        compiler_params=pltpu.CompilerParams(
            dimension_semantics=("arbitrary",), vmem_limit_bytes=VMEM_LIMIT_BYTES),
        name="moe_experts",
    )(blk_e, next_e, n_used, xs, w_gate, w_up, w_down)


def _row_gather_wait(src_hbm, dst, sem, n_rows):
    pltpu.make_async_copy(src_hbm.at[pl.ds(0, n_rows), :], dst, sem).wait()


def _combine_kernel(posc_ref, posn_ref, x1_ref, route_ref, mod_ref, gf_ref, y_hbm, o_ref, ybuf, sem,
                    *, n_steps, final_norm):
    i = pl.program_id(0)
    slot = i % 2
    t_rows = x1_ref.shape[0]

    def start(pos_ref, s):
        for k in range(TOP_K):
            for r in range(t_rows):
                pltpu.make_async_copy(y_hbm.at[pl.ds(pos_ref[0, k, r], 1), :],
                                      ybuf.at[s, k, pl.ds(r, 1), :], sem.at[s]).start(priority=r % 2)

    @pl.when(i == 0)
    def _prime():
        start(posc_ref, 0)

    @pl.when(i + 1 < n_steps)
    def _prefetch():
        start(posn_ref, 1 - slot)

    for k in range(TOP_K):
        _row_gather_wait(y_hbm, ybuf.at[slot, k], sem.at[slot], t_rows)
    rec = route_ref[...]
    y0 = _unpack_bf16_pairs(ybuf[slot, 0], F32)
    y1 = _unpack_bf16_pairs(ybuf[slot, 1], F32)
    moe = y0 * rec[:, REC_W:REC_W + 1] + y1 * rec[:, REC_W + 1:REC_W + 2]
    x2 = x1_ref[...] + mod_ref[0, 5:6, :] * moe
    if final_norm:
        x2 = (x2 * lax.rsqrt(jnp.mean(x2 * x2, axis=-1, keepdims=True) + RMS_EPS)) * gf_ref[...]
    o_ref[...] = x2


def _combine_call(x1, route, mod3, g_final, y_rows, pos3, *, seq_len, final_norm):
    n_tok, d = x1.shape
    t = T_COMBINE
    n_steps = n_tok // t
    tiles_per_seq = seq_len // t
    last = n_steps - 1
    return pl.pallas_call(
        functools.partial(_combine_kernel, n_steps=n_steps, final_norm=final_norm),
        grid=(n_steps,),
        in_specs=[
            pl.BlockSpec((1, TOP_K, t), lambda i: (i, 0, 0), memory_space=pltpu.SMEM),
            pl.BlockSpec((1, TOP_K, t), lambda i: (jnp.minimum(i + 1, last), 0, 0), memory_space=pltpu.SMEM),
            pl.BlockSpec((t, d), lambda i: (i, 0)),
            pl.BlockSpec((t, ROUTE_LANES), lambda i: (i, 0)),
            pl.BlockSpec((1,) + mod3.shape[1:], lambda i: (i // tiles_per_seq, 0, 0)),
            pl.BlockSpec((1, d), lambda i: (0, 0)),
            pl.BlockSpec(memory_space=pl.ANY),
        ],
        out_specs=pl.BlockSpec((t, d), lambda i: (i, 0)),
        out_shape=jax.ShapeDtypeStruct((n_tok, d), F32),
        scratch_shapes=[pltpu.VMEM((2, TOP_K, t, y_rows.shape[1]), y_rows.dtype),
                        pltpu.SemaphoreType.DMA((2,))],
        compiler_params=pltpu.CompilerParams(
            dimension_semantics=("arbitrary",), vmem_limit_bytes=VMEM_LIMIT_BYTES),
        name="moe_combine",
    )(pos3, pos3, x1, route, mod3, g_final, y_rows)


def kernel(x, c, w_ada, b_ada, g_norm1, w_in, conv_a, w_a_out, conv_b, b_conv_b, ln_conv_g, ln_conv_b,
           w_b_out, w_o, g_norm2, w_router_g, b_router_g, w_router_e, b_router_e, w_gate, w_up, w_down,
           g_final):
    bsz, seq_len, d = x.shape
    depth = w_ada.shape[0]
    n_tok = bsz * seq_len
    n_assign = n_tok * TOP_K
    assert seq_len % T_MIX == 0 and seq_len % T_COMBINE == 0 and d % (2 * LANES) == 0
    assert n_tok % T_SCATTER == 0
    assert n_tok % T_PLAN == 0
    assert n_assign % ROW_BLK == 0
    assert N_GROUPS + N_EXPERTS <= ROUTE_LANES
    n_rows = n_assign + N_EXPERTS * ROW_BLK

    c_pad = jnp.zeros((SUBLANES, d), F32).at[:bsz].set(c)
    xt = x.reshape(n_tok, d)
    row = lambda a: a.reshape(1, -1)
    for l in range(depth):
        mod = _mod_call(c_pad, w_ada[l], row(b_ada[l]))
        mod3 = mod[:bsz].reshape(bsz, 6, d)
        w_r = jnp.zeros((d, ROUTE_LANES), F32)
        w_r = w_r.at[:, :N_GROUPS].set(w_router_g[l]).at[:, N_GROUPS:N_GROUPS + N_EXPERTS].set(w_router_e[l])
        b_r = jnp.zeros((1, ROUTE_LANES), F32)
        b_r = b_r.at[0, :N_GROUPS].set(b_router_g[l]).at[0, N_GROUPS:N_GROUPS + N_EXPERTS].set(b_router_e[l])
        wr_hi, wr_lo = _split_bf16(w_r)
        wr_both = jnp.concatenate([wr_hi, wr_lo], axis=1)
        x1, h2p, route, cnt = _mixer_call(
            xt, mod3, row(g_norm1[l]), conv_a[l], conv_b[l], row(b_conv_b[l]), row(ln_conv_g[l]),
            row(ln_conv_b[l]), row(g_norm2[l]),
            w_in[l], w_a_out[l], w_b_out[l], w_o[l], wr_hi, wr_both, b_r, seq_len=seq_len)
        pstart_row, blk_e, next_e, last_blk, n_used = _segment_layout(cnt[0], n_rows // ROW_BLK)
        pos = _plan_call(route, pstart_row)
        pos_tiles = lambda t: pos[:, :TOP_K].reshape(n_tok // t, t, TOP_K).transpose(0, 2, 1)
        xs = _scatter_call(pos_tiles(T_SCATTER), h2p, last_blk, n_used, n_rows)
        y_rows = _expert_call(xs, blk_e, next_e, n_used, w_gate[l], w_up[l], w_down[l])
        xt = _combine_call(x1, route, mod3, row(g_final), y_rows, pos_tiles(T_COMBINE), seq_len=seq_len,
                           final_norm=(l == depth - 1))
    return xt.reshape(bsz, seq_len, d)
```

```python
import functools

import jax
import jax.numpy as jnp
from jax import lax
from jax.experimental import pallas as pl
from jax.experimental.pallas import tpu as pltpu

F32 = jnp.float32
BF16 = jnp.bfloat16
U32 = jnp.uint32

N_GROUPS = 4
EXPERTS_PER_GROUP = 8
N_EXPERTS = N_GROUPS * EXPERTS_PER_GROUP
TOP_K = 2
K_SHORT = 3
K_CONF = 31
RMS_EPS = 1e-6
LN_EPS = 1e-5

LANES = 128
SUBLANES = 8
T_MIX = 256
HALO_B = 16
HALO_A = 8
CONV_ROWS = 64
CONV_BLOCKS_LIVE = 3
PAIR_LAG = 1
ROUTE_AFTER_GROUP = 1
ROW_BLK = 512
T_SCATTER = 1024
T_COMBINE = 512
T_PLAN = 2048
W_STAGE_COLS = 512
ROUTE_LANES = LANES
REC_E, REC_W, REC_RANK = 0, 2, 4
VMEM_LIMIT_BYTES = 56 * 1024 * 1024


def _sigmoid(v):
    return 1.0 / (1.0 + jnp.exp(-v))


def _split_bf16(v):
    hi = v.astype(BF16)
    lo = (v - hi.astype(F32)).astype(BF16)
    return hi, lo


def _dot(a, b):
    return jnp.dot(a, b, preferred_element_type=F32)


def _const_spec(shape):
    nd = len(shape)
    return pl.BlockSpec(shape, lambda *_: (0,) * nd, pipeline_mode=pl.Buffered(1))


def _lane_ids(shape):
    return lax.broadcasted_iota(jnp.int32, shape, 1).astype(F32)


def _mod_kernel(c_ref, w_ref, b_ref, o_ref):
    c = c_ref[...]
    a_hi, a_lo = _split_bf16(c * _sigmoid(c))
    w_hi, w_lo = _split_bf16(w_ref[...])
    o_ref[...] = _dot(a_hi, w_hi) + _dot(a_lo, w_hi) + _dot(a_hi, w_lo) + b_ref[...]


def _mod_call(c_pad, w_ada, b_ada):
    rows, d = c_pad.shape
    n_out = w_ada.shape[1]
    blk = 1024
    return pl.pallas_call(
        _mod_kernel,
        grid=(n_out // blk,),
        in_specs=[
            pl.BlockSpec((rows, d), lambda j: (0, 0)),
            pl.BlockSpec((d, blk), lambda j: (0, j)),
            pl.BlockSpec((1, blk), lambda j: (0, j)),
        ],
        out_specs=pl.BlockSpec((rows, blk), lambda j: (0, j)),
        out_shape=jax.ShapeDtypeStruct((rows, n_out), F32),
        compiler_params=pltpu.CompilerParams(dimension_semantics=("arbitrary",)),
        name="adaln_mod",
    )(c_pad, w_ada, b_ada)


def _zero_bits_of(v):
    return (lax.bitcast_convert_type(v, U32) >> 16) >> 16


def _depthwise_lane_block(buf_ref, w_ref, out_ref, phase_ref, n_taps, row0, rows, cb, after=None):
    first, stop = rows
    max_off = row0 + n_taps - 1
    n_keep = stop - first + (max_off // SUBLANES) * SUBLANES
    cols = slice(cb * LANES, (cb + 1) * LANES)
    for s in sorted({(row0 + k) % SUBLANES for k in range(n_taps)}):
        phase_ref[s, pl.ds(0, n_keep), :] = buf_ref[pl.ds(first + s, n_keep), cols]
    all_blocks = list(range(first, stop, CONV_ROWS))
    for g0 in range(0, len(all_blocks), CONV_BLOCKS_LIVE):
        row_blocks = all_blocks[g0:g0 + CONV_BLOCKS_LIVE]
        acc = [None] * len(row_blocks)
        for k in range(n_taps):
            q, s = divmod(row0 + k, SUBLANES)
            w_row = w_ref[k:k + 1, cols]
            if after is not None:
                w_row = lax.bitcast_convert_type(lax.bitcast_convert_type(w_row, U32) | after, F32)
            w_k = jnp.broadcast_to(w_row, (CONV_ROWS, LANES))
            for j, r0 in enumerate(row_blocks):
                term = phase_ref[s, pl.ds(r0 - first + q * SUBLANES, CONV_ROWS), :] * w_k
                acc[j] = term if acc[j] is None else acc[j] + term
        for j, r0 in enumerate(row_blocks):
            out_ref[pl.ds(r0, CONV_ROWS), cols] = acc[j]
    return _zero_bits_of(acc[-1][0:1, :])


def _route(logits, run_ref, live):
    shape = logits.shape
    lane = _lane_ids(shape)
    big = float(4 * ROUTE_LANES)
    neg_inf = -jnp.inf

    def first_argmax(v):
        m = jnp.max(v, axis=-1, keepdims=True)
        return m, jnp.min(jnp.where(v == m, lane, big), axis=-1, keepdims=True)

    is_group = lane < N_GROUPS
    g_max, g_idx = first_argmax(jnp.where(is_group, logits, neg_inf))
    g_prob = 1.0 / jnp.sum(jnp.where(is_group, jnp.exp(logits - g_max), 0.0), axis=-1, keepdims=True)
    lo = N_GROUPS + EXPERTS_PER_GROUP * g_idx
    in_group = (lane >= lo) & (lane < lo + EXPERTS_PER_GROUP)
    e_logits = jnp.where(in_group, logits, neg_inf)
    m1, i1 = first_argmax(e_logits)
    m2, i2 = first_argmax(jnp.where(lane == i1, neg_inf, e_logits))
    t = jnp.exp(m2 - m1)
    p1 = 1.0 / (1.0 + t)
    p2 = t / (1.0 + t)
    e1 = i1 - N_GROUPS
    e2 = i2 - N_GROUPS

    hot1 = lane == e1
    hot2 = lane == e2
    both = jnp.where(hot1 | hot2, 1.0, 0.0)
    n_rows = shape[0]
    earlier = (lax.broadcasted_iota(jnp.int32, (n_rows, n_rows), 0)
               > lax.broadcasted_iota(jnp.int32, (n_rows, n_rows), 1))
    before = _dot(jnp.where(earlier, 1.0, 0.0).astype(BF16), both.astype(BF16)) + run_ref[0:1, :]
    rank1 = jnp.sum(jnp.where(hot1, before, 0.0), axis=-1, keepdims=True)
    rank2 = jnp.sum(jnp.where(hot2, before, 0.0), axis=-1, keepdims=True)
    run_ref[0:1, :] = run_ref[0:1, :] + live * jnp.sum(both, axis=0, keepdims=True)

    rec = jnp.where(lane == REC_E, e1, 0.0)
    rec = jnp.where(lane == REC_E + 1, e2, rec)
    rec = jnp.where(lane == REC_W, g_prob * p1, rec)
    rec = jnp.where(lane == REC_W + 1, g_prob * p2, rec)
    rec = jnp.where(lane == REC_RANK, rank1, rec)
    rec = jnp.where(lane == REC_RANK + 1, rank2, rec)
    return rec


def _pack_bf16_pairs(v_bf16):
    half = v_bf16.shape[1] // 2
    lo = lax.bitcast_convert_type(v_bf16[:, :half].astype(F32), U32)
    hi = lax.bitcast_convert_type(v_bf16[:, half:].astype(F32), U32)
    return (lo >> 16) | hi


def _unpack_bf16_pairs(p_u32, dtype=BF16):
    lo = lax.bitcast_convert_type(p_u32 << 16, F32).astype(dtype)
    hi = lax.bitcast_convert_type(p_u32 & jnp.uint32(0xFFFF0000), F32).astype(dtype)
    return jnp.concatenate([lo, hi], axis=1)


def _stage_weight_bf16(src_hbm, dst_ref, stage_ref, sem):
    chunk = stage_ref.shape[2]
    n_chunks = src_hbm.shape[1] // chunk

    def cols(j):
        return pl.ds(pl.multiple_of(j * chunk, chunk), chunk)

    def copy(j, slot):
        return pltpu.make_async_copy(src_hbm.at[:, cols(j)], stage_ref.at[slot], sem.at[slot])

    copy(0, 0).start()

    def body(j, carry):
        slot = j % 2

        @pl.when(j + 1 < n_chunks)
        def _next():
            copy(j + 1, 1 - slot).start()

        copy(j, slot).wait()
        dst_ref[:, cols(j)] = stage_ref[slot].astype(BF16)
        return carry

    lax.fori_loop(0, n_chunks, body, 0)


def _mixer_kernel(x_ref, modn_ref, modp_ref, g1_ref, ca_ref, cb_ref, bcb_ref, lng_ref, lnb_ref, g2_ref,
                  wrh_ref, wrl_ref, br_ref, win_hbm, wa_hbm, wb_hbm, wo_hbm,
                  x1_ref, h2p_ref, route_ref, cnt_ref,
                  wu_ref, wcx_ref, ba_ref, sga_ref, sgb_ref, xs_ref, cva_ref, cvb_ref, run_ref, logit_ref,
                  phase_ref, win_ref, wa_ref, wb_ref, wo_ref, stage_ref, stage_sem,
                  *, n_tiles):
    i = pl.program_id(0)
    t_rows, d = x_ref.shape

    @pl.when(i == 0)
    def _init():
        for ref in (wu_ref, wcx_ref, ba_ref, sga_ref, sgb_ref, xs_ref, run_ref, logit_ref):
            ref[...] = jnp.zeros(ref.shape, ref.dtype)
        for src, dst in ((win_hbm, win_ref), (wa_hbm, wa_ref), (wb_hbm, wb_ref), (wo_hbm, wo_ref)):
            _stage_weight_bf16(src, dst, stage_ref, stage_sem)

    x = x_ref[...]
    sh1 = modn_ref[0, 0:1, :]
    sc1 = modn_ref[0, 1:2, :]
    h = (x * lax.rsqrt(jnp.mean(x * x, axis=-1, keepdims=True) + RMS_EPS)) * g1_ref[...]
    hb = (h * (1.0 + sc1) + sh1).astype(BF16)

    def proj(g, after):
        lhs = hb
        if after is not None:
            bits = pltpu.bitcast(hb, U32)
            lhs = pltpu.bitcast(bits | jnp.broadcast_to(after[0:1, 0:1], bits.shape), BF16)
        return _dot(lhs, win_ref[:, g * d:(g + 1) * d])

    def conv_work(rows):
        items = []
        for cb in range(d // LANES):
            items.append(functools.partial(_depthwise_lane_block, wcx_ref, ca_ref, cva_ref,
                                           phase_ref.at[2], K_SHORT, HALO_A - K_SHORT // 2, rows, cb))
            items.append(functools.partial(_depthwise_lane_block, wu_ref, cb_ref, cvb_ref,
                                           phase_ref.at[cb % 2], K_CONF, HALO_B - K_CONF // 2, rows, cb))
        return items

    early = conv_work((0, t_rows - CONV_ROWS))
    n_proj = 7
    z = []
    conv_done = None
    for g in range(n_proj):
        z.append(proj(g, conv_done))
        released = _zero_bits_of(z[g - PAIR_LAG][0:1, 0:LANES]) if g >= PAIR_LAG else None
        for item in early[g * len(early) // n_proj:(g + 1) * len(early) // n_proj]:
            conv_done = item(after=released)
        if g == ROUTE_AFTER_GROUP:
            route_ref[...] = _route(logit_ref[...], run_ref, (i > 1).astype(F32))
            cnt_ref[...] = jnp.broadcast_to(run_ref[0:1, :], cnt_ref.shape)
    b_a = z[0]
    cx = z[1] * z[2]
    u = z[3] * _sigmoid(z[4])
    sg_a = _sigmoid(z[5])
    sg_b = _sigmoid(z[6])

    same_seq = ((i % n_tiles) != 0).astype(F32)
    wu_ref[pl.ds(HALO_B + t_rows, HALO_B), :] = u[0:HALO_B] * same_seq
    wcx_ref[pl.ds(HALO_A + t_rows, HALO_A), :] = cx[0:HALO_A] * same_seq

    for item in conv_work((t_rows - CONV_ROWS, t_rows)):
        item()
    y_a = _dot((ba_ref[...] * cva_ref[...]).astype(BF16), wa_ref[...])
    v = cvb_ref[...] + bcb_ref[...]
    mu = jnp.mean(v, axis=-1, keepdims=True)
    vc = v - mu
    var = jnp.mean(vc * vc, axis=-1, keepdims=True)
    v = (vc * lax.rsqrt(var + LN_EPS)) * lng_ref[...] + lnb_ref[...]
    y_b = _dot((v * _sigmoid(v)).astype(BF16), wb_ref[...])
    merged = sga_ref[...] * y_a + sgb_ref[...] * y_b
    mix = _dot(merged.astype(BF16), wo_ref[...])

    gt1 = modp_ref[0, 2:3, :]
    sh2 = modp_ref[0, 3:4, :]
    sc2 = modp_ref[0, 4:5, :]
    x1 = xs_ref[...] + gt1 * mix
    h2 = (x1 * lax.rsqrt(jnp.mean(x1 * x1, axis=-1, keepdims=True) + RMS_EPS)) * g2_ref[...]
    h2 = h2 * (1.0 + sc2) + sh2
    h2_hi, h2_lo = _split_bf16(h2)
    h2p = _pack_bf16_pairs(h2_hi)
    hi_part = _dot(h2_hi, wrl_ref[...])
    logit_ref[...] = (hi_part[:, :ROUTE_LANES] + _dot(h2_lo, wrh_ref[...]) + hi_part[:, ROUTE_LANES:]
                      + br_ref[...])

    @pl.when(i < pl.num_programs(0) - 1)
    def _emit_rows():
        x1_ref[...] = x1
        h2p_ref[...] = h2p

    tail_u = wu_ref[pl.ds(t_rows, HALO_B), :]
    wu_ref[pl.ds(0, HALO_B), :] = tail_u * same_seq
    wu_ref[pl.ds(HALO_B, t_rows), :] = u
    tail_cx = wcx_ref[pl.ds(t_rows, HALO_A), :]
    wcx_ref[pl.ds(0, HALO_A), :] = tail_cx * same_seq
    wcx_ref[pl.ds(HALO_A, t_rows), :] = cx
    ba_ref[...] = b_a
    sga_ref[...] = sg_a
    sgb_ref[...] = sg_b
    xs_ref[...] = x


def _mixer_call(x2d, mod3, g1, conv_a, conv_b, b_conv_b, ln_g, ln_b, g2,
                w_in, w_a, w_b, w_o, wr_hi, wr_lo, b_r, *, seq_len):
    n_tok, d = x2d.shape
    t = T_MIX
    n_tiles = seq_len // t
    n_chunks = n_tok // t
    last = n_chunks - 1

    def cur(i):
        return jnp.minimum(i, last)

    def prev(i):
        return jnp.clip(i - 1, 0, last)

    def prev2(i):
        return jnp.clip(i - 2, 0, last)

    row_spec_prev = lambda width: pl.BlockSpec((t, width), lambda i: (prev(i), 0))
    vec = lambda a: _const_spec(a.shape)
    hbm = pl.BlockSpec(memory_space=pl.ANY)
    in_specs = [
        pl.BlockSpec((t, d), lambda i: (cur(i), 0)),
        pl.BlockSpec((1,) + mod3.shape[1:], lambda i: (cur(i) // n_tiles, 0, 0)),
        pl.BlockSpec((1,) + mod3.shape[1:], lambda i: (prev(i) // n_tiles, 0, 0)),
        vec(g1), vec(conv_a), vec(conv_b), vec(b_conv_b), vec(ln_g), vec(ln_b), vec(g2),
        vec(wr_hi), vec(wr_lo), vec(b_r),
        hbm, hbm, hbm, hbm,
    ]
    out_specs = [row_spec_prev(d), row_spec_prev(d // 2),
                 pl.BlockSpec((t, ROUTE_LANES), lambda i: (prev2(i), 0)),
                 pl.BlockSpec((SUBLANES, ROUTE_LANES), lambda i: (0, 0))]
    out_shape = [
        jax.ShapeDtypeStruct((n_tok, d), F32),
        jax.ShapeDtypeStruct((n_tok, d // 2), U32),
        jax.ShapeDtypeStruct((n_tok, ROUTE_LANES), F32),
        jax.ShapeDtypeStruct((SUBLANES, ROUTE_LANES), F32),
    ]
    scratch = [
        pltpu.VMEM((t + 2 * HALO_B, d), F32),
        pltpu.VMEM((t + 2 * HALO_A, d), F32),
        pltpu.VMEM((t, d), F32), pltpu.VMEM((t, d), F32), pltpu.VMEM((t, d), F32), pltpu.VMEM((t, d), F32),
        pltpu.VMEM((t, d), F32), pltpu.VMEM((t, d), F32),
        pltpu.VMEM((SUBLANES, ROUTE_LANES), F32),
        pltpu.VMEM((t, ROUTE_LANES), F32),
        pltpu.VMEM((3, SUBLANES, t + 2 * HALO_B, LANES), F32),
        pltpu.VMEM(w_in.shape, BF16), pltpu.VMEM(w_a.shape, BF16), pltpu.VMEM(w_b.shape, BF16),
        pltpu.VMEM(w_o.shape, BF16),
        pltpu.VMEM((2, d, W_STAGE_COLS), F32), pltpu.SemaphoreType.DMA((2,)),
    ]
    return pl.pallas_call(
        functools.partial(_mixer_kernel, n_tiles=n_tiles),
        grid=(n_chunks + 2,),
        in_specs=in_specs,
        out_specs=out_specs,
        out_shape=out_shape,
        scratch_shapes=scratch,
        compiler_params=pltpu.CompilerParams(
            dimension_semantics=("arbitrary",), vmem_limit_bytes=VMEM_LIMIT_BYTES),
        name="mixer_router",
    )(x2d, mod3, mod3, g1, conv_a, conv_b, b_conv_b, ln_g, ln_b, g2,
      wr_hi, wr_lo, b_r, w_in, w_a, w_b, w_o)


def _segment_layout(counts_row, n_blocks):
    counts = counts_row[:N_EXPERTS].astype(jnp.int32)
    nblk = (counts + ROW_BLK - 1) // ROW_BLK
    blk_end = jnp.cumsum(nblk)
    blk0 = blk_end - nblk
    n_used = blk_end[-1:].astype(jnp.int32)
    pstart_row = jnp.zeros((1, ROUTE_LANES), F32).at[0, :N_EXPERTS].set((blk0 * ROW_BLK).astype(F32))
    blk = jnp.arange(n_blocks, dtype=jnp.int32)[:, None]
    blk_e = jnp.minimum(jnp.sum(blk_end[None, :] <= blk, axis=1), N_EXPERTS - 1).astype(jnp.int32)
    seg_end = jnp.min(jnp.where(blk_end[None, :] > blk, blk_end[None, :], n_blocks), axis=1)
    next_e = jnp.sum(blk_end[None, :] <= seg_end[:, None], axis=1).astype(jnp.int32)
    next_e = jnp.where(seg_end >= n_used[0], -1, next_e)
    last_blk = jnp.where(nblk > 0, blk_end - 1, -1).astype(jnp.int32)
    return pstart_row, blk_e, next_e, last_blk, n_used


def _for_padded_blocks(last_blk_ref, n_used_ref, n_blocks, action):
    def per_expert(e, carry):
        @pl.when(last_blk_ref[e] >= 0)
        def _():
            action(last_blk_ref[e])
        return carry

    def per_unused(b, carry):
        action(b)
        return carry

    lax.fori_loop(0, last_blk_ref.shape[0], per_expert, 0)
    lax.fori_loop(n_used_ref[0], n_blocks, per_unused, 0)


def _plan_kernel(last_blk_ref, n_used_ref, route_ref, pstart_ref, pos_ref, buf_hbm, zeros_ref, zero_sem):
    i = pl.program_id(0)
    n_blocks = buf_hbm.shape[0] // ROW_BLK

    def block_copy(b):
        dst = buf_hbm.at[pl.ds(pl.multiple_of(b * ROW_BLK, ROW_BLK), ROW_BLK), :]
        return pltpu.make_async_copy(zeros_ref, dst, zero_sem)

    @pl.when(i == 0)
    def _start_fill():
        zeros_ref[...] = jnp.zeros(zeros_ref.shape, zeros_ref.dtype)
        _for_padded_blocks(last_blk_ref, n_used_ref, n_blocks, lambda b: block_copy(b).start())

    rec = route_ref[...]
    lane = _lane_ids(rec.shape)
    pos = jnp.zeros(rec.shape, F32)
    for k in range(TOP_K):
        e = rec[:, REC_E + k:REC_E + k + 1]
        seg = jnp.sum(jnp.where(lane == e, pstart_ref[...], 0.0), axis=-1, keepdims=True)
        pos = jnp.where(lane == k, seg + rec[:, REC_RANK + k:REC_RANK + k + 1], pos)
    pos_ref[...] = pos.astype(jnp.int32)

    @pl.when(i == pl.num_programs(0) - 1)
    def _finish_fill():
        _for_padded_blocks(last_blk_ref, n_used_ref, n_blocks, lambda b: block_copy(b).wait())


def _plan_call(route, pstart_row, last_blk, n_used, n_rows, row_width, row_dtype):
    n_tok = route.shape[0]
    t = T_PLAN
    grid_spec = pltpu.PrefetchScalarGridSpec(
        num_scalar_prefetch=2,
        grid=(n_tok // t,),
        in_specs=[pl.BlockSpec((t, ROUTE_LANES), lambda i, *_: (i, 0)),
                  pl.BlockSpec((1, ROUTE_LANES), lambda i, *_: (0, 0))],
        out_specs=[pl.BlockSpec((t, ROUTE_LANES), lambda i, *_: (i, 0)),
                   pl.BlockSpec(memory_space=pl.ANY)],
        scratch_shapes=[pltpu.VMEM((ROW_BLK, row_width), row_dtype), pltpu.SemaphoreType.DMA(())],
    )
    return pl.pallas_call(
        _plan_kernel,
        grid_spec=grid_spec,
        out_shape=[jax.ShapeDtypeStruct((n_tok, ROUTE_LANES), jnp.int32),
                   jax.ShapeDtypeStruct((n_rows, row_width), row_dtype)],
        compiler_params=pltpu.CompilerParams(dimension_semantics=("arbitrary",)),
        name="moe_plan",
    )(last_blk, n_used, route, pstart_row)


def _scatter_kernel(pos_ref, rows_ref, buf_in_hbm, buf_hbm, sem):
    del buf_in_hbm
    t_rows = rows_ref.shape[0]
    for r in range(t_rows):
        for k in range(TOP_K):
            pltpu.make_async_copy(rows_ref.at[pl.ds(r, 1), :],
                                  buf_hbm.at[pl.ds(pos_ref[0, k, r], 1), :], sem).start(priority=k)
    for _ in range(TOP_K):
        pltpu.make_async_copy(rows_ref, buf_hbm.at[pl.ds(0, t_rows), :], sem).wait()


def _scatter_call(pos3, rows, buf):
    n_tok, width = rows.shape
    t = T_SCATTER
    return pl.pallas_call(
        _scatter_kernel,
        grid=(n_tok // t,),
        in_specs=[
            pl.BlockSpec((1, TOP_K, t), lambda i: (i, 0, 0), memory_space=pltpu.SMEM),
            pl.BlockSpec((t, width), lambda i: (i, 0)),
            pl.BlockSpec(memory_space=pl.ANY),
        ],
        out_specs=pl.BlockSpec(memory_space=pl.ANY),
        out_shape=jax.ShapeDtypeStruct(buf.shape, buf.dtype),
        scratch_shapes=[pltpu.SemaphoreType.DMA(())],
        input_output_aliases={2: 0},
        compiler_params=pltpu.CompilerParams(dimension_semantics=("arbitrary",)),
        name="moe_scatter",
    )(pos3, rows, buf)


def _expert_kernel(blk_e_ref, next_e_ref, n_used_ref, xs_hbm, wg_hbm, wu_hbm, wd_hbm, y_hbm,
                   wg_st, wu_st, wd_st, wg_bf, wu_bf, wd_bf, xbuf, ybuf, w_sem, in_sem, out_sem):
    n_used = n_used_ref[0]

    def rows(b):
        return pl.ds(pl.multiple_of(b * ROW_BLK, ROW_BLK), ROW_BLK)

    def load(b, slot):
        return pltpu.make_async_copy(xs_hbm.at[rows(b), :], xbuf.at[slot], in_sem.at[slot])

    def store(b, slot):
        return pltpu.make_async_copy(ybuf.at[slot], y_hbm.at[rows(b), :], out_sem.at[slot])

    def weight_copies(e, wslot):
        return [pltpu.make_async_copy(src.at[e], dst.at[wslot], w_sem.at[wslot, n])
                for n, (src, dst) in enumerate(((wg_hbm, wg_st), (wu_hbm, wu_st), (wd_hbm, wd_st)))]

    for copy in weight_copies(blk_e_ref[0], 0):
        copy.start(priority=1)
    load(0, 0).start()

    def body(b, wslot):
        e = blk_e_ref[b]
        new_expert = jnp.logical_or(b == 0, e != blk_e_ref[jnp.maximum(b - 1, 0)])
        slot = b % 2

        @pl.when(new_expert)
        def _switch_expert():
            for copy in weight_copies(e, wslot):
                copy.wait()
            wg_bf[...] = wg_st[wslot].astype(BF16)
            wu_bf[...] = wu_st[wslot].astype(BF16)
            wd_bf[...] = wd_st[wslot].astype(BF16)
            nxt = next_e_ref[b]

            @pl.when(nxt >= 0)
            def _prefetch_weights():
                for copy in weight_copies(nxt, 1 - wslot):
                    copy.start(priority=1)

        @pl.when(b + 1 < n_used)
        def _prefetch_rows():
            load(b + 1, 1 - slot).start()

        load(b, slot).wait()

        @pl.when(b >= 2)
        def _window_free():
            store(b - 2, slot).wait()

        xb = _unpack_bf16_pairs(xbuf[slot])
        a = _dot(xb, wg_bf[...])
        u = _dot(xb, wu_bf[...])
        hid = ((a * _sigmoid(a)) * u).astype(BF16)
        ybuf[slot] = _pack_bf16_pairs(_dot(hid, wd_bf[...]).astype(BF16))
        store(b, slot).start()
        return jnp.where(new_expert, 1 - wslot, wslot)

    lax.fori_loop(0, n_used, body, jnp.int32(0))

    @pl.when(n_used >= 2)
    def _drain_older():
        store(n_used - 2, n_used % 2).wait()

    store(n_used - 1, (n_used - 1) % 2).wait()

    ybuf[0] = jnp.zeros(ybuf.shape[1:], ybuf.dtype)

    def fill(b, carry):
        copy = pltpu.make_async_copy(ybuf.at[0], y_hbm.at[rows(b), :], out_sem.at[0])
        copy.start()
        copy.wait()
        return carry

    lax.fori_loop(n_used, y_hbm.shape[0] // ROW_BLK, fill, 0)


def _expert_call(xs, blk_e, next_e, n_used, w_gate, w_up, w_down):
    n_rows, half = xs.shape
    d = 2 * half
    d_e = w_gate.shape[-1]
    hbm = pl.BlockSpec(memory_space=pl.ANY)
    grid_spec = pltpu.PrefetchScalarGridSpec(
        num_scalar_prefetch=3,
        grid=(1,),
        in_specs=[hbm, hbm, hbm, hbm],
        out_specs=hbm,
        scratch_shapes=[
            pltpu.VMEM((2, d, d_e), F32), pltpu.VMEM((2, d, d_e), F32), pltpu.VMEM((2, d_e, d), F32),
            pltpu.VMEM((d, d_e), BF16), pltpu.VMEM((d, d_e), BF16), pltpu.VMEM((d_e, d), BF16),
            pltpu.VMEM((2, ROW_BLK, half), xs.dtype), pltpu.VMEM((2, ROW_BLK, half), xs.dtype),
            pltpu.SemaphoreType.DMA((2, 3)), pltpu.SemaphoreType.DMA((2,)), pltpu.SemaphoreType.DMA((2,)),
        ],
    )
    return pl.pallas_call(
        _expert_kernel,
        grid_spec=grid_spec,
        out_shape=jax.ShapeDtypeStruct((n_rows, half), xs.dtype),
        compiler_params=pltpu.CompilerParams(
            dimension_semantics=("arbitrary",), vmem_limit_bytes=VMEM_LIMIT_BYTES),
        name="moe_experts",
    )(blk_e, next_e, n_used, xs, w_gate, w_up, w_down)


def _row_gather_wait(src_hbm, dst, sem, n_rows):
    pltpu.make_async_copy(src_hbm.at[pl.ds(0, n_rows), :], dst, sem).wait()


def _combine_kernel(posc_ref, posn_ref, x1_ref, route_ref, mod_ref, gf_ref, y_hbm, o_ref, ybuf, sem,
                    *, n_steps, final_norm):
    i = pl.program_id(0)
    slot = i % 2
    t_rows = x1_ref.shape[0]

    def start(pos_ref, s):
        for k in range(TOP_K):
            for r in range(t_rows):
                pltpu.make_async_copy(y_hbm.at[pl.ds(pos_ref[0, k, r], 1), :],
                                      ybuf.at[s, k, pl.ds(r, 1), :], sem.at[s]).start(priority=r % 2)

    @pl.when(i == 0)
    def _prime():
        start(posc_ref, 0)

    @pl.when(i + 1 < n_steps)
    def _prefetch():
        start(posn_ref, 1 - slot)

    for k in range(TOP_K):
        _row_gather_wait(y_hbm, ybuf.at[slot, k], sem.at[slot], t_rows)
    rec = route_ref[...]
    y0 = _unpack_bf16_pairs(ybuf[slot, 0], F32)
    y1 = _unpack_bf16_pairs(ybuf[slot, 1], F32)
    moe = y0 * rec[:, REC_W:REC_W + 1] + y1 * rec[:, REC_W + 1:REC_W + 2]
    x2 = x1_ref[...] + mod_ref[0, 5:6, :] * moe
    if final_norm:
        x2 = (x2 * lax.rsqrt(jnp.mean(x2 * x2, axis=-1, keepdims=True) + RMS_EPS)) * gf_ref[...]
    o_ref[...] = x2


def _combine_call(x1, route, mod3, g_final, y_rows, pos3, *, seq_len, final_norm):
    n_tok, d = x1.shape
    t = T_COMBINE
    n_steps = n_tok // t
    tiles_per_seq = seq_len // t
    last = n_steps - 1
    return pl.pallas_call(
        functools.partial(_combine_kernel, n_steps=n_steps, final_norm=final_norm),
        grid=(n_steps,),
        in_specs=[
            pl.BlockSpec((1, TOP_K, t), lambda i: (i, 0, 0), memory_space=pltpu.SMEM),
            pl.BlockSpec((1, TOP_K, t), lambda i: (jnp.minimum(i + 1, last), 0, 0), memory_space=pltpu.SMEM),
            pl.BlockSpec((t, d), lambda i: (i, 0)),
            pl.BlockSpec((t, ROUTE_LANES), lambda i: (i, 0)),
            pl.BlockSpec((1,) + mod3.shape[1:], lambda i: (i // tiles_per_seq, 0, 0)),
            pl.BlockSpec((1, d), lambda i: (0, 0)),
            pl.BlockSpec(memory_space=pl.ANY),
        ],
        out_specs=pl.BlockSpec((t, d), lambda i: (i, 0)),
        out_shape=jax.ShapeDtypeStruct((n_tok, d), F32),
        scratch_shapes=[pltpu.VMEM((2, TOP_K, t, y_rows.shape[1]), y_rows.dtype),
                        pltpu.SemaphoreType.DMA((2,))],
        compiler_params=pltpu.CompilerParams(
            dimension_semantics=("arbitrary",), vmem_limit_bytes=VMEM_LIMIT_BYTES),
        name="moe_combine",
    )(pos3, pos3, x1, route, mod3, g_final, y_rows)


def kernel(x, c, w_ada, b_ada, g_norm1, w_in, conv_a, w_a_out, conv_b, b_conv_b, ln_conv_g, ln_conv_b,
           w_b_out, w_o, g_norm2, w_router_g, b_router_g, w_router_e, b_router_e, w_gate, w_up, w_down,
           g_final):
    bsz, seq_len, d = x.shape
    depth = w_ada.shape[0]
    n_tok = bsz * seq_len
    n_assign = n_tok * TOP_K
    assert seq_len % T_MIX == 0 and seq_len % T_COMBINE == 0 and d % (2 * LANES) == 0
    assert n_tok % T_SCATTER == 0
    assert n_tok % T_PLAN == 0
    assert n_assign % ROW_BLK == 0
    assert N_GROUPS + N_EXPERTS <= ROUTE_LANES
    n_rows = n_assign + N_EXPERTS * ROW_BLK

    c_pad = jnp.zeros((SUBLANES, d), F32).at[:bsz].set(c)
    xt = x.reshape(n_tok, d)
    row = lambda a: a.reshape(1, -1)
    for l in range(depth):
        mod = _mod_call(c_pad, w_ada[l], row(b_ada[l]))
        mod3 = mod[:bsz].reshape(bsz, 6, d)
        w_r = jnp.zeros((d, ROUTE_LANES), F32)
        w_r = w_r.at[:, :N_GROUPS].set(w_router_g[l]).at[:, N_GROUPS:N_GROUPS + N_EXPERTS].set(w_router_e[l])
        b_r = jnp.zeros((1, ROUTE_LANES), F32)
        b_r = b_r.at[0, :N_GROUPS].set(b_router_g[l]).at[0, N_GROUPS:N_GROUPS + N_EXPERTS].set(b_router_e[l])
        wr_hi, wr_lo = _split_bf16(w_r)
        wr_both = jnp.concatenate([wr_hi, wr_lo], axis=1)
        x1, h2p, route, cnt = _mixer_call(
            xt, mod3, row(g_norm1[l]), conv_a[l], conv_b[l], row(b_conv_b[l]), row(ln_conv_g[l]),
            row(ln_conv_b[l]), row(g_norm2[l]),
            w_in[l], w_a_out[l], w_b_out[l], w_o[l], wr_hi, wr_both, b_r, seq_len=seq_len)
        pstart_row, blk_e, next_e, last_blk, n_used = _segment_layout(cnt[0], n_rows // ROW_BLK)
        pos, xs_blank = _plan_call(route, pstart_row, last_blk, n_used, n_rows, h2p.shape[1], h2p.dtype)
        pos_tiles = lambda t: pos[:, :TOP_K].reshape(n_tok // t, t, TOP_K).transpose(0, 2, 1)
        xs = _scatter_call(pos_tiles(T_SCATTER), h2p, xs_blank)
        y_rows = _expert_call(xs, blk_e, next_e, n_used, w_gate[l], w_up[l], w_down[l])
        xt = _combine_call(x1, route, mod3, row(g_final), y_rows, pos_tiles(T_COMBINE), seq_len=seq_len,
                           final_norm=(l == depth - 1))
    return xt.reshape(bsz, seq_len, d)
```

```python
import functools

import jax
import jax.numpy as jnp
from jax import lax
from jax.experimental import pallas as pl
from jax.experimental.pallas import tpu as pltpu

F32 = jnp.float32
BF16 = jnp.bfloat16
U32 = jnp.uint32

N_GROUPS = 4
EXPERTS_PER_GROUP = 8
N_EXPERTS = N_GROUPS * EXPERTS_PER_GROUP
TOP_K = 2
K_SHORT = 3
K_CONF = 31
RMS_EPS = 1e-6
LN_EPS = 1e-5

LANES = 128
SUBLANES = 8
T_MIX = 256
HALO_B = 16
HALO_A = 8
CONV_ROWS = 32
CONV_BLOCKS_LIVE = 7
PAIR_LAG = 1
ROUTE_AFTER_GROUP = 1
ROW_BLK = 512
T_SCATTER = 1024
T_COMBINE = 256
T_PLAN = 2048
W_STAGE_COLS = 512
ROUTE_LANES = LANES
REC_E, REC_W, REC_RANK = 0, 2, 4
VMEM_LIMIT_BYTES = 56 * 1024 * 1024


def _sigmoid(v):
    return 1.0 / (1.0 + jnp.exp(-v))


def _split_bf16(v):
    hi = v.astype(BF16)
    lo = (v - hi.astype(F32)).astype(BF16)
    return hi, lo


def _dot(a, b):
    return jnp.dot(a, b, preferred_element_type=F32)


def _const_spec(shape):
    nd = len(shape)
    return pl.BlockSpec(shape, lambda *_: (0,) * nd, pipeline_mode=pl.Buffered(1))


def _lane_ids(shape):
    return lax.broadcasted_iota(jnp.int32, shape, 1).astype(F32)


def _mod_kernel(c_ref, w_ref, b_ref, o_ref):
    c = c_ref[...]
    a_hi, a_lo = _split_bf16(c * _sigmoid(c))
    w_hi, w_lo = _split_bf16(w_ref[...])
    o_ref[...] = _dot(a_hi, w_hi) + _dot(a_lo, w_hi) + _dot(a_hi, w_lo) + b_ref[...]


def _mod_call(c_pad, w_ada, b_ada):
    rows, d = c_pad.shape
    n_out = w_ada.shape[1]
    blk = 1024
    return pl.pallas_call(
        _mod_kernel,
        grid=(n_out // blk,),
        in_specs=[
            pl.BlockSpec((rows, d), lambda j: (0, 0)),
            pl.BlockSpec((d, blk), lambda j: (0, j)),
            pl.BlockSpec((1, blk), lambda j: (0, j)),
        ],
        out_specs=pl.BlockSpec((rows, blk), lambda j: (0, j)),
        out_shape=jax.ShapeDtypeStruct((rows, n_out), F32),
        compiler_params=pltpu.CompilerParams(dimension_semantics=("arbitrary",)),
        name="adaln_mod",
    )(c_pad, w_ada, b_ada)


def _zero_bits_of(v):
    return (lax.bitcast_convert_type(v, U32) >> 16) >> 16


def _depthwise_lane_block(buf_ref, w_ref, out_ref, phase_ref, n_taps, row0, rows, cb, after=None):
    first, stop = rows
    max_off = row0 + n_taps - 1
    n_keep = stop - first + (max_off // SUBLANES) * SUBLANES
    cols = slice(cb * LANES, (cb + 1) * LANES)
    for s in sorted({(row0 + k) % SUBLANES for k in range(n_taps)}):
        phase_ref[s, pl.ds(0, n_keep), :] = buf_ref[pl.ds(first + s, n_keep), cols]
    all_blocks = list(range(first, stop, CONV_ROWS))
    for g0 in range(0, len(all_blocks), CONV_BLOCKS_LIVE):
        row_blocks = all_blocks[g0:g0 + CONV_BLOCKS_LIVE]
        acc = [None] * len(row_blocks)
        for k in range(n_taps):
            q, s = divmod(row0 + k, SUBLANES)
            w_row = w_ref[k:k + 1, cols]
            if after is not None:
                w_row = lax.bitcast_convert_type(lax.bitcast_convert_type(w_row, U32) | after, F32)
            w_k = jnp.broadcast_to(w_row, (CONV_ROWS, LANES))
            for j, r0 in enumerate(row_blocks):
                term = phase_ref[s, pl.ds(r0 - first + q * SUBLANES, CONV_ROWS), :] * w_k
                acc[j] = term if acc[j] is None else acc[j] + term
        for j, r0 in enumerate(row_blocks):
            out_ref[pl.ds(r0, CONV_ROWS), cols] = acc[j]
    return _zero_bits_of(acc[-1][0:1, :])


def _route(logits, run_ref, live):
    shape = logits.shape
    lane = _lane_ids(shape)
    big = float(4 * ROUTE_LANES)
    neg_inf = -jnp.inf

    def first_argmax(v):
        m = jnp.max(v, axis=-1, keepdims=True)
        return m, jnp.min(jnp.where(v == m, lane, big), axis=-1, keepdims=True)

    is_group = lane < N_GROUPS
    g_max, g_idx = first_argmax(jnp.where(is_group, logits, neg_inf))
    g_prob = 1.0 / jnp.sum(jnp.where(is_group, jnp.exp(logits - g_max), 0.0), axis=-1, keepdims=True)
    lo = N_GROUPS + EXPERTS_PER_GROUP * g_idx
    in_group = (lane >= lo) & (lane < lo + EXPERTS_PER_GROUP)
    e_logits = jnp.where(in_group, logits, neg_inf)
    m1, i1 = first_argmax(e_logits)
    m2, i2 = first_argmax(jnp.where(lane == i1, neg_inf, e_logits))
    t = jnp.exp(m2 - m1)
    p1 = 1.0 / (1.0 + t)
    p2 = t / (1.0 + t)
    e1 = i1 - N_GROUPS
    e2 = i2 - N_GROUPS

    hot1 = lane == e1
    hot2 = lane == e2
    both = jnp.where(hot1 | hot2, 1.0, 0.0)
    n_rows = shape[0]
    earlier = (lax.broadcasted_iota(jnp.int32, (n_rows, n_rows), 0)
               > lax.broadcasted_iota(jnp.int32, (n_rows, n_rows), 1))
    before = _dot(jnp.where(earlier, 1.0, 0.0).astype(BF16), both.astype(BF16)) + run_ref[0:1, :]
    rank1 = jnp.sum(jnp.where(hot1, before, 0.0), axis=-1, keepdims=True)
    rank2 = jnp.sum(jnp.where(hot2, before, 0.0), axis=-1, keepdims=True)
    run_ref[0:1, :] = run_ref[0:1, :] + live * jnp.sum(both, axis=0, keepdims=True)

    rec = jnp.where(lane == REC_E, e1, 0.0)
    rec = jnp.where(lane == REC_E + 1, e2, rec)
    rec = jnp.where(lane == REC_W, g_prob * p1, rec)
    rec = jnp.where(lane == REC_W + 1, g_prob * p2, rec)
    rec = jnp.where(lane == REC_RANK, rank1, rec)
    rec = jnp.where(lane == REC_RANK + 1, rank2, rec)
    return rec


def _pack_bf16_pairs(v_bf16):
    half = v_bf16.shape[1] // 2
    lo = lax.bitcast_convert_type(v_bf16[:, :half].astype(F32), U32)
    hi = lax.bitcast_convert_type(v_bf16[:, half:].astype(F32), U32)
    return (lo >> 16) | hi


def _unpack_bf16_pairs(p_u32, dtype=BF16):
    lo = lax.bitcast_convert_type(p_u32 << 16, F32).astype(dtype)
    hi = lax.bitcast_convert_type(p_u32 & jnp.uint32(0xFFFF0000), F32).astype(dtype)
    return jnp.concatenate([lo, hi], axis=1)


def _stage_weight_bf16(src_hbm, dst_ref, stage_ref, sem):
    chunk = stage_ref.shape[2]
    n_chunks = src_hbm.shape[1] // chunk

    def cols(j):
        return pl.ds(pl.multiple_of(j * chunk, chunk), chunk)

    def copy(j, slot):
        return pltpu.make_async_copy(src_hbm.at[:, cols(j)], stage_ref.at[slot], sem.at[slot])

    copy(0, 0).start()

    def body(j, carry):
        slot = j % 2

        @pl.when(j + 1 < n_chunks)
        def _next():
            copy(j + 1, 1 - slot).start()

        copy(j, slot).wait()
        dst_ref[:, cols(j)] = stage_ref[slot].astype(BF16)
        return carry

    lax.fori_loop(0, n_chunks, body, 0)


def _mixer_kernel(x_ref, modn_ref, modp_ref, g1_ref, ca_ref, cb_ref, bcb_ref, lng_ref, lnb_ref, g2_ref,
                  wrh_ref, wrl_ref, br_ref, win_hbm, wa_hbm, wb_hbm, wo_hbm,
                  x1_ref, h2p_ref, route_ref, cnt_ref,
                  wu_ref, wcx_ref, ba_ref, sga_ref, sgb_ref, xs_ref, cva_ref, cvb_ref, run_ref, logit_ref,
                  phase_ref, win_ref, wa_ref, wb_ref, wo_ref, stage_ref, stage_sem,
                  *, n_tiles):
    i = pl.program_id(0)
    t_rows, d = x_ref.shape

    @pl.when(i == 0)
    def _init():
        for ref in (wu_ref, wcx_ref, ba_ref, sga_ref, sgb_ref, xs_ref, run_ref, logit_ref):
            ref[...] = jnp.zeros(ref.shape, ref.dtype)
        for src, dst in ((win_hbm, win_ref), (wa_hbm, wa_ref), (wb_hbm, wb_ref), (wo_hbm, wo_ref)):
            _stage_weight_bf16(src, dst, stage_ref, stage_sem)

    x = x_ref[...]
    sh1 = modn_ref[0, 0:1, :]
    sc1 = modn_ref[0, 1:2, :]
    h = (x * lax.rsqrt(jnp.mean(x * x, axis=-1, keepdims=True) + RMS_EPS)) * g1_ref[...]
    hb = (h * (1.0 + sc1) + sh1).astype(BF16)

    def proj(g, after):
        lhs = hb
        if after is not None:
            bits = pltpu.bitcast(hb, U32)
            lhs = pltpu.bitcast(bits | jnp.broadcast_to(after[0:1, 0:1], bits.shape), BF16)
        return _dot(lhs, win_ref[:, g * d:(g + 1) * d])

    def conv_work(rows):
        items = []
        for cb in range(d // LANES):
            items.append(functools.partial(_depthwise_lane_block, wcx_ref, ca_ref, cva_ref,
                                           phase_ref.at[2], K_SHORT, HALO_A - K_SHORT // 2, rows, cb))
            items.append(functools.partial(_depthwise_lane_block, wu_ref, cb_ref, cvb_ref,
                                           phase_ref.at[cb % 2], K_CONF, HALO_B - K_CONF // 2, rows, cb))
        return items

    early = conv_work((0, t_rows - CONV_ROWS))
    n_proj = 7
    z = []
    conv_done = None
    for g in range(n_proj):
        z.append(proj(g, conv_done))
        released = _zero_bits_of(z[g - PAIR_LAG][0:1, 0:LANES]) if g >= PAIR_LAG else None
        for item in early[g * len(early) // n_proj:(g + 1) * len(early) // n_proj]:
            conv_done = item(after=released)
        if g == ROUTE_AFTER_GROUP:
            route_ref[...] = _route(logit_ref[...], run_ref, (i > 1).astype(F32))
            cnt_ref[...] = jnp.broadcast_to(run_ref[0:1, :], cnt_ref.shape)
    b_a = z[0]
    cx = z[1] * z[2]
    u = z[3] * _sigmoid(z[4])
    sg_a = _sigmoid(z[5])
    sg_b = _sigmoid(z[6])

    same_seq = ((i % n_tiles) != 0).astype(F32)
    wu_ref[pl.ds(HALO_B + t_rows, HALO_B), :] = u[0:HALO_B] * same_seq
    wcx_ref[pl.ds(HALO_A + t_rows, HALO_A), :] = cx[0:HALO_A] * same_seq

    for item in conv_work((t_rows - CONV_ROWS, t_rows)):
        item()
    y_a = _dot((ba_ref[...] * cva_ref[...]).astype(BF16), wa_ref[...])
    v = cvb_ref[...] + bcb_ref[...]
    mu = jnp.mean(v, axis=-1, keepdims=True)
    vc = v - mu
    var = jnp.mean(vc * vc, axis=-1, keepdims=True)
    v = (vc * lax.rsqrt(var + LN_EPS)) * lng_ref[...] + lnb_ref[...]
    y_b = _dot((v * _sigmoid(v)).astype(BF16), wb_ref[...])
    merged = sga_ref[...] * y_a + sgb_ref[...] * y_b
    mix = _dot(merged.astype(BF16), wo_ref[...])

    gt1 = modp_ref[0, 2:3, :]
    sh2 = modp_ref[0, 3:4, :]
    sc2 = modp_ref[0, 4:5, :]
    x1 = xs_ref[...] + gt1 * mix
    h2 = (x1 * lax.rsqrt(jnp.mean(x1 * x1, axis=-1, keepdims=True) + RMS_EPS)) * g2_ref[...]
    h2 = h2 * (1.0 + sc2) + sh2
    h2_hi, h2_lo = _split_bf16(h2)
    h2p = _pack_bf16_pairs(h2_hi)
    hi_part = _dot(h2_hi, wrl_ref[...])
    logit_ref[...] = (hi_part[:, :ROUTE_LANES] + _dot(h2_lo, wrh_ref[...]) + hi_part[:, ROUTE_LANES:]
                      + br_ref[...])

    @pl.when(i < pl.num_programs(0) - 1)
    def _emit_rows():
        x1_ref[...] = x1
        h2p_ref[...] = h2p

    tail_u = wu_ref[pl.ds(t_rows, HALO_B), :]
    wu_ref[pl.ds(0, HALO_B), :] = tail_u * same_seq
    wu_ref[pl.ds(HALO_B, t_rows), :] = u
    tail_cx = wcx_ref[pl.ds(t_rows, HALO_A), :]
    wcx_ref[pl.ds(0, HALO_A), :] = tail_cx * same_seq
    wcx_ref[pl.ds(HALO_A, t_rows), :] = cx
    ba_ref[...] = b_a
    sga_ref[...] = sg_a
    sgb_ref[...] = sg_b
    xs_ref[...] = x


def _mixer_call(x2d, mod3, g1, conv_a, conv_b, b_conv_b, ln_g, ln_b, g2,
                w_in, w_a, w_b, w_o, wr_hi, wr_lo, b_r, *, seq_len):
    n_tok, d = x2d.shape
    t = T_MIX
    n_tiles = seq_len // t
    n_chunks = n_tok // t
    last = n_chunks - 1

    def cur(i):
        return jnp.minimum(i, last)

    def prev(i):
        return jnp.clip(i - 1, 0, last)

    def prev2(i):
        return jnp.clip(i - 2, 0, last)

    row_spec_prev = lambda width: pl.BlockSpec((t, width), lambda i: (prev(i), 0))
    vec = lambda a: _const_spec(a.shape)
    hbm = pl.BlockSpec(memory_space=pl.ANY)
    in_specs = [
        pl.BlockSpec((t, d), lambda i: (cur(i), 0)),
        pl.BlockSpec((1,) + mod3.shape[1:], lambda i: (cur(i) // n_tiles, 0, 0)),
        pl.BlockSpec((1,) + mod3.shape[1:], lambda i: (prev(i) // n_tiles, 0, 0)),
        vec(g1), vec(conv_a), vec(conv_b), vec(b_conv_b), vec(ln_g), vec(ln_b), vec(g2),
        vec(wr_hi), vec(wr_lo), vec(b_r),
        hbm, hbm, hbm, hbm,
    ]
    out_specs = [row_spec_prev(d), row_spec_prev(d // 2),
                 pl.BlockSpec((t, ROUTE_LANES), lambda i: (prev2(i), 0)),
                 pl.BlockSpec((SUBLANES, ROUTE_LANES), lambda i: (0, 0))]
    out_shape = [
        jax.ShapeDtypeStruct((n_tok, d), F32),
        jax.ShapeDtypeStruct((n_tok, d // 2), U32),
        jax.ShapeDtypeStruct((n_tok, ROUTE_LANES), F32),
        jax.ShapeDtypeStruct((SUBLANES, ROUTE_LANES), F32),
    ]
    scratch = [
        pltpu.VMEM((t + 2 * HALO_B, d), F32),
        pltpu.VMEM((t + 2 * HALO_A, d), F32),
        pltpu.VMEM((t, d), F32), pltpu.VMEM((t, d), F32), pltpu.VMEM((t, d), F32), pltpu.VMEM((t, d), F32),
        pltpu.VMEM((t, d), F32), pltpu.VMEM((t, d), F32),
        pltpu.VMEM((SUBLANES, ROUTE_LANES), F32),
        pltpu.VMEM((t, ROUTE_LANES), F32),
        pltpu.VMEM((3, SUBLANES, t + 2 * HALO_B, LANES), F32),
        pltpu.VMEM(w_in.shape, BF16), pltpu.VMEM(w_a.shape, BF16), pltpu.VMEM(w_b.shape, BF16),
        pltpu.VMEM(w_o.shape, BF16),
        pltpu.VMEM((2, d, W_STAGE_COLS), F32), pltpu.SemaphoreType.DMA((2,)),
    ]
    return pl.pallas_call(
        functools.partial(_mixer_kernel, n_tiles=n_tiles),
        grid=(n_chunks + 2,),
        in_specs=in_specs,
        out_specs=out_specs,
        out_shape=out_shape,
        scratch_shapes=scratch,
        compiler_params=pltpu.CompilerParams(
            dimension_semantics=("arbitrary",), vmem_limit_bytes=VMEM_LIMIT_BYTES),
        name="mixer_router",
    )(x2d, mod3, mod3, g1, conv_a, conv_b, b_conv_b, ln_g, ln_b, g2,
      wr_hi, wr_lo, b_r, w_in, w_a, w_b, w_o)


def _segment_layout(counts_row, n_blocks):
    counts = counts_row[:N_EXPERTS].astype(jnp.int32)
    nblk = (counts + ROW_BLK - 1) // ROW_BLK
    blk_end = jnp.cumsum(nblk)
    blk0 = blk_end - nblk
    n_used = blk_end[-1:].astype(jnp.int32)
    pstart_row = jnp.zeros((1, ROUTE_LANES), F32).at[0, :N_EXPERTS].set((blk0 * ROW_BLK).astype(F32))
    blk = jnp.arange(n_blocks, dtype=jnp.int32)[:, None]
    blk_e = jnp.minimum(jnp.sum(blk_end[None, :] <= blk, axis=1), N_EXPERTS - 1).astype(jnp.int32)
    seg_end = jnp.min(jnp.where(blk_end[None, :] > blk, blk_end[None, :], n_blocks), axis=1)
    next_e = jnp.sum(blk_end[None, :] <= seg_end[:, None], axis=1).astype(jnp.int32)
    next_e = jnp.where(seg_end >= n_used[0], -1, next_e)
    last_blk = jnp.where(nblk > 0, blk_end - 1, -1).astype(jnp.int32)
    return pstart_row, blk_e, next_e, last_blk, n_used


def _plan_kernel(route_ref, pstart_ref, pos_ref):
    rec = route_ref[...]
    lane = _lane_ids(rec.shape)
    pos = jnp.zeros(rec.shape, F32)
    for k in range(TOP_K):
        e = rec[:, REC_E + k:REC_E + k + 1]
        seg = jnp.sum(jnp.where(lane == e, pstart_ref[...], 0.0), axis=-1, keepdims=True)
        pos = jnp.where(lane == k, seg + rec[:, REC_RANK + k:REC_RANK + k + 1], pos)
    pos_ref[...] = pos.astype(jnp.int32)


def _plan_call(route, pstart_row):
    n_tok = route.shape[0]
    t = T_PLAN
    return pl.pallas_call(
        _plan_kernel,
        grid=(n_tok // t,),
        in_specs=[pl.BlockSpec((t, ROUTE_LANES), lambda i: (i, 0)),
                  pl.BlockSpec((1, ROUTE_LANES), lambda i: (0, 0))],
        out_specs=pl.BlockSpec((t, ROUTE_LANES), lambda i: (i, 0)),
        out_shape=jax.ShapeDtypeStruct((n_tok, ROUTE_LANES), jnp.int32),
        compiler_params=pltpu.CompilerParams(dimension_semantics=("arbitrary",)),
        name="moe_plan",
    )(route, pstart_row)


def _scatter_kernel(last_blk_ref, n_used_ref, pos_ref, rows_ref, buf_hbm, zeros_ref, sem, zero_sem):
    t_rows = rows_ref.shape[0]
    n_blocks = buf_hbm.shape[0] // ROW_BLK

    @pl.when(pl.program_id(0) == 0)
    def _define_padding():
        zeros_ref[...] = jnp.zeros(zeros_ref.shape, zeros_ref.dtype)

        def block_copy(b):
            dst = buf_hbm.at[pl.ds(pl.multiple_of(b * ROW_BLK, ROW_BLK), ROW_BLK), :]
            return pltpu.make_async_copy(zeros_ref, dst, zero_sem)

        def for_padded_blocks(action):
            def per_expert(e, carry):
                @pl.when(last_blk_ref[e] >= 0)
                def _():
                    action(last_blk_ref[e])
                return carry

            def per_unused(b, carry):
                action(b)
                return carry

            lax.fori_loop(0, last_blk_ref.shape[0], per_expert, 0)
            lax.fori_loop(n_used_ref[0], n_blocks, per_unused, 0)

        for_padded_blocks(lambda b: block_copy(b).start())
        for_padded_blocks(lambda b: block_copy(b).wait())

    for r in range(t_rows):
        for k in range(TOP_K):
            pltpu.make_async_copy(rows_ref.at[pl.ds(r, 1), :],
                                  buf_hbm.at[pl.ds(pos_ref[0, k, r], 1), :], sem).start(priority=k)
    for _ in range(TOP_K):
        pltpu.make_async_copy(rows_ref, buf_hbm.at[pl.ds(0, t_rows), :], sem).wait()


def _scatter_call(pos3, rows, last_blk, n_used, n_rows):
    n_tok, width = rows.shape
    t = T_SCATTER
    grid_spec = pltpu.PrefetchScalarGridSpec(
        num_scalar_prefetch=2,
        grid=(n_tok // t,),
        in_specs=[
            pl.BlockSpec((1, TOP_K, t), lambda i, *_: (i, 0, 0), memory_space=pltpu.SMEM),
            pl.BlockSpec((t, width), lambda i, *_: (i, 0)),
        ],
        out_specs=pl.BlockSpec(memory_space=pl.ANY),
        scratch_shapes=[pltpu.VMEM((ROW_BLK, width), rows.dtype),
                        pltpu.SemaphoreType.DMA(()), pltpu.SemaphoreType.DMA(())],
    )
    return pl.pallas_call(
        _scatter_kernel,
        grid_spec=grid_spec,
        out_shape=jax.ShapeDtypeStruct((n_rows, width), rows.dtype),
        compiler_params=pltpu.CompilerParams(dimension_semantics=("arbitrary",)),
        name="moe_scatter",
    )(last_blk, n_used, pos3, rows)


def _expert_kernel(blk_e_ref, next_e_ref, n_used_ref, xs_hbm, wg_hbm, wu_hbm, wd_hbm, y_hbm,
                   wg_st, wu_st, wd_st, wg_bf, wu_bf, wd_bf, xbuf, ybuf, w_sem, in_sem, out_sem):
    n_used = n_used_ref[0]

    def rows(b):
        return pl.ds(pl.multiple_of(b * ROW_BLK, ROW_BLK), ROW_BLK)

    def load(b, slot):
        return pltpu.make_async_copy(xs_hbm.at[rows(b), :], xbuf.at[slot], in_sem.at[slot])

    def store(b, slot):
        return pltpu.make_async_copy(ybuf.at[slot], y_hbm.at[rows(b), :], out_sem.at[slot])

    def weight_copies(e, wslot):
        return [pltpu.make_async_copy(src.at[e], dst.at[wslot], w_sem.at[wslot, n])
                for n, (src, dst) in enumerate(((wg_hbm, wg_st), (wu_hbm, wu_st), (wd_hbm, wd_st)))]

    for copy in weight_copies(blk_e_ref[0], 0):
        copy.start(priority=1)
    load(0, 0).start()

    def body(b, wslot):
        e = blk_e_ref[b]
        new_expert = jnp.logical_or(b == 0, e != blk_e_ref[jnp.maximum(b - 1, 0)])
        slot = b % 2

        @pl.when(new_expert)
        def _switch_expert():
            for copy in weight_copies(e, wslot):
                copy.wait()
            wg_bf[...] = wg_st[wslot].astype(BF16)
            wu_bf[...] = wu_st[wslot].astype(BF16)
            wd_bf[...] = wd_st[wslot].astype(BF16)
            nxt = next_e_ref[b]

            @pl.when(nxt >= 0)
            def _prefetch_weights():
                for copy in weight_copies(nxt, 1 - wslot):
                    copy.start(priority=1)

        @pl.when(b + 1 < n_used)
        def _prefetch_rows():
            load(b + 1, 1 - slot).start()

        load(b, slot).wait()

        @pl.when(b >= 2)
        def _window_free():
            store(b - 2, slot).wait()

        xb = _unpack_bf16_pairs(xbuf[slot])
        a = _dot(xb, wg_bf[...])
        u = _dot(xb, wu_bf[...])
        hid = ((a * _sigmoid(a)) * u).astype(BF16)
        ybuf[slot] = _pack_bf16_pairs(_dot(hid, wd_bf[...]).astype(BF16))
        store(b, slot).start()
        return jnp.where(new_expert, 1 - wslot, wslot)

    lax.fori_loop(0, n_used, body, jnp.int32(0))

    @pl.when(n_used >= 2)
    def _drain_older():
        store(n_used - 2, n_used % 2).wait()

    store(n_used - 1, (n_used - 1) % 2).wait()

    ybuf[0] = jnp.zeros(ybuf.shape[1:], ybuf.dtype)

    def fill(b, carry):
        copy = pltpu.make_async_copy(ybuf.at[0], y_hbm.at[rows(b), :], out_sem.at[0])
        copy.start()
        copy.wait()
        return carry

    lax.fori_loop(n_used, y_hbm.shape[0] // ROW_BLK, fill, 0)


def _expert_call(xs, blk_e, next_e, n_used, w_gate, w_up, w_down):
    n_rows, half = xs.shape
    d = 2 * half
    d_e = w_gate.shape[-1]
    hbm = pl.BlockSpec(memory_space=pl.ANY)
    grid_spec = pltpu.PrefetchScalarGridSpec(
        num_scalar_prefetch=3,
        grid=(1,),
        in_specs=[hbm, hbm, hbm, hbm],
        out_specs=hbm,
        scratch_shapes=[
            pltpu.VMEM((2, d, d_e), F32), pltpu.VMEM((2, d, d_e), F32), pltpu.VMEM((2, d_e, d), F32),
            pltpu.VMEM((d, d_e), BF16), pltpu.VMEM((d, d_e), BF16), pltpu.VMEM((d_e, d), BF16),
            pltpu.VMEM((2, ROW_BLK, half), xs.dtype), pltpu.VMEM((2, ROW_BLK, half), xs.dtype),
            pltpu.SemaphoreType.DMA((2, 3)), pltpu.SemaphoreType.DMA((2,)), pltpu.SemaphoreType.DMA((2,)),
        ],
    )
    return pl.pallas_call(
        _expert_kernel,
        grid_spec=grid_spec,
        out_shape=jax.ShapeDtypeStruct((n_rows, half), xs.dtype),
        compiler_params=pltpu.CompilerParams(
            dimension_semantics=("arbitrary",), vmem_limit_bytes=VMEM_LIMIT_BYTES),
        name="moe_experts",
    )(blk_e, next_e, n_used, xs, w_gate, w_up, w_down)


def _row_gather_wait(src_hbm, dst, sem, n_rows):
    pltpu.make_async_copy(src_hbm.at[pl.ds(0, n_rows), :], dst, sem).wait()


def _combine_kernel(posc_ref, posn_ref, x1_ref, route_ref, mod_ref, gf_ref, y_hbm, o_ref, ybuf, sem,
                    *, n_steps, final_norm):
    i = pl.program_id(0)
    slot = i % 2
    t_rows = x1_ref.shape[0]

    def start(pos_ref, s):
        for k in range(TOP_K):
            for r in range(t_rows):
                pltpu.make_async_copy(y_hbm.at[pl.ds(pos_ref[0, k, r], 1), :],
                                      ybuf.at[s, k, pl.ds(r, 1), :], sem.at[s]).start(priority=r % 2)

    @pl.when(i == 0)
    def _prime():
        start(posc_ref, 0)

    @pl.when(i + 1 < n_steps)
    def _prefetch():
        start(posn_ref, 1 - slot)

    for k in range(TOP_K):
        _row_gather_wait(y_hbm, ybuf.at[slot, k], sem.at[slot], t_rows)
    rec = route_ref[...]
    y0 = _unpack_bf16_pairs(ybuf[slot, 0], F32)
    y1 = _unpack_bf16_pairs(ybuf[slot, 1], F32)
    moe = y0 * rec[:, REC_W:REC_W + 1] + y1 * rec[:, REC_W + 1:REC_W + 2]
    x2 = x1_ref[...] + mod_ref[0, 5:6, :] * moe
    if final_norm:
        x2 = (x2 * lax.rsqrt(jnp.mean(x2 * x2, axis=-1, keepdims=True) + RMS_EPS)) * gf_ref[...]
    o_ref[...] = x2


def _combine_call(x1, route, mod3, g_final, y_rows, pos3, *, seq_len, final_norm):
    n_tok, d = x1.shape
    t = T_COMBINE
    n_steps = n_tok // t
    tiles_per_seq = seq_len // t
    last = n_steps - 1
    return pl.pallas_call(
        functools.partial(_combine_kernel, n_steps=n_steps, final_norm=final_norm),
        grid=(n_steps,),
        in_specs=[
            pl.BlockSpec((1, TOP_K, t), lambda i: (i, 0, 0), memory_space=pltpu.SMEM),
            pl.BlockSpec((1, TOP_K, t), lambda i: (jnp.minimum(i + 1, last), 0, 0), memory_space=pltpu.SMEM),
            pl.BlockSpec((t, d), lambda i: (i, 0)),
            pl.BlockSpec((t, ROUTE_LANES), lambda i: (i, 0)),
            pl.BlockSpec((1,) + mod3.shape[1:], lambda i: (i // tiles_per_seq, 0, 0)),
            pl.BlockSpec((1, d), lambda i: (0, 0)),
            pl.BlockSpec(memory_space=pl.ANY),
        ],
        out_specs=pl.BlockSpec((t, d), lambda i: (i, 0)),
        out_shape=jax.ShapeDtypeStruct((n_tok, d), F32),
        scratch_shapes=[pltpu.VMEM((2, TOP_K, t, y_rows.shape[1]), y_rows.dtype),
                        pltpu.SemaphoreType.DMA((2,))],
        compiler_params=pltpu.CompilerParams(
            dimension_semantics=("arbitrary",), vmem_limit_bytes=VMEM_LIMIT_BYTES),
        name="moe_combine",
    )(pos3, pos3, x1, route, mod3, g_final, y_rows)


def kernel(x, c, w_ada, b_ada, g_norm1, w_in, conv_a, w_a_out, conv_b, b_conv_b, ln_conv_g, ln_conv_b,
           w_b_out, w_o, g_norm2, w_router_g, b_router_g, w_router_e, b_router_e, w_gate, w_up, w_down,
           g_final):
    bsz, seq_len, d = x.shape
    depth = w_ada.shape[0]
    n_tok = bsz * seq_len
    n_assign = n_tok * TOP_K
    assert seq_len % T_MIX == 0 and seq_len % T_COMBINE == 0 and d % (2 * LANES) == 0
    assert n_tok % T_SCATTER == 0
    assert n_tok % T_PLAN == 0
    assert n_assign % ROW_BLK == 0
    assert N_GROUPS + N_EXPERTS <= ROUTE_LANES
    n_rows = n_assign + N_EXPERTS * ROW_BLK

    c_pad = jnp.zeros((SUBLANES, d), F32).at[:bsz].set(c)
    xt = x.reshape(n_tok, d)
    row = lambda a: a.reshape(1, -1)
    for l in range(depth):
        mod = _mod_call(c_pad, w_ada[l], row(b_ada[l]))
        mod3 = mod[:bsz].reshape(bsz, 6, d)
        w_r = jnp.zeros((d, ROUTE_LANES), F32)
        w_r = w_r.at[:, :N_GROUPS].set(w_router_g[l]).at[:, N_GROUPS:N_GROUPS + N_EXPERTS].set(w_router_e[l])
        b_r = jnp.zeros((1, ROUTE_LANES), F32)
        b_r = b_r.at[0, :N_GROUPS].set(b_router_g[l]).at[0, N_GROUPS:N_GROUPS + N_EXPERTS].set(b_router_e[l])
        wr_hi, wr_lo = _split_bf16(w_r)
        wr_both = jnp.concatenate([wr_hi, wr_lo], axis=1)
        x1, h2p, route, cnt = _mixer_call(
            xt, mod3, row(g_norm1[l]), conv_a[l], conv_b[l], row(b_conv_b[l]), row(ln_conv_g[l]),
            row(ln_conv_b[l]), row(g_norm2[l]),
            w_in[l], w_a_out[l], w_b_out[l], w_o[l], wr_hi, wr_both, b_r, seq_len=seq_len)
        pstart_row, blk_e, next_e, last_blk, n_used = _segment_layout(cnt[0], n_rows // ROW_BLK)
        pos = _plan_call(route, pstart_row)
        pos_tiles = lambda t: pos[:, :TOP_K].reshape(n_tok // t, t, TOP_K).transpose(0, 2, 1)
        xs = _scatter_call(pos_tiles(T_SCATTER), h2p, last_blk, n_used, n_rows)
        y_rows = _expert_call(xs, blk_e, next_e, n_used, w_gate[l], w_up[l], w_down[l])
        xt = _combine_call(x1, route, mod3, row(g_final), y_rows, pos_tiles(T_COMBINE), seq_len=seq_len,
                           final_norm=(l == depth - 1))
    return xt.reshape(bsz, seq_len, d)
```

```python
import functools

import jax
import jax.numpy as jnp
from jax import lax
from jax.experimental import pallas as pl
from jax.experimental.pallas import tpu as pltpu

F32 = jnp.float32
BF16 = jnp.bfloat16
U32 = jnp.uint32

N_GROUPS = 4
EXPERTS_PER_GROUP = 8
N_EXPERTS = N_GROUPS * EXPERTS_PER_GROUP
TOP_K = 2
K_SHORT = 3
K_CONF = 31
RMS_EPS = 1e-6
LN_EPS = 1e-5

LANES = 128
SUBLANES = 8
T_MIX = 256
HALO_B = 16
HALO_A = 8
CONV_ROWS = 16
CONV_BLOCKS_LIVE = 15
PAIR_LAG = 1
ROUTE_AFTER_GROUP = 1
ROW_BLK = 512
T_SCATTER = 1024
T_COMBINE = 512
T_PLAN = 2048
W_STAGE_COLS = 512
ROUTE_LANES = LANES
REC_E, REC_W, REC_RANK = 0, 2, 4
VMEM_LIMIT_BYTES = 56 * 1024 * 1024


def _sigmoid(v):
    return 1.0 / (1.0 + jnp.exp(-v))


def _split_bf16(v):
    hi = v.astype(BF16)
    lo = (v - hi.astype(F32)).astype(BF16)
    return hi, lo


def _dot(a, b):
    return jnp.dot(a, b, preferred_element_type=F32)


def _const_spec(shape):
    nd = len(shape)
    return pl.BlockSpec(shape, lambda *_: (0,) * nd, pipeline_mode=pl.Buffered(1))


def _lane_ids(shape):
    return lax.broadcasted_iota(jnp.int32, shape, 1).astype(F32)


def _mod_kernel(c_ref, w_ref, b_ref, o_ref):
    c = c_ref[...]
    a_hi, a_lo = _split_bf16(c * _sigmoid(c))
    w_hi, w_lo = _split_bf16(w_ref[...])
    o_ref[...] = _dot(a_hi, w_hi) + _dot(a_lo, w_hi) + _dot(a_hi, w_lo) + b_ref[...]


def _mod_call(c_pad, w_ada, b_ada):
    rows, d = c_pad.shape
    n_out = w_ada.shape[1]
    blk = 1024
    return pl.pallas_call(
        _mod_kernel,
        grid=(n_out // blk,),
        in_specs=[
            pl.BlockSpec((rows, d), lambda j: (0, 0)),
            pl.BlockSpec((d, blk), lambda j: (0, j)),
            pl.BlockSpec((1, blk), lambda j: (0, j)),
        ],
        out_specs=pl.BlockSpec((rows, blk), lambda j: (0, j)),
        out_shape=jax.ShapeDtypeStruct((rows, n_out), F32),
        compiler_params=pltpu.CompilerParams(dimension_semantics=("arbitrary",)),
        name="adaln_mod",
    )(c_pad, w_ada, b_ada)


def _zero_bits_of(v):
    return (lax.bitcast_convert_type(v, U32) >> 16) >> 16


def _depthwise_lane_block(buf_ref, w_ref, out_ref, phase_ref, n_taps, row0, rows, cb, after=None):
    first, stop = rows
    max_off = row0 + n_taps - 1
    n_keep = stop - first + (max_off // SUBLANES) * SUBLANES
    cols = slice(cb * LANES, (cb + 1) * LANES)
    for s in sorted({(row0 + k) % SUBLANES for k in range(n_taps)}):
        phase_ref[s, pl.ds(0, n_keep), :] = buf_ref[pl.ds(first + s, n_keep), cols]
    all_blocks = list(range(first, stop, CONV_ROWS))
    for g0 in range(0, len(all_blocks), CONV_BLOCKS_LIVE):
        row_blocks = all_blocks[g0:g0 + CONV_BLOCKS_LIVE]
        acc = [None] * len(row_blocks)
        for k in range(n_taps):
            q, s = divmod(row0 + k, SUBLANES)
            w_row = w_ref[k:k + 1, cols]
            if after is not None:
                w_row = lax.bitcast_convert_type(lax.bitcast_convert_type(w_row, U32) | after, F32)
            w_k = jnp.broadcast_to(w_row, (CONV_ROWS, LANES))
            for j, r0 in enumerate(row_blocks):
                term = phase_ref[s, pl.ds(r0 - first + q * SUBLANES, CONV_ROWS), :] * w_k
                acc[j] = term if acc[j] is None else acc[j] + term
        for j, r0 in enumerate(row_blocks):
            out_ref[pl.ds(r0, CONV_ROWS), cols] = acc[j]
    return _zero_bits_of(acc[-1][0:1, :])


def _route(logits, run_ref, live):
    shape = logits.shape
    lane = _lane_ids(shape)
    big = float(4 * ROUTE_LANES)
    neg_inf = -jnp.inf

    def first_argmax(v):
        m = jnp.max(v, axis=-1, keepdims=True)
        return m, jnp.min(jnp.where(v == m, lane, big), axis=-1, keepdims=True)

    is_group = lane < N_GROUPS
    g_max, g_idx = first_argmax(jnp.where(is_group, logits, neg_inf))
    g_prob = 1.0 / jnp.sum(jnp.where(is_group, jnp.exp(logits - g_max), 0.0), axis=-1, keepdims=True)
    lo = N_GROUPS + EXPERTS_PER_GROUP * g_idx
    in_group = (lane >= lo) & (lane < lo + EXPERTS_PER_GROUP)
    e_logits = jnp.where(in_group, logits, neg_inf)
    m1, i1 = first_argmax(e_logits)
    m2, i2 = first_argmax(jnp.where(lane == i1, neg_inf, e_logits))
    t = jnp.exp(m2 - m1)
    p1 = 1.0 / (1.0 + t)
    p2 = t / (1.0 + t)
    e1 = i1 - N_GROUPS
    e2 = i2 - N_GROUPS

    hot1 = lane == e1
    hot2 = lane == e2
    both = jnp.where(hot1 | hot2, 1.0, 0.0)
    n_rows = shape[0]
    earlier = (lax.broadcasted_iota(jnp.int32, (n_rows, n_rows), 0)
               > lax.broadcasted_iota(jnp.int32, (n_rows, n_rows), 1))
    before = _dot(jnp.where(earlier, 1.0, 0.0).astype(BF16), both.astype(BF16)) + run_ref[0:1, :]
    rank1 = jnp.sum(jnp.where(hot1, before, 0.0), axis=-1, keepdims=True)
    rank2 = jnp.sum(jnp.where(hot2, before, 0.0), axis=-1, keepdims=True)
    run_ref[0:1, :] = run_ref[0:1, :] + live * jnp.sum(both, axis=0, keepdims=True)

    rec = jnp.where(lane == REC_E, e1, 0.0)
    rec = jnp.where(lane == REC_E + 1, e2, rec)
    rec = jnp.where(lane == REC_W, g_prob * p1, rec)
    rec = jnp.where(lane == REC_W + 1, g_prob * p2, rec)
    rec = jnp.where(lane == REC_RANK, rank1, rec)
    rec = jnp.where(lane == REC_RANK + 1, rank2, rec)
    return rec


def _pack_bf16_pairs(v_bf16):
    half = v_bf16.shape[1] // 2
    lo = lax.bitcast_convert_type(v_bf16[:, :half].astype(F32), U32)
    hi = lax.bitcast_convert_type(v_bf16[:, half:].astype(F32), U32)
    return (lo >> 16) | hi


def _unpack_bf16_pairs(p_u32, dtype=BF16):
    lo = lax.bitcast_convert_type(p_u32 << 16, F32).astype(dtype)
    hi = lax.bitcast_convert_type(p_u32 & jnp.uint32(0xFFFF0000), F32).astype(dtype)
    return jnp.concatenate([lo, hi], axis=1)


def _stage_weight_bf16(src_hbm, dst_ref, stage_ref, sem):
    chunk = stage_ref.shape[2]
    n_chunks = src_hbm.shape[1] // chunk

    def cols(j):
        return pl.ds(pl.multiple_of(j * chunk, chunk), chunk)

    def copy(j, slot):
        return pltpu.make_async_copy(src_hbm.at[:, cols(j)], stage_ref.at[slot], sem.at[slot])

    copy(0, 0).start()

    def body(j, carry):
        slot = j % 2

        @pl.when(j + 1 < n_chunks)
        def _next():
            copy(j + 1, 1 - slot).start()

        copy(j, slot).wait()
        dst_ref[:, cols(j)] = stage_ref[slot].astype(BF16)
        return carry

    lax.fori_loop(0, n_chunks, body, 0)


def _mixer_kernel(x_ref, modn_ref, modp_ref, g1_ref, ca_ref, cb_ref, bcb_ref, lng_ref, lnb_ref, g2_ref,
                  wrh_ref, wrl_ref, br_ref, win_hbm, wa_hbm, wb_hbm, wo_hbm,
                  x1_ref, h2p_ref, route_ref, cnt_ref,
                  wu_ref, wcx_ref, ba_ref, sga_ref, sgb_ref, xs_ref, cva_ref, cvb_ref, run_ref, logit_ref,
                  phase_ref, win_ref, wa_ref, wb_ref, wo_ref, stage_ref, stage_sem,
                  *, n_tiles):
    i = pl.program_id(0)
    t_rows, d = x_ref.shape

    @pl.when(i == 0)
    def _init():
        for ref in (wu_ref, wcx_ref, ba_ref, sga_ref, sgb_ref, xs_ref, run_ref, logit_ref):
            ref[...] = jnp.zeros(ref.shape, ref.dtype)
        for src, dst in ((win_hbm, win_ref), (wa_hbm, wa_ref), (wb_hbm, wb_ref), (wo_hbm, wo_ref)):
            _stage_weight_bf16(src, dst, stage_ref, stage_sem)

    x = x_ref[...]
    sh1 = modn_ref[0, 0:1, :]
    sc1 = modn_ref[0, 1:2, :]
    h = (x * lax.rsqrt(jnp.mean(x * x, axis=-1, keepdims=True) + RMS_EPS)) * g1_ref[...]
    hb = (h * (1.0 + sc1) + sh1).astype(BF16)

    def proj(g, after):
        lhs = hb
        if after is not None:
            bits = pltpu.bitcast(hb, U32)
            lhs = pltpu.bitcast(bits | jnp.broadcast_to(after[0:1, 0:1], bits.shape), BF16)
        return _dot(lhs, win_ref[:, g * d:(g + 1) * d])

    def conv_work(rows):
        items = []
        for cb in range(d // LANES):
            items.append(functools.partial(_depthwise_lane_block, wcx_ref, ca_ref, cva_ref,
                                           phase_ref.at[2], K_SHORT, HALO_A - K_SHORT // 2, rows, cb))
            items.append(functools.partial(_depthwise_lane_block, wu_ref, cb_ref, cvb_ref,
                                           phase_ref.at[cb % 2], K_CONF, HALO_B - K_CONF // 2, rows, cb))
        return items

    early = conv_work((0, t_rows - CONV_ROWS))
    n_proj = 7
    z = []
    conv_done = None
    for g in range(n_proj):
        z.append(proj(g, conv_done))
        released = _zero_bits_of(z[g - PAIR_LAG][0:1, 0:LANES]) if g >= PAIR_LAG else None
        for item in early[g * len(early) // n_proj:(g + 1) * len(early) // n_proj]:
            conv_done = item(after=released)
        if g == ROUTE_AFTER_GROUP:
            route_ref[...] = _route(logit_ref[...], run_ref, (i > 1).astype(F32))
            cnt_ref[...] = jnp.broadcast_to(run_ref[0:1, :], cnt_ref.shape)
    b_a = z[0]
    cx = z[1] * z[2]
    u = z[3] * _sigmoid(z[4])
    sg_a = _sigmoid(z[5])
    sg_b = _sigmoid(z[6])

    same_seq = ((i % n_tiles) != 0).astype(F32)
    wu_ref[pl.ds(HALO_B + t_rows, HALO_B), :] = u[0:HALO_B] * same_seq
    wcx_ref[pl.ds(HALO_A + t_rows, HALO_A), :] = cx[0:HALO_A] * same_seq

    for item in conv_work((t_rows - CONV_ROWS, t_rows)):
        item()
    y_a = _dot((ba_ref[...] * cva_ref[...]).astype(BF16), wa_ref[...])
    v = cvb_ref[...] + bcb_ref[...]
    mu = jnp.mean(v, axis=-1, keepdims=True)
    vc = v - mu
    var = jnp.mean(vc * vc, axis=-1, keepdims=True)
    v = (vc * lax.rsqrt(var + LN_EPS)) * lng_ref[...] + lnb_ref[...]
    y_b = _dot((v * _sigmoid(v)).astype(BF16), wb_ref[...])
    merged = sga_ref[...] * y_a + sgb_ref[...] * y_b
    mix = _dot(merged.astype(BF16), wo_ref[...])

    gt1 = modp_ref[0, 2:3, :]
    sh2 = modp_ref[0, 3:4, :]
    sc2 = modp_ref[0, 4:5, :]
    x1 = xs_ref[...] + gt1 * mix
    h2 = (x1 * lax.rsqrt(jnp.mean(x1 * x1, axis=-1, keepdims=True) + RMS_EPS)) * g2_ref[...]
    h2 = h2 * (1.0 + sc2) + sh2
    h2_hi, h2_lo = _split_bf16(h2)
    h2p = _pack_bf16_pairs(h2_hi)
    hi_part = _dot(h2_hi, wrl_ref[...])
    logit_ref[...] = (hi_part[:, :ROUTE_LANES] + _dot(h2_lo, wrh_ref[...]) + hi_part[:, ROUTE_LANES:]
                      + br_ref[...])

    @pl.when(i < pl.num_programs(0) - 1)
    def _emit_rows():
        x1_ref[...] = x1
        h2p_ref[...] = h2p

    tail_u = wu_ref[pl.ds(t_rows, HALO_B), :]
    wu_ref[pl.ds(0, HALO_B), :] = tail_u * same_seq
    wu_ref[pl.ds(HALO_B, t_rows), :] = u
    tail_cx = wcx_ref[pl.ds(t_rows, HALO_A), :]
    wcx_ref[pl.ds(0, HALO_A), :] = tail_cx * same_seq
    wcx_ref[pl.ds(HALO_A, t_rows), :] = cx
    ba_ref[...] = b_a
    sga_ref[...] = sg_a
    sgb_ref[...] = sg_b
    xs_ref[...] = x


def _mixer_call(x2d, mod3, g1, conv_a, conv_b, b_conv_b, ln_g, ln_b, g2,
                w_in, w_a, w_b, w_o, wr_hi, wr_lo, b_r, *, seq_len):
    n_tok, d = x2d.shape
    t = T_MIX
    n_tiles = seq_len // t
    n_chunks = n_tok // t
    last = n_chunks - 1

    def cur(i):
        return jnp.minimum(i, last)

    def prev(i):
        return jnp.clip(i - 1, 0, last)

    def prev2(i):
        return jnp.clip(i - 2, 0, last)

    row_spec_prev = lambda width: pl.BlockSpec((t, width), lambda i: (prev(i), 0))
    vec = lambda a: _const_spec(a.shape)
    hbm = pl.BlockSpec(memory_space=pl.ANY)
    in_specs = [
        pl.BlockSpec((t, d), lambda i: (cur(i), 0)),
        pl.BlockSpec((1,) + mod3.shape[1:], lambda i: (cur(i) // n_tiles, 0, 0)),
        pl.BlockSpec((1,) + mod3.shape[1:], lambda i: (prev(i) // n_tiles, 0, 0)),
        vec(g1), vec(conv_a), vec(conv_b), vec(b_conv_b), vec(ln_g), vec(ln_b), vec(g2),
        vec(wr_hi), vec(wr_lo), vec(b_r),
        hbm, hbm, hbm, hbm,
    ]
    out_specs = [row_spec_prev(d), row_spec_prev(d // 2),
                 pl.BlockSpec((t, ROUTE_LANES), lambda i: (prev2(i), 0)),
                 pl.BlockSpec((SUBLANES, ROUTE_LANES), lambda i: (0, 0))]
    out_shape = [
        jax.ShapeDtypeStruct((n_tok, d), F32),
        jax.ShapeDtypeStruct((n_tok, d // 2), U32),
        jax.ShapeDtypeStruct((n_tok, ROUTE_LANES), F32),
        jax.ShapeDtypeStruct((SUBLANES, ROUTE_LANES), F32),
    ]
    scratch = [
        pltpu.VMEM((t + 2 * HALO_B, d), F32),
        pltpu.VMEM((t + 2 * HALO_A, d), F32),
        pltpu.VMEM((t, d), F32), pltpu.VMEM((t, d), F32), pltpu.VMEM((t, d), F32), pltpu.VMEM((t, d), F32),
        pltpu.VMEM((t, d), F32), pltpu.VMEM((t, d), F32),
        pltpu.VMEM((SUBLANES, ROUTE_LANES), F32),
        pltpu.VMEM((t, ROUTE_LANES), F32),
        pltpu.VMEM((3, SUBLANES, t + 2 * HALO_B, LANES), F32),
        pltpu.VMEM(w_in.shape, BF16), pltpu.VMEM(w_a.shape, BF16), pltpu.VMEM(w_b.shape, BF16),
        pltpu.VMEM(w_o.shape, BF16),
        pltpu.VMEM((2, d, W_STAGE_COLS), F32), pltpu.SemaphoreType.DMA((2,)),
    ]
    return pl.pallas_call(
        functools.partial(_mixer_kernel, n_tiles=n_tiles),
        grid=(n_chunks + 2,),
        in_specs=in_specs,
        out_specs=out_specs,
        out_shape=out_shape,
        scratch_shapes=scratch,
        compiler_params=pltpu.CompilerParams(
            dimension_semantics=("arbitrary",), vmem_limit_bytes=VMEM_LIMIT_BYTES),
        name="mixer_router",
    )(x2d, mod3, mod3, g1, conv_a, conv_b, b_conv_b, ln_g, ln_b, g2,
      wr_hi, wr_lo, b_r, w_in, w_a, w_b, w_o)


def _segment_layout(counts_row, n_blocks):
    counts = counts_row[:N_EXPERTS].astype(jnp.int32)
    nblk = (counts + ROW_BLK - 1) // ROW_BLK
    blk_end = jnp.cumsum(nblk)
    blk0 = blk_end - nblk
    n_used = blk_end[-1:].astype(jnp.int32)
    pstart_row = jnp.zeros((1, ROUTE_LANES), F32).at[0, :N_EXPERTS].set((blk0 * ROW_BLK).astype(F32))
    blk = jnp.arange(n_blocks, dtype=jnp.int32)[:, None]
    blk_e = jnp.minimum(jnp.sum(blk_end[None, :] <= blk, axis=1), N_EXPERTS - 1).astype(jnp.int32)
    seg_end = jnp.min(jnp.where(blk_end[None, :] > blk, blk_end[None, :], n_blocks), axis=1)
    next_e = jnp.sum(blk_end[None, :] <= seg_end[:, None], axis=1).astype(jnp.int32)
    next_e = jnp.where(seg_end >= n_used[0], -1, next_e)
    last_blk = jnp.where(nblk > 0, blk_end - 1, -1).astype(jnp.int32)
    return pstart_row, blk_e, next_e, last_blk, n_used


def _plan_kernel(route_ref, pstart_ref, pos_ref):
    rec = route_ref[...]
    lane = _lane_ids(rec.shape)
    pos = jnp.zeros(rec.shape, F32)
    for k in range(TOP_K):
        e = rec[:, REC_E + k:REC_E + k + 1]
        seg = jnp.sum(jnp.where(lane == e, pstart_ref[...], 0.0), axis=-1, keepdims=True)
        pos = jnp.where(lane == k, seg + rec[:, REC_RANK + k:REC_RANK + k + 1], pos)
    pos_ref[...] = pos.astype(jnp.int32)


def _plan_call(route, pstart_row):
    n_tok = route.shape[0]
    t = T_PLAN
    return pl.pallas_call(
        _plan_kernel,
        grid=(n_tok // t,),
        in_specs=[pl.BlockSpec((t, ROUTE_LANES), lambda i: (i, 0)),
                  pl.BlockSpec((1, ROUTE_LANES), lambda i: (0, 0))],
        out_specs=pl.BlockSpec((t, ROUTE_LANES), lambda i: (i, 0)),
        out_shape=jax.ShapeDtypeStruct((n_tok, ROUTE_LANES), jnp.int32),
        compiler_params=pltpu.CompilerParams(dimension_semantics=("arbitrary",)),
        name="moe_plan",
    )(route, pstart_row)


def _scatter_kernel(last_blk_ref, n_used_ref, pos_ref, rows_ref, buf_hbm, zeros_ref, sem, zero_sem):
    t_rows = rows_ref.shape[0]
    n_blocks = buf_hbm.shape[0] // ROW_BLK

    @pl.when(pl.program_id(0) == 0)
    def _define_padding():
        zeros_ref[...] = jnp.zeros(zeros_ref.shape, zeros_ref.dtype)

        def block_copy(b):
            dst = buf_hbm.at[pl.ds(pl.multiple_of(b * ROW_BLK, ROW_BLK), ROW_BLK), :]
            return pltpu.make_async_copy(zeros_ref, dst, zero_sem)

        def for_padded_blocks(action):
            def per_expert(e, carry):
                @pl.when(last_blk_ref[e] >= 0)
                def _():
                    action(last_blk_ref[e])
                return carry

            def per_unused(b, carry):
                action(b)
                return carry

            lax.fori_loop(0, last_blk_ref.shape[0], per_expert, 0)
            lax.fori_loop(n_used_ref[0], n_blocks, per_unused, 0)

        for_padded_blocks(lambda b: block_copy(b).start())
        for_padded_blocks(lambda b: block_copy(b).wait())

    for r in range(t_rows):
        for k in range(TOP_K):
            pltpu.make_async_copy(rows_ref.at[pl.ds(r, 1), :],
                                  buf_hbm.at[pl.ds(pos_ref[0, k, r], 1), :], sem).start(priority=k)
    for _ in range(TOP_K):
        pltpu.make_async_copy(rows_ref, buf_hbm.at[pl.ds(0, t_rows), :], sem).wait()


def _scatter_call(pos3, rows, last_blk, n_used, n_rows):
    n_tok, width = rows.shape
    t = T_SCATTER
    grid_spec = pltpu.PrefetchScalarGridSpec(
        num_scalar_prefetch=2,
        grid=(n_tok // t,),
        in_specs=[
            pl.BlockSpec((1, TOP_K, t), lambda i, *_: (i, 0, 0), memory_space=pltpu.SMEM),
            pl.BlockSpec((t, width), lambda i, *_: (i, 0)),
        ],
        out_specs=pl.BlockSpec(memory_space=pl.ANY),
        scratch_shapes=[pltpu.VMEM((ROW_BLK, width), rows.dtype),
                        pltpu.SemaphoreType.DMA(()), pltpu.SemaphoreType.DMA(())],
    )
    return pl.pallas_call(
        _scatter_kernel,
        grid_spec=grid_spec,
        out_shape=jax.ShapeDtypeStruct((n_rows, width), rows.dtype),
        compiler_params=pltpu.CompilerParams(dimension_semantics=("arbitrary",)),
        name="moe_scatter",
    )(last_blk, n_used, pos3, rows)


def _expert_kernel(blk_e_ref, next_e_ref, n_used_ref, xs_hbm, wg_hbm, wu_hbm, wd_hbm, y_hbm,
                   wg_st, wu_st, wd_st, wg_bf, wu_bf, wd_bf, xbuf, ybuf, w_sem, in_sem, out_sem):
    n_used = n_used_ref[0]

    def rows(b):
        return pl.ds(pl.multiple_of(b * ROW_BLK, ROW_BLK), ROW_BLK)

    def load(b, slot):
        return pltpu.make_async_copy(xs_hbm.at[rows(b), :], xbuf.at[slot], in_sem.at[slot])

    def store(b, slot):
        return pltpu.make_async_copy(ybuf.at[slot], y_hbm.at[rows(b), :], out_sem.at[slot])

    def weight_copies(e, wslot):
        return [pltpu.make_async_copy(src.at[e], dst.at[wslot], w_sem.at[wslot, n])
                for n, (src, dst) in enumerate(((wg_hbm, wg_st), (wu_hbm, wu_st), (wd_hbm, wd_st)))]

    for copy in weight_copies(blk_e_ref[0], 0):
        copy.start(priority=1)
    load(0, 0).start()

    def body(b, wslot):
        e = blk_e_ref[b]
        new_expert = jnp.logical_or(b == 0, e != blk_e_ref[jnp.maximum(b - 1, 0)])
        slot = b % 2

        @pl.when(new_expert)
        def _switch_expert():
            for copy in weight_copies(e, wslot):
                copy.wait()
            wg_bf[...] = wg_st[wslot].astype(BF16)
            wu_bf[...] = wu_st[wslot].astype(BF16)
            wd_bf[...] = wd_st[wslot].astype(BF16)
            nxt = next_e_ref[b]

            @pl.when(nxt >= 0)
            def _prefetch_weights():
                for copy in weight_copies(nxt, 1 - wslot):
                    copy.start(priority=1)

        @pl.when(b + 1 < n_used)
        def _prefetch_rows():
            load(b + 1, 1 - slot).start()

        load(b, slot).wait()

        @pl.when(b >= 2)
        def _window_free():
            store(b - 2, slot).wait()

        xb = _unpack_bf16_pairs(xbuf[slot])
        a = _dot(xb, wg_bf[...])
        u = _dot(xb, wu_bf[...])
        hid = ((a * _sigmoid(a)) * u).astype(BF16)
        ybuf[slot] = _pack_bf16_pairs(_dot(hid, wd_bf[...]).astype(BF16))
        store(b, slot).start()
        return jnp.where(new_expert, 1 - wslot, wslot)

    lax.fori_loop(0, n_used, body, jnp.int32(0))

    @pl.when(n_used >= 2)
    def _drain_older():
        store(n_used - 2, n_used % 2).wait()

    store(n_used - 1, (n_used - 1) % 2).wait()

    ybuf[0] = jnp.zeros(ybuf.shape[1:], ybuf.dtype)

    def fill(b, carry):
        copy = pltpu.make_async_copy(ybuf.at[0], y_hbm.at[rows(b), :], out_sem.at[0])
        copy.start()
        copy.wait()
        return carry

    lax.fori_loop(n_used, y_hbm.shape[0] // ROW_BLK, fill, 0)


def _expert_call(xs, blk_e, next_e, n_used, w_gate, w_up, w_down):
    n_rows, half = xs.shape
    d = 2 * half
    d_e = w_gate.shape[-1]
    hbm = pl.BlockSpec(memory_space=pl.ANY)
    grid_spec = pltpu.PrefetchScalarGridSpec(
        num_scalar_prefetch=3,
        grid=(1,),
        in_specs=[hbm, hbm, hbm, hbm],
        out_specs=hbm,
        scratch_shapes=[
            pltpu.VMEM((2, d, d_e), F32), pltpu.VMEM((2, d, d_e), F32), pltpu.VMEM((2, d_e, d), F32),
            pltpu.VMEM((d, d_e), BF16), pltpu.VMEM((d, d_e), BF16), pltpu.VMEM((d_e, d), BF16),
            pltpu.VMEM((2, ROW_BLK, half), xs.dtype), pltpu.VMEM((2, ROW_BLK, half), xs.dtype),
            pltpu.SemaphoreType.DMA((2, 3)), pltpu.SemaphoreType.DMA((2,)), pltpu.SemaphoreType.DMA((2,)),
        ],
    )
    return pl.pallas_call(
        _expert_kernel,
        grid_spec=grid_spec,
        out_shape=jax.ShapeDtypeStruct((n_rows, half), xs.dtype),
        compiler_params=pltpu.CompilerParams(
            dimension_semantics=("arbitrary",), vmem_limit_bytes=VMEM_LIMIT_BYTES),
        name="moe_experts",
    )(blk_e, next_e, n_used, xs, w_gate, w_up, w_down)


def _row_gather_wait(src_hbm, dst, sem, n_rows):
    pltpu.make_async_copy(src_hbm.at[pl.ds(0, n_rows), :], dst, sem).wait()


def _combine_kernel(posc_ref, posn_ref, x1_ref, route_ref, mod_ref, gf_ref, y_hbm, o_ref, ybuf, sem,
                    *, n_steps, final_norm):
    i = pl.program_id(0)
    slot = i % 2
    t_rows = x1_ref.shape[0]

    def start(pos_ref, s):
        for k in range(TOP_K):
            for r in range(t_rows):
                pltpu.make_async_copy(y_hbm.at[pl.ds(pos_ref[0, k, r], 1), :],
                                      ybuf.at[s, k, pl.ds(r, 1), :], sem.at[s]).start(priority=r % 2)

    @pl.when(i == 0)
    def _prime():
        start(posc_ref, 0)

    @pl.when(i + 1 < n_steps)
    def _prefetch():
        start(posn_ref, 1 - slot)

    for k in range(TOP_K):
        _row_gather_wait(y_hbm, ybuf.at[slot, k], sem.at[slot], t_rows)
    rec = route_ref[...]
    y0 = _unpack_bf16_pairs(ybuf[slot, 0], F32)
    y1 = _unpack_bf16_pairs(ybuf[slot, 1], F32)
    moe = y0 * rec[:, REC_W:REC_W + 1] + y1 * rec[:, REC_W + 1:REC_W + 2]
    x2 = x1_ref[...] + mod_ref[0, 5:6, :] * moe
    if final_norm:
        x2 = (x2 * lax.rsqrt(jnp.mean(x2 * x2, axis=-1, keepdims=True) + RMS_EPS)) * gf_ref[...]
    o_ref[...] = x2


def _combine_call(x1, route, mod3, g_final, y_rows, pos3, *, seq_len, final_norm):
    n_tok, d = x1.shape
    t = T_COMBINE
    n_steps = n_tok // t
    tiles_per_seq = seq_len // t
    last = n_steps - 1
    return pl.pallas_call(
        functools.partial(_combine_kernel, n_steps=n_steps, final_norm=final_norm),
        grid=(n_steps,),
        in_specs=[
            pl.BlockSpec((1, TOP_K, t), lambda i: (i, 0, 0), memory_space=pltpu.SMEM),
            pl.BlockSpec((1, TOP_K, t), lambda i: (jnp.minimum(i + 1, last), 0, 0), memory_space=pltpu.SMEM),
            pl.BlockSpec((t, d), lambda i: (i, 0)),
            pl.BlockSpec((t, ROUTE_LANES), lambda i: (i, 0)),
            pl.BlockSpec((1,) + mod3.shape[1:], lambda i: (i // tiles_per_seq, 0, 0)),
            pl.BlockSpec((1, d), lambda i: (0, 0)),
            pl.BlockSpec(memory_space=pl.ANY),
        ],
        out_specs=pl.BlockSpec((t, d), lambda i: (i, 0)),
        out_shape=jax.ShapeDtypeStruct((n_tok, d), F32),
        scratch_shapes=[pltpu.VMEM((2, TOP_K, t, y_rows.shape[1]), y_rows.dtype),
                        pltpu.SemaphoreType.DMA((2,))],
        compiler_params=pltpu.CompilerParams(
            dimension_semantics=("arbitrary",), vmem_limit_bytes=VMEM_LIMIT_BYTES),
        name="moe_combine",
    )(pos3, pos3, x1, route, mod3, g_final, y_rows)


def kernel(x, c, w_ada, b_ada, g_norm1, w_in, conv_a, w_a_out, conv_b, b_conv_b, ln_conv_g, ln_conv_b,
           w_b_out, w_o, g_norm2, w_router_g, b_router_g, w_router_e, b_router_e, w_gate, w_up, w_down,
           g_final):
    bsz, seq_len, d = x.shape
    depth = w_ada.shape[0]
    n_tok = bsz * seq_len
    n_assign = n_tok * TOP_K
    assert seq_len % T_MIX == 0 and seq_len % T_COMBINE == 0 and d % (2 * LANES) == 0
    assert n_tok % T_SCATTER == 0
    assert n_tok % T_PLAN == 0
    assert n_assign % ROW_BLK == 0
    assert N_GROUPS + N_EXPERTS <= ROUTE_LANES
    n_rows = n_assign + N_EXPERTS * ROW_BLK

    c_pad = jnp.zeros((SUBLANES, d), F32).at[:bsz].set(c)
    xt = x.reshape(n_tok, d)
    row = lambda a: a.reshape(1, -1)
    for l in range(depth):
        mod = _mod_call(c_pad, w_ada[l], row(b_ada[l]))
        mod3 = mod[:bsz].reshape(bsz, 6, d)
        w_r = jnp.zeros((d, ROUTE_LANES), F32)
        w_r = w_r.at[:, :N_GROUPS].set(w_router_g[l]).at[:, N_GROUPS:N_GROUPS + N_EXPERTS].set(w_router_e[l])
        b_r = jnp.zeros((1, ROUTE_LANES), F32)
        b_r = b_r.at[0, :N_GROUPS].set(b_router_g[l]).at[0, N_GROUPS:N_GROUPS + N_EXPERTS].set(b_router_e[l])
        wr_hi, wr_lo = _split_bf16(w_r)
        wr_both = jnp.concatenate([wr_hi, wr_lo], axis=1)
        x1, h2p, route, cnt = _mixer_call(
            xt, mod3, row(g_norm1[l]), conv_a[l], conv_b[l], row(b_conv_b[l]), row(ln_conv_g[l]),
            row(ln_conv_b[l]), row(g_norm2[l]),
            w_in[l], w_a_out[l], w_b_out[l], w_o[l], wr_hi, wr_both, b_r, seq_len=seq_len)
        pstart_row, blk_e, next_e, last_blk, n_used = _segment_layout(cnt[0], n_rows // ROW_BLK)
        pos = _plan_call(route, pstart_row)
        pos_tiles = lambda t: pos[:, :TOP_K].reshape(n_tok // t, t, TOP_K).transpose(0, 2, 1)
        xs = _scatter_call(pos_tiles(T_SCATTER), h2p, last_blk, n_used, n_rows)
        y_rows = _expert_call(xs, blk_e, next_e, n_used, w_gate[l], w_up[l], w_down[l])
        xt = _combine_call(x1, route, mod3, row(g_final), y_rows, pos_tiles(T_COMBINE), seq_len=seq_len,
                           final_norm=(l == depth - 1))
    return xt.reshape(bsz, seq_len, d)
```

```python
import functools

import jax
import jax.numpy as jnp
from jax import lax
from jax.experimental import pallas as pl
from jax.experimental.pallas import tpu as pltpu

F32 = jnp.float32
BF16 = jnp.bfloat16
U32 = jnp.uint32

N_GROUPS = 4
EXPERTS_PER_GROUP = 8
N_EXPERTS = N_GROUPS * EXPERTS_PER_GROUP
TOP_K = 2
K_SHORT = 3
K_CONF = 31
RMS_EPS = 1e-6
LN_EPS = 1e-5

LANES = 128
SUBLANES = 8
T_MIX = 256
HALO_B = 16
HALO_A = 8
CONV_ROWS = 16
CONV_BLOCKS_LIVE = 15
PAIR_LAG = 1
ROUTE_AFTER_GROUP = 1
ROW_BLK = 512
T_SCATTER = 1024
T_COMBINE = 512
T_PLAN = 2048
W_STAGE_COLS = 512
ROUTE_LANES = LANES
REC_E, REC_W, REC_RANK = 0, 2, 4
VMEM_LIMIT_BYTES = 56 * 1024 * 1024


def _sigmoid(v):
    return 1.0 / (1.0 + jnp.exp(-v))


def _split_bf16(v):
    hi = v.astype(BF16)
    lo = (v - hi.astype(F32)).astype(BF16)
    return hi, lo


def _dot(a, b):
    return jnp.dot(a, b, preferred_element_type=F32)


def _const_spec(shape):
    nd = len(shape)
    return pl.BlockSpec(shape, lambda *_: (0,) * nd, pipeline_mode=pl.Buffered(1))


def _lane_ids(shape):
    return lax.broadcasted_iota(jnp.int32, shape, 1).astype(F32)


def _mod_kernel(c_ref, w_ref, b_ref, o_ref):
    c = c_ref[...]
    a_hi, a_lo = _split_bf16(c * _sigmoid(c))
    w_hi, w_lo = _split_bf16(w_ref[...])
    o_ref[...] = _dot(a_hi, w_hi) + _dot(a_lo, w_hi) + _dot(a_hi, w_lo) + b_ref[...]


def _mod_call(c_pad, w_ada, b_ada):
    rows, d = c_pad.shape
    n_out = w_ada.shape[1]
    blk = 1024
    return pl.pallas_call(
        _mod_kernel,
        grid=(n_out // blk,),
        in_specs=[
            pl.BlockSpec((rows, d), lambda j: (0, 0)),
            pl.BlockSpec((d, blk), lambda j: (0, j)),
            pl.BlockSpec((1, blk), lambda j: (0, j)),
        ],
        out_specs=pl.BlockSpec((rows, blk), lambda j: (0, j)),
        out_shape=jax.ShapeDtypeStruct((rows, n_out), F32),
        compiler_params=pltpu.CompilerParams(dimension_semantics=("arbitrary",)),
        name="adaln_mod",
    )(c_pad, w_ada, b_ada)


def _zero_bits_of(v):
    return (lax.bitcast_convert_type(v, U32) >> 16) >> 16


def _depthwise_lane_block(buf_ref, w_ref, out_ref, phase_ref, n_taps, row0, rows, cb, after=None):
    first, stop = rows
    max_off = row0 + n_taps - 1
    n_keep = stop - first + (max_off // SUBLANES) * SUBLANES
    cols = slice(cb * LANES, (cb + 1) * LANES)
    for s in sorted({(row0 + k) % SUBLANES for k in range(n_taps)}):
        phase_ref[s, pl.ds(0, n_keep), :] = buf_ref[pl.ds(first + s, n_keep), cols]
    all_blocks = list(range(first, stop, CONV_ROWS))
    for g0 in range(0, len(all_blocks), CONV_BLOCKS_LIVE):
        row_blocks = all_blocks[g0:g0 + CONV_BLOCKS_LIVE]
        acc = [None] * len(row_blocks)
        for k in range(n_taps):
            q, s = divmod(row0 + k, SUBLANES)
            w_row = w_ref[k:k + 1, cols]
            if after is not None:
                w_row = lax.bitcast_convert_type(lax.bitcast_convert_type(w_row, U32) | after, F32)
            w_k = jnp.broadcast_to(w_row, (CONV_ROWS, LANES))
            for j, r0 in enumerate(row_blocks):
                term = phase_ref[s, pl.ds(r0 - first + q * SUBLANES, CONV_ROWS), :] * w_k
                acc[j] = term if acc[j] is None else acc[j] + term
        for j, r0 in enumerate(row_blocks):
            out_ref[pl.ds(r0, CONV_ROWS), cols] = acc[j]
    return _zero_bits_of(acc[-1][0:1, :])


def _route(logits, run_ref, live):
    shape = logits.shape
    lane = _lane_ids(shape)
    big = float(4 * ROUTE_LANES)
    neg_inf = -jnp.inf

    def first_argmax(v):
        m = jnp.max(v, axis=-1, keepdims=True)
        return m, jnp.min(jnp.where(v == m, lane, big), axis=-1, keepdims=True)

    is_group = lane < N_GROUPS
    g_max, g_idx = first_argmax(jnp.where(is_group, logits, neg_inf))
    g_prob = 1.0 / jnp.sum(jnp.where(is_group, jnp.exp(logits - g_max), 0.0), axis=-1, keepdims=True)
    lo = N_GROUPS + EXPERTS_PER_GROUP * g_idx
    in_group = (lane >= lo) & (lane < lo + EXPERTS_PER_GROUP)
    e_logits = jnp.where(in_group, logits, neg_inf)
    m1, i1 = first_argmax(e_logits)
    m2, i2 = first_argmax(jnp.where(lane == i1, neg_inf, e_logits))
    t = jnp.exp(m2 - m1)
    p1 = 1.0 / (1.0 + t)
    p2 = t / (1.0 + t)
    e1 = i1 - N_GROUPS
    e2 = i2 - N_GROUPS

    hot1 = lane == e1
    hot2 = lane == e2
    both = jnp.where(hot1 | hot2, 1.0, 0.0)
    n_rows = shape[0]
    earlier = (lax.broadcasted_iota(jnp.int32, (n_rows, n_rows), 0)
               > lax.broadcasted_iota(jnp.int32, (n_rows, n_rows), 1))
    before = _dot(jnp.where(earlier, 1.0, 0.0).astype(BF16), both.astype(BF16)) + run_ref[0:1, :]
    rank1 = jnp.sum(jnp.where(hot1, before, 0.0), axis=-1, keepdims=True)
    rank2 = jnp.sum(jnp.where(hot2, before, 0.0), axis=-1, keepdims=True)
    run_ref[0:1, :] = run_ref[0:1, :] + live * jnp.sum(both, axis=0, keepdims=True)

    rec = jnp.where(lane == REC_E, e1, 0.0)
    rec = jnp.where(lane == REC_E + 1, e2, rec)
    rec = jnp.where(lane == REC_W, g_prob * p1, rec)
    rec = jnp.where(lane == REC_W + 1, g_prob * p2, rec)
    rec = jnp.where(lane == REC_RANK, rank1, rec)
    rec = jnp.where(lane == REC_RANK + 1, rank2, rec)
    return rec


def _pack_bf16_pairs(v_bf16):
    half = v_bf16.shape[1] // 2
    lo = lax.bitcast_convert_type(v_bf16[:, :half].astype(F32), U32)
    hi = lax.bitcast_convert_type(v_bf16[:, half:].astype(F32), U32)
    return (lo >> 16) | hi


def _unpack_bf16_pairs(p_u32, dtype=BF16):
    lo = lax.bitcast_convert_type(p_u32 << 16, F32).astype(dtype)
    hi = lax.bitcast_convert_type(p_u32 & jnp.uint32(0xFFFF0000), F32).astype(dtype)
    return jnp.concatenate([lo, hi], axis=1)


def _stage_weight_bf16(src_hbm, dst_ref, stage_ref, sem):
    chunk = stage_ref.shape[2]
    n_chunks = src_hbm.shape[1] // chunk

    def cols(j):
        return pl.ds(pl.multiple_of(j * chunk, chunk), chunk)

    def copy(j, slot):
        return pltpu.make_async_copy(src_hbm.at[:, cols(j)], stage_ref.at[slot], sem.at[slot])

    copy(0, 0).start()

    def body(j, carry):
        slot = j % 2

        @pl.when(j + 1 < n_chunks)
        def _next():
            copy(j + 1, 1 - slot).start()

        copy(j, slot).wait()
        dst_ref[:, cols(j)] = stage_ref[slot].astype(BF16)
        return carry

    lax.fori_loop(0, n_chunks, body, 0)


def _mixer_kernel(x_ref, modn_ref, modp_ref, g1_ref, ca_ref, cb_ref, bcb_ref, lng_ref, lnb_ref, g2_ref,
                  wrh_ref, wrl_ref, br_ref, win_hbm, wa_hbm, wb_hbm, wo_hbm,
                  x1_ref, h2p_ref, route_ref, cnt_ref,
                  wu_ref, wcx_ref, ba_ref, sga_ref, sgb_ref, xs_ref, cva_ref, cvb_ref, run_ref, logit_ref,
                  phase_ref, win_ref, wa_ref, wb_ref, wo_ref, stage_ref, stage_sem,
                  *, n_tiles):
    i = pl.program_id(0)
    t_rows, d = x_ref.shape

    @pl.when(i == 0)
    def _init():
        for ref in (wu_ref, wcx_ref, ba_ref, sga_ref, sgb_ref, xs_ref, run_ref, logit_ref):
            ref[...] = jnp.zeros(ref.shape, ref.dtype)
        for src, dst in ((win_hbm, win_ref), (wa_hbm, wa_ref), (wb_hbm, wb_ref), (wo_hbm, wo_ref)):
            _stage_weight_bf16(src, dst, stage_ref, stage_sem)

    x = x_ref[...]
    sh1 = modn_ref[0, 0:1, :]
    sc1 = modn_ref[0, 1:2, :]
    h = (x * lax.rsqrt(jnp.mean(x * x, axis=-1, keepdims=True) + RMS_EPS)) * g1_ref[...]
    hb = (h * (1.0 + sc1) + sh1).astype(BF16)

    def proj(g, after):
        lhs = hb
        if after is not None:
            bits = pltpu.bitcast(hb, U32)
            lhs = pltpu.bitcast(bits | jnp.broadcast_to(after[0:1, 0:1], bits.shape), BF16)
        return _dot(lhs, win_ref[:, g * d:(g + 1) * d])

    def conv_work(rows):
        items = []
        for cb in range(d // LANES):
            items.append(functools.partial(_depthwise_lane_block, wcx_ref, ca_ref, cva_ref,
                                           phase_ref.at[2], K_SHORT, HALO_A - K_SHORT // 2, rows, cb))
            items.append(functools.partial(_depthwise_lane_block, wu_ref, cb_ref, cvb_ref,
                                           phase_ref.at[cb % 2], K_CONF, HALO_B - K_CONF // 2, rows, cb))
        return items

    early = conv_work((0, t_rows - CONV_ROWS))
    n_proj = 7
    z = []
    conv_done = None
    for g in range(n_proj):
        z.append(proj(g, conv_done))
        released = _zero_bits_of(z[g - PAIR_LAG][0:1, 0:LANES]) if g >= PAIR_LAG else None
        for item in early[g * len(early) // n_proj:(g + 1) * len(early) // n_proj]:
            conv_done = item(after=released)
        if g == ROUTE_AFTER_GROUP:
            route_ref[...] = _route(logit_ref[...], run_ref, (i > 1).astype(F32))
            cnt_ref[...] = jnp.broadcast_to(run_ref[0:1, :], cnt_ref.shape)
    b_a = z[0]
    cx = z[1] * z[2]
    u = z[3] * _sigmoid(z[4])
    sg_a = _sigmoid(z[5])
    sg_b = _sigmoid(z[6])

    same_seq = ((i % n_tiles) != 0).astype(F32)
    wu_ref[pl.ds(HALO_B + t_rows, HALO_B), :] = u[0:HALO_B] * same_seq
    wcx_ref[pl.ds(HALO_A + t_rows, HALO_A), :] = cx[0:HALO_A] * same_seq

    for item in conv_work((t_rows - CONV_ROWS, t_rows)):
        item()
    y_a = _dot((ba_ref[...] * cva_ref[...]).astype(BF16), wa_ref[...])
    v = cvb_ref[...] + bcb_ref[...]
    mu = jnp.mean(v, axis=-1, keepdims=True)
    vc = v - mu
    var = jnp.mean(vc * vc, axis=-1, keepdims=True)
    v = (vc * lax.rsqrt(var + LN_EPS)) * lng_ref[...] + lnb_ref[...]
    y_b = _dot((v * _sigmoid(v)).astype(BF16), wb_ref[...])
    merged = sga_ref[...] * y_a + sgb_ref[...] * y_b
    mix = _dot(merged.astype(BF16), wo_ref[...])

    gt1 = modp_ref[0, 2:3, :]
    sh2 = modp_ref[0, 3:4, :]
    sc2 = modp_ref[0, 4:5, :]
    x1 = xs_ref[...] + gt1 * mix
    h2 = (x1 * lax.rsqrt(jnp.mean(x1 * x1, axis=-1, keepdims=True) + RMS_EPS)) * g2_ref[...]
    h2 = h2 * (1.0 + sc2) + sh2
    h2_hi, h2_lo = _split_bf16(h2)
    h2p = _pack_bf16_pairs(h2_hi)
    hi_part = _dot(h2_hi, wrl_ref[...])
    logit_ref[...] = (hi_part[:, :ROUTE_LANES] + _dot(h2_lo, wrh_ref[...]) + hi_part[:, ROUTE_LANES:]
                      + br_ref[...])

    @pl.when(i < pl.num_programs(0) - 1)
    def _emit_rows():
        x1_ref[...] = x1
        h2p_ref[...] = h2p

    tail_u = wu_ref[pl.ds(t_rows, HALO_B), :]
    wu_ref[pl.ds(0, HALO_B), :] = tail_u * same_seq
    wu_ref[pl.ds(HALO_B, t_rows), :] = u
    tail_cx = wcx_ref[pl.ds(t_rows, HALO_A), :]
    wcx_ref[pl.ds(0, HALO_A), :] = tail_cx * same_seq
    wcx_ref[pl.ds(HALO_A, t_rows), :] = cx
    ba_ref[...] = b_a
    sga_ref[...] = sg_a
    sgb_ref[...] = sg_b
    xs_ref[...] = x


def _mixer_call(x2d, mod3, g1, conv_a, conv_b, b_conv_b, ln_g, ln_b, g2,
                w_in, w_a, w_b, w_o, wr_hi, wr_lo, b_r, *, seq_len):
    n_tok, d = x2d.shape
    t = T_MIX
    n_tiles = seq_len // t
    n_chunks = n_tok // t
    last = n_chunks - 1

    def cur(i):
        return jnp.minimum(i, last)

    def prev(i):
        return jnp.clip(i - 1, 0, last)

    def prev2(i):
        return jnp.clip(i - 2, 0, last)

    row_spec_prev = lambda width: pl.BlockSpec((t, width), lambda i: (prev(i), 0))
    vec = lambda a: _const_spec(a.shape)
    hbm = pl.BlockSpec(memory_space=pl.ANY)
    in_specs = [
        pl.BlockSpec((t, d), lambda i: (cur(i), 0)),
        pl.BlockSpec((1,) + mod3.shape[1:], lambda i: (cur(i) // n_tiles, 0, 0)),
        pl.BlockSpec((1,) + mod3.shape[1:], lambda i: (prev(i) // n_tiles, 0, 0)),
        vec(g1), vec(conv_a), vec(conv_b), vec(b_conv_b), vec(ln_g), vec(ln_b), vec(g2),
        vec(wr_hi), vec(wr_lo), vec(b_r),
        hbm, hbm, hbm, hbm,
    ]
    out_specs = [row_spec_prev(d), row_spec_prev(d // 2),
                 pl.BlockSpec((t, ROUTE_LANES), lambda i: (prev2(i), 0)),
                 pl.BlockSpec((SUBLANES, ROUTE_LANES), lambda i: (0, 0))]
    out_shape = [
        jax.ShapeDtypeStruct((n_tok, d), F32),
        jax.ShapeDtypeStruct((n_tok, d // 2), U32),
        jax.ShapeDtypeStruct((n_tok, ROUTE_LANES), F32),
        jax.ShapeDtypeStruct((SUBLANES, ROUTE_LANES), F32),
    ]
    scratch = [
        pltpu.VMEM((t + 2 * HALO_B, d), F32),
        pltpu.VMEM((t + 2 * HALO_A, d), F32),
        pltpu.VMEM((t, d), F32), pltpu.VMEM((t, d), F32), pltpu.VMEM((t, d), F32), pltpu.VMEM((t, d), F32),
        pltpu.VMEM((t, d), F32), pltpu.VMEM((t, d), F32),
        pltpu.VMEM((SUBLANES, ROUTE_LANES), F32),
        pltpu.VMEM((t, ROUTE_LANES), F32),
        pltpu.VMEM((3, SUBLANES, t + 2 * HALO_B, LANES), F32),
        pltpu.VMEM(w_in.shape, BF16), pltpu.VMEM(w_a.shape, BF16), pltpu.VMEM(w_b.shape, BF16),
        pltpu.VMEM(w_o.shape, BF16),
        pltpu.VMEM((2, d, W_STAGE_COLS), F32), pltpu.SemaphoreType.DMA((2,)),
    ]
    return pl.pallas_call(
        functools.partial(_mixer_kernel, n_tiles=n_tiles),
        grid=(n_chunks + 2,),
        in_specs=in_specs,
        out_specs=out_specs,
        out_shape=out_shape,
        scratch_shapes=scratch,
        compiler_params=pltpu.CompilerParams(
            dimension_semantics=("arbitrary",), vmem_limit_bytes=VMEM_LIMIT_BYTES),
        name="mixer_router",
    )(x2d, mod3, mod3, g1, conv_a, conv_b, b_conv_b, ln_g, ln_b, g2,
      wr_hi, wr_lo, b_r, w_in, w_a, w_b, w_o)


def _segment_layout(counts_row, n_blocks):
    counts = counts_row[:N_EXPERTS].astype(jnp.int32)
    nblk = (counts + ROW_BLK - 1) // ROW_BLK
    blk_end = jnp.cumsum(nblk)
    blk0 = blk_end - nblk
    n_used = blk_end[-1:].astype(jnp.int32)
    pstart_row = jnp.zeros((1, ROUTE_LANES), F32).at[0, :N_EXPERTS].set((blk0 * ROW_BLK).astype(F32))
    blk = jnp.arange(n_blocks, dtype=jnp.int32)[:, None]
    blk_e = jnp.minimum(jnp.sum(blk_end[None, :] <= blk, axis=1), N_EXPERTS - 1).astype(jnp.int32)
    seg_end = jnp.min(jnp.where(blk_end[None, :] > blk, blk_end[None, :], n_blocks), axis=1)
    next_e = jnp.sum(blk_end[None, :] <= seg_end[:, None], axis=1).astype(jnp.int32)
    next_e = jnp.where(seg_end >= n_used[0], -1, next_e)
    last_blk = jnp.where(nblk > 0, blk_end - 1, -1).astype(jnp.int32)
    return pstart_row, blk_e, next_e, last_blk, n_used


def _plan_kernel(route_ref, pstart_ref, pos_ref):
    rec = route_ref[...]
    lane = _lane_ids(rec.shape)
    pos = jnp.zeros(rec.shape, F32)
    for k in range(TOP_K):
        e = rec[:, REC_E + k:REC_E + k + 1]
        seg = jnp.sum(jnp.where(lane == e, pstart_ref[...], 0.0), axis=-1, keepdims=True)
        pos = jnp.where(lane == k, seg + rec[:, REC_RANK + k:REC_RANK + k + 1], pos)
    pos_ref[...] = pos.astype(jnp.int32)


def _plan_call(route, pstart_row):
    n_tok = route.shape[0]
    t = T_PLAN
    return pl.pallas_call(
        _plan_kernel,
        grid=(n_tok // t,),
        in_specs=[pl.BlockSpec((t, ROUTE_LANES), lambda i: (i, 0)),
                  pl.BlockSpec((1, ROUTE_LANES), lambda i: (0, 0))],
        out_specs=pl.BlockSpec((t, ROUTE_LANES), lambda i: (i, 0)),
        out_shape=jax.ShapeDtypeStruct((n_tok, ROUTE_LANES), jnp.int32),
        compiler_params=pltpu.CompilerParams(dimension_semantics=("arbitrary",)),
        name="moe_plan",
    )(route, pstart_row)


def _scatter_kernel(last_blk_ref, n_used_ref, pos_ref, rows_ref, buf_hbm, zeros_ref, sem, zero_sem):
    t_rows = rows_ref.shape[0]
    n_blocks = buf_hbm.shape[0] // ROW_BLK

    @pl.when(pl.program_id(0) == 0)
    def _define_padding():
        zeros_ref[...] = jnp.zeros(zeros_ref.shape, zeros_ref.dtype)

        def block_copy(b):
            dst = buf_hbm.at[pl.ds(pl.multiple_of(b * ROW_BLK, ROW_BLK), ROW_BLK), :]
            return pltpu.make_async_copy(zeros_ref, dst, zero_sem)

        def for_padded_blocks(action):
            def per_expert(e, carry):
                @pl.when(last_blk_ref[e] >= 0)
                def _():
                    action(last_blk_ref[e])
                return carry

            def per_unused(b, carry):
                action(b)
                return carry

            lax.fori_loop(0, last_blk_ref.shape[0], per_expert, 0)
            lax.fori_loop(n_used_ref[0], n_blocks, per_unused, 0)

        for_padded_blocks(lambda b: block_copy(b).start())
        for_padded_blocks(lambda b: block_copy(b).wait())

    for r in range(t_rows):
        for k in range(TOP_K):
            pltpu.make_async_copy(rows_ref.at[pl.ds(r, 1), :],
                                  buf_hbm.at[pl.ds(pos_ref[0, k, r], 1), :], sem).start(priority=k)
    for _ in range(TOP_K):
        pltpu.make_async_copy(rows_ref, buf_hbm.at[pl.ds(0, t_rows), :], sem).wait()


def _scatter_call(pos3, rows, last_blk, n_used, n_rows):
    n_tok, width = rows.shape
    t = T_SCATTER
    grid_spec = pltpu.PrefetchScalarGridSpec(
        num_scalar_prefetch=2,
        grid=(n_tok // t,),
        in_specs=[
            pl.BlockSpec((1, TOP_K, t), lambda i, *_: (i, 0, 0), memory_space=pltpu.SMEM),
            pl.BlockSpec((t, width), lambda i, *_: (i, 0)),
        ],
        out_specs=pl.BlockSpec(memory_space=pl.ANY),
        scratch_shapes=[pltpu.VMEM((ROW_BLK, width), rows.dtype),
                        pltpu.SemaphoreType.DMA(()), pltpu.SemaphoreType.DMA(())],
    )
    return pl.pallas_call(
        _scatter_kernel,
        grid_spec=grid_spec,
        out_shape=jax.ShapeDtypeStruct((n_rows, width), rows.dtype),
        compiler_params=pltpu.CompilerParams(dimension_semantics=("arbitrary",)),
        name="moe_scatter",
    )(last_blk, n_used, pos3, rows)


def _expert_kernel(blk_e_ref, next_e_ref, n_used_ref, xs_hbm, wg_hbm, wu_hbm, wd_hbm, y_hbm,
                   wg_st, wu_st, wd_st, wg_bf, wu_bf, wd_bf, xbuf, ybuf, w_sem, in_sem, out_sem):
    n_used = n_used_ref[0]

    def rows(b):
        return pl.ds(pl.multiple_of(b * ROW_BLK, ROW_BLK), ROW_BLK)

    def load(b, slot):
        return pltpu.make_async_copy(xs_hbm.at[rows(b), :], xbuf.at[slot], in_sem.at[slot])

    def store(b, slot):
        return pltpu.make_async_copy(ybuf.at[slot], y_hbm.at[rows(b), :], out_sem.at[slot])

    def weight_copies(e, wslot):
        return [pltpu.make_async_copy(src.at[e], dst.at[wslot], w_sem.at[wslot, n])
                for n, (src, dst) in enumerate(((wg_hbm, wg_st), (wu_hbm, wu_st), (wd_hbm, wd_st)))]

    for copy in weight_copies(blk_e_ref[0], 0):
        copy.start(priority=1)
    load(0, 0).start()

    def body(b, wslot):
        e = blk_e_ref[b]
        new_expert = jnp.logical_or(b == 0, e != blk_e_ref[jnp.maximum(b - 1, 0)])
        slot = b % 2

        @pl.when(new_expert)
        def _switch_expert():
            for copy in weight_copies(e, wslot):
                copy.wait()
            wg_bf[...] = wg_st[wslot].astype(BF16)
            wu_bf[...] = wu_st[wslot].astype(BF16)
            wd_bf[...] = wd_st[wslot].astype(BF16)
            nxt = next_e_ref[b]

            @pl.when(nxt >= 0)
            def _prefetch_weights():
                for copy in weight_copies(nxt, 1 - wslot):
                    copy.start(priority=1)

        @pl.when(b + 1 < n_used)
        def _prefetch_rows():
            load(b + 1, 1 - slot).start()

        load(b, slot).wait()

        @pl.when(b >= 2)
        def _window_free():
            store(b - 2, slot).wait()

        xb = _unpack_bf16_pairs(xbuf[slot])
        a = _dot(xb, wg_bf[...])
        u = _dot(xb, wu_bf[...])
        hid = ((a * _sigmoid(a)) * u).astype(BF16)
        ybuf[slot] = _pack_bf16_pairs(_dot(hid, wd_bf[...]).astype(BF16))
        store(b, slot).start()
        return jnp.where(new_expert, 1 - wslot, wslot)

    lax.fori_loop(0, n_used, body, jnp.int32(0))

    @pl.when(n_used >= 2)
    def _drain_older():
        store(n_used - 2, n_used % 2).wait()

    store(n_used - 1, (n_used - 1) % 2).wait()

    ybuf[0] = jnp.zeros(ybuf.shape[1:], ybuf.dtype)

    def fill(b):
        return pltpu.make_async_copy(ybuf.at[0], y_hbm.at[rows(b), :], out_sem.at[0])

    def start_fill(b, carry):
        fill(b).start()
        return carry

    def wait_fill(b, carry):
        fill(b).wait()
        return carry

    n_blocks = y_hbm.shape[0] // ROW_BLK
    lax.fori_loop(n_used, n_blocks, start_fill, 0)
    lax.fori_loop(n_used, n_blocks, wait_fill, 0)


def _expert_call(xs, blk_e, next_e, n_used, w_gate, w_up, w_down):
    n_rows, half = xs.shape
    d = 2 * half
    d_e = w_gate.shape[-1]
    hbm = pl.BlockSpec(memory_space=pl.ANY)
    grid_spec = pltpu.PrefetchScalarGridSpec(
        num_scalar_prefetch=3,
        grid=(1,),
        in_specs=[hbm, hbm, hbm, hbm],
        out_specs=hbm,
        scratch_shapes=[
            pltpu.VMEM((2, d, d_e), F32), pltpu.VMEM((2, d, d_e), F32), pltpu.VMEM((2, d_e, d), F32),
            pltpu.VMEM((d, d_e), BF16), pltpu.VMEM((d, d_e), BF16), pltpu.VMEM((d_e, d), BF16),
            pltpu.VMEM((2, ROW_BLK, half), xs.dtype), pltpu.VMEM((2, ROW_BLK, half), xs.dtype),
            pltpu.SemaphoreType.DMA((2, 3)), pltpu.SemaphoreType.DMA((2,)), pltpu.SemaphoreType.DMA((2,)),
        ],
    )
    return pl.pallas_call(
        _expert_kernel,
        grid_spec=grid_spec,
        out_shape=jax.ShapeDtypeStruct((n_rows, half), xs.dtype),
        compiler_params=pltpu.CompilerParams(
            dimension_semantics=("arbitrary",), vmem_limit_bytes=VMEM_LIMIT_BYTES),
        name="moe_experts",
    )(blk_e, next_e, n_used, xs, w_gate, w_up, w_down)


def _row_gather_wait(src_hbm, dst, sem, n_rows):
    pltpu.make_async_copy(src_hbm.at[pl.ds(0, n_rows), :], dst, sem).wait()


def _combine_kernel(posc_ref, posn_ref, x1_ref, route_ref, mod_ref, gf_ref, y_hbm, o_ref, ybuf, sem,
                    *, n_steps, final_norm):
    i = pl.program_id(0)
    slot = i % 2
    t_rows = x1_ref.shape[0]

    def start(pos_ref, s):
        for k in range(TOP_K):
            for r in range(t_rows):
                pltpu.make_async_copy(y_hbm.at[pl.ds(pos_ref[0, k, r], 1), :],
                                      ybuf.at[s, k, pl.ds(r, 1), :], sem.at[s]).start(priority=r % 2)

    @pl.when(i == 0)
    def _prime():
        start(posc_ref, 0)

    @pl.when(i + 1 < n_steps)
    def _prefetch():
        start(posn_ref, 1 - slot)

    for k in range(TOP_K):
        _row_gather_wait(y_hbm, ybuf.at[slot, k], sem.at[slot], t_rows)
    rec = route_ref[...]
    y0 = _unpack_bf16_pairs(ybuf[slot, 0], F32)
    y1 = _unpack_bf16_pairs(ybuf[slot, 1], F32)
    moe = y0 * rec[:, REC_W:REC_W + 1] + y1 * rec[:, REC_W + 1:REC_W + 2]
    x2 = x1_ref[...] + mod_ref[0, 5:6, :] * moe
    if final_norm:
        x2 = (x2 * lax.rsqrt(jnp.mean(x2 * x2, axis=-1, keepdims=True) + RMS_EPS)) * gf_ref[...]
    o_ref[...] = x2


def _combine_call(x1, route, mod3, g_final, y_rows, pos3, *, seq_len, final_norm):
    n_tok, d = x1.shape
    t = T_COMBINE
    n_steps = n_tok // t
    tiles_per_seq = seq_len // t
    last = n_steps - 1
    return pl.pallas_call(
        functools.partial(_combine_kernel, n_steps=n_steps, final_norm=final_norm),
        grid=(n_steps,),
        in_specs=[
            pl.BlockSpec((1, TOP_K, t), lambda i: (i, 0, 0), memory_space=pltpu.SMEM),
            pl.BlockSpec((1, TOP_K, t), lambda i: (jnp.minimum(i + 1, last), 0, 0), memory_space=pltpu.SMEM),
            pl.BlockSpec((t, d), lambda i: (i, 0)),
            pl.BlockSpec((t, ROUTE_LANES), lambda i: (i, 0)),
            pl.BlockSpec((1,) + mod3.shape[1:], lambda i: (i // tiles_per_seq, 0, 0)),
            pl.BlockSpec((1, d), lambda i: (0, 0)),
            pl.BlockSpec(memory_space=pl.ANY),
        ],
        out_specs=pl.BlockSpec((t, d), lambda i: (i, 0)),
        out_shape=jax.ShapeDtypeStruct((n_tok, d), F32),
        scratch_shapes=[pltpu.VMEM((2, TOP_K, t, y_rows.shape[1]), y_rows.dtype),
                        pltpu.SemaphoreType.DMA((2,))],
        compiler_params=pltpu.CompilerParams(
            dimension_semantics=("arbitrary",), vmem_limit_bytes=VMEM_LIMIT_BYTES),
        name="moe_combine",
    )(pos3, pos3, x1, route, mod3, g_final, y_rows)


def kernel(x, c, w_ada, b_ada, g_norm1, w_in, conv_a, w_a_out, conv_b, b_conv_b, ln_conv_g, ln_conv_b,
           w_b_out, w_o, g_norm2, w_router_g, b_router_g, w_router_e, b_router_e, w_gate, w_up, w_down,
           g_final):
    bsz, seq_len, d = x.shape
    depth = w_ada.shape[0]
    n_tok = bsz * seq_len
    n_assign = n_tok * TOP_K
    assert seq_len % T_MIX == 0 and seq_len % T_COMBINE == 0 and d % (2 * LANES) == 0
    assert n_tok % T_SCATTER == 0
    assert n_tok % T_PLAN == 0
    assert n_assign % ROW_BLK == 0
    assert N_GROUPS + N_EXPERTS <= ROUTE_LANES
    n_rows = n_assign + N_EXPERTS * ROW_BLK

    c_pad = jnp.zeros((SUBLANES, d), F32).at[:bsz].set(c)
    xt = x.reshape(n_tok, d)
    row = lambda a: a.reshape(1, -1)
    for l in range(depth):
        mod = _mod_call(c_pad, w_ada[l], row(b_ada[l]))
        mod3 = mod[:bsz].reshape(bsz, 6, d)
        w_r = jnp.zeros((d, ROUTE_LANES), F32)
        w_r = w_r.at[:, :N_GROUPS].set(w_router_g[l]).at[:, N_GROUPS:N_GROUPS + N_EXPERTS].set(w_router_e[l])
        b_r = jnp.zeros((1, ROUTE_LANES), F32)
        b_r = b_r.at[0, :N_GROUPS].set(b_router_g[l]).at[0, N_GROUPS:N_GROUPS + N_EXPERTS].set(b_router_e[l])
        wr_hi, wr_lo = _split_bf16(w_r)
        wr_both = jnp.concatenate([wr_hi, wr_lo], axis=1)
        x1, h2p, route, cnt = _mixer_call(
            xt, mod3, row(g_norm1[l]), conv_a[l], conv_b[l], row(b_conv_b[l]), row(ln_conv_g[l]),
            row(ln_conv_b[l]), row(g_norm2[l]),
            w_in[l], w_a_out[l], w_b_out[l], w_o[l], wr_hi, wr_both, b_r, seq_len=seq_len)
        pstart_row, blk_e, next_e, last_blk, n_used = _segment_layout(cnt[0], n_rows // ROW_BLK)
        pos = _plan_call(route, pstart_row)
        pos_tiles = lambda t: pos[:, :TOP_K].reshape(n_tok // t, t, TOP_K).transpose(0, 2, 1)
        xs = _scatter_call(pos_tiles(T_SCATTER), h2p, last_blk, n_used, n_rows)
        y_rows = _expert_call(xs, blk_e, next_e, n_used, w_gate[l], w_up[l], w_down[l])
        xt = _combine_call(x1, route, mod3, row(g_final), y_rows, pos_tiles(T_COMBINE), seq_len=seq_len,
                           final_norm=(l == depth - 1))
    return xt.reshape(bsz, seq_len, d)
```

```python
import functools

import jax
import jax.numpy as jnp
from jax import lax
from jax.experimental import pallas as pl
from jax.experimental.pallas import tpu as pltpu

F32 = jnp.float32
BF16 = jnp.bfloat16
U32 = jnp.uint32

N_GROUPS = 4
EXPERTS_PER_GROUP = 8
N_EXPERTS = N_GROUPS * EXPERTS_PER_GROUP
TOP_K = 2
K_SHORT = 3
K_CONF = 31
RMS_EPS = 1e-6
LN_EPS = 1e-5

LANES = 128
SUBLANES = 8
T_MIX = 256
HALO_B = 16
HALO_A = 8
CONV_ROWS = 16
CONV_BLOCKS_LIVE = 15
PAIR_LAG = 1
ROUTE_AFTER_GROUP = 1
ROW_BLK = 512
T_SCATTER = 1024
T_COMBINE = 512
T_PLAN = 2048
W_STAGE_COLS = 512
ROUTE_LANES = LANES
REC_E, REC_W, REC_RANK = 0, 2, 4
VMEM_LIMIT_BYTES = 56 * 1024 * 1024


def _sigmoid(v):
    return 1.0 / (1.0 + jnp.exp(-v))


def _split_bf16(v):
    hi = v.astype(BF16)
    lo = (v - hi.astype(F32)).astype(BF16)
    return hi, lo


def _dot(a, b):
    return jnp.dot(a, b, preferred_element_type=F32)


def _const_spec(shape):
    nd = len(shape)
    return pl.BlockSpec(shape, lambda *_: (0,) * nd, pipeline_mode=pl.Buffered(1))


def _lane_ids(shape):
    return lax.broadcasted_iota(jnp.int32, shape, 1).astype(F32)


def _mod_kernel(c_ref, w_ref, b_ref, o_ref):
    c = c_ref[...]
    a_hi, a_lo = _split_bf16(c * _sigmoid(c))
    w_hi, w_lo = _split_bf16(w_ref[...])
    o_ref[...] = _dot(a_hi, w_hi) + _dot(a_lo, w_hi) + _dot(a_hi, w_lo) + b_ref[...]


def _mod_call(c_pad, w_ada, b_ada):
    rows, d = c_pad.shape
    n_out = w_ada.shape[1]
    blk = 1024
    return pl.pallas_call(
        _mod_kernel,
        grid=(n_out // blk,),
        in_specs=[
            pl.BlockSpec((rows, d), lambda j: (0, 0)),
            pl.BlockSpec((d, blk), lambda j: (0, j)),
            pl.BlockSpec((1, blk), lambda j: (0, j)),
        ],
        out_specs=pl.BlockSpec((rows, blk), lambda j: (0, j)),
        out_shape=jax.ShapeDtypeStruct((rows, n_out), F32),
        compiler_params=pltpu.CompilerParams(dimension_semantics=("arbitrary",)),
        name="adaln_mod",
    )(c_pad, w_ada, b_ada)


def _zero_bits_of(v):
    return (lax.bitcast_convert_type(v, U32) >> 16) >> 16


def _depthwise_lane_block(buf_ref, w_ref, out_ref, phase_ref, n_taps, row0, rows, cb, after=None):
    first, stop = rows
    max_off = row0 + n_taps - 1
    n_keep = stop - first + (max_off // SUBLANES) * SUBLANES
    cols = slice(cb * LANES, (cb + 1) * LANES)
    for s in sorted({(row0 + k) % SUBLANES for k in range(n_taps)}):
        phase_ref[s, pl.ds(0, n_keep), :] = buf_ref[pl.ds(first + s, n_keep), cols]
    all_blocks = list(range(first, stop, CONV_ROWS))
    for g0 in range(0, len(all_blocks), CONV_BLOCKS_LIVE):
        row_blocks = all_blocks[g0:g0 + CONV_BLOCKS_LIVE]
        acc = [None] * len(row_blocks)
        for k in range(n_taps):
            q, s = divmod(row0 + k, SUBLANES)
            w_row = w_ref[k:k + 1, cols]
            if after is not None:
                w_row = lax.bitcast_convert_type(lax.bitcast_convert_type(w_row, U32) | after, F32)
            w_k = jnp.broadcast_to(w_row, (CONV_ROWS, LANES))
            for j, r0 in enumerate(row_blocks):
                term = phase_ref[s, pl.ds(r0 - first + q * SUBLANES, CONV_ROWS), :] * w_k
                acc[j] = term if acc[j] is None else acc[j] + term
        for j, r0 in enumerate(row_blocks):
            out_ref[pl.ds(r0, CONV_ROWS), cols] = acc[j]
    return _zero_bits_of(acc[-1][0:1, :])


def _route(logits, run_ref, live):
    shape = logits.shape
    lane = _lane_ids(shape)
    big = float(4 * ROUTE_LANES)
    neg_inf = -jnp.inf

    def first_argmax(v):
        m = jnp.max(v, axis=-1, keepdims=True)
        return m, jnp.min(jnp.where(v == m, lane, big), axis=-1, keepdims=True)

    is_group = lane < N_GROUPS
    g_max, g_idx = first_argmax(jnp.where(is_group, logits, neg_inf))
    g_prob = 1.0 / jnp.sum(jnp.where(is_group, jnp.exp(logits - g_max), 0.0), axis=-1, keepdims=True)
    lo = N_GROUPS + EXPERTS_PER_GROUP * g_idx
    in_group = (lane >= lo) & (lane < lo + EXPERTS_PER_GROUP)
    e_logits = jnp.where(in_group, logits, neg_inf)
    m1, i1 = first_argmax(e_logits)
    m2, i2 = first_argmax(jnp.where(lane == i1, neg_inf, e_logits))
    t = jnp.exp(m2 - m1)
    p1 = 1.0 / (1.0 + t)
    p2 = t / (1.0 + t)
    e1 = i1 - N_GROUPS
    e2 = i2 - N_GROUPS

    hot1 = lane == e1
    hot2 = lane == e2
    both = jnp.where(hot1 | hot2, 1.0, 0.0)
    n_rows = shape[0]
    earlier = (lax.broadcasted_iota(jnp.int32, (n_rows, n_rows), 0)
               > lax.broadcasted_iota(jnp.int32, (n_rows, n_rows), 1))
    before = _dot(jnp.where(earlier, 1.0, 0.0).astype(BF16), both.astype(BF16)) + run_ref[0:1, :]
    rank1 = jnp.sum(jnp.where(hot1, before, 0.0), axis=-1, keepdims=True)
    rank2 = jnp.sum(jnp.where(hot2, before, 0.0), axis=-1, keepdims=True)
    run_ref[0:1, :] = run_ref[0:1, :] + live * jnp.sum(both, axis=0, keepdims=True)

    rec = jnp.where(lane == REC_E, e1, 0.0)
    rec = jnp.where(lane == REC_E + 1, e2, rec)
    rec = jnp.where(lane == REC_W, g_prob * p1, rec)
    rec = jnp.where(lane == REC_W + 1, g_prob * p2, rec)
    rec = jnp.where(lane == REC_RANK, rank1, rec)
    rec = jnp.where(lane == REC_RANK + 1, rank2, rec)
    return rec


def _pack_bf16_pairs(v_bf16):
    half = v_bf16.shape[1] // 2
    lo = lax.bitcast_convert_type(v_bf16[:, :half].astype(F32), U32)
    hi = lax.bitcast_convert_type(v_bf16[:, half:].astype(F32), U32)
    return (lo >> 16) | hi


def _unpack_bf16_pairs(p_u32, dtype=BF16):
    lo = lax.bitcast_convert_type(p_u32 << 16, F32).astype(dtype)
    hi = lax.bitcast_convert_type(p_u32 & jnp.uint32(0xFFFF0000), F32).astype(dtype)
    return jnp.concatenate([lo, hi], axis=1)


def _stage_weight_bf16(src_hbm, dst_ref, stage_ref, sem):
    chunk = stage_ref.shape[2]
    n_chunks = src_hbm.shape[1] // chunk

    def cols(j):
        return pl.ds(pl.multiple_of(j * chunk, chunk), chunk)

    def copy(j, slot):
        return pltpu.make_async_copy(src_hbm.at[:, cols(j)], stage_ref.at[slot], sem.at[slot])

    def convert(j, slot):
        copy(j, slot).wait()
        dst_ref[:, cols(j)] = stage_ref[slot].astype(BF16)

    assert n_chunks % 2 == 0
    copy(0, 0).start(priority=0)

    def body(pair, carry):
        j = 2 * pair
        copy(j + 1, 1).start(priority=1)
        convert(j, 0)

        @pl.when(j + 2 < n_chunks)
        def _next():
            copy(j + 2, 0).start(priority=0)

        convert(j + 1, 1)
        return carry

    lax.fori_loop(0, n_chunks // 2, body, 0)


def _mixer_kernel(x_ref, modn_ref, modp_ref, g1_ref, ca_ref, cb_ref, bcb_ref, lng_ref, lnb_ref, g2_ref,
                  wrh_ref, wrl_ref, br_ref, win_hbm, wa_hbm, wb_hbm, wo_hbm,
                  x1_ref, h2p_ref, route_ref, cnt_ref,
                  wu_ref, wcx_ref, ba_ref, sga_ref, sgb_ref, xs_ref, cva_ref, cvb_ref, run_ref, logit_ref,
                  phase_ref, win_ref, wa_ref, wb_ref, wo_ref, stage_ref, stage_sem,
                  *, n_tiles):
    i = pl.program_id(0)
    t_rows, d = x_ref.shape

    @pl.when(i == 0)
    def _init():
        for ref in (wu_ref, wcx_ref, ba_ref, sga_ref, sgb_ref, xs_ref, run_ref, logit_ref):
            ref[...] = jnp.zeros(ref.shape, ref.dtype)
        for src, dst in ((win_hbm, win_ref), (wa_hbm, wa_ref), (wb_hbm, wb_ref), (wo_hbm, wo_ref)):
            _stage_weight_bf16(src, dst, stage_ref, stage_sem)

    x = x_ref[...]
    sh1 = modn_ref[0, 0:1, :]
    sc1 = modn_ref[0, 1:2, :]
    h = (x * lax.rsqrt(jnp.mean(x * x, axis=-1, keepdims=True) + RMS_EPS)) * g1_ref[...]
    hb = (h * (1.0 + sc1) + sh1).astype(BF16)

    def proj(g, after):
        lhs = hb
        if after is not None:
            bits = pltpu.bitcast(hb, U32)
            lhs = pltpu.bitcast(bits | jnp.broadcast_to(after[0:1, 0:1], bits.shape), BF16)
        return _dot(lhs, win_ref[:, g * d:(g + 1) * d])

    def conv_work(rows):
        items = []
        for cb in range(d // LANES):
            items.append(functools.partial(_depthwise_lane_block, wcx_ref, ca_ref, cva_ref,
                                           phase_ref.at[2], K_SHORT, HALO_A - K_SHORT // 2, rows, cb))
            items.append(functools.partial(_depthwise_lane_block, wu_ref, cb_ref, cvb_ref,
                                           phase_ref.at[cb % 2], K_CONF, HALO_B - K_CONF // 2, rows, cb))
        return items

    early = conv_work((0, t_rows - CONV_ROWS))
    n_proj = 7
    z = []
    conv_done = None
    for g in range(n_proj):
        z.append(proj(g, conv_done))
        released = _zero_bits_of(z[g - PAIR_LAG][0:1, 0:LANES]) if g >= PAIR_LAG else None
        for item in early[g * len(early) // n_proj:(g + 1) * len(early) // n_proj]:
            conv_done = item(after=released)
        if g == ROUTE_AFTER_GROUP:
            route_ref[...] = _route(logit_ref[...], run_ref, (i > 1).astype(F32))
            cnt_ref[...] = jnp.broadcast_to(run_ref[0:1, :], cnt_ref.shape)
    b_a = z[0]
    cx = z[1] * z[2]
    u = z[3] * _sigmoid(z[4])
    sg_a = _sigmoid(z[5])
    sg_b = _sigmoid(z[6])

    same_seq = ((i % n_tiles) != 0).astype(F32)
    wu_ref[pl.ds(HALO_B + t_rows, HALO_B), :] = u[0:HALO_B] * same_seq
    wcx_ref[pl.ds(HALO_A + t_rows, HALO_A), :] = cx[0:HALO_A] * same_seq

    for item in conv_work((t_rows - CONV_ROWS, t_rows)):
        item()
    y_a = _dot((ba_ref[...] * cva_ref[...]).astype(BF16), wa_ref[...])
    v = cvb_ref[...] + bcb_ref[...]
    mu = jnp.mean(v, axis=-1, keepdims=True)
    vc = v - mu
    var = jnp.mean(vc * vc, axis=-1, keepdims=True)
    v = (vc * lax.rsqrt(var + LN_EPS)) * lng_ref[...] + lnb_ref[...]
    y_b = _dot((v * _sigmoid(v)).astype(BF16), wb_ref[...])
    merged = sga_ref[...] * y_a + sgb_ref[...] * y_b
    mix = _dot(merged.astype(BF16), wo_ref[...])

    gt1 = modp_ref[0, 2:3, :]
    sh2 = modp_ref[0, 3:4, :]
    sc2 = modp_ref[0, 4:5, :]
    x1 = xs_ref[...] + gt1 * mix
    h2 = (x1 * lax.rsqrt(jnp.mean(x1 * x1, axis=-1, keepdims=True) + RMS_EPS)) * g2_ref[...]
    h2 = h2 * (1.0 + sc2) + sh2
    h2_hi, h2_lo = _split_bf16(h2)
    h2p = _pack_bf16_pairs(h2_hi)
    hi_part = _dot(h2_hi, wrl_ref[...])
    logit_ref[...] = (hi_part[:, :ROUTE_LANES] + _dot(h2_lo, wrh_ref[...]) + hi_part[:, ROUTE_LANES:]
                      + br_ref[...])

    @pl.when(i < pl.num_programs(0) - 1)
    def _emit_rows():
        x1_ref[...] = x1
        h2p_ref[...] = h2p

    tail_u = wu_ref[pl.ds(t_rows, HALO_B), :]
    wu_ref[pl.ds(0, HALO_B), :] = tail_u * same_seq
    wu_ref[pl.ds(HALO_B, t_rows), :] = u
    tail_cx = wcx_ref[pl.ds(t_rows, HALO_A), :]
    wcx_ref[pl.ds(0, HALO_A), :] = tail_cx * same_seq
    wcx_ref[pl.ds(HALO_A, t_rows), :] = cx
    ba_ref[...] = b_a
    sga_ref[...] = sg_a
    sgb_ref[...] = sg_b
    xs_ref[...] = x


def _mixer_call(x2d, mod3, g1, conv_a, conv_b, b_conv_b, ln_g, ln_b, g2,
                w_in, w_a, w_b, w_o, wr_hi, wr_lo, b_r, *, seq_len):
    n_tok, d = x2d.shape
    t = T_MIX
    n_tiles = seq_len // t
    n_chunks = n_tok // t
    last = n_chunks - 1

    def cur(i):
        return jnp.minimum(i, last)

    def prev(i):
        return jnp.clip(i - 1, 0, last)

    def prev2(i):
        return jnp.clip(i - 2, 0, last)

    row_spec_prev = lambda width: pl.BlockSpec((t, width), lambda i: (prev(i), 0))
    vec = lambda a: _const_spec(a.shape)
    hbm = pl.BlockSpec(memory_space=pl.ANY)
    in_specs = [
        pl.BlockSpec((t, d), lambda i: (cur(i), 0)),
        pl.BlockSpec((1,) + mod3.shape[1:], lambda i: (cur(i) // n_tiles, 0, 0)),
        pl.BlockSpec((1,) + mod3.shape[1:], lambda i: (prev(i) // n_tiles, 0, 0)),
        vec(g1), vec(conv_a), vec(conv_b), vec(b_conv_b), vec(ln_g), vec(ln_b), vec(g2),
        vec(wr_hi), vec(wr_lo), vec(b_r),
        hbm, hbm, hbm, hbm,
    ]
    out_specs = [row_spec_prev(d), row_spec_prev(d // 2),
                 pl.BlockSpec((t, ROUTE_LANES), lambda i: (prev2(i), 0)),
                 pl.BlockSpec((SUBLANES, ROUTE_LANES), lambda i: (0, 0))]
    out_shape = [
        jax.ShapeDtypeStruct((n_tok, d), F32),
        jax.ShapeDtypeStruct((n_tok, d // 2), U32),
        jax.ShapeDtypeStruct((n_tok, ROUTE_LANES), F32),
        jax.ShapeDtypeStruct((SUBLANES, ROUTE_LANES), F32),
    ]
    scratch = [
        pltpu.VMEM((t + 2 * HALO_B, d), F32),
        pltpu.VMEM((t + 2 * HALO_A, d), F32),
        pltpu.VMEM((t, d), F32), pltpu.VMEM((t, d), F32), pltpu.VMEM((t, d), F32), pltpu.VMEM((t, d), F32),
        pltpu.VMEM((t, d), F32), pltpu.VMEM((t, d), F32),
        pltpu.VMEM((SUBLANES, ROUTE_LANES), F32),
        pltpu.VMEM((t, ROUTE_LANES), F32),
        pltpu.VMEM((3, SUBLANES, t + 2 * HALO_B, LANES), F32),
        pltpu.VMEM(w_in.shape, BF16), pltpu.VMEM(w_a.shape, BF16), pltpu.VMEM(w_b.shape, BF16),
        pltpu.VMEM(w_o.shape, BF16),
        pltpu.VMEM((2, d, W_STAGE_COLS), F32), pltpu.SemaphoreType.DMA((2,)),
    ]
    return pl.pallas_call(
        functools.partial(_mixer_kernel, n_tiles=n_tiles),
        grid=(n_chunks + 2,),
        in_specs=in_specs,
        out_specs=out_specs,
        out_shape=out_shape,
        scratch_shapes=scratch,
        compiler_params=pltpu.CompilerParams(
            dimension_semantics=("arbitrary",), vmem_limit_bytes=VMEM_LIMIT_BYTES),
        name="mixer_router",
    )(x2d, mod3, mod3, g1, conv_a, conv_b, b_conv_b, ln_g, ln_b, g2,
      wr_hi, wr_lo, b_r, w_in, w_a, w_b, w_o)


def _segment_layout(counts_row, n_blocks):
    counts = counts_row[:N_EXPERTS].astype(jnp.int32)
    nblk = (counts + ROW_BLK - 1) // ROW_BLK
    blk_end = jnp.cumsum(nblk)
    blk0 = blk_end - nblk
    n_used = blk_end[-1:].astype(jnp.int32)
    pstart_row = jnp.zeros((1, ROUTE_LANES), F32).at[0, :N_EXPERTS].set((blk0 * ROW_BLK).astype(F32))
    blk = jnp.arange(n_blocks, dtype=jnp.int32)[:, None]
    blk_e = jnp.minimum(jnp.sum(blk_end[None, :] <= blk, axis=1), N_EXPERTS - 1).astype(jnp.int32)
    seg_end = jnp.min(jnp.where(blk_end[None, :] > blk, blk_end[None, :], n_blocks), axis=1)
    next_e = jnp.sum(blk_end[None, :] <= seg_end[:, None], axis=1).astype(jnp.int32)
    next_e = jnp.where(seg_end >= n_used[0], -1, next_e)
    last_blk = jnp.where(nblk > 0, blk_end - 1, -1).astype(jnp.int32)
    return pstart_row, blk_e, next_e, last_blk, n_used


def _plan_kernel(route_ref, pstart_ref, pos_ref):
    rec = route_ref[...]
    lane = _lane_ids(rec.shape)
    pos = jnp.zeros(rec.shape, F32)
    for k in range(TOP_K):
        e = rec[:, REC_E + k:REC_E + k + 1]
        seg = jnp.sum(jnp.where(lane == e, pstart_ref[...], 0.0), axis=-1, keepdims=True)
        pos = jnp.where(lane == k, seg + rec[:, REC_RANK + k:REC_RANK + k + 1], pos)
    pos_ref[...] = pos.astype(jnp.int32)


def _plan_call(route, pstart_row):
    n_tok = route.shape[0]
    t = T_PLAN
    return pl.pallas_call(
        _plan_kernel,
        grid=(n_tok // t,),
        in_specs=[pl.BlockSpec((t, ROUTE_LANES), lambda i: (i, 0)),
                  pl.BlockSpec((1, ROUTE_LANES), lambda i: (0, 0))],
        out_specs=pl.BlockSpec((t, ROUTE_LANES), lambda i: (i, 0)),
        out_shape=jax.ShapeDtypeStruct((n_tok, ROUTE_LANES), jnp.int32),
        compiler_params=pltpu.CompilerParams(dimension_semantics=("arbitrary",)),
        name="moe_plan",
    )(route, pstart_row)


def _scatter_kernel(last_blk_ref, n_used_ref, pos_ref, rows_ref, buf_hbm, zeros_ref, sem, zero_sem):
    t_rows = rows_ref.shape[0]
    n_blocks = buf_hbm.shape[0] // ROW_BLK

    @pl.when(pl.program_id(0) == 0)
    def _define_padding():
        zeros_ref[...] = jnp.zeros(zeros_ref.shape, zeros_ref.dtype)

        def block_copy(b):
            dst = buf_hbm.at[pl.ds(pl.multiple_of(b * ROW_BLK, ROW_BLK), ROW_BLK), :]
            return pltpu.make_async_copy(zeros_ref, dst, zero_sem)

        def for_padded_blocks(action):
            def per_expert(e, carry):
                @pl.when(last_blk_ref[e] >= 0)
                def _():
                    action(last_blk_ref[e])
                return carry

            def per_unused(b, carry):
                action(b)
                return carry

            lax.fori_loop(0, last_blk_ref.shape[0], per_expert, 0)
            lax.fori_loop(n_used_ref[0], n_blocks, per_unused, 0)

        for_padded_blocks(lambda b: block_copy(b).start())
        for_padded_blocks(lambda b: block_copy(b).wait())

    for r in range(t_rows):
        for k in range(TOP_K):
            pltpu.make_async_copy(rows_ref.at[pl.ds(r, 1), :],
                                  buf_hbm.at[pl.ds(pos_ref[0, k, r], 1), :], sem).start(priority=k)
    for _ in range(TOP_K):
        pltpu.make_async_copy(rows_ref, buf_hbm.at[pl.ds(0, t_rows), :], sem).wait()


def _scatter_call(pos3, rows, last_blk, n_used, n_rows):
    n_tok, width = rows.shape
    t = T_SCATTER
    grid_spec = pltpu.PrefetchScalarGridSpec(
        num_scalar_prefetch=2,
        grid=(n_tok // t,),
        in_specs=[
            pl.BlockSpec((1, TOP_K, t), lambda i, *_: (i, 0, 0), memory_space=pltpu.SMEM),
            pl.BlockSpec((t, width), lambda i, *_: (i, 0)),
        ],
        out_specs=pl.BlockSpec(memory_space=pl.ANY),
        scratch_shapes=[pltpu.VMEM((ROW_BLK, width), rows.dtype),
                        pltpu.SemaphoreType.DMA(()), pltpu.SemaphoreType.DMA(())],
    )
    return pl.pallas_call(
        _scatter_kernel,
        grid_spec=grid_spec,
        out_shape=jax.ShapeDtypeStruct((n_rows, width), rows.dtype),
        compiler_params=pltpu.CompilerParams(dimension_semantics=("arbitrary",)),
        name="moe_scatter",
    )(last_blk, n_used, pos3, rows)


def _expert_kernel(blk_e_ref, next_e_ref, n_used_ref, xs_hbm, wg_hbm, wu_hbm, wd_hbm, y_hbm,
                   wg_st, wu_st, wd_st, wg_bf, wu_bf, wd_bf, xbuf, ybuf, w_sem, in_sem, out_sem):
    n_used = n_used_ref[0]

    def rows(b):
        return pl.ds(pl.multiple_of(b * ROW_BLK, ROW_BLK), ROW_BLK)

    def load(b, slot):
        return pltpu.make_async_copy(xs_hbm.at[rows(b), :], xbuf.at[slot], in_sem.at[slot])

    def store(b, slot):
        return pltpu.make_async_copy(ybuf.at[slot], y_hbm.at[rows(b), :], out_sem.at[slot])

    def weight_copies(e, wslot):
        return [pltpu.make_async_copy(src.at[e], dst.at[wslot], w_sem.at[wslot, n])
                for n, (src, dst) in enumerate(((wg_hbm, wg_st), (wu_hbm, wu_st), (wd_hbm, wd_st)))]

    for copy in weight_copies(blk_e_ref[0], 0):
        copy.start(priority=1)
    load(0, 0).start()

    def body(b, wslot):
        e = blk_e_ref[b]
        new_expert = jnp.logical_or(b == 0, e != blk_e_ref[jnp.maximum(b - 1, 0)])
        slot = b % 2

        @pl.when(new_expert)
        def _switch_expert():
            for copy in weight_copies(e, wslot):
                copy.wait()
            wg_bf[...] = wg_st[wslot].astype(BF16)
            wu_bf[...] = wu_st[wslot].astype(BF16)
            wd_bf[...] = wd_st[wslot].astype(BF16)
            nxt = next_e_ref[b]

            @pl.when(nxt >= 0)
            def _prefetch_weights():
                for copy in weight_copies(nxt, 1 - wslot):
                    copy.start(priority=1)

        @pl.when(b + 1 < n_used)
        def _prefetch_rows():
            load(b + 1, 1 - slot).start()

        load(b, slot).wait()

        @pl.when(b >= 2)
        def _window_free():
            store(b - 2, slot).wait()

        xb = _unpack_bf16_pairs(xbuf[slot])
        a = _dot(xb, wg_bf[...])
        u = _dot(xb, wu_bf[...])
        hid = ((a * _sigmoid(a)) * u).astype(BF16)
        ybuf[slot] = _pack_bf16_pairs(_dot(hid, wd_bf[...]).astype(BF16))
        store(b, slot).start()
        return jnp.where(new_expert, 1 - wslot, wslot)

    lax.fori_loop(0, n_used, body, jnp.int32(0))

    @pl.when(n_used >= 2)
    def _drain_older():
        store(n_used - 2, n_used % 2).wait()

    store(n_used - 1, (n_used - 1) % 2).wait()

    ybuf[0] = jnp.zeros(ybuf.shape[1:], ybuf.dtype)

    def fill(b):
        return pltpu.make_async_copy(ybuf.at[0], y_hbm.at[rows(b), :], out_sem.at[0])

    def start_fill(b, carry):
        fill(b).start()
        return carry

    def wait_fill(b, carry):
        fill(b).wait()
        return carry

    n_blocks = y_hbm.shape[0] // ROW_BLK
    lax.fori_loop(n_used, n_blocks, start_fill, 0)
    lax.fori_loop(n_used, n_blocks, wait_fill, 0)


def _expert_call(xs, blk_e, next_e, n_used, w_gate, w_up, w_down):
    n_rows, half = xs.shape
    d = 2 * half
    d_e = w_gate.shape[-1]
    hbm = pl.BlockSpec(memory_space=pl.ANY)
    grid_spec = pltpu.PrefetchScalarGridSpec(
        num_scalar_prefetch=3,
        grid=(1,),
        in_specs=[hbm, hbm, hbm, hbm],
        out_specs=hbm,
        scratch_shapes=[
            pltpu.VMEM((2, d, d_e), F32), pltpu.VMEM((2, d, d_e), F32), pltpu.VMEM((2, d_e, d), F32),
            pltpu.VMEM((d, d_e), BF16), pltpu.VMEM((d, d_e), BF16), pltpu.VMEM((d_e, d), BF16),
            pltpu.VMEM((2, ROW_BLK, half), xs.dtype), pltpu.VMEM((2, ROW_BLK, half), xs.dtype),
            pltpu.SemaphoreType.DMA((2, 3)), pltpu.SemaphoreType.DMA((2,)), pltpu.SemaphoreType.DMA((2,)),
        ],
    )
    return pl.pallas_call(
        _expert_kernel,
        grid_spec=grid_spec,
        out_shape=jax.ShapeDtypeStruct((n_rows, half), xs.dtype),
        compiler_params=pltpu.CompilerParams(
            dimension_semantics=("arbitrary",), vmem_limit_bytes=VMEM_LIMIT_BYTES),
        name="moe_experts",
    )(blk_e, next_e, n_used, xs, w_gate, w_up, w_down)


def _row_gather_wait(src_hbm, dst, sem, n_rows):
    pltpu.make_async_copy(src_hbm.at[pl.ds(0, n_rows), :], dst, sem).wait()


def _combine_kernel(posc_ref, posn_ref, x1_ref, route_ref, mod_ref, gf_ref, y_hbm, o_ref, ybuf, sem,
                    *, n_steps, final_norm):
    i = pl.program_id(0)
    slot = i % 2
    t_rows = x1_ref.shape[0]

    def start(pos_ref, s):
        for k in range(TOP_K):
            for r in range(t_rows):
                pltpu.make_async_copy(y_hbm.at[pl.ds(pos_ref[0, k, r], 1), :],
                                      ybuf.at[s, k, pl.ds(r, 1), :], sem.at[s]).start(priority=r % 2)

    @pl.when(i == 0)
    def _prime():
        start(posc_ref, 0)

    @pl.when(i + 1 < n_steps)
    def _prefetch():
        start(posn_ref, 1 - slot)

    for k in range(TOP_K):
        _row_gather_wait(y_hbm, ybuf.at[slot, k], sem.at[slot], t_rows)
    rec = route_ref[...]
    y0 = _unpack_bf16_pairs(ybuf[slot, 0], F32)
    y1 = _unpack_bf16_pairs(ybuf[slot, 1], F32)
    moe = y0 * rec[:, REC_W:REC_W + 1] + y1 * rec[:, REC_W + 1:REC_W + 2]
    x2 = x1_ref[...] + mod_ref[0, 5:6, :] * moe
    if final_norm:
        x2 = (x2 * lax.rsqrt(jnp.mean(x2 * x2, axis=-1, keepdims=True) + RMS_EPS)) * gf_ref[...]
    o_ref[...] = x2


def _combine_call(x1, route, mod3, g_final, y_rows, pos3, *, seq_len, final_norm):
    n_tok, d = x1.shape
    t = T_COMBINE
    n_steps = n_tok // t
    tiles_per_seq = seq_len // t
    last = n_steps - 1
    return pl.pallas_call(
        functools.partial(_combine_kernel, n_steps=n_steps, final_norm=final_norm),
        grid=(n_steps,),
        in_specs=[
            pl.BlockSpec((1, TOP_K, t), lambda i: (i, 0, 0), memory_space=pltpu.SMEM),
            pl.BlockSpec((1, TOP_K, t), lambda i: (jnp.minimum(i + 1, last), 0, 0), memory_space=pltpu.SMEM),
            pl.BlockSpec((t, d), lambda i: (i, 0)),
            pl.BlockSpec((t, ROUTE_LANES), lambda i: (i, 0)),
            pl.BlockSpec((1,) + mod3.shape[1:], lambda i: (i // tiles_per_seq, 0, 0)),
            pl.BlockSpec((1, d), lambda i: (0, 0)),
            pl.BlockSpec(memory_space=pl.ANY),
        ],
        out_specs=pl.BlockSpec((t, d), lambda i: (i, 0)),
        out_shape=jax.ShapeDtypeStruct((n_tok, d), F32),
        scratch_shapes=[pltpu.VMEM((2, TOP_K, t, y_rows.shape[1]), y_rows.dtype),
                        pltpu.SemaphoreType.DMA((2,))],
        compiler_params=pltpu.CompilerParams(
            dimension_semantics=("arbitrary",), vmem_limit_bytes=VMEM_LIMIT_BYTES),
        name="moe_combine",
    )(pos3, pos3, x1, route, mod3, g_final, y_rows)


def kernel(x, c, w_ada, b_ada, g_norm1, w_in, conv_a, w_a_out, conv_b, b_conv_b, ln_conv_g, ln_conv_b,
           w_b_out, w_o, g_norm2, w_router_g, b_router_g, w_router_e, b_router_e, w_gate, w_up, w_down,
           g_final):
    bsz, seq_len, d = x.shape
    depth = w_ada.shape[0]
    n_tok = bsz * seq_len
    n_assign = n_tok * TOP_K
    assert seq_len % T_MIX == 0 and seq_len % T_COMBINE == 0 and d % (2 * LANES) == 0
    assert n_tok % T_SCATTER == 0
    assert n_tok % T_PLAN == 0
    assert n_assign % ROW_BLK == 0
    assert N_GROUPS + N_EXPERTS <= ROUTE_LANES
    n_rows = n_assign + N_EXPERTS * ROW_BLK

    c_pad = jnp.zeros((SUBLANES, d), F32).at[:bsz].set(c)
    xt = x.reshape(n_tok, d)
    row = lambda a: a.reshape(1, -1)
    for l in range(depth):
        mod = _mod_call(c_pad, w_ada[l], row(b_ada[l]))
        mod3 = mod[:bsz].reshape(bsz, 6, d)
        w_r = jnp.zeros((d, ROUTE_LANES), F32)
        w_r = w_r.at[:, :N_GROUPS].set(w_router_g[l]).at[:, N_GROUPS:N_GROUPS + N_EXPERTS].set(w_router_e[l])
        b_r = jnp.zeros((1, ROUTE_LANES), F32)
        b_r = b_r.at[0, :N_GROUPS].set(b_router_g[l]).at[0, N_GROUPS:N_GROUPS + N_EXPERTS].set(b_router_e[l])
        wr_hi, wr_lo = _split_bf16(w_r)
        wr_both = jnp.concatenate([wr_hi, wr_lo], axis=1)
        x1, h2p, route, cnt = _mixer_call(
            xt, mod3, row(g_norm1[l]), conv_a[l], conv_b[l], row(b_conv_b[l]), row(ln_conv_g[l]),
            row(ln_conv_b[l]), row(g_norm2[l]),
            w_in[l], w_a_out[l], w_b_out[l], w_o[l], wr_hi, wr_both, b_r, seq_len=seq_len)
        pstart_row, blk_e, next_e, last_blk, n_used = _segment_layout(cnt[0], n_rows // ROW_BLK)
        pos = _plan_call(route, pstart_row)
        pos_tiles = lambda t: pos[:, :TOP_K].reshape(n_tok // t, t, TOP_K).transpose(0, 2, 1)
        xs = _scatter_call(pos_tiles(T_SCATTER), h2p, last_blk, n_used, n_rows)
        y_rows = _expert_call(xs, blk_e, next_e, n_used, w_gate[l], w_up[l], w_down[l])
        xt = _combine_call(x1, route, mod3, row(g_final), y_rows, pos_tiles(T_COMBINE), seq_len=seq_len,
                           final_norm=(l == depth - 1))
    return xt.reshape(bsz, seq_len, d)
```

```python
import functools

import jax
import jax.numpy as jnp
from jax import lax
from jax.experimental import pallas as pl
from jax.experimental.pallas import tpu as pltpu

F32 = jnp.float32
BF16 = jnp.bfloat16
U32 = jnp.uint32

N_GROUPS = 4
EXPERTS_PER_GROUP = 8
N_EXPERTS = N_GROUPS * EXPERTS_PER_GROUP
TOP_K = 2
K_SHORT = 3
K_CONF = 31
RMS_EPS = 1e-6
LN_EPS = 1e-5

LANES = 128
SUBLANES = 8
T_MIX = 256
HALO_B = 16
HALO_A = 8
CONV_ROWS = 16
CONV_BLOCKS_LIVE = 15
PAIR_LAG = 1
ROUTE_AFTER_GROUP = 1
ROW_BLK = 512
T_SCATTER = 1024
T_COMBINE = 512
T_PLAN = 2048
W_STAGE_COLS = 512
ROUTE_LANES = LANES
REC_E, REC_W, REC_RANK = 0, 2, 4
VMEM_LIMIT_BYTES = 56 * 1024 * 1024


def _sigmoid(v):
    return 1.0 / (1.0 + jnp.exp(-v))


def _split_bf16(v):
    hi = v.astype(BF16)
    lo = (v - hi.astype(F32)).astype(BF16)
    return hi, lo


def _dot(a, b):
    return jnp.dot(a, b, preferred_element_type=F32)


def _const_spec(shape):
    nd = len(shape)
    return pl.BlockSpec(shape, lambda *_: (0,) * nd, pipeline_mode=pl.Buffered(1))


def _lane_ids(shape):
    return lax.broadcasted_iota(jnp.int32, shape, 1).astype(F32)


def _mod_kernel(c_ref, w_ref, b_ref, o_ref):
    c = c_ref[...]
    a_hi, a_lo = _split_bf16(c * _sigmoid(c))
    w_hi, w_lo = _split_bf16(w_ref[...])
    o_ref[...] = _dot(a_hi, w_hi) + _dot(a_lo, w_hi) + _dot(a_hi, w_lo) + b_ref[...]


def _mod_call(c_pad, w_ada, b_ada):
    rows, d = c_pad.shape
    n_out = w_ada.shape[1]
    blk = 1024
    return pl.pallas_call(
        _mod_kernel,
        grid=(n_out // blk,),
        in_specs=[
            pl.BlockSpec((rows, d), lambda j: (0, 0)),
            pl.BlockSpec((d, blk), lambda j: (0, j)),
            pl.BlockSpec((1, blk), lambda j: (0, j)),
        ],
        out_specs=pl.BlockSpec((rows, blk), lambda j: (0, j)),
        out_shape=jax.ShapeDtypeStruct((rows, n_out), F32),
        compiler_params=pltpu.CompilerParams(dimension_semantics=("arbitrary",)),
        name="adaln_mod",
    )(c_pad, w_ada, b_ada)


def _zero_bits_of(v):
    return (lax.bitcast_convert_type(v, U32) >> 16) >> 16


def _depthwise_lane_block(buf_ref, w_ref, out_ref, phase_ref, n_taps, row0, rows, cb, after=None):
    first, stop = rows
    max_off = row0 + n_taps - 1
    n_keep = stop - first + (max_off // SUBLANES) * SUBLANES
    cols = slice(cb * LANES, (cb + 1) * LANES)
    for s in sorted({(row0 + k) % SUBLANES for k in range(n_taps)}):
        phase_ref[s, pl.ds(0, n_keep), :] = buf_ref[pl.ds(first + s, n_keep), cols]
    all_blocks = list(range(first, stop, CONV_ROWS))
    for g0 in range(0, len(all_blocks), CONV_BLOCKS_LIVE):
        row_blocks = all_blocks[g0:g0 + CONV_BLOCKS_LIVE]
        acc = [None] * len(row_blocks)
        for k in range(n_taps):
            q, s = divmod(row0 + k, SUBLANES)
            w_row = w_ref[k:k + 1, cols]
            if after is not None:
                w_row = lax.bitcast_convert_type(lax.bitcast_convert_type(w_row, U32) | after, F32)
            w_k = jnp.broadcast_to(w_row, (CONV_ROWS, LANES))
            for j, r0 in enumerate(row_blocks):
                term = phase_ref[s, pl.ds(r0 - first + q * SUBLANES, CONV_ROWS), :] * w_k
                acc[j] = term if acc[j] is None else acc[j] + term
        for j, r0 in enumerate(row_blocks):
            out_ref[pl.ds(r0, CONV_ROWS), cols] = acc[j]
    return _zero_bits_of(acc[-1][0:1, :])


def _route(logits, run_ref, live):
    shape = logits.shape
    lane = _lane_ids(shape)
    big = float(4 * ROUTE_LANES)
    neg_inf = -jnp.inf

    def first_argmax(v):
        m = jnp.max(v, axis=-1, keepdims=True)
        return m, jnp.min(jnp.where(v == m, lane, big), axis=-1, keepdims=True)

    is_group = lane < N_GROUPS
    g_max, g_idx = first_argmax(jnp.where(is_group, logits, neg_inf))
    g_prob = 1.0 / jnp.sum(jnp.where(is_group, jnp.exp(logits - g_max), 0.0), axis=-1, keepdims=True)
    lo = N_GROUPS + EXPERTS_PER_GROUP * g_idx
    in_group = (lane >= lo) & (lane < lo + EXPERTS_PER_GROUP)
    e_logits = jnp.where(in_group, logits, neg_inf)
    m1, i1 = first_argmax(e_logits)
    m2, i2 = first_argmax(jnp.where(lane == i1, neg_inf, e_logits))
    t = jnp.exp(m2 - m1)
    p1 = 1.0 / (1.0 + t)
    p2 = t / (1.0 + t)
    e1 = i1 - N_GROUPS
    e2 = i2 - N_GROUPS

    hot1 = lane == e1
    hot2 = lane == e2
    both = jnp.where(hot1 | hot2, 1.0, 0.0)
    n_rows = shape[0]
    earlier = (lax.broadcasted_iota(jnp.int32, (n_rows, n_rows), 0)
               > lax.broadcasted_iota(jnp.int32, (n_rows, n_rows), 1))
    before = _dot(jnp.where(earlier, 1.0, 0.0).astype(BF16), both.astype(BF16)) + run_ref[0:1, :]
    rank1 = jnp.sum(jnp.where(hot1, before, 0.0), axis=-1, keepdims=True)
    rank2 = jnp.sum(jnp.where(hot2, before, 0.0), axis=-1, keepdims=True)
    run_ref[0:1, :] = run_ref[0:1, :] + live * jnp.sum(both, axis=0, keepdims=True)

    rec = jnp.where(lane == REC_E, e1, 0.0)
    rec = jnp.where(lane == REC_E + 1, e2, rec)
    rec = jnp.where(lane == REC_W, g_prob * p1, rec)
    rec = jnp.where(lane == REC_W + 1, g_prob * p2, rec)
    rec = jnp.where(lane == REC_RANK, rank1, rec)
    rec = jnp.where(lane == REC_RANK + 1, rank2, rec)
    return rec


def _pack_bf16_pairs(v_bf16):
    half = v_bf16.shape[1] // 2
    lo = lax.bitcast_convert_type(v_bf16[:, :half].astype(F32), U32)
    hi = lax.bitcast_convert_type(v_bf16[:, half:].astype(F32), U32)
    return (lo >> 16) | hi


def _unpack_bf16_pairs(p_u32, dtype=BF16):
    lo = lax.bitcast_convert_type(p_u32 << 16, F32).astype(dtype)
    hi = lax.bitcast_convert_type(p_u32 & jnp.uint32(0xFFFF0000), F32).astype(dtype)
    return jnp.concatenate([lo, hi], axis=1)


def _stage_weight_bf16(src_hbm, dst_ref, stage_ref, sem):
    chunk = stage_ref.shape[2]
    n_chunks = src_hbm.shape[1] // chunk

    def cols(j):
        return pl.ds(pl.multiple_of(j * chunk, chunk), chunk)

    def copy(j, slot):
        return pltpu.make_async_copy(src_hbm.at[:, cols(j)], stage_ref.at[slot], sem.at[slot])

    copy(0, 0).start()

    def body(j, carry):
        slot = j % 2

        @pl.when(j + 1 < n_chunks)
        def _next():
            copy(j + 1, 1 - slot).start()

        copy(j, slot).wait()
        dst_ref[:, cols(j)] = stage_ref[slot].astype(BF16)
        return carry

    lax.fori_loop(0, n_chunks, body, 0)


def _mixer_kernel(x_ref, modn_ref, modp_ref, g1_ref, ca_ref, cb_ref, bcb_ref, lng_ref, lnb_ref, g2_ref,
                  wrh_ref, wrl_ref, br_ref, win_hbm, wa_hbm, wb_hbm, wo_hbm,
                  x1_ref, h2p_ref, route_ref, cnt_ref,
                  wu_ref, wcx_ref, ba_ref, sga_ref, sgb_ref, xs_ref, cva_ref, cvb_ref, run_ref, logit_ref,
                  phase_ref, win_ref, wa_ref, wb_ref, wo_ref, stage_ref, stage_sem,
                  *, n_tiles):
    i = pl.program_id(0)
    t_rows, d = x_ref.shape

    @pl.when(i == 0)
    def _init():
        for ref in (wu_ref, wcx_ref, ba_ref, sga_ref, sgb_ref, xs_ref, run_ref, logit_ref):
            ref[...] = jnp.zeros(ref.shape, ref.dtype)
        for src, dst in ((win_hbm, win_ref), (wa_hbm, wa_ref), (wb_hbm, wb_ref), (wo_hbm, wo_ref)):
            _stage_weight_bf16(src, dst, stage_ref, stage_sem)

    x = x_ref[...]
    sh1 = modn_ref[0, 0:1, :]
    sc1 = modn_ref[0, 1:2, :]
    h = (x * lax.rsqrt(jnp.mean(x * x, axis=-1, keepdims=True) + RMS_EPS)) * g1_ref[...]
    hb = (h * (1.0 + sc1) + sh1).astype(BF16)

    def proj(g, after):
        lhs = hb
        if after is not None:
            bits = pltpu.bitcast(hb, U32)
            lhs = pltpu.bitcast(bits | jnp.broadcast_to(after[0:1, 0:1], bits.shape), BF16)
        return _dot(lhs, win_ref[:, g * d:(g + 1) * d])

    def conv_work(rows):
        items = []
        for cb in range(d // LANES):
            items.append(functools.partial(_depthwise_lane_block, wcx_ref, ca_ref, cva_ref,
                                           phase_ref.at[2], K_SHORT, HALO_A - K_SHORT // 2, rows, cb))
            items.append(functools.partial(_depthwise_lane_block, wu_ref, cb_ref, cvb_ref,
                                           phase_ref.at[cb % 2], K_CONF, HALO_B - K_CONF // 2, rows, cb))
        return items

    early = conv_work((0, t_rows - CONV_ROWS))
    n_proj = 7
    z = []
    conv_done = None
    for g in range(n_proj):
        z.append(proj(g, conv_done))
        released = _zero_bits_of(z[g - PAIR_LAG][0:1, 0:LANES]) if g >= PAIR_LAG else None
        for item in early[g * len(early) // n_proj:(g + 1) * len(early) // n_proj]:
            conv_done = item(after=released)
        if g == ROUTE_AFTER_GROUP:
            route_ref[...] = _route(logit_ref[...], run_ref, (i > 1).astype(F32))
            cnt_ref[...] = jnp.broadcast_to(run_ref[0:1, :], cnt_ref.shape)
    b_a = z[0]
    cx = z[1] * z[2]
    u = z[3] * _sigmoid(z[4])
    sg_a = _sigmoid(z[5])
    sg_b = _sigmoid(z[6])

    same_seq = ((i % n_tiles) != 0).astype(F32)
    wu_ref[pl.ds(HALO_B + t_rows, HALO_B), :] = u[0:HALO_B] * same_seq
    wcx_ref[pl.ds(HALO_A + t_rows, HALO_A), :] = cx[0:HALO_A] * same_seq

    for item in conv_work((t_rows - CONV_ROWS, t_rows)):
        item()
    y_a = _dot((ba_ref[...] * cva_ref[...]).astype(BF16), wa_ref[...])
    v = cvb_ref[...] + bcb_ref[...]
    mu = jnp.mean(v, axis=-1, keepdims=True)
    vc = v - mu
    var = jnp.mean(vc * vc, axis=-1, keepdims=True)
    v = (vc * lax.rsqrt(var + LN_EPS)) * lng_ref[...] + lnb_ref[...]
    y_b = _dot((v * _sigmoid(v)).astype(BF16), wb_ref[...])
    merged = sga_ref[...] * y_a + sgb_ref[...] * y_b
    mix = _dot(merged.astype(BF16), wo_ref[...])

    gt1 = modp_ref[0, 2:3, :]
    sh2 = modp_ref[0, 3:4, :]
    sc2 = modp_ref[0, 4:5, :]
    x1 = xs_ref[...] + gt1 * mix
    h2 = (x1 * lax.rsqrt(jnp.mean(x1 * x1, axis=-1, keepdims=True) + RMS_EPS)) * g2_ref[...]
    h2 = h2 * (1.0 + sc2) + sh2
    h2_hi, h2_lo = _split_bf16(h2)
    h2p = _pack_bf16_pairs(h2_hi)
    hi_part = _dot(h2_hi, wrl_ref[...])
    logit_ref[...] = (hi_part[:, :ROUTE_LANES] + _dot(h2_lo, wrh_ref[...]) + hi_part[:, ROUTE_LANES:]
                      + br_ref[...])

    @pl.when(i < pl.num_programs(0) - 1)
    def _emit_rows():
        x1_ref[...] = x1
        h2p_ref[...] = h2p

    tail_u = wu_ref[pl.ds(t_rows, HALO_B), :]
    wu_ref[pl.ds(0, HALO_B), :] = tail_u * same_seq
    wu_ref[pl.ds(HALO_B, t_rows), :] = u
    tail_cx = wcx_ref[pl.ds(t_rows, HALO_A), :]
    wcx_ref[pl.ds(0, HALO_A), :] = tail_cx * same_seq
    wcx_ref[pl.ds(HALO_A, t_rows), :] = cx
    ba_ref[...] = b_a
    sga_ref[...] = sg_a
    sgb_ref[...] = sg_b
    xs_ref[...] = x


def _mixer_call(x2d, mod3, g1, conv_a, conv_b, b_conv_b, ln_g, ln_b, g2,
                w_in, w_a, w_b, w_o, wr_hi, wr_lo, b_r, *, seq_len):
    n_tok, d = x2d.shape
    t = T_MIX
    n_tiles = seq_len // t
    n_chunks = n_tok // t
    last = n_chunks - 1

    def cur(i):
        return jnp.minimum(i, last)

    def prev(i):
        return jnp.clip(i - 1, 0, last)

    def prev2(i):
        return jnp.clip(i - 2, 0, last)

    row_spec_prev = lambda width: pl.BlockSpec((t, width), lambda i: (prev(i), 0))
    vec = lambda a: _const_spec(a.shape)
    hbm = pl.BlockSpec(memory_space=pl.ANY)
    in_specs = [
        pl.BlockSpec((t, d), lambda i: (cur(i), 0)),
        pl.BlockSpec((1,) + mod3.shape[1:], lambda i: (cur(i) // n_tiles, 0, 0)),
        pl.BlockSpec((1,) + mod3.shape[1:], lambda i: (prev(i) // n_tiles, 0, 0)),
        vec(g1), vec(conv_a), vec(conv_b), vec(b_conv_b), vec(ln_g), vec(ln_b), vec(g2),
        vec(wr_hi), vec(wr_lo), vec(b_r),
        hbm, hbm, hbm, hbm,
    ]
    out_specs = [row_spec_prev(d), row_spec_prev(d // 2),
                 pl.BlockSpec((t, ROUTE_LANES), lambda i: (prev2(i), 0)),
                 pl.BlockSpec((SUBLANES, ROUTE_LANES), lambda i: (0, 0))]
    out_shape = [
        jax.ShapeDtypeStruct((n_tok, d), F32),
        jax.ShapeDtypeStruct((n_tok, d // 2), U32),
        jax.ShapeDtypeStruct((n_tok, ROUTE_LANES), F32),
        jax.ShapeDtypeStruct((SUBLANES, ROUTE_LANES), F32),
    ]
    scratch = [
        pltpu.VMEM((t + 2 * HALO_B, d), F32),
        pltpu.VMEM((t + 2 * HALO_A, d), F32),
        pltpu.VMEM((t, d), F32), pltpu.VMEM((t, d), F32), pltpu.VMEM((t, d), F32), pltpu.VMEM((t, d), F32),
        pltpu.VMEM((t, d), F32), pltpu.VMEM((t, d), F32),
        pltpu.VMEM((SUBLANES, ROUTE_LANES), F32),
        pltpu.VMEM((t, ROUTE_LANES), F32),
        pltpu.VMEM((3, SUBLANES, t + 2 * HALO_B, LANES), F32),
        pltpu.VMEM(w_in.shape, BF16), pltpu.VMEM(w_a.shape, BF16), pltpu.VMEM(w_b.shape, BF16),
        pltpu.VMEM(w_o.shape, BF16),
        pltpu.VMEM((2, d, W_STAGE_COLS), F32), pltpu.SemaphoreType.DMA((2,)),
    ]
    return pl.pallas_call(
        functools.partial(_mixer_kernel, n_tiles=n_tiles),
        grid=(n_chunks + 2,),
        in_specs=in_specs,
        out_specs=out_specs,
        out_shape=out_shape,
        scratch_shapes=scratch,
        compiler_params=pltpu.CompilerParams(
            dimension_semantics=("arbitrary",), vmem_limit_bytes=VMEM_LIMIT_BYTES),
        name="mixer_router",
    )(x2d, mod3, mod3, g1, conv_a, conv_b, b_conv_b, ln_g, ln_b, g2,
      wr_hi, wr_lo, b_r, w_in, w_a, w_b, w_o)


def _segment_layout(counts_row, n_blocks):
    counts = counts_row[:N_EXPERTS].astype(jnp.int32)
    nblk = (counts + ROW_BLK - 1) // ROW_BLK
    blk_end = jnp.cumsum(nblk)
    blk0 = blk_end - nblk
    n_used = blk_end[-1:].astype(jnp.int32)
    pstart_row = jnp.zeros((1, ROUTE_LANES), F32).at[0, :N_EXPERTS].set((blk0 * ROW_BLK).astype(F32))
    blk = jnp.arange(n_blocks, dtype=jnp.int32)[:, None]
    blk_e = jnp.minimum(jnp.sum(blk_end[None, :] <= blk, axis=1), N_EXPERTS - 1).astype(jnp.int32)
    seg_end = jnp.min(jnp.where(blk_end[None, :] > blk, blk_end[None, :], n_blocks), axis=1)
    next_e = jnp.sum(blk_end[None, :] <= seg_end[:, None], axis=1).astype(jnp.int32)
    next_e = jnp.where(seg_end >= n_used[0], -1, next_e)
    last_blk = jnp.where(nblk > 0, blk_end - 1, -1).astype(jnp.int32)
    return pstart_row, blk_e, next_e, last_blk, n_used


def _plan_kernel(route_ref, pstart_ref, pos_ref):
    rec = route_ref[...]
    lane = _lane_ids(rec.shape)
    pos = jnp.zeros(rec.shape, F32)
    for k in range(TOP_K):
        e = rec[:, REC_E + k:REC_E + k + 1]
        seg = jnp.sum(jnp.where(lane == e, pstart_ref[...], 0.0), axis=-1, keepdims=True)
        pos = jnp.where(lane == k, seg + rec[:, REC_RANK + k:REC_RANK + k + 1], pos)
    pos_ref[...] = pos.astype(jnp.int32)


def _plan_call(route, pstart_row):
    n_tok = route.shape[0]
    t = T_PLAN
    return pl.pallas_call(
        _plan_kernel,
        grid=(n_tok // t,),
        in_specs=[pl.BlockSpec((t, ROUTE_LANES), lambda i: (i, 0)),
                  pl.BlockSpec((1, ROUTE_LANES), lambda i: (0, 0))],
        out_specs=pl.BlockSpec((t, ROUTE_LANES), lambda i: (i, 0)),
        out_shape=jax.ShapeDtypeStruct((n_tok, ROUTE_LANES), jnp.int32),
        compiler_params=pltpu.CompilerParams(dimension_semantics=("arbitrary",)),
        name="moe_plan",
    )(route, pstart_row)


def _scatter_kernel(last_blk_ref, n_used_ref, pos_ref, rows_ref, buf_hbm, zeros_ref, sem, zero_sem):
    t_rows = rows_ref.shape[0]
    n_blocks = buf_hbm.shape[0] // ROW_BLK

    @pl.when(pl.program_id(0) == 0)
    def _define_padding():
        zeros_ref[...] = jnp.zeros(zeros_ref.shape, zeros_ref.dtype)

        def block_copy(b):
            dst = buf_hbm.at[pl.ds(pl.multiple_of(b * ROW_BLK, ROW_BLK), ROW_BLK), :]
            return pltpu.make_async_copy(zeros_ref, dst, zero_sem)

        def for_padded_blocks(action):
            def per_expert(e, carry):
                @pl.when(last_blk_ref[e] >= 0)
                def _():
                    action(last_blk_ref[e])
                return carry

            def per_unused(b, carry):
                action(b)
                return carry

            lax.fori_loop(0, last_blk_ref.shape[0], per_expert, 0)
            lax.fori_loop(n_used_ref[0], n_blocks, per_unused, 0)

        for_padded_blocks(lambda b: block_copy(b).start())
        for_padded_blocks(lambda b: block_copy(b).wait())

    for r in range(t_rows):
        for k in range(TOP_K):
            pltpu.make_async_copy(rows_ref.at[pl.ds(r, 1), :],
                                  buf_hbm.at[pl.ds(pos_ref[0, k, r], 1), :], sem).start(priority=k)
    for _ in range(TOP_K):
        pltpu.make_async_copy(rows_ref, buf_hbm.at[pl.ds(0, t_rows), :], sem).wait()


def _scatter_call(pos3, rows, last_blk, n_used, n_rows):
    n_tok, width = rows.shape
    t = T_SCATTER
    grid_spec = pltpu.PrefetchScalarGridSpec(
        num_scalar_prefetch=2,
        grid=(n_tok // t,),
        in_specs=[
            pl.BlockSpec((1, TOP_K, t), lambda i, *_: (i, 0, 0), memory_space=pltpu.SMEM),
            pl.BlockSpec((t, width), lambda i, *_: (i, 0)),
        ],
        out_specs=pl.BlockSpec(memory_space=pl.ANY),
        scratch_shapes=[pltpu.VMEM((ROW_BLK, width), rows.dtype),
                        pltpu.SemaphoreType.DMA(()), pltpu.SemaphoreType.DMA(())],
    )
    return pl.pallas_call(
        _scatter_kernel,
        grid_spec=grid_spec,
        out_shape=jax.ShapeDtypeStruct((n_rows, width), rows.dtype),
        compiler_params=pltpu.CompilerParams(dimension_semantics=("arbitrary",)),
        name="moe_scatter",
    )(last_blk, n_used, pos3, rows)


def _expert_kernel(blk_e_ref, next_e_ref, n_used_ref, xs_hbm, wg_hbm, wu_hbm, wd_hbm, y_hbm,
                   wg_st, wu_st, wd_st, wg_bf, wu_bf, wd_bf, xbuf, ybuf, w_sem, in_sem, out_sem):
    n_used = n_used_ref[0]

    def rows(b):
        return pl.ds(pl.multiple_of(b * ROW_BLK, ROW_BLK), ROW_BLK)

    def load(b, slot):
        return pltpu.make_async_copy(xs_hbm.at[rows(b), :], xbuf.at[slot], in_sem.at[slot])

    def store(b, slot):
        return pltpu.make_async_copy(ybuf.at[slot], y_hbm.at[rows(b), :], out_sem.at[slot])

    def weight_copies(e, wslot):
        return [pltpu.make_async_copy(src.at[e], dst.at[wslot], w_sem.at[wslot, n])
                for n, (src, dst) in enumerate(((wg_hbm, wg_st), (wu_hbm, wu_st), (wd_hbm, wd_st)))]

    for copy in weight_copies(blk_e_ref[0], 0):
        copy.start(priority=1)
    load(0, 0).start()

    def body(b, wslot):
        e = blk_e_ref[b]
        new_expert = jnp.logical_or(b == 0, e != blk_e_ref[jnp.maximum(b - 1, 0)])
        slot = b % 2

        @pl.when(new_expert)
        def _switch_expert():
            for copy in weight_copies(e, wslot):
                copy.wait()
            wg_bf[...] = wg_st[wslot].astype(BF16)
            wu_bf[...] = wu_st[wslot].astype(BF16)
            wd_bf[...] = wd_st[wslot].astype(BF16)
            nxt = next_e_ref[b]

            @pl.when(nxt >= 0)
            def _prefetch_weights():
                for copy in weight_copies(nxt, 1 - wslot):
                    copy.start(priority=1)

        @pl.when(b + 1 < n_used)
        def _prefetch_rows():
            load(b + 1, 1 - slot).start()

        load(b, slot).wait()

        @pl.when(b >= 2)
        def _window_free():
            store(b - 2, slot).wait()

        xb = _unpack_bf16_pairs(xbuf[slot])
        a = _dot(xb, wg_bf[...])
        u = _dot(xb, wu_bf[...])
        hid = ((a * _sigmoid(a)) * u).astype(BF16)
        ybuf[slot] = _pack_bf16_pairs(_dot(hid, wd_bf[...]).astype(BF16))
        store(b, slot).start()
        return jnp.where(new_expert, 1 - wslot, wslot)

    lax.fori_loop(0, n_used, body, jnp.int32(0))

    @pl.when(n_used >= 2)
    def _drain_older():
        store(n_used - 2, n_used % 2).wait()

    store(n_used - 1, (n_used - 1) % 2).wait()

    ybuf[0] = jnp.zeros(ybuf.shape[1:], ybuf.dtype)

    def fill(b):
        return pltpu.make_async_copy(ybuf.at[0], y_hbm.at[rows(b), :], out_sem.at[0])

    def start_fill(b, carry):
        fill(b).start()
        return carry

    def wait_fill(b, carry):
        fill(b).wait()
        return carry

    n_blocks = y_hbm.shape[0] // ROW_BLK
    lax.fori_loop(n_used, n_blocks, start_fill, 0)
    lax.fori_loop(n_used, n_blocks, wait_fill, 0)


def _expert_call(xs, blk_e, next_e, n_used, w_gate, w_up, w_down):
    n_rows, half = xs.shape
    d = 2 * half
    d_e = w_gate.shape[-1]
    hbm = pl.BlockSpec(memory_space=pl.ANY)
    grid_spec = pltpu.PrefetchScalarGridSpec(
        num_scalar_prefetch=3,
        grid=(1,),
        in_specs=[hbm, hbm, hbm, hbm],
        out_specs=hbm,
        scratch_shapes=[
            pltpu.VMEM((2, d, d_e), F32), pltpu.VMEM((2, d, d_e), F32), pltpu.VMEM((2, d_e, d), F32),
            pltpu.VMEM((d, d_e), BF16), pltpu.VMEM((d, d_e), BF16), pltpu.VMEM((d_e, d), BF16),
            pltpu.VMEM((2, ROW_BLK, half), xs.dtype), pltpu.VMEM((2, ROW_BLK, half), xs.dtype),
            pltpu.SemaphoreType.DMA((2, 3)), pltpu.SemaphoreType.DMA((2,)), pltpu.SemaphoreType.DMA((2,)),
        ],
    )
    return pl.pallas_call(
        _expert_kernel,
        grid_spec=grid_spec,
        out_shape=jax.ShapeDtypeStruct((n_rows, half), xs.dtype),
        compiler_params=pltpu.CompilerParams(
            dimension_semantics=("arbitrary",), vmem_limit_bytes=VMEM_LIMIT_BYTES),
        name="moe_experts",
    )(blk_e, next_e, n_used, xs, w_gate, w_up, w_down)


def _row_gather_wait(src_hbm, dst, sem, n_rows):
    pltpu.make_async_copy(src_hbm.at[pl.ds(0, n_rows), :], dst, sem).wait()


def _combine_kernel(posc_ref, posn_ref, x1_ref, route_ref, mod_ref, gf_ref, y_hbm, o_ref, ybuf, sem,
                    *, n_steps, final_norm):
    i = pl.program_id(0)
    slot = i % 2
    t_rows = x1_ref.shape[0]

    def start(pos_ref, s):
        for k in range(TOP_K):
            for r in range(t_rows):
                pltpu.make_async_copy(y_hbm.at[pl.ds(pos_ref[0, k, r], 1), :],
                                      ybuf.at[s, k, pl.ds(r, 1), :], sem.at[s]).start(priority=k)

    @pl.when(i == 0)
    def _prime():
        start(posc_ref, 0)

    @pl.when(i + 1 < n_steps)
    def _prefetch():
        start(posn_ref, 1 - slot)

    for k in range(TOP_K):
        _row_gather_wait(y_hbm, ybuf.at[slot, k], sem.at[slot], t_rows)
    rec = route_ref[...]
    y0 = _unpack_bf16_pairs(ybuf[slot, 0], F32)
    y1 = _unpack_bf16_pairs(ybuf[slot, 1], F32)
    moe = y0 * rec[:, REC_W:REC_W + 1] + y1 * rec[:, REC_W + 1:REC_W + 2]
    x2 = x1_ref[...] + mod_ref[0, 5:6, :] * moe
    if final_norm:
        x2 = (x2 * lax.rsqrt(jnp.mean(x2 * x2, axis=-1, keepdims=True) + RMS_EPS)) * gf_ref[...]
    o_ref[...] = x2


def _combine_call(x1, route, mod3, g_final, y_rows, pos3, *, seq_len, final_norm):
    n_tok, d = x1.shape
    t = T_COMBINE
    n_steps = n_tok // t
    tiles_per_seq = seq_len // t
    last = n_steps - 1
    return pl.pallas_call(
        functools.partial(_combine_kernel, n_steps=n_steps, final_norm=final_norm),
        grid=(n_steps,),
        in_specs=[
            pl.BlockSpec((1, TOP_K, t), lambda i: (i, 0, 0), memory_space=pltpu.SMEM),
            pl.BlockSpec((1, TOP_K, t), lambda i: (jnp.minimum(i + 1, last), 0, 0), memory_space=pltpu.SMEM),
            pl.BlockSpec((t, d), lambda i: (i, 0)),
            pl.BlockSpec((t, ROUTE_LANES), lambda i: (i, 0)),
            pl.BlockSpec((1,) + mod3.shape[1:], lambda i: (i // tiles_per_seq, 0, 0)),
            pl.BlockSpec((1, d), lambda i: (0, 0)),
            pl.BlockSpec(memory_space=pl.ANY),
        ],
        out_specs=pl.BlockSpec((t, d), lambda i: (i, 0)),
        out_shape=jax.ShapeDtypeStruct((n_tok, d), F32),
        scratch_shapes=[pltpu.VMEM((2, TOP_K, t, y_rows.shape[1]), y_rows.dtype),
                        pltpu.SemaphoreType.DMA((2,))],
        compiler_params=pltpu.CompilerParams(
            dimension_semantics=("arbitrary",), vmem_limit_bytes=VMEM_LIMIT_BYTES),
        name="moe_combine",
    )(pos3, pos3, x1, route, mod3, g_final, y_rows)


def kernel(x, c, w_ada, b_ada, g_norm1, w_in, conv_a, w_a_out, conv_b, b_conv_b, ln_conv_g, ln_conv_b,
           w_b_out, w_o, g_norm2, w_router_g, b_router_g, w_router_e, b_router_e, w_gate, w_up, w_down,
           g_final):
    bsz, seq_len, d = x.shape
    depth = w_ada.shape[0]
    n_tok = bsz * seq_len
    n_assign = n_tok * TOP_K
    assert seq_len % T_MIX == 0 and seq_len % T_COMBINE == 0 and d % (2 * LANES) == 0
    assert n_tok % T_SCATTER == 0
    assert n_tok % T_PLAN == 0
    assert n_assign % ROW_BLK == 0
    assert N_GROUPS + N_EXPERTS <= ROUTE_LANES
    n_rows = n_assign + N_EXPERTS * ROW_BLK

    c_pad = jnp.zeros((SUBLANES, d), F32).at[:bsz].set(c)
    xt = x.reshape(n_tok, d)
    row = lambda a: a.reshape(1, -1)
    for l in range(depth):
        mod = _mod_call(c_pad, w_ada[l], row(b_ada[l]))
        mod3 = mod[:bsz].reshape(bsz, 6, d)
        w_r = jnp.zeros((d, ROUTE_LANES), F32)
        w_r = w_r.at[:, :N_GROUPS].set(w_router_g[l]).at[:, N_GROUPS:N_GROUPS + N_EXPERTS].set(w_router_e[l])
        b_r = jnp.zeros((1, ROUTE_LANES), F32)
        b_r = b_r.at[0, :N_GROUPS].set(b_router_g[l]).at[0, N_GROUPS:N_GROUPS + N_EXPERTS].set(b_router_e[l])
        wr_hi, wr_lo = _split_bf16(w_r)
        wr_both = jnp.concatenate([wr_hi, wr_lo], axis=1)
        x1, h2p, route, cnt = _mixer_call(
            xt, mod3, row(g_norm1[l]), conv_a[l], conv_b[l], row(b_conv_b[l]), row(ln_conv_g[l]),
            row(ln_conv_b[l]), row(g_norm2[l]),
            w_in[l], w_a_out[l], w_b_out[l], w_o[l], wr_hi, wr_both, b_r, seq_len=seq_len)
        pstart_row, blk_e, next_e, last_blk, n_used = _segment_layout(cnt[0], n_rows // ROW_BLK)
        pos = _plan_call(route, pstart_row)
        pos_tiles = lambda t: pos[:, :TOP_K].reshape(n_tok // t, t, TOP_K).transpose(0, 2, 1)
        xs = _scatter_call(pos_tiles(T_SCATTER), h2p, last_blk, n_used, n_rows)
        y_rows = _expert_call(xs, blk_e, next_e, n_used, w_gate[l], w_up[l], w_down[l])
        xt = _combine_call(x1, route, mod3, row(g_final), y_rows, pos_tiles(T_COMBINE), seq_len=seq_len,
                           final_norm=(l == depth - 1))
    return xt.reshape(bsz, seq_len, d)
```

```python
import functools

import jax
import jax.numpy as jnp
from jax import lax
from jax.experimental import pallas as pl
from jax.experimental.pallas import tpu as pltpu

F32 = jnp.float32
BF16 = jnp.bfloat16
U32 = jnp.uint32

N_GROUPS = 4
EXPERTS_PER_GROUP = 8
N_EXPERTS = N_GROUPS * EXPERTS_PER_GROUP
TOP_K = 2
K_SHORT = 3
K_CONF = 31
RMS_EPS = 1e-6
LN_EPS = 1e-5

LANES = 128
SUBLANES = 8
T_MIX = 256
HALO_B = 16
HALO_A = 8
CONV_ROWS = 16
CONV_BLOCKS_LIVE = 15
PAIR_LAG = 1
ROUTE_AFTER_GROUP = 1
ROW_BLK = 512
T_SCATTER = 1024
T_COMBINE = 512
T_PLAN = 2048
W_STAGE_COLS = 512
ROUTE_LANES = LANES
REC_E, REC_W, REC_RANK = 0, 2, 4
VMEM_LIMIT_BYTES = 56 * 1024 * 1024


def _sigmoid(v):
    return 1.0 / (1.0 + jnp.exp(-v))


def _split_bf16(v):
    hi = v.astype(BF16)
    lo = (v - hi.astype(F32)).astype(BF16)
    return hi, lo


def _dot(a, b):
    return jnp.dot(a, b, preferred_element_type=F32)


def _const_spec(shape):
    nd = len(shape)
    return pl.BlockSpec(shape, lambda *_: (0,) * nd, pipeline_mode=pl.Buffered(1))


def _lane_ids(shape):
    return lax.broadcasted_iota(jnp.int32, shape, 1).astype(F32)


def _mod_kernel(c_ref, w_ref, b_ref, o_ref):
    c = c_ref[...]
    a_hi, a_lo = _split_bf16(c * _sigmoid(c))
    w_hi, w_lo = _split_bf16(w_ref[...])
    o_ref[...] = _dot(a_hi, w_hi) + _dot(a_lo, w_hi) + _dot(a_hi, w_lo) + b_ref[...]


def _mod_call(c_pad, w_ada, b_ada):
    rows, d = c_pad.shape
    n_out = w_ada.shape[1]
    blk = 1024
    return pl.pallas_call(
        _mod_kernel,
        grid=(n_out // blk,),
        in_specs=[
            pl.BlockSpec((rows, d), lambda j: (0, 0)),
            pl.BlockSpec((d, blk), lambda j: (0, j)),
            pl.BlockSpec((1, blk), lambda j: (0, j)),
        ],
        out_specs=pl.BlockSpec((rows, blk), lambda j: (0, j)),
        out_shape=jax.ShapeDtypeStruct((rows, n_out), F32),
        compiler_params=pltpu.CompilerParams(dimension_semantics=("arbitrary",)),
        name="adaln_mod",
    )(c_pad, w_ada, b_ada)


def _zero_bits_of(v):
    return (lax.bitcast_convert_type(v, U32) >> 16) >> 16


def _depthwise_lane_block(buf_ref, w_ref, out_ref, phase_ref, n_taps, row0, rows, cb, after=None):
    first, stop = rows
    max_off = row0 + n_taps - 1
    n_keep = stop - first + (max_off // SUBLANES) * SUBLANES
    cols = slice(cb * LANES, (cb + 1) * LANES)
    for s in sorted({(row0 + k) % SUBLANES for k in range(n_taps)}):
        phase_ref[s, pl.ds(0, n_keep), :] = buf_ref[pl.ds(first + s, n_keep), cols]
    all_blocks = list(range(first, stop, CONV_ROWS))
    for g0 in range(0, len(all_blocks), CONV_BLOCKS_LIVE):
        row_blocks = all_blocks[g0:g0 + CONV_BLOCKS_LIVE]
        acc = [None] * len(row_blocks)
        for k in range(n_taps):
            q, s = divmod(row0 + k, SUBLANES)
            w_row = w_ref[k:k + 1, cols]
            if after is not None:
                w_row = lax.bitcast_convert_type(lax.bitcast_convert_type(w_row, U32) | after, F32)
            w_k = jnp.broadcast_to(w_row, (CONV_ROWS, LANES))
            for j, r0 in enumerate(row_blocks):
                term = phase_ref[s, pl.ds(r0 - first + q * SUBLANES, CONV_ROWS), :] * w_k
                acc[j] = term if acc[j] is None else acc[j] + term
        for j, r0 in enumerate(row_blocks):
            out_ref[pl.ds(r0, CONV_ROWS), cols] = acc[j]
    return _zero_bits_of(acc[-1][0:1, :])


def _route(logits, run_ref, live):
    shape = logits.shape
    lane = _lane_ids(shape)
    big = float(4 * ROUTE_LANES)
    neg_inf = -jnp.inf

    def first_argmax(v):
        m = jnp.max(v, axis=-1, keepdims=True)
        return m, jnp.min(jnp.where(v == m, lane, big), axis=-1, keepdims=True)

    is_group = lane < N_GROUPS
    g_max, g_idx = first_argmax(jnp.where(is_group, logits, neg_inf))
    g_prob = 1.0 / jnp.sum(jnp.where(is_group, jnp.exp(logits - g_max), 0.0), axis=-1, keepdims=True)
    lo = N_GROUPS + EXPERTS_PER_GROUP * g_idx
    in_group = (lane >= lo) & (lane < lo + EXPERTS_PER_GROUP)
    e_logits = jnp.where(in_group, logits, neg_inf)
    m1, i1 = first_argmax(e_logits)
    m2, i2 = first_argmax(jnp.where(lane == i1, neg_inf, e_logits))
    t = jnp.exp(m2 - m1)
    p1 = 1.0 / (1.0 + t)
    p2 = t / (1.0 + t)
    e1 = i1 - N_GROUPS
    e2 = i2 - N_GROUPS

    hot1 = lane == e1
    hot2 = lane == e2
    both = jnp.where(hot1 | hot2, 1.0, 0.0)
    n_rows = shape[0]
    earlier = (lax.broadcasted_iota(jnp.int32, (n_rows, n_rows), 0)
               > lax.broadcasted_iota(jnp.int32, (n_rows, n_rows), 1))
    before = _dot(jnp.where(earlier, 1.0, 0.0).astype(BF16), both.astype(BF16)) + run_ref[0:1, :]
    rank1 = jnp.sum(jnp.where(hot1, before, 0.0), axis=-1, keepdims=True)
    rank2 = jnp.sum(jnp.where(hot2, before, 0.0), axis=-1, keepdims=True)
    run_ref[0:1, :] = run_ref[0:1, :] + live * jnp.sum(both, axis=0, keepdims=True)

    rec = jnp.where(lane == REC_E, e1, 0.0)
    rec = jnp.where(lane == REC_E + 1, e2, rec)
    rec = jnp.where(lane == REC_W, g_prob * p1, rec)
    rec = jnp.where(lane == REC_W + 1, g_prob * p2, rec)
    rec = jnp.where(lane == REC_RANK, rank1, rec)
    rec = jnp.where(lane == REC_RANK + 1, rank2, rec)
    return rec


def _pack_bf16_pairs(v_bf16):
    half = v_bf16.shape[1] // 2
    lo = lax.bitcast_convert_type(v_bf16[:, :half].astype(F32), U32)
    hi = lax.bitcast_convert_type(v_bf16[:, half:].astype(F32), U32)
    return (lo >> 16) | hi


def _unpack_bf16_pairs(p_u32, dtype=BF16):
    lo = lax.bitcast_convert_type(p_u32 << 16, F32).astype(dtype)
    hi = lax.bitcast_convert_type(p_u32 & jnp.uint32(0xFFFF0000), F32).astype(dtype)
    return jnp.concatenate([lo, hi], axis=1)


def _stage_weight_bf16(src_hbm, dst_ref, stage_ref, sem):
    chunk = stage_ref.shape[2]
    n_chunks = src_hbm.shape[1] // chunk

    def cols(j):
        return pl.ds(pl.multiple_of(j * chunk, chunk), chunk)

    def copy(j, slot):
        return pltpu.make_async_copy(src_hbm.at[:, cols(j)], stage_ref.at[slot], sem.at[slot])

    copy(0, 0).start()

    def body(j, carry):
        slot = j % 2

        @pl.when(j + 1 < n_chunks)
        def _next():
            copy(j + 1, 1 - slot).start()

        copy(j, slot).wait()
        dst_ref[:, cols(j)] = stage_ref[slot].astype(BF16)
        return carry

    lax.fori_loop(0, n_chunks, body, 0)


def _mixer_kernel(x_ref, modn_ref, modp_ref, g1_ref, ca_ref, cb_ref, bcb_ref, lng_ref, lnb_ref, g2_ref,
                  wrh_ref, wrl_ref, br_ref, win_hbm, wa_hbm, wb_hbm, wo_hbm,
                  x1_ref, h2p_ref, route_ref, cnt_ref,
                  wu_ref, wcx_ref, ba_ref, sga_ref, sgb_ref, xs_ref, cva_ref, cvb_ref, run_ref, logit_ref,
                  phase_ref, win_ref, wa_ref, wb_ref, wo_ref, stage_ref, stage_sem,
                  *, n_tiles):
    i = pl.program_id(0)
    t_rows, d = x_ref.shape

    @pl.when(i == 0)
    def _init():
        for ref in (wu_ref, wcx_ref, ba_ref, sga_ref, sgb_ref, xs_ref, run_ref, logit_ref):
            ref[...] = jnp.zeros(ref.shape, ref.dtype)
        for src, dst in ((win_hbm, win_ref), (wa_hbm, wa_ref), (wb_hbm, wb_ref), (wo_hbm, wo_ref)):
            _stage_weight_bf16(src, dst, stage_ref, stage_sem)

    x = x_ref[...]
    sh1 = modn_ref[0, 0:1, :]
    sc1 = modn_ref[0, 1:2, :]
    h = (x * lax.rsqrt(jnp.mean(x * x, axis=-1, keepdims=True) + RMS_EPS)) * g1_ref[...]
    hb = (h * (1.0 + sc1) + sh1).astype(BF16)

    def proj(g, after):
        lhs = hb
        if after is not None:
            bits = pltpu.bitcast(hb, U32)
            lhs = pltpu.bitcast(bits | jnp.broadcast_to(after[0:1, 0:1], bits.shape), BF16)
        return _dot(lhs, win_ref[:, g * d:(g + 1) * d])

    def conv_work(rows):
        items = []
        for cb in range(d // LANES):
            items.append(functools.partial(_depthwise_lane_block, wcx_ref, ca_ref, cva_ref,
                                           phase_ref.at[2], K_SHORT, HALO_A - K_SHORT // 2, rows, cb))
            items.append(functools.partial(_depthwise_lane_block, wu_ref, cb_ref, cvb_ref,
                                           phase_ref.at[cb % 2], K_CONF, HALO_B - K_CONF // 2, rows, cb))
        return items

    early = conv_work((0, t_rows - CONV_ROWS))
    n_proj = 7
    z = []
    conv_done = None
    for g in range(n_proj):
        z.append(proj(g, conv_done))
        released = _zero_bits_of(z[g - PAIR_LAG][0:1, 0:LANES]) if g >= PAIR_LAG else None
        for item in early[g * len(early) // n_proj:(g + 1) * len(early) // n_proj]:
            conv_done = item(after=released)
        if g == ROUTE_AFTER_GROUP:
            route_ref[...] = _route(logit_ref[...], run_ref, (i > 1).astype(F32))
            cnt_ref[...] = jnp.broadcast_to(run_ref[0:1, :], cnt_ref.shape)
    b_a = z[0]
    cx = z[1] * z[2]
    u = z[3] * _sigmoid(z[4])
    sg_a = _sigmoid(z[5])
    sg_b = _sigmoid(z[6])

    same_seq = ((i % n_tiles) != 0).astype(F32)
    wu_ref[pl.ds(HALO_B + t_rows, HALO_B), :] = u[0:HALO_B] * same_seq
    wcx_ref[pl.ds(HALO_A + t_rows, HALO_A), :] = cx[0:HALO_A] * same_seq

    for item in conv_work((t_rows - CONV_ROWS, t_rows)):
        item()
    y_a = _dot((ba_ref[...] * cva_ref[...]).astype(BF16), wa_ref[...])
    v = cvb_ref[...] + bcb_ref[...]
    mu = jnp.mean(v, axis=-1, keepdims=True)
    vc = v - mu
    var = jnp.mean(vc * vc, axis=-1, keepdims=True)
    v = (vc * lax.rsqrt(var + LN_EPS)) * lng_ref[...] + lnb_ref[...]
    y_b = _dot((v * _sigmoid(v)).astype(BF16), wb_ref[...])
    merged = sga_ref[...] * y_a + sgb_ref[...] * y_b
    mix = _dot(merged.astype(BF16), wo_ref[...])

    gt1 = modp_ref[0, 2:3, :]
    sh2 = modp_ref[0, 3:4, :]
    sc2 = modp_ref[0, 4:5, :]
    x1 = xs_ref[...] + gt1 * mix
    h2 = (x1 * lax.rsqrt(jnp.mean(x1 * x1, axis=-1, keepdims=True) + RMS_EPS)) * g2_ref[...]
    h2 = h2 * (1.0 + sc2) + sh2
    h2_hi, h2_lo = _split_bf16(h2)
    h2p = _pack_bf16_pairs(h2_hi)
    hi_part = _dot(h2_hi, wrl_ref[...])
    logit_ref[...] = (hi_part[:, :ROUTE_LANES] + _dot(h2_lo, wrh_ref[...]) + hi_part[:, ROUTE_LANES:]
                      + br_ref[...])

    @pl.when(i < pl.num_programs(0) - 1)
    def _emit_rows():
        x1_ref[...] = x1
        h2p_ref[...] = h2p

    tail_u = wu_ref[pl.ds(t_rows, HALO_B), :]
    wu_ref[pl.ds(0, HALO_B), :] = tail_u * same_seq
    wu_ref[pl.ds(HALO_B, t_rows), :] = u
    tail_cx = wcx_ref[pl.ds(t_rows, HALO_A), :]
    wcx_ref[pl.ds(0, HALO_A), :] = tail_cx * same_seq
    wcx_ref[pl.ds(HALO_A, t_rows), :] = cx
    ba_ref[...] = b_a
    sga_ref[...] = sg_a
    sgb_ref[...] = sg_b
    xs_ref[...] = x


def _mixer_call(x2d, mod3, g1, conv_a, conv_b, b_conv_b, ln_g, ln_b, g2,
                w_in, w_a, w_b, w_o, wr_hi, wr_lo, b_r, *, seq_len):
    n_tok, d = x2d.shape
    t = T_MIX
    n_tiles = seq_len // t
    n_chunks = n_tok // t
    last = n_chunks - 1

    def cur(i):
        return jnp.minimum(i, last)

    def prev(i):
        return jnp.clip(i - 1, 0, last)

    def prev2(i):
        return jnp.clip(i - 2, 0, last)

    row_spec_prev = lambda width: pl.BlockSpec((t, width), lambda i: (prev(i), 0))
    vec = lambda a: _const_spec(a.shape)
    hbm = pl.BlockSpec(memory_space=pl.ANY)
    in_specs = [
        pl.BlockSpec((t, d), lambda i: (cur(i), 0)),
        pl.BlockSpec((1,) + mod3.shape[1:], lambda i: (cur(i) // n_tiles, 0, 0)),
        pl.BlockSpec((1,) + mod3.shape[1:], lambda i: (prev(i) // n_tiles, 0, 0)),
        vec(g1), vec(conv_a), vec(conv_b), vec(b_conv_b), vec(ln_g), vec(ln_b), vec(g2),
        vec(wr_hi), vec(wr_lo), vec(b_r),
        hbm, hbm, hbm, hbm,
    ]
    out_specs = [row_spec_prev(d), row_spec_prev(d // 2),
                 pl.BlockSpec((t, ROUTE_LANES), lambda i: (prev2(i), 0)),
                 pl.BlockSpec((SUBLANES, ROUTE_LANES), lambda i: (0, 0))]
    out_shape = [
        jax.ShapeDtypeStruct((n_tok, d), F32),
        jax.ShapeDtypeStruct((n_tok, d // 2), U32),
        jax.ShapeDtypeStruct((n_tok, ROUTE_LANES), F32),
        jax.ShapeDtypeStruct((SUBLANES, ROUTE_LANES), F32),
    ]
    scratch = [
        pltpu.VMEM((t + 2 * HALO_B, d), F32),
        pltpu.VMEM((t + 2 * HALO_A, d), F32),
        pltpu.VMEM((t, d), F32), pltpu.VMEM((t, d), F32), pltpu.VMEM((t, d), F32), pltpu.VMEM((t, d), F32),
        pltpu.VMEM((t, d), F32), pltpu.VMEM((t, d), F32),
        pltpu.VMEM((SUBLANES, ROUTE_LANES), F32),
        pltpu.VMEM((t, ROUTE_LANES), F32),
        pltpu.VMEM((3, SUBLANES, t + 2 * HALO_B, LANES), F32),
        pltpu.VMEM(w_in.shape, BF16), pltpu.VMEM(w_a.shape, BF16), pltpu.VMEM(w_b.shape, BF16),
        pltpu.VMEM(w_o.shape, BF16),
        pltpu.VMEM((2, d, W_STAGE_COLS), F32), pltpu.SemaphoreType.DMA((2,)),
    ]
    return pl.pallas_call(
        functools.partial(_mixer_kernel, n_tiles=n_tiles),
        grid=(n_chunks + 2,),
        in_specs=in_specs,
        out_specs=out_specs,
        out_shape=out_shape,
        scratch_shapes=scratch,
        compiler_params=pltpu.CompilerParams(
            dimension_semantics=("arbitrary",), vmem_limit_bytes=VMEM_LIMIT_BYTES),
        name="mixer_router",
    )(x2d, mod3, mod3, g1, conv_a, conv_b, b_conv_b, ln_g, ln_b, g2,
      wr_hi, wr_lo, b_r, w_in, w_a, w_b, w_o)


def _segment_layout(counts_row, n_blocks):
    counts = counts_row[:N_EXPERTS].astype(jnp.int32)
    nblk = (counts + ROW_BLK - 1) // ROW_BLK
    blk_end = jnp.cumsum(nblk)
    blk0 = blk_end - nblk
    n_used = blk_end[-1:].astype(jnp.int32)
    pstart_row = jnp.zeros((1, ROUTE_LANES), F32).at[0, :N_EXPERTS].set((blk0 * ROW_BLK).astype(F32))
    blk = jnp.arange(n_blocks, dtype=jnp.int32)[:, None]
    blk_e = jnp.minimum(jnp.sum(blk_end[None, :] <= blk, axis=1), N_EXPERTS - 1).astype(jnp.int32)
    seg_end = jnp.min(jnp.where(blk_end[None, :] > blk, blk_end[None, :], n_blocks), axis=1)
    next_e = jnp.sum(blk_end[None, :] <= seg_end[:, None], axis=1).astype(jnp.int32)
    next_e = jnp.where(seg_end >= n_used[0], -1, next_e)
    last_blk = jnp.where(nblk > 0, blk_end - 1, -1).astype(jnp.int32)
    return pstart_row, blk_e, next_e, last_blk, n_used


def _plan_kernel(route_ref, pstart_ref, pos_ref):
    rec = route_ref[...]
    lane = _lane_ids(rec.shape)
    pos = jnp.zeros(rec.shape, F32)
    for k in range(TOP_K):
        e = rec[:, REC_E + k:REC_E + k + 1]
        seg = jnp.sum(jnp.where(lane == e, pstart_ref[...], 0.0), axis=-1, keepdims=True)
        pos = jnp.where(lane == k, seg + rec[:, REC_RANK + k:REC_RANK + k + 1], pos)
    pos_ref[...] = pos.astype(jnp.int32)


def _plan_call(route, pstart_row):
    n_tok = route.shape[0]
    t = T_PLAN
    return pl.pallas_call(
        _plan_kernel,
        grid=(n_tok // t,),
        in_specs=[pl.BlockSpec((t, ROUTE_LANES), lambda i: (i, 0)),
                  pl.BlockSpec((1, ROUTE_LANES), lambda i: (0, 0))],
        out_specs=pl.BlockSpec((t, ROUTE_LANES), lambda i: (i, 0)),
        out_shape=jax.ShapeDtypeStruct((n_tok, ROUTE_LANES), jnp.int32),
        compiler_params=pltpu.CompilerParams(dimension_semantics=("arbitrary",)),
        name="moe_plan",
    )(route, pstart_row)


def _scatter_kernel(last_blk_ref, n_used_ref, pos_ref, rows_ref, buf_hbm, zeros_ref, sem, zero_sem):
    t_rows = rows_ref.shape[0]
    n_blocks = buf_hbm.shape[0] // ROW_BLK

    @pl.when(pl.program_id(0) == 0)
    def _define_padding():
        zeros_ref[...] = jnp.zeros(zeros_ref.shape, zeros_ref.dtype)

        def block_copy(b):
            dst = buf_hbm.at[pl.ds(pl.multiple_of(b * ROW_BLK, ROW_BLK), ROW_BLK), :]
            return pltpu.make_async_copy(zeros_ref, dst, zero_sem)

        def for_padded_blocks(action):
            def per_expert(e, carry):
                @pl.when(last_blk_ref[e] >= 0)
                def _():
                    action(last_blk_ref[e])
                return carry

            def per_unused(b, carry):
                action(b)
                return carry

            lax.fori_loop(0, last_blk_ref.shape[0], per_expert, 0)
            lax.fori_loop(n_used_ref[0], n_blocks, per_unused, 0)

        for_padded_blocks(lambda b: block_copy(b).start())
        for_padded_blocks(lambda b: block_copy(b).wait())

    for r in range(t_rows):
        for k in range(TOP_K):
            pltpu.make_async_copy(rows_ref.at[pl.ds(r, 1), :],
                                  buf_hbm.at[pl.ds(pos_ref[0, k, r], 1), :], sem).start(priority=k)
    for _ in range(TOP_K):
        pltpu.make_async_copy(rows_ref, buf_hbm.at[pl.ds(0, t_rows), :], sem).wait()


def _scatter_call(pos3, rows, last_blk, n_used, n_rows):
    n_tok, width = rows.shape
    t = T_SCATTER
    grid_spec = pltpu.PrefetchScalarGridSpec(
        num_scalar_prefetch=2,
        grid=(n_tok // t,),
        in_specs=[
            pl.BlockSpec((1, TOP_K, t), lambda i, *_: (i, 0, 0), memory_space=pltpu.SMEM),
            pl.BlockSpec((t, width), lambda i, *_: (i, 0)),
        ],
        out_specs=pl.BlockSpec(memory_space=pl.ANY),
        scratch_shapes=[pltpu.VMEM((ROW_BLK, width), rows.dtype),
                        pltpu.SemaphoreType.DMA(()), pltpu.SemaphoreType.DMA(())],
    )
    return pl.pallas_call(
        _scatter_kernel,
        grid_spec=grid_spec,
        out_shape=jax.ShapeDtypeStruct((n_rows, width), rows.dtype),
        compiler_params=pltpu.CompilerParams(dimension_semantics=("arbitrary",)),
        name="moe_scatter",
    )(last_blk, n_used, pos3, rows)


def _expert_kernel(blk_e_ref, next_e_ref, n_used_ref, xs_hbm, wg_hbm, wu_hbm, wd_hbm, y_hbm,
                   wg_st, wu_st, wd_st, wg_bf, wu_bf, wd_bf, xbuf, ybuf, w_sem, in_sem, out_sem):
    n_used = n_used_ref[0]

    def rows(b):
        return pl.ds(pl.multiple_of(b * ROW_BLK, ROW_BLK), ROW_BLK)

    def load(b, slot):
        return pltpu.make_async_copy(xs_hbm.at[rows(b), :], xbuf.at[slot], in_sem.at[slot])

    def store(b, slot):
        return pltpu.make_async_copy(ybuf.at[slot], y_hbm.at[rows(b), :], out_sem.at[slot])

    def weight_copies(e, wslot):
        return [pltpu.make_async_copy(src.at[e], dst.at[wslot], w_sem.at[wslot, n])
                for n, (src, dst) in enumerate(((wg_hbm, wg_st), (wu_hbm, wu_st), (wd_hbm, wd_st)))]

    for copy in weight_copies(blk_e_ref[0], 0):
        copy.start(priority=1)
    load(0, 0).start()

    def body(b, wslot):
        e = blk_e_ref[b]
        new_expert = jnp.logical_or(b == 0, e != blk_e_ref[jnp.maximum(b - 1, 0)])
        slot = b % 2

        @pl.when(b + 1 < n_used)
        def _prefetch_rows():
            load(b + 1, 1 - slot).start()

        @pl.when(new_expert)
        def _switch_expert():
            nxt = next_e_ref[b]

            @pl.when(nxt >= 0)
            def _prefetch_weights():
                for copy in weight_copies(nxt, 1 - wslot):
                    copy.start(priority=1)

            for copy in weight_copies(e, wslot):
                copy.wait()
            wg_bf[...] = wg_st[wslot].astype(BF16)
            wu_bf[...] = wu_st[wslot].astype(BF16)
            wd_bf[...] = wd_st[wslot].astype(BF16)

        load(b, slot).wait()

        @pl.when(b >= 2)
        def _window_free():
            store(b - 2, slot).wait()

        xb = _unpack_bf16_pairs(xbuf[slot])
        a = _dot(xb, wg_bf[...])
        u = _dot(xb, wu_bf[...])
        hid = ((a * _sigmoid(a)) * u).astype(BF16)
        ybuf[slot] = _pack_bf16_pairs(_dot(hid, wd_bf[...]).astype(BF16))
        store(b, slot).start()
        return jnp.where(new_expert, 1 - wslot, wslot)

    lax.fori_loop(0, n_used, body, jnp.int32(0))

    @pl.when(n_used >= 2)
    def _drain_older():
        store(n_used - 2, n_used % 2).wait()

    store(n_used - 1, (n_used - 1) % 2).wait()

    ybuf[0] = jnp.zeros(ybuf.shape[1:], ybuf.dtype)

    def fill(b):
        return pltpu.make_async_copy(ybuf.at[0], y_hbm.at[rows(b), :], out_sem.at[0])

    def start_fill(b, carry):
        fill(b).start()
        return carry

    def wait_fill(b, carry):
        fill(b).wait()
        return carry

    n_blocks = y_hbm.shape[0] // ROW_BLK
    lax.fori_loop(n_used, n_blocks, start_fill, 0)
    lax.fori_loop(n_used, n_blocks, wait_fill, 0)


def _expert_call(xs, blk_e, next_e, n_used, w_gate, w_up, w_down):
    n_rows, half = xs.shape
    d = 2 * half
    d_e = w_gate.shape[-1]
    hbm = pl.BlockSpec(memory_space=pl.ANY)
    grid_spec = pltpu.PrefetchScalarGridSpec(
        num_scalar_prefetch=3,
        grid=(1,),
        in_specs=[hbm, hbm, hbm, hbm],
        out_specs=hbm,
        scratch_shapes=[
            pltpu.VMEM((2, d, d_e), F32), pltpu.VMEM((2, d, d_e), F32), pltpu.VMEM((2, d_e, d), F32),
            pltpu.VMEM((d, d_e), BF16), pltpu.VMEM((d, d_e), BF16), pltpu.VMEM((d_e, d), BF16),
            pltpu.VMEM((2, ROW_BLK, half), xs.dtype), pltpu.VMEM((2, ROW_BLK, half), xs.dtype),
            pltpu.SemaphoreType.DMA((2, 3)), pltpu.SemaphoreType.DMA((2,)), pltpu.SemaphoreType.DMA((2,)),
        ],
    )
    return pl.pallas_call(
        _expert_kernel,
        grid_spec=grid_spec,
        out_shape=jax.ShapeDtypeStruct((n_rows, half), xs.dtype),
        compiler_params=pltpu.CompilerParams(
            dimension_semantics=("arbitrary",), vmem_limit_bytes=VMEM_LIMIT_BYTES),
        name="moe_experts",
    )(blk_e, next_e, n_used, xs, w_gate, w_up, w_down)


def _row_gather_wait(src_hbm, dst, sem, n_rows):
    pltpu.make_async_copy(src_hbm.at[pl.ds(0, n_rows), :], dst, sem).wait()


def _combine_kernel(posc_ref, posn_ref, x1_ref, route_ref, mod_ref, gf_ref, y_hbm, o_ref, ybuf, sem,
                    *, n_steps, final_norm):
    i = pl.program_id(0)
    slot = i % 2
    t_rows = x1_ref.shape[0]

    def start(pos_ref, s):
        for k in range(TOP_K):
            for r in range(t_rows):
                pltpu.make_async_copy(y_hbm.at[pl.ds(pos_ref[0, k, r], 1), :],
                                      ybuf.at[s, k, pl.ds(r, 1), :], sem.at[s]).start(priority=r % 2)

    @pl.when(i == 0)
    def _prime():
        start(posc_ref, 0)

    @pl.when(i + 1 < n_steps)
    def _prefetch():
        start(posn_ref, 1 - slot)

    for k in range(TOP_K):
        _row_gather_wait(y_hbm, ybuf.at[slot, k], sem.at[slot], t_rows)
    rec = route_ref[...]
    y0 = _unpack_bf16_pairs(ybuf[slot, 0], F32)
    y1 = _unpack_bf16_pairs(ybuf[slot, 1], F32)
    moe = y0 * rec[:, REC_W:REC_W + 1] + y1 * rec[:, REC_W + 1:REC_W + 2]
    x2 = x1_ref[...] + mod_ref[0, 5:6, :] * moe
    if final_norm:
        x2 = (x2 * lax.rsqrt(jnp.mean(x2 * x2, axis=-1, keepdims=True) + RMS_EPS)) * gf_ref[...]
    o_ref[...] = x2


def _combine_call(x1, route, mod3, g_final, y_rows, pos3, *, seq_len, final_norm):
    n_tok, d = x1.shape
    t = T_COMBINE
    n_steps = n_tok // t
    tiles_per_seq = seq_len // t
    last = n_steps - 1
    return pl.pallas_call(
        functools.partial(_combine_kernel, n_steps=n_steps, final_norm=final_norm),
        grid=(n_steps,),
        in_specs=[
            pl.BlockSpec((1, TOP_K, t), lambda i: (i, 0, 0), memory_space=pltpu.SMEM),
            pl.BlockSpec((1, TOP_K, t), lambda i: (jnp.minimum(i + 1, last), 0, 0), memory_space=pltpu.SMEM),
            pl.BlockSpec((t, d), lambda i: (i, 0)),
            pl.BlockSpec((t, ROUTE_LANES), lambda i: (i, 0)),
            pl.BlockSpec((1,) + mod3.shape[1:], lambda i: (i // tiles_per_seq, 0, 0)),
            pl.BlockSpec((1, d), lambda i: (0, 0)),
            pl.BlockSpec(memory_space=pl.ANY),
        ],
        out_specs=pl.BlockSpec((t, d), lambda i: (i, 0)),
        out_shape=jax.ShapeDtypeStruct((n_tok, d), F32),
        scratch_shapes=[pltpu.VMEM((2, TOP_K, t, y_rows.shape[1]), y_rows.dtype),
                        pltpu.SemaphoreType.DMA((2,))],
        compiler_params=pltpu.CompilerParams(
            dimension_semantics=("arbitrary",), vmem_limit_bytes=VMEM_LIMIT_BYTES),
        name="moe_combine",
    )(pos3, pos3, x1, route, mod3, g_final, y_rows)


def kernel(x, c, w_ada, b_ada, g_norm1, w_in, conv_a, w_a_out, conv_b, b_conv_b, ln_conv_g, ln_conv_b,
           w_b_out, w_o, g_norm2, w_router_g, b_router_g, w_router_e, b_router_e, w_gate, w_up, w_down,
           g_final):
    bsz, seq_len, d = x.shape
    depth = w_ada.shape[0]
    n_tok = bsz * seq_len
    n_assign = n_tok * TOP_K
    assert seq_len % T_MIX == 0 and seq_len % T_COMBINE == 0 and d % (2 * LANES) == 0
    assert n_tok % T_SCATTER == 0
    assert n_tok % T_PLAN == 0
    assert n_assign % ROW_BLK == 0
    assert N_GROUPS + N_EXPERTS <= ROUTE_LANES
    n_rows = n_assign + N_EXPERTS * ROW_BLK

    c_pad = jnp.zeros((SUBLANES, d), F32).at[:bsz].set(c)
    xt = x.reshape(n_tok, d)
    row = lambda a: a.reshape(1, -1)
    for l in range(depth):
        mod = _mod_call(c_pad, w_ada[l], row(b_ada[l]))
        mod3 = mod[:bsz].reshape(bsz, 6, d)
        w_r = jnp.zeros((d, ROUTE_LANES), F32)
        w_r = w_r.at[:, :N_GROUPS].set(w_router_g[l]).at[:, N_GROUPS:N_GROUPS + N_EXPERTS].set(w_router_e[l])
        b_r = jnp.zeros((1, ROUTE_LANES), F32)
        b_r = b_r.at[0, :N_GROUPS].set(b_router_g[l]).at[0, N_GROUPS:N_GROUPS + N_EXPERTS].set(b_router_e[l])
        wr_hi, wr_lo = _split_bf16(w_r)
        wr_both = jnp.concatenate([wr_hi, wr_lo], axis=1)
        x1, h2p, route, cnt = _mixer_call(
            xt, mod3, row(g_norm1[l]), conv_a[l], conv_b[l], row(b_conv_b[l]), row(ln_conv_g[l]),
            row(ln_conv_b[l]), row(g_norm2[l]),
            w_in[l], w_a_out[l], w_b_out[l], w_o[l], wr_hi, wr_both, b_r, seq_len=seq_len)
        pstart_row, blk_e, next_e, last_blk, n_used = _segment_layout(cnt[0], n_rows // ROW_BLK)
        pos = _plan_call(route, pstart_row)
        pos_tiles = lambda t: pos[:, :TOP_K].reshape(n_tok // t, t, TOP_K).transpose(0, 2, 1)
        xs = _scatter_call(pos_tiles(T_SCATTER), h2p, last_blk, n_used, n_rows)
        y_rows = _expert_call(xs, blk_e, next_e, n_used, w_gate[l], w_up[l], w_down[l])
        xt = _combine_call(x1, route, mod3, row(g_final), y_rows, pos_tiles(T_COMBINE), seq_len=seq_len,
                           final_norm=(l == depth - 1))
    return xt.reshape(bsz, seq_len, d)
```

```python
import functools

import jax
import jax.numpy as jnp
from jax import lax
from jax.experimental import pallas as pl
from jax.experimental.pallas import tpu as pltpu

F32 = jnp.float32
BF16 = jnp.bfloat16
U32 = jnp.uint32

N_GROUPS = 4
EXPERTS_PER_GROUP = 8
N_EXPERTS = N_GROUPS * EXPERTS_PER_GROUP
TOP_K = 2
K_SHORT = 3
K_CONF = 31
RMS_EPS = 1e-6
LN_EPS = 1e-5

LANES = 128
SUBLANES = 8
T_MIX = 256
HALO_B = 16
HALO_A = 8
CONV_ROWS = 16
CONV_BLOCKS_LIVE = 15
PAIR_LAG = 1
ROUTE_AFTER_GROUP = 1
ROW_BLK = 512
T_SCATTER = 1024
T_COMBINE = 512
T_PLAN = 2048
W_STAGE_COLS = 512
ROUTE_LANES = LANES
REC_E, REC_W, REC_RANK = 0, 2, 4
VMEM_LIMIT_BYTES = 56 * 1024 * 1024


def _sigmoid(v):
    return 1.0 / (1.0 + jnp.exp(-v))


def _split_bf16(v):
    hi = v.astype(BF16)
    lo = (v - hi.astype(F32)).astype(BF16)
    return hi, lo


def _dot(a, b):
    return jnp.dot(a, b, preferred_element_type=F32)


def _const_spec(shape):
    nd = len(shape)
    return pl.BlockSpec(shape, lambda *_: (0,) * nd, pipeline_mode=pl.Buffered(1))


def _lane_ids(shape):
    return lax.broadcasted_iota(jnp.int32, shape, 1).astype(F32)


def _mod_kernel(c_ref, w_ref, b_ref, o_ref):
    c = c_ref[...]
    a_hi, a_lo = _split_bf16(c * _sigmoid(c))
    w_hi, w_lo = _split_bf16(w_ref[...])
    o_ref[...] = _dot(a_hi, w_hi) + _dot(a_lo, w_hi) + _dot(a_hi, w_lo) + b_ref[...]


def _mod_call(c_pad, w_ada, b_ada):
    rows, d = c_pad.shape
    n_out = w_ada.shape[1]
    blk = 1024
    return pl.pallas_call(
        _mod_kernel,
        grid=(n_out // blk,),
        in_specs=[
            pl.BlockSpec((rows, d), lambda j: (0, 0)),
            pl.BlockSpec((d, blk), lambda j: (0, j)),
            pl.BlockSpec((1, blk), lambda j: (0, j)),
        ],
        out_specs=pl.BlockSpec((rows, blk), lambda j: (0, j)),
        out_shape=jax.ShapeDtypeStruct((rows, n_out), F32),
        compiler_params=pltpu.CompilerParams(dimension_semantics=("arbitrary",)),
        name="adaln_mod",
    )(c_pad, w_ada, b_ada)


def _zero_bits_of(v):
    return (lax.bitcast_convert_type(v, U32) >> 16) >> 16


def _depthwise_lane_block(buf_ref, w_ref, out_ref, phase_ref, n_taps, row0, rows, cb, after=None):
    first, stop = rows
    max_off = row0 + n_taps - 1
    n_keep = stop - first + (max_off // SUBLANES) * SUBLANES
    cols = slice(cb * LANES, (cb + 1) * LANES)
    for s in sorted({(row0 + k) % SUBLANES for k in range(n_taps)}):
        phase_ref[s, pl.ds(0, n_keep), :] = buf_ref[pl.ds(first + s, n_keep), cols]
    all_blocks = list(range(first, stop, CONV_ROWS))
    for g0 in range(0, len(all_blocks), CONV_BLOCKS_LIVE):
        row_blocks = all_blocks[g0:g0 + CONV_BLOCKS_LIVE]
        acc = [None] * len(row_blocks)
        for k in range(n_taps):
            q, s = divmod(row0 + k, SUBLANES)
            w_row = w_ref[k:k + 1, cols]
            if after is not None:
                w_row = lax.bitcast_convert_type(lax.bitcast_convert_type(w_row, U32) | after, F32)
            w_k = jnp.broadcast_to(w_row, (CONV_ROWS, LANES))
            for j, r0 in enumerate(row_blocks):
                term = phase_ref[s, pl.ds(r0 - first + q * SUBLANES, CONV_ROWS), :] * w_k
                acc[j] = term if acc[j] is None else acc[j] + term
        for j, r0 in enumerate(row_blocks):
            out_ref[pl.ds(r0, CONV_ROWS), cols] = acc[j]
    return _zero_bits_of(acc[-1][0:1, :])


def _route(logits, run_ref, live):
    shape = logits.shape
    lane = _lane_ids(shape)
    big = float(4 * ROUTE_LANES)
    neg_inf = -jnp.inf

    def first_argmax(v):
        m = jnp.max(v, axis=-1, keepdims=True)
        return m, jnp.min(jnp.where(v == m, lane, big), axis=-1, keepdims=True)

    is_group = lane < N_GROUPS
    g_max, g_idx = first_argmax(jnp.where(is_group, logits, neg_inf))
    g_prob = 1.0 / jnp.sum(jnp.where(is_group, jnp.exp(logits - g_max), 0.0), axis=-1, keepdims=True)
    lo = N_GROUPS + EXPERTS_PER_GROUP * g_idx
    in_group = (lane >= lo) & (lane < lo + EXPERTS_PER_GROUP)
    e_logits = jnp.where(in_group, logits, neg_inf)
    m1, i1 = first_argmax(e_logits)
    m2, i2 = first_argmax(jnp.where(lane == i1, neg_inf, e_logits))
    t = jnp.exp(m2 - m1)
    p1 = 1.0 / (1.0 + t)
    p2 = t / (1.0 + t)
    e1 = i1 - N_GROUPS
    e2 = i2 - N_GROUPS

    hot1 = lane == e1
    hot2 = lane == e2
    both = jnp.where(hot1 | hot2, 1.0, 0.0)
    n_rows = shape[0]
    earlier = (lax.broadcasted_iota(jnp.int32, (n_rows, n_rows), 0)
               > lax.broadcasted_iota(jnp.int32, (n_rows, n_rows), 1))
    before = _dot(jnp.where(earlier, 1.0, 0.0).astype(BF16), both.astype(BF16)) + run_ref[0:1, :]
    rank1 = jnp.sum(jnp.where(hot1, before, 0.0), axis=-1, keepdims=True)
    rank2 = jnp.sum(jnp.where(hot2, before, 0.0), axis=-1, keepdims=True)
    run_ref[0:1, :] = run_ref[0:1, :] + live * jnp.sum(both, axis=0, keepdims=True)

    rec = jnp.where(lane == REC_E, e1, 0.0)
    rec = jnp.where(lane == REC_E + 1, e2, rec)
    rec = jnp.where(lane == REC_W, g_prob * p1, rec)
    rec = jnp.where(lane == REC_W + 1, g_prob * p2, rec)
    rec = jnp.where(lane == REC_RANK, rank1, rec)
    rec = jnp.where(lane == REC_RANK + 1, rank2, rec)
    return rec


def _pack_bf16_pairs(v_bf16):
    half = v_bf16.shape[1] // 2
    lo = lax.bitcast_convert_type(v_bf16[:, :half].astype(F32), U32)
    hi = lax.bitcast_convert_type(v_bf16[:, half:].astype(F32), U32)
    return (lo >> 16) | hi


def _unpack_bf16_pairs(p_u32, dtype=BF16):
    lo = lax.bitcast_convert_type(p_u32 << 16, F32).astype(dtype)
    hi = lax.bitcast_convert_type(p_u32 & jnp.uint32(0xFFFF0000), F32).astype(dtype)
    return jnp.concatenate([lo, hi], axis=1)


def _stage_weight_bf16(src_hbm, dst_ref, stage_ref, sem):
    chunk = stage_ref.shape[2]
    n_chunks = src_hbm.shape[1] // chunk

    def cols(j):
        return pl.ds(pl.multiple_of(j * chunk, chunk), chunk)

    def copy(j, slot):
        return pltpu.make_async_copy(src_hbm.at[:, cols(j)], stage_ref.at[slot], sem.at[slot])

    copy(0, 0).start()

    def body(j, carry):
        slot = j % 2

        @pl.when(j + 1 < n_chunks)
        def _next():
            copy(j + 1, 1 - slot).start()

        copy(j, slot).wait()
        dst_ref[:, cols(j)] = stage_ref[slot].astype(BF16)
        return carry

    lax.fori_loop(0, n_chunks, body, 0)


def _mixer_kernel(x_ref, modn_ref, modp_ref, g1_ref, ca_ref, cb_ref, bcb_ref, lng_ref, lnb_ref, g2_ref,
                  wrh_ref, wrl_ref, br_ref, win_hbm, wa_hbm, wb_hbm, wo_hbm,
                  x1_ref, h2p_ref, route_ref, cnt_ref,
                  wu_ref, wcx_ref, ba_ref, sga_ref, sgb_ref, xs_ref, cva_ref, cvb_ref, run_ref, logit_ref,
                  phase_ref, win_ref, wa_ref, wb_ref, wo_ref, stage_ref, stage_sem,
                  *, n_tiles):
    i = pl.program_id(0)
    t_rows, d = x_ref.shape

    @pl.when(i == 0)
    def _init():
        for ref in (wu_ref, wcx_ref, ba_ref, sga_ref, sgb_ref, xs_ref, run_ref, logit_ref):
            ref[...] = jnp.zeros(ref.shape, ref.dtype)
        for src, dst in ((win_hbm, win_ref), (wa_hbm, wa_ref), (wb_hbm, wb_ref), (wo_hbm, wo_ref)):
            _stage_weight_bf16(src, dst, stage_ref, stage_sem)

    x = x_ref[...]
    sh1 = modn_ref[0, 0:1, :]
    sc1 = modn_ref[0, 1:2, :]
    h = (x * lax.rsqrt(jnp.mean(x * x, axis=-1, keepdims=True) + RMS_EPS)) * g1_ref[...]
    hb = (h * (1.0 + sc1) + sh1).astype(BF16)

    def proj(g, after):
        lhs = hb
        if after is not None:
            bits = pltpu.bitcast(hb, U32)
            lhs = pltpu.bitcast(bits | jnp.broadcast_to(after[0:1, 0:1], bits.shape), BF16)
        return _dot(lhs, win_ref[:, g * d:(g + 1) * d])

    def conv_work(rows):
        items = []
        for cb in range(d // LANES):
            items.append(functools.partial(_depthwise_lane_block, wcx_ref, ca_ref, cva_ref,
                                           phase_ref.at[2], K_SHORT, HALO_A - K_SHORT // 2, rows, cb))
            items.append(functools.partial(_depthwise_lane_block, wu_ref, cb_ref, cvb_ref,
                                           phase_ref.at[cb % 2], K_CONF, HALO_B - K_CONF // 2, rows, cb))
        return items

    early = conv_work((0, t_rows - CONV_ROWS))
    n_proj = 7
    z = []
    conv_done = None
    for g in range(n_proj):
        z.append(proj(g, conv_done))
        released = _zero_bits_of(z[g - PAIR_LAG][0:1, 0:LANES]) if g >= PAIR_LAG else None
        for item in early[g * len(early) // n_proj:(g + 1) * len(early) // n_proj]:
            conv_done = item(after=released)
        if g == ROUTE_AFTER_GROUP:
            route_ref[...] = _route(logit_ref[...], run_ref, (i > 1).astype(F32))
            cnt_ref[...] = jnp.broadcast_to(run_ref[0:1, :], cnt_ref.shape)
    b_a = z[0]
    cx = z[1] * z[2]
    u = z[3] * _sigmoid(z[4])
    sg_a = _sigmoid(z[5])
    sg_b = _sigmoid(z[6])

    same_seq = ((i % n_tiles) != 0).astype(F32)
    wu_ref[pl.ds(HALO_B + t_rows, HALO_B), :] = u[0:HALO_B] * same_seq
    wcx_ref[pl.ds(HALO_A + t_rows, HALO_A), :] = cx[0:HALO_A] * same_seq

    for item in conv_work((t_rows - CONV_ROWS, t_rows)):
        item()
    y_a = _dot((ba_ref[...] * cva_ref[...]).astype(BF16), wa_ref[...])
    v = cvb_ref[...] + bcb_ref[...]
    mu = jnp.mean(v, axis=-1, keepdims=True)
    vc = v - mu
    var = jnp.mean(vc * vc, axis=-1, keepdims=True)
    v = (vc * lax.rsqrt(var + LN_EPS)) * lng_ref[...] + lnb_ref[...]
    y_b = _dot((v * _sigmoid(v)).astype(BF16), wb_ref[...])
    merged = sga_ref[...] * y_a + sgb_ref[...] * y_b
    mix = _dot(merged.astype(BF16), wo_ref[...])

    gt1 = modp_ref[0, 2:3, :]
    sh2 = modp_ref[0, 3:4, :]
    sc2 = modp_ref[0, 4:5, :]
    x1 = xs_ref[...] + gt1 * mix
    h2 = (x1 * lax.rsqrt(jnp.mean(x1 * x1, axis=-1, keepdims=True) + RMS_EPS)) * g2_ref[...]
    h2 = h2 * (1.0 + sc2) + sh2
    h2_hi, h2_lo = _split_bf16(h2)
    h2p = _pack_bf16_pairs(h2_hi)
    hi_part = _dot(h2_hi, wrl_ref[...])
    logit_ref[...] = (hi_part[:, :ROUTE_LANES] + _dot(h2_lo, wrh_ref[...]) + hi_part[:, ROUTE_LANES:]
                      + br_ref[...])

    @pl.when(i < pl.num_programs(0) - 1)
    def _emit_rows():
        x1_ref[...] = x1
        h2p_ref[...] = h2p

    tail_u = wu_ref[pl.ds(t_rows, HALO_B), :]
    wu_ref[pl.ds(0, HALO_B), :] = tail_u * same_seq
    wu_ref[pl.ds(HALO_B, t_rows), :] = u
    tail_cx = wcx_ref[pl.ds(t_rows, HALO_A), :]
    wcx_ref[pl.ds(0, HALO_A), :] = tail_cx * same_seq
    wcx_ref[pl.ds(HALO_A, t_rows), :] = cx
    ba_ref[...] = b_a
    sga_ref[...] = sg_a
    sgb_ref[...] = sg_b
    xs_ref[...] = x


def _mixer_call(x2d, mod3, g1, conv_a, conv_b, b_conv_b, ln_g, ln_b, g2,
                w_in, w_a, w_b, w_o, wr_hi, wr_lo, b_r, *, seq_len):
    n_tok, d = x2d.shape
    t = T_MIX
    n_tiles = seq_len // t
    n_chunks = n_tok // t
    last = n_chunks - 1

    def cur(i):
        return jnp.minimum(i, last)

    def prev(i):
        return jnp.clip(i - 1, 0, last)

    def prev2(i):
        return jnp.clip(i - 2, 0, last)

    row_spec_prev = lambda width: pl.BlockSpec((t, width), lambda i: (prev(i), 0))
    vec = lambda a: _const_spec(a.shape)
    hbm = pl.BlockSpec(memory_space=pl.ANY)
    in_specs = [
        pl.BlockSpec((t, d), lambda i: (cur(i), 0)),
        pl.BlockSpec((1,) + mod3.shape[1:], lambda i: (cur(i) // n_tiles, 0, 0)),
        pl.BlockSpec((1,) + mod3.shape[1:], lambda i: (prev(i) // n_tiles, 0, 0)),
        vec(g1), vec(conv_a), vec(conv_b), vec(b_conv_b), vec(ln_g), vec(ln_b), vec(g2),
        vec(wr_hi), vec(wr_lo), vec(b_r),
        hbm, hbm, hbm, hbm,
    ]
    out_specs = [row_spec_prev(d), row_spec_prev(d // 2),
                 pl.BlockSpec((t, ROUTE_LANES), lambda i: (prev2(i), 0)),
                 pl.BlockSpec((SUBLANES, ROUTE_LANES), lambda i: (0, 0))]
    out_shape = [
        jax.ShapeDtypeStruct((n_tok, d), F32),
        jax.ShapeDtypeStruct((n_tok, d // 2), U32),
        jax.ShapeDtypeStruct((n_tok, ROUTE_LANES), F32),
        jax.ShapeDtypeStruct((SUBLANES, ROUTE_LANES), F32),
    ]
    scratch = [
        pltpu.VMEM((t + 2 * HALO_B, d), F32),
        pltpu.VMEM((t + 2 * HALO_A, d), F32),
        pltpu.VMEM((t, d), F32), pltpu.VMEM((t, d), F32), pltpu.VMEM((t, d), F32), pltpu.VMEM((t, d), F32),
        pltpu.VMEM((t, d), F32), pltpu.VMEM((t, d), F32),
        pltpu.VMEM((SUBLANES, ROUTE_LANES), F32),
        pltpu.VMEM((t, ROUTE_LANES), F32),
        pltpu.VMEM((3, SUBLANES, t + 2 * HALO_B, LANES), F32),
        pltpu.VMEM(w_in.shape, BF16), pltpu.VMEM(w_a.shape, BF16), pltpu.VMEM(w_b.shape, BF16),
        pltpu.VMEM(w_o.shape, BF16),
        pltpu.VMEM((2, d, W_STAGE_COLS), F32), pltpu.SemaphoreType.DMA((2,)),
    ]
    return pl.pallas_call(
        functools.partial(_mixer_kernel, n_tiles=n_tiles),
        grid=(n_chunks + 2,),
        in_specs=in_specs,
        out_specs=out_specs,
        out_shape=out_shape,
        scratch_shapes=scratch,
        compiler_params=pltpu.CompilerParams(
            dimension_semantics=("arbitrary",), vmem_limit_bytes=VMEM_LIMIT_BYTES),
        name="mixer_router",
    )(x2d, mod3, mod3, g1, conv_a, conv_b, b_conv_b, ln_g, ln_b, g2,
      wr_hi, wr_lo, b_r, w_in, w_a, w_b, w_o)


def _segment_layout(counts_row, n_blocks):
    counts = counts_row[:N_EXPERTS].astype(jnp.int32)
    nblk = (counts + ROW_BLK - 1) // ROW_BLK
    blk_end = jnp.cumsum(nblk)
    blk0 = blk_end - nblk
    n_used = blk_end[-1:].astype(jnp.int32)
    pstart_row = jnp.zeros((1, ROUTE_LANES), F32).at[0, :N_EXPERTS].set((blk0 * ROW_BLK).astype(F32))
    blk = jnp.arange(n_blocks, dtype=jnp.int32)[:, None]
    blk_e = jnp.minimum(jnp.sum(blk_end[None, :] <= blk, axis=1), N_EXPERTS - 1).astype(jnp.int32)
    seg_end = jnp.min(jnp.where(blk_end[None, :] > blk, blk_end[None, :], n_blocks), axis=1)
    next_e = jnp.sum(blk_end[None, :] <= seg_end[:, None], axis=1).astype(jnp.int32)
    next_e = jnp.where(seg_end >= n_used[0], -1, next_e)
    last_blk = jnp.where(nblk > 0, blk_end - 1, -1).astype(jnp.int32)
    return pstart_row, blk_e, next_e, last_blk, n_used


def _plan_kernel(route_ref, pstart_ref, pos_ref):
    rec = route_ref[...]
    lane = _lane_ids(rec.shape)
    pos = jnp.zeros(rec.shape, F32)
    for k in range(TOP_K):
        e = rec[:, REC_E + k:REC_E + k + 1]
        seg = jnp.sum(jnp.where(lane == e, pstart_ref[...], 0.0), axis=-1, keepdims=True)
        pos = jnp.where(lane == k, seg + rec[:, REC_RANK + k:REC_RANK + k + 1], pos)
    pos_ref[...] = pos.astype(jnp.int32)


def _plan_call(route, pstart_row):
    n_tok = route.shape[0]
    t = T_PLAN
    return pl.pallas_call(
        _plan_kernel,
        grid=(n_tok // t,),
        in_specs=[pl.BlockSpec((t, ROUTE_LANES), lambda i: (i, 0)),
                  pl.BlockSpec((1, ROUTE_LANES), lambda i: (0, 0))],
        out_specs=pl.BlockSpec((t, ROUTE_LANES), lambda i: (i, 0)),
        out_shape=jax.ShapeDtypeStruct((n_tok, ROUTE_LANES), jnp.int32),
        compiler_params=pltpu.CompilerParams(dimension_semantics=("arbitrary",)),
        name="moe_plan",
    )(route, pstart_row)


def _scatter_kernel(last_blk_ref, n_used_ref, pos_ref, rows_ref, buf_hbm, zeros_ref, sem, zero_sem):
    t_rows = rows_ref.shape[0]
    n_blocks = buf_hbm.shape[0] // ROW_BLK

    @pl.when(pl.program_id(0) == 0)
    def _define_padding():
        zeros_ref[...] = jnp.zeros(zeros_ref.shape, zeros_ref.dtype)

        def block_copy(b):
            dst = buf_hbm.at[pl.ds(pl.multiple_of(b * ROW_BLK, ROW_BLK), ROW_BLK), :]
            return pltpu.make_async_copy(zeros_ref, dst, zero_sem)

        def for_padded_blocks(action):
            def per_expert(e, carry):
                @pl.when(last_blk_ref[e] >= 0)
                def _():
                    action(last_blk_ref[e])
                return carry

            def per_unused(b, carry):
                action(b)
                return carry

            lax.fori_loop(0, last_blk_ref.shape[0], per_expert, 0)
            lax.fori_loop(n_used_ref[0], n_blocks, per_unused, 0)

        for_padded_blocks(lambda b: block_copy(b).start())
        for_padded_blocks(lambda b: block_copy(b).wait())

    for r in range(t_rows):
        for k in range(TOP_K):
            pltpu.make_async_copy(rows_ref.at[pl.ds(r, 1), :],
                                  buf_hbm.at[pl.ds(pos_ref[0, k, r], 1), :], sem).start(priority=k)
    for _ in range(TOP_K):
        pltpu.make_async_copy(rows_ref, buf_hbm.at[pl.ds(0, t_rows), :], sem).wait()


def _scatter_call(pos3, rows, last_blk, n_used, n_rows):
    n_tok, width = rows.shape
    t = T_SCATTER
    grid_spec = pltpu.PrefetchScalarGridSpec(
        num_scalar_prefetch=2,
        grid=(n_tok // t,),
        in_specs=[
            pl.BlockSpec((1, TOP_K, t), lambda i, *_: (i, 0, 0), memory_space=pltpu.SMEM),
            pl.BlockSpec((t, width), lambda i, *_: (i, 0)),
        ],
        out_specs=pl.BlockSpec(memory_space=pl.ANY),
        scratch_shapes=[pltpu.VMEM((ROW_BLK, width), rows.dtype),
                        pltpu.SemaphoreType.DMA(()), pltpu.SemaphoreType.DMA(())],
    )
    return pl.pallas_call(
        _scatter_kernel,
        grid_spec=grid_spec,
        out_shape=jax.ShapeDtypeStruct((n_rows, width), rows.dtype),
        compiler_params=pltpu.CompilerParams(dimension_semantics=("arbitrary",)),
        name="moe_scatter",
    )(last_blk, n_used, pos3, rows)


def _expert_kernel(blk_e_ref, next_e_ref, n_used_ref, xs_hbm, wg_hbm, wu_hbm, wd_hbm, y_hbm,
                   wg_st, wu_st, wd_st, wg_bf, wu_bf, wd_bf, xbuf, ybuf, w_sem, in_sem, out_sem):
    n_used = n_used_ref[0]

    def rows(b):
        return pl.ds(pl.multiple_of(b * ROW_BLK, ROW_BLK), ROW_BLK)

    def load(b, slot):
        return pltpu.make_async_copy(xs_hbm.at[rows(b), :], xbuf.at[slot], in_sem.at[slot])

    def store(b, slot):
        return pltpu.make_async_copy(ybuf.at[slot], y_hbm.at[rows(b), :], out_sem.at[slot])

    def weight_copies(e, wslot):
        return [pltpu.make_async_copy(src.at[e], dst.at[wslot], w_sem.at[wslot, n])
                for n, (src, dst) in enumerate(((wg_hbm, wg_st), (wu_hbm, wu_st), (wd_hbm, wd_st)))]

    for copy in weight_copies(blk_e_ref[0], 0):
        copy.start(priority=1)
    load(0, 0).start()

    def body(b, wslot):
        e = blk_e_ref[b]
        new_expert = jnp.logical_or(b == 0, e != blk_e_ref[jnp.maximum(b - 1, 0)])
        slot = b % 2

        @pl.when(b + 1 < n_used)
        def _prefetch_rows():
            load(b + 1, 1 - slot).start()

        @pl.when(new_expert)
        def _switch_expert():
            nxt = next_e_ref[b]

            @pl.when(nxt >= 0)
            def _prefetch_weights():
                for copy in weight_copies(nxt, 1 - wslot):
                    copy.start(priority=1)

            staged = ((wg_st, wg_bf), (wu_st, wu_bf), (wd_st, wd_bf))
            for copy, (f32_ref, bf16_ref) in zip(weight_copies(e, wslot), staged):
                copy.wait()
                bf16_ref[...] = f32_ref[wslot].astype(BF16)

        load(b, slot).wait()

        @pl.when(b >= 2)
        def _window_free():
            store(b - 2, slot).wait()

        xb = _unpack_bf16_pairs(xbuf[slot])
        a = _dot(xb, wg_bf[...])
        u = _dot(xb, wu_bf[...])
        hid = ((a * _sigmoid(a)) * u).astype(BF16)
        ybuf[slot] = _pack_bf16_pairs(_dot(hid, wd_bf[...]).astype(BF16))
        store(b, slot).start()
        return jnp.where(new_expert, 1 - wslot, wslot)

    lax.fori_loop(0, n_used, body, jnp.int32(0))

    @pl.when(n_used >= 2)
    def _drain_older():
        store(n_used - 2, n_used % 2).wait()

    store(n_used - 1, (n_used - 1) % 2).wait()

    ybuf[0] = jnp.zeros(ybuf.shape[1:], ybuf.dtype)

    def fill(b):
        return pltpu.make_async_copy(ybuf.at[0], y_hbm.at[rows(b), :], out_sem.at[0])

    def start_fill(b, carry):
        fill(b).start()
        return carry

    def wait_fill(b, carry):
        fill(b).wait()
        return carry

    n_blocks = y_hbm.shape[0] // ROW_BLK
    lax.fori_loop(n_used, n_blocks, start_fill, 0)
    lax.fori_loop(n_used, n_blocks, wait_fill, 0)


def _expert_call(xs, blk_e, next_e, n_used, w_gate, w_up, w_down):
    n_rows, half = xs.shape
    d = 2 * half
    d_e = w_gate.shape[-1]
    hbm = pl.BlockSpec(memory_space=pl.ANY)
    grid_spec = pltpu.PrefetchScalarGridSpec(
        num_scalar_prefetch=3,
        grid=(1,),
        in_specs=[hbm, hbm, hbm, hbm],
        out_specs=hbm,
        scratch_shapes=[
            pltpu.VMEM((2, d, d_e), F32), pltpu.VMEM((2, d, d_e), F32), pltpu.VMEM((2, d_e, d), F32),
            pltpu.VMEM((d, d_e), BF16), pltpu.VMEM((d, d_e), BF16), pltpu.VMEM((d_e, d), BF16),
            pltpu.VMEM((2, ROW_BLK, half), xs.dtype), pltpu.VMEM((2, ROW_BLK, half), xs.dtype),
            pltpu.SemaphoreType.DMA((2, 3)), pltpu.SemaphoreType.DMA((2,)), pltpu.SemaphoreType.DMA((2,)),
        ],
    )
    return pl.pallas_call(
        _expert_kernel,
        grid_spec=grid_spec,
        out_shape=jax.ShapeDtypeStruct((n_rows, half), xs.dtype),
        compiler_params=pltpu.CompilerParams(
            dimension_semantics=("arbitrary",), vmem_limit_bytes=VMEM_LIMIT_BYTES),
        name="moe_experts",
    )(blk_e, next_e, n_used, xs, w_gate, w_up, w_down)


def _row_gather_wait(src_hbm, dst, sem, n_rows):
    pltpu.make_async_copy(src_hbm.at[pl.ds(0, n_rows), :], dst, sem).wait()


def _combine_kernel(posc_ref, posn_ref, x1_ref, route_ref, mod_ref, gf_ref, y_hbm, o_ref, ybuf, sem,
                    *, n_steps, final_norm):
    i = pl.program_id(0)
    slot = i % 2
    t_rows = x1_ref.shape[0]

    def start(pos_ref, s):
        for k in range(TOP_K):
            for r in range(t_rows):
                pltpu.make_async_copy(y_hbm.at[pl.ds(pos_ref[0, k, r], 1), :],
                                      ybuf.at[s, k, pl.ds(r, 1), :], sem.at[s]).start(priority=r % 2)

    @pl.when(i == 0)
    def _prime():
        start(posc_ref, 0)

    @pl.when(i + 1 < n_steps)
    def _prefetch():
        start(posn_ref, 1 - slot)

    for k in range(TOP_K):
        _row_gather_wait(y_hbm, ybuf.at[slot, k], sem.at[slot], t_rows)
    rec = route_ref[...]
    y0 = _unpack_bf16_pairs(ybuf[slot, 0], F32)
    y1 = _unpack_bf16_pairs(ybuf[slot, 1], F32)
    moe = y0 * rec[:, REC_W:REC_W + 1] + y1 * rec[:, REC_W + 1:REC_W + 2]
    x2 = x1_ref[...] + mod_ref[0, 5:6, :] * moe
    if final_norm:
        x2 = (x2 * lax.rsqrt(jnp.mean(x2 * x2, axis=-1, keepdims=True) + RMS_EPS)) * gf_ref[...]
    o_ref[...] = x2


def _combine_call(x1, route, mod3, g_final, y_rows, pos3, *, seq_len, final_norm):
    n_tok, d = x1.shape
    t = T_COMBINE
    n_steps = n_tok // t
    tiles_per_seq = seq_len // t
    last = n_steps - 1
    return pl.pallas_call(
        functools.partial(_combine_kernel, n_steps=n_steps, final_norm=final_norm),
        grid=(n_steps,),
        in_specs=[
            pl.BlockSpec((1, TOP_K, t), lambda i: (i, 0, 0), memory_space=pltpu.SMEM),
            pl.BlockSpec((1, TOP_K, t), lambda i: (jnp.minimum(i + 1, last), 0, 0), memory_space=pltpu.SMEM),
            pl.BlockSpec((t, d), lambda i: (i, 0)),
            pl.BlockSpec((t, ROUTE_LANES), lambda i: (i, 0)),
            pl.BlockSpec((1,) + mod3.shape[1:], lambda i: (i // tiles_per_seq, 0, 0)),
            pl.BlockSpec((1, d), lambda i: (0, 0)),
            pl.BlockSpec(memory_space=pl.ANY),
        ],
        out_specs=pl.BlockSpec((t, d), lambda i: (i, 0)),
        out_shape=jax.ShapeDtypeStruct((n_tok, d), F32),
        scratch_shapes=[pltpu.VMEM((2, TOP_K, t, y_rows.shape[1]), y_rows.dtype),
                        pltpu.SemaphoreType.DMA((2,))],
        compiler_params=pltpu.CompilerParams(
            dimension_semantics=("arbitrary",), vmem_limit_bytes=VMEM_LIMIT_BYTES),
        name="moe_combine",
    )(pos3, pos3, x1, route, mod3, g_final, y_rows)


def kernel(x, c, w_ada, b_ada, g_norm1, w_in, conv_a, w_a_out, conv_b, b_conv_b, ln_conv_g, ln_conv_b,
           w_b_out, w_o, g_norm2, w_router_g, b_router_g, w_router_e, b_router_e, w_gate, w_up, w_down,
           g_final):
    bsz, seq_len, d = x.shape
    depth = w_ada.shape[0]
    n_tok = bsz * seq_len
    n_assign = n_tok * TOP_K
    assert seq_len % T_MIX == 0 and seq_len % T_COMBINE == 0 and d % (2 * LANES) == 0
    assert n_tok % T_SCATTER == 0
    assert n_tok % T_PLAN == 0
    assert n_assign % ROW_BLK == 0
    assert N_GROUPS + N_EXPERTS <= ROUTE_LANES
    n_rows = n_assign + N_EXPERTS * ROW_BLK

    c_pad = jnp.zeros((SUBLANES, d), F32).at[:bsz].set(c)
    xt = x.reshape(n_tok, d)
    row = lambda a: a.reshape(1, -1)
    for l in range(depth):
        mod = _mod_call(c_pad, w_ada[l], row(b_ada[l]))
        mod3 = mod[:bsz].reshape(bsz, 6, d)
        w_r = jnp.zeros((d, ROUTE_LANES), F32)
        w_r = w_r.at[:, :N_GROUPS].set(w_router_g[l]).at[:, N_GROUPS:N_GROUPS + N_EXPERTS].set(w_router_e[l])
        b_r = jnp.zeros((1, ROUTE_LANES), F32)
        b_r = b_r.at[0, :N_GROUPS].set(b_router_g[l]).at[0, N_GROUPS:N_GROUPS + N_EXPERTS].set(b_router_e[l])
        wr_hi, wr_lo = _split_bf16(w_r)
        wr_both = jnp.concatenate([wr_hi, wr_lo], axis=1)
        x1, h2p, route, cnt = _mixer_call(
            xt, mod3, row(g_norm1[l]), conv_a[l], conv_b[l], row(b_conv_b[l]), row(ln_conv_g[l]),
            row(ln_conv_b[l]), row(g_norm2[l]),
            w_in[l], w_a_out[l], w_b_out[l], w_o[l], wr_hi, wr_both, b_r, seq_len=seq_len)
        pstart_row, blk_e, next_e, last_blk, n_used = _segment_layout(cnt[0], n_rows // ROW_BLK)
        pos = _plan_call(route, pstart_row)
        pos_tiles = lambda t: pos[:, :TOP_K].reshape(n_tok // t, t, TOP_K).transpose(0, 2, 1)
        xs = _scatter_call(pos_tiles(T_SCATTER), h2p, last_blk, n_used, n_rows)
        y_rows = _expert_call(xs, blk_e, next_e, n_used, w_gate[l], w_up[l], w_down[l])
        xt = _combine_call(x1, route, mod3, row(g_final), y_rows, pos_tiles(T_COMBINE), seq_len=seq_len,
                           final_norm=(l == depth - 1))
    return xt.reshape(bsz, seq_len, d)
```
